```python
import jax, jax.numpy as jnp
from jax import lax
import numpy as np

D_MODEL = 1024
BATCH = 16
SEQ = 2048
DEPTH = 2

GRID_W = 64
CTX_LEN = 256
N_EVEN = (DEPTH + 1) // 2
N_ODD = DEPTH // 2
EPS = 1e-6
N_MOD = 6

A_HEADS = 8
A_KV_HEADS = 2
A_GROUP = A_HEADS // A_KV_HEADS
A_HEAD_DIM = 64
ROPE_THETA = 10000.0
Q_BLOCK = 128

B_HEADS = 4
B_DK = 64
B_DV = 128
B_GATE_RANK = 16
B_GATE_TAU = 16.0
B_CHUNK = 64

AB_SIZES = (A_HEADS * A_HEAD_DIM, A_KV_HEADS * A_HEAD_DIM, A_KV_HEADS * A_HEAD_DIM,
            B_HEADS * B_DK, B_HEADS * B_DK, B_HEADS * B_DV, B_HEADS * B_DV,
            B_GATE_RANK, B_GATE_RANK)
AB_WIDTH = sum(AB_SIZES)
AB_OUT = A_HEADS * A_HEAD_DIM + B_HEADS * B_DV

C_HEADS = 16
C_HEAD_DIM = 64
C_WIDTH = C_HEADS * C_HEAD_DIM
NA_ROWS = 8
NA_COLS = 16

D_FF = 3584
N_EXPERTS = 8
TOP_K = 2

kernel_name = 'hybrid_dit_gqa_gla_natten_moe'


def _rmsnorm(x, g):
    x32 = x.astype(jnp.float32)
    y = x32 * lax.rsqrt(jnp.mean(x32 * x32, axis=-1, keepdims=True) + EPS)
    return (y * g.astype(jnp.float32)).astype(x.dtype)


def _modulate(h, shift, scale):
    return h * (1 + scale) + shift


def _adaln(cond, w_mod, b_mod):
    m = jax.nn.silu(cond) @ w_mod + b_mod
    return jnp.split(m[..., None, :], N_MOD, axis=-1)


def _swiglu(h, wg, wu, wd):
    return (jax.nn.silu(h @ wg) * (h @ wu)) @ wd


def _moe(h, w_router, wg, wu, wd):
    logits = (h @ w_router).astype(jnp.float32)
    top_v, top_i = lax.top_k(logits, TOP_K)
    gates = jax.nn.softmax(top_v, axis=-1)
    comb = jnp.sum(gates[..., None] * jax.nn.one_hot(top_i, N_EXPERTS, dtype=jnp.float32), axis=-2).astype(h.dtype)
    out = jnp.zeros_like(h)
    for e in range(N_EXPERTS):
        out = out + comb[..., e:e + 1] * _swiglu(h, wg[e], wu[e], wd[e])
    return out


def _rope_2d(seq_len):
    t = jnp.arange(seq_len, dtype=jnp.int32)
    row = (t // GRID_W).astype(jnp.float32)
    col = (t % GRID_W).astype(jnp.float32)
    n_axis = A_HEAD_DIM // 4
    inv = jnp.power(ROPE_THETA, -jnp.arange(n_axis, dtype=jnp.float32) / n_axis)
    ang = jnp.concatenate([row[:, None] * inv, col[:, None] * inv], axis=-1)
    return jnp.cos(ang), jnp.sin(ang)


def _apply_rope(x, cos, sin):
    shp = (1, x.shape[1]) + (1,) * (x.ndim - 3) + (cos.shape[-1],)
    cos = cos.reshape(shp).astype(x.dtype)
    sin = sin.reshape(shp).astype(x.dtype)
    xp = x.reshape(x.shape[:-1] + (-1, 2))
    x0, x1 = xp[..., 0], xp[..., 1]
    return jnp.stack([x0 * cos - x1 * sin, x0 * sin + x1 * cos], axis=-1).reshape(x.shape)


def _sdpa(q, k, v):
    s = jnp.einsum('bqkgd,bskd->bkgqs', q, k).astype(jnp.float32)
    p = jax.nn.softmax(s, axis=-1).astype(v.dtype)
    return jnp.einsum('bkgqs,bskd->bqkgd', p, v)


def _blocked_sdpa(q, k, v):
    bsz, seq_len = q.shape[:2]
    nb = seq_len // Q_BLOCK
    qb = jnp.moveaxis(q.reshape((bsz, nb, Q_BLOCK) + q.shape[2:]), 1, 0)
    ob = lax.map(lambda qi: _sdpa(qi, k, v), qb)
    return jnp.moveaxis(ob, 0, 1).reshape(q.shape)


def _log_decay(z, w2, b):
    bsz, tot, _ = z.shape
    la = jax.nn.log_sigmoid((z @ w2 + b).astype(jnp.float32)) / B_GATE_TAU
    return la.reshape(bsz, tot, B_HEADS, B_DK)


def _gla_chunked(q, k, v, log_a):
    bsz, tot, nh, dk = q.shape
    dv = v.shape[-1]
    n = tot // B_CHUNK

    def chunks(a):
        return a.astype(jnp.float32).reshape(bsz, n, B_CHUNK, nh, a.shape[-1]).transpose(0, 3, 1, 2, 4)

    qc, kc, vc = chunks(q), chunks(k), chunks(v)
    bcum = jnp.cumsum(chunks(log_a), axis=3)
    b_last = bcum[:, :, :, -1:, :]
    q_dec = qc * jnp.exp(bcum)
    k_intra = kc * jnp.exp(-bcum)
    k_state = kc * jnp.exp(b_last - bcum)
    lower = jnp.tril(jnp.ones((B_CHUNK, B_CHUNK), dtype=bool))
    att = jnp.where(lower, jnp.einsum('bhncd,bhnsd->bhncs', q_dec, k_intra), 0.0)
    o_intra = jnp.einsum('bhncs,bhnsv->bhncv', att, vc)
    d_state = jnp.einsum('bhncd,bhncv->bhndv', k_state, vc)
    chunk_decay = jnp.exp(b_last[:, :, :, 0, :])

    def step(state, inp):
        dec, ds = inp
        return dec[..., None] * state + ds, state

    s0 = jnp.zeros((bsz, nh, dk, dv), jnp.float32)
    _, s_prev = lax.scan(step, s0, (jnp.moveaxis(chunk_decay, 2, 0), jnp.moveaxis(d_state, 2, 0)))
    o_inter = jnp.einsum('bhncd,nbhdv->bhncv', q_dec, s_prev)
    o = (o_intra + o_inter).transpose(0, 2, 3, 1, 4).reshape(bsz, tot, nh, dv)
    return o.astype(v.dtype)


def _mixer_ab(h_lat, h_ctx, w_in, g_q, g_k, w_a2_f, b_a_f, w_a2_b, b_a_b, g_gla, w_out, need_ctx):
    bsz, seq_len, _ = h_lat.shape
    n_ctx = h_ctx.shape[1]
    tot = n_ctx + seq_len
    proj = jnp.concatenate([h_ctx, h_lat], axis=1) @ w_in
    q_a, k_a, v_a, q_b, k_b, v_b, r_b, z_f, z_b = jnp.split(proj, np.cumsum(AB_SIZES)[:-1].tolist(), axis=-1)

    q_a = _rmsnorm(q_a.reshape(bsz, tot, A_KV_HEADS, A_GROUP, A_HEAD_DIM), g_q) * (A_HEAD_DIM ** -0.5)
    k_a = _rmsnorm(k_a.reshape(bsz, tot, A_KV_HEADS, A_HEAD_DIM), g_k)
    v_a = v_a.reshape(bsz, tot, A_KV_HEADS, A_HEAD_DIM)
    cos, sin = _rope_2d(seq_len)
    q_lat = _apply_rope(q_a[:, n_ctx:], cos, sin)
    k_all = jnp.concatenate([k_a[:, :n_ctx], _apply_rope(k_a[:, n_ctx:], cos, sin)], axis=1)
    o_a_lat = _blocked_sdpa(q_lat, k_all, v_a).reshape(bsz, seq_len, A_HEADS * A_HEAD_DIM)

    q_b = q_b.reshape(bsz, tot, B_HEADS, B_DK) * (B_DK ** -0.5)
    k_b = k_b.reshape(bsz, tot, B_HEADS, B_DK)
    v_b = v_b.reshape(bsz, tot, B_HEADS, B_DV)
    la_f = _log_decay(z_f, w_a2_f, b_a_f)
    la_b = _log_decay(z_b, w_a2_b, b_a_b)

    def rev(a):
        return jnp.concatenate([a[:, :n_ctx][:, ::-1], a[:, n_ctx:][:, ::-1]], axis=1)

    o_f = _gla_chunked(q_b, k_b, v_b, la_f)
    o_r = rev(_gla_chunked(rev(q_b), rev(k_b), rev(v_b), rev(la_b)))
    o_b = _rmsnorm(o_f + o_r, g_gla).reshape(bsz, tot, B_HEADS * B_DV) * jax.nn.silu(r_b)

    y_lat = jnp.concatenate([o_a_lat, o_b[:, n_ctx:]], axis=-1) @ w_out
    if not need_ctx:
        return y_lat, None
    o_a_ctx = _sdpa(q_a[:, :n_ctx], k_a[:, :n_ctx], v_a[:, :n_ctx]).reshape(bsz, n_ctx, A_HEADS * A_HEAD_DIM)
    y_ctx = jnp.concatenate([o_a_ctx, o_b[:, :n_ctx]], axis=-1) @ w_out
    return y_lat, y_ctx


def _mixer_c(h_lat, h_ctx, w_in, rpb, w_out, need_ctx):
    bsz, seq_len, _ = h_lat.shape
    n_ctx = h_ctx.shape[1]
    rows = seq_len // GRID_W
    kr = min(NA_ROWS, rows)
    kc = NA_COLS
    scale = C_HEAD_DIM ** -0.5
    q, k, v = jnp.split(h_lat @ w_in, 3, axis=-1)
    grid = (bsz, rows, GRID_W, C_HEADS, C_HEAD_DIM)
    q = q.reshape(grid) * scale
    k = k.reshape(grid)
    v = v.reshape(grid)
    k_ctx, v_ctx = jnp.split(h_ctx @ w_in[:, C_WIDTH:], 2, axis=-1)
    k_ctx = k_ctx.reshape(bsz, n_ctx, C_HEADS, C_HEAD_DIM)
    v_ctx = v_ctx.reshape(bsz, n_ctx, C_HEADS, C_HEAD_DIM)

    cols = jnp.arange(GRID_W, dtype=jnp.int32)
    col_start = jnp.clip(cols - kc // 2, 0, GRID_W - kc)
    in_win = (cols[None, :] >= col_start[:, None]) & (cols[None, :] < col_start[:, None] + kc)
    col_idx = jnp.clip(cols[None, :] - cols[:, None] + NA_COLS - 1, 0, 2 * NA_COLS - 2)
    rpb_cols = rpb.astype(jnp.float32)[:, :, col_idx]

    def row_block(r):
        rs = jnp.clip(r - kr // 2, 0, rows - kr)
        dr_idx = rs + jnp.arange(kr, dtype=jnp.int32) - r + NA_ROWS - 1
        bias = jnp.where(in_win[None, None], rpb_cols[:, dr_idx], -jnp.inf)
        bias = bias.transpose(0, 2, 1, 3)
        q_r = lax.dynamic_index_in_dim(q, r, axis=1, keepdims=False)
        k_blk = lax.dynamic_slice_in_dim(k, rs, kr, axis=1)
        v_blk = lax.dynamic_slice_in_dim(v, rs, kr, axis=1)
        s_loc = jnp.einsum('bqhd,bikhd->bhqik', q_r, k_blk).astype(jnp.float32) + bias
        s_ctx = jnp.einsum('bqhd,bshd->bhqs', q_r, k_ctx).astype(jnp.float32)
        s = jnp.concatenate([s_loc.reshape(bsz, C_HEADS, GRID_W, kr * GRID_W), s_ctx], axis=-1)
        p = jax.nn.softmax(s, axis=-1).astype(v.dtype)
        p_loc = p[..., :kr * GRID_W].reshape(bsz, C_HEADS, GRID_W, kr, GRID_W)
        p_ctx = p[..., kr * GRID_W:]
        return jnp.einsum('bhqik,bikhd->bqhd', p_loc, v_blk) + jnp.einsum('bhqs,bshd->bqhd', p_ctx, v_ctx)

    o = lax.map(row_block, jnp.arange(rows, dtype=jnp.int32))
    y_lat = jnp.moveaxis(o, 0, 1).reshape(bsz, seq_len, C_WIDTH) @ w_out
    if not need_ctx:
        return y_lat, None
    q_ctx = (h_ctx @ w_in[:, :C_WIDTH]).reshape(bsz, n_ctx, C_HEADS, 1, C_HEAD_DIM) * scale
    o_ctx = _sdpa(q_ctx, k_ctx, v_ctx).reshape(bsz, n_ctx, C_WIDTH)
    return y_lat, o_ctx @ w_out


def setup_inputs(seed: int = 0) -> dict:
    key = jax.random.key(seed)
    keys = iter(jax.random.split(key, 40))

    def nrm(shape, s):
        return jax.random.normal(next(keys), shape, jnp.float32) * s

    def gain(shape):
        return 1.0 + nrm(shape, 0.02)

    d = D_MODEL
    return {
        'x': nrm((BATCH, SEQ, d), 1.0),
        'c': nrm((BATCH, d), 1.0),
        'ctx': nrm((BATCH, CTX_LEN, d), 1.0),
        'c_ctx': nrm((d,), 1.0),
        'w_mod': nrm((DEPTH, d, N_MOD * d), 0.5 * d ** -0.5),
        'b_mod': nrm((DEPTH, N_MOD * d), 0.01),
        'g_norm1': gain((DEPTH, d)),
        'g_norm2': gain((DEPTH, d)),
        'w_in_ab': nrm((N_EVEN, d, AB_WIDTH), d ** -0.5),
        'g_q': gain((N_EVEN, A_HEAD_DIM)),
        'g_k': gain((N_EVEN, A_HEAD_DIM)),
        'w_a2_f': nrm((N_EVEN, B_GATE_RANK, B_HEADS * B_DK), B_GATE_RANK ** -0.5),
        'b_a_f': nrm((N_EVEN, B_HEADS * B_DK), 0.1),
        'w_a2_b': nrm((N_EVEN, B_GATE_RANK, B_HEADS * B_DK), B_GATE_RANK ** -0.5),
        'b_a_b': nrm((N_EVEN, B_HEADS * B_DK), 0.1),
        'g_gla': gain((N_EVEN, B_DV)),
        'w_out_ab': nrm((N_EVEN, AB_OUT, d), AB_OUT ** -0.5),
        'w_ff_gate': nrm((N_EVEN, d, D_FF), d ** -0.5),
        'w_ff_up': nrm((N_EVEN, d, D_FF), d ** -0.5),
        'w_ff_down': nrm((N_EVEN, D_FF, d), D_FF ** -0.5),
        'w_in_c': nrm((N_ODD, d, 3 * C_WIDTH), d ** -0.5),
        'rpb_c': nrm((N_ODD, C_HEADS, 2 * NA_ROWS - 1, 2 * NA_COLS - 1), 0.02),
        'w_out_c': nrm((N_ODD, C_WIDTH, d), C_WIDTH ** -0.5),
        'w_router': nrm((N_ODD, d, N_EXPERTS), d ** -0.5),
        'w_moe_gate': nrm((N_ODD, N_EXPERTS, d, D_FF), d ** -0.5),
        'w_moe_up': nrm((N_ODD, N_EXPERTS, d, D_FF), d ** -0.5),
        'w_moe_down': nrm((N_ODD, N_EXPERTS, D_FF, d), D_FF ** -0.5),
        'g_final': gain((d,)),
    }


def reference(x, c, ctx, c_ctx, w_mod, b_mod, g_norm1, g_norm2, w_in_ab, g_q, g_k, w_a2_f, b_a_f,
              w_a2_b, b_a_b, g_gla, w_out_ab, w_ff_gate, w_ff_up, w_ff_down, w_in_c, rpb_c, w_out_c,
              w_router, w_moe_gate, w_moe_up, w_moe_down, g_final):
    for i in range(DEPTH):
        j = i // 2
        need_ctx = i < DEPTH - 1
        sh1, sc1, gt1, sh2, sc2, gt2 = _adaln(c, w_mod[i], b_mod[i])
        cmod = _adaln(c_ctx, w_mod[i], b_mod[i])
        h_lat = _modulate(_rmsnorm(x, g_norm1[i]), sh1, sc1)
        h_ctx = _modulate(_rmsnorm(ctx, g_norm1[i]), cmod[0], cmod[1])
        if i % 2 == 0:
            y_lat, y_ctx = _mixer_ab(h_lat, h_ctx, w_in_ab[j], g_q[j], g_k[j], w_a2_f[j], b_a_f[j],
                                     w_a2_b[j], b_a_b[j], g_gla[j], w_out_ab[j], need_ctx)

            def ffn(h):
                return _swiglu(h, w_ff_gate[j], w_ff_up[j], w_ff_down[j])
        else:
            y_lat, y_ctx = _mixer_c(h_lat, h_ctx, w_in_c[j], rpb_c[j], w_out_c[j], need_ctx)

            def ffn(h):
                return _moe(h, w_router[j], w_moe_gate[j], w_moe_up[j], w_moe_down[j])
        x = x + gt1 * y_lat
        x = x + gt2 * ffn(_modulate(_rmsnorm(x, g_norm2[i]), sh2, sc2))
        if need_ctx:
            ctx = ctx + cmod[2] * y_ctx
            ctx = ctx + cmod[5] * ffn(_modulate(_rmsnorm(ctx, g_norm2[i]), cmod[3], cmod[4]))
    return _rmsnorm(x, g_final)
```

```python
import functools

import numpy as np
import jax
import jax.numpy as jnp
from jax import lax
from jax.experimental import pallas as pl
from jax.experimental.pallas import tpu as pltpu

F32 = jnp.float32
BF = jnp.bfloat16

D = 1024
EPS = 1e-6
N_MOD = 6
MOD_ROWS = 8
GRID_W = 64
HD = 64
A_HEADS = 8
A_KV = 2
A_GROUP = A_HEADS // A_KV
ROPE_THETA = 10000.0
B_HEADS = 4
B_DK = 64
B_DV = 128
B_RANK = 16
B_TAU = 16.0
B_CHUNK = 64
C_HEADS = 16
NA_ROWS = 8
NA_COLS = 16
N_EXPERTS = 8
LANES = 128
VMEM_LIMIT = 56 * 2 ** 20

NEG_INF = float("-inf")


def _params(sem):
    return pltpu.CompilerParams(dimension_semantics=sem, vmem_limit_bytes=VMEM_LIMIT)


def _dot(a, b):
    return jnp.dot(a, b, preferred_element_type=F32)


def _dot_nt(a, b):
    return lax.dot_general(a, b, (((1,), (1,)), ((), ())), preferred_element_type=F32)


def _dot_tn(a, b):
    return lax.dot_general(a, b, (((0,), (0,)), ((), ())), preferred_element_type=F32)


def _silu(x):
    return x / (1.0 + jnp.exp(-x))


def _split_dot(lhs_bf_exact, x):
    hi = x.astype(BF)
    lo = (x - hi.astype(F32)).astype(BF)
    return _dot(lhs_bf_exact, hi) + _dot(lhs_bf_exact, lo)


def _norm_mod(x, g, shift, scale):
    ms = jnp.mean(x * x, axis=-1, keepdims=True)
    return (x * lax.rsqrt(ms + EPS) * g) * (1.0 + scale) + shift


def _adaln_body(c_ref, w_ref, b_ref, o_ref):
    s = _silu(c_ref[...])
    o_ref[0] = _dot(s.astype(BF), w_ref[0].astype(BF)) + b_ref[0]


def _adaln(cond, w_mod, b_mod):
    depth, _, n = w_mod.shape
    rows = cond.shape[0]
    tn = 1536
    return pl.pallas_call(
        _adaln_body,
        grid=(depth, n // tn),
        in_specs=[pl.BlockSpec((rows, D), lambda l, j: (0, 0)),
                  pl.BlockSpec((1, D, tn), lambda l, j: (l, 0, j)),
                  pl.BlockSpec((1, 1, tn), lambda l, j: (l, 0, j))],
        out_specs=pl.BlockSpec((1, rows, tn), lambda l, j: (l, 0, j)),
        out_shape=jax.ShapeDtypeStruct((depth, rows, n), F32),
        compiler_params=_params(("arbitrary", "arbitrary")),
        name="adaln",
    )(cond, w_mod, b_mod.reshape(depth, 1, n))


def _qk_norm_rope(p, gqk, bd, ct, st):
    ss = _split_dot_rhs(p * p, bd)
    y = p * lax.rsqrt(ss * (1.0 / HD) + EPS) * gqk
    lane = lax.broadcasted_iota(jnp.int32, ct.shape, 1)
    first = (lane % HD) < (HD // 2)
    outs = []
    for c in range(p.shape[1] // LANES):
        yc = y[:, c * LANES:(c + 1) * LANES]
        partner = jnp.where(first, pltpu.roll(yc, LANES - HD // 2, 1), pltpu.roll(yc, HD // 2, 1))
        outs.append(yc * ct + partner * st)
    return jnp.concatenate(outs, axis=1)


def _split_dot_rhs(x, rhs_bf_exact):
    hi = x.astype(BF)
    lo = (x - hi.astype(F32)).astype(BF)
    return _dot(hi, rhs_bf_exact) + _dot(lo, rhs_bf_exact)


def _proj_body(x_ref, g_ref, m_ref, w_ref, *rest, widths, rope):
    if rope:
        ct_ref, st_ref, gqk_ref, bd_ref = rest[:4]
        o_refs = rest[4:]
    else:
        o_refs = rest
    h = _norm_mod(x_ref[...], g_ref[...], m_ref[0, 0:1, :], m_ref[0, 1:2, :]).astype(BF)
    off = 0
    oi = 0
    for gi, wd in enumerate(widths):
        p = _dot(h, w_ref[:, off:off + wd])
        off += wd
        if rope and gi == 0:
            p = _qk_norm_rope(p, gqk_ref[...], bd_ref[...], ct_ref[...], st_ref[...])
            nq = A_HEADS * HD
            o_refs[0][...] = p[:, :nq].astype(BF)
            o_refs[1][...] = p[:, nq:].astype(BF)
            oi = 2
        else:
            o_refs[oi][...] = p.astype(BF)
            oi += 1


def _proj(x2, g, mpack, mod_of_tile, w, widths, out_widths, tm, rope_args=None):
    rows = x2.shape[0]
    n = w.shape[1]
    in_specs = [pl.BlockSpec((tm, D), lambda i: (i, 0)),
                pl.BlockSpec((1, D), lambda i: (0, 0)),
                pl.BlockSpec((1, MOD_ROWS, D), lambda i: (mod_of_tile(i), 0, 0)),
                pl.BlockSpec((D, n), lambda i: (0, 0))]
    args = [x2, g.reshape(1, D), mpack, w]
    if rope_args is not None:
        ct, st, gqk, bd = rope_args
        nt = ct.shape[0] // tm
        in_specs += [pl.BlockSpec((tm, LANES), lambda i: (i % nt, 0)),
                     pl.BlockSpec((tm, LANES), lambda i: (i % nt, 0)),
                     pl.BlockSpec(gqk.shape, lambda i: (0, 0)),
                     pl.BlockSpec(bd.shape, lambda i: (0, 0))]
        args += [ct, st, gqk, bd]
    return pl.pallas_call(
        functools.partial(_proj_body, widths=widths, rope=rope_args is not None),
        grid=(rows // tm,),
        in_specs=in_specs,
        out_specs=[pl.BlockSpec((tm, ow), lambda i: (i, 0)) for ow in out_widths],
        out_shape=[jax.ShapeDtypeStruct((rows, ow), BF) for ow in out_widths],
        compiler_params=_params(("arbitrary",)),
        name="norm_mod_proj",
    )(*args)


def _gqa_body(q_ref, kc_ref, vc_ref, *rest, with_lat, tq):
    if with_lat:
        kl_ref, vl_ref, o_ref = rest
    else:
        (o_ref,) = rest
    for j in range(A_KV):
        cols = slice(HD * j, HD * (j + 1))
        qg = jnp.concatenate(
            [q_ref[:, HD * (A_GROUP * j + g):HD * (A_GROUP * j + g + 1)] for g in range(A_GROUP)], axis=0)
        s_c = _dot_nt(qg, kc_ref[:, cols])
        m = jnp.max(s_c, axis=1, keepdims=True)
        if with_lat:
            s_l = _dot_nt(qg, kl_ref[:, cols])
            m = jnp.maximum(m, jnp.max(s_l, axis=1, keepdims=True))
        p_c = jnp.exp(s_c - m)
        den = jnp.sum(p_c, axis=1, keepdims=True)
        acc = _dot(p_c.astype(BF), vc_ref[:, cols])
        if with_lat:
            p_l = jnp.exp(s_l - m)
            den = den + jnp.sum(p_l, axis=1, keepdims=True)
            acc = acc + _dot(p_l.astype(BF), vl_ref[:, cols])
        o = acc / den
        for g in range(A_GROUP):
            h = A_GROUP * j + g
            o_ref[:, HD * h:HD * (h + 1)] = o[g * tq:(g + 1) * tq].astype(BF)


def _gqa(q, k_ctx, v_ctx, k_lat, v_lat, bsz, tq):
    sq = q.shape[0] // bsz
    nc = k_ctx.shape[0] // bsz
    nq = sq // tq
    with_lat = k_lat is not None
    in_specs = [pl.BlockSpec((tq, A_HEADS * HD), lambda b, i: (b * nq + i, 0)),
                pl.BlockSpec((nc, A_KV * HD), lambda b, i: (b, 0)),
                pl.BlockSpec((nc, A_KV * HD), lambda b, i: (b, 0))]
    args = [q, k_ctx, v_ctx]
    if with_lat:
        sl = k_lat.shape[0] // bsz
        in_specs += [pl.BlockSpec((sl, A_KV * HD), lambda b, i: (b, 0)),
                     pl.BlockSpec((sl, A_KV * HD), lambda b, i: (b, 0))]
        args += [k_lat, v_lat]
    return pl.pallas_call(
        functools.partial(_gqa_body, with_lat=with_lat, tq=tq),
        grid=(bsz, nq),
        in_specs=in_specs,
        out_specs=pl.BlockSpec((tq, A_HEADS * HD), lambda b, i: (b * nq + i, 0)),
        out_shape=jax.ShapeDtypeStruct(q.shape, BF),
        compiler_params=_params(("arbitrary", "arbitrary")),
        name="gqa_attention",
    )(*args)


GLA_Q = 0
GLA_K = B_HEADS * B_DK
GLA_V = 2 * B_HEADS * B_DK
GLA_R = GLA_V + B_HEADS * B_DV
GLA_W = GLA_R + B_HEADS * B_DV


def _gla_body(gc_ref, gl_ref, zc_ref, zl_ref, w2f_ref, w2b_ref, bf_ref, bb_ref, gg_ref,
              oc_ref, ol_ref, ofc_ref, ofl_ref, st_ref):
    nk = B_HEADS * B_DK
    ch = B_CHUNK
    row = lax.broadcasted_iota(jnp.int32, (ch, ch), 0)
    col = lax.broadcasted_iota(jnp.int32, (ch, ch), 1)
    lane = lax.broadcasted_iota(jnp.int32, (ch, nk), 1)
    head_masks = [(lane // B_DK) == h for h in range(B_HEADS)]

    def chunk_out(g_ref, z_ref, r0, fwd):
        blk = g_ref[pl.ds(r0, ch), :]
        q = blk[:, GLA_Q:GLA_Q + nk].astype(F32)
        k = blk[:, GLA_K:GLA_K + nk].astype(F32)
        v = blk[:, GLA_V:GLA_V + B_HEADS * B_DV]
        z = z_ref[pl.ds(r0, ch), :]
        pre = _dot(z, (w2f_ref if fwd else w2b_ref)[...]) + (bf_ref if fwd else bb_ref)[...]
        la = -(jnp.maximum(-pre, 0.0) + jnp.log(1.0 + jnp.exp(-jnp.abs(pre)))) * (1.0 / B_TAU)
        keep = (row >= col) if fwd else (row <= col)
        bcum = _split_dot(keep.astype(BF), la)
        b_last = bcum[ch - 1:ch, :] if fwd else bcum[0:1, :]
        q_dec = q * jnp.exp(bcum)
        k_intra = (k * jnp.exp(-bcum)).astype(BF)
        k_state = k * jnp.exp(b_last - bcum)
        s_t = st_ref[...]
        s_b = s_t.astype(BF)
        d_s = jnp.zeros_like(s_t)
        outs = []
        for h in range(B_HEADS):
            qm = jnp.where(head_masks[h], q_dec, 0.0).astype(BF)
            att = jnp.where(keep, _dot_nt(qm, k_intra), 0.0)
            vh = v[:, B_DV * h:B_DV * (h + 1)]
            outs.append(_dot(att.astype(BF), vh) + _dot_nt(qm, s_b))
            ksm = jnp.where(head_masks[h], k_state, 0.0).astype(BF)
            d_s = d_s + _dot_tn(vh, ksm)
        st_ref[...] = jnp.exp(b_last) * s_t + d_s
        return jnp.concatenate(outs, axis=1), blk

    def fwd_pass(g_ref, z_ref, of_ref, n):
        def body(i, carry):
            r0 = pl.multiple_of(i * ch, ch)
            o, _ = chunk_out(g_ref, z_ref, r0, True)
            of_ref[pl.ds(r0, ch), :] = o
            return carry
        lax.fori_loop(0, n, body, 0)

    def bwd_pass(g_ref, z_ref, of_ref, o_ref, n):
        def body(i, carry):
            r0 = pl.multiple_of((n - 1 - i) * ch, ch)
            o, blk = chunk_out(g_ref, z_ref, r0, False)
            o = o + of_ref[pl.ds(r0, ch), :]
            parts = []
            for h in range(B_HEADS):
                oh = o[:, B_DV * h:B_DV * (h + 1)]
                ms = jnp.mean(oh * oh, axis=-1, keepdims=True)
                parts.append(oh * lax.rsqrt(ms + EPS))
            y = jnp.concatenate(parts, axis=1) * gg_ref[...]
            r = blk[:, GLA_R:GLA_R + B_HEADS * B_DV].astype(F32)
            o_ref[pl.ds(r0, ch), :] = (y * _silu(r)).astype(BF)
            return carry
        lax.fori_loop(0, n, body, 0)

    nc = gc_ref.shape[0] // ch
    nl = gl_ref.shape[0] // ch
    st_ref[...] = jnp.zeros_like(st_ref)
    fwd_pass(gc_ref, zc_ref, ofc_ref, nc)
    fwd_pass(gl_ref, zl_ref, ofl_ref, nl)
    st_ref[...] = jnp.zeros_like(st_ref)
    bwd_pass(gc_ref, zc_ref, ofc_ref, oc_ref, nc)
    bwd_pass(gl_ref, zl_ref, ofl_ref, ol_ref, nl)


def _gla(g_ctx, g_lat, z_ctx, z_lat, w2f, w2b, b_f, b_b, gg, bsz):
    nc = g_ctx.shape[0] // bsz
    sl = g_lat.shape[0] // bsz
    nk = B_HEADS * B_DK
    nv = B_HEADS * B_DV
    full = lambda shape: pl.BlockSpec(shape, lambda b: (0, 0))
    return pl.pallas_call(
        _gla_body,
        grid=(bsz,),
        in_specs=[pl.BlockSpec((nc, GLA_W), lambda b: (b, 0)),
                  pl.BlockSpec((sl, GLA_W), lambda b: (b, 0)),
                  pl.BlockSpec((nc, LANES), lambda b: (b, 0)),
                  pl.BlockSpec((sl, LANES), lambda b: (b, 0)),
                  full((LANES, nk)), full((LANES, nk)), full((1, nk)), full((1, nk)), full((1, nv))],
        out_specs=[pl.BlockSpec((nc, nv), lambda b: (b, 0)),
                   pl.BlockSpec((sl, nv), lambda b: (b, 0))],
        out_shape=[jax.ShapeDtypeStruct((g_ctx.shape[0], nv), BF),
                   jax.ShapeDtypeStruct((g_lat.shape[0], nv), BF)],
        scratch_shapes=[pltpu.VMEM((nc, nv), F32), pltpu.VMEM((sl, nv), F32),
                        pltpu.VMEM((B_DV, nk), F32)],
        compiler_params=_params(("arbitrary",)),
        name="gla_bidir",
    )(g_ctx, g_lat, z_ctx, z_lat, w2f, w2b, b_f, b_b, gg)


def _outproj_body(x_ref, m_ref, *rest, n_in):
    o_ref = rest[-1]
    acc = _dot(rest[0][...], rest[1][...])
    for t in range(1, n_in):
        acc = acc + _dot(rest[2 * t][...], rest[2 * t + 1][...])
    o_ref[...] = x_ref[...] + m_ref[0, 2:3, :] * acc


def _outproj(x2, mpack, mod_of_tile, pairs, tm):
    rows = x2.shape[0]
    in_specs = [pl.BlockSpec((tm, D), lambda i: (i, 0)),
                pl.BlockSpec((1, MOD_ROWS, D), lambda i: (mod_of_tile(i), 0, 0))]
    args = [x2, mpack]
    for a, w in pairs:
        in_specs += [pl.BlockSpec((tm, a.shape[1]), lambda i: (i, 0)),
                     pl.BlockSpec(w.shape, lambda i: (0, 0))]
        args += [a, w]
    return pl.pallas_call(
        functools.partial(_outproj_body, n_in=len(pairs)),
        grid=(rows // tm,),
        in_specs=in_specs,
        out_specs=pl.BlockSpec((tm, D), lambda i: (i, 0)),
        out_shape=jax.ShapeDtypeStruct((rows, D), F32),
        compiler_params=_params(("arbitrary",)),
        name="outproj_residual",
    )(*args)


def _ffn_body(x_ref, g_ref, m_ref, wg_ref, wu_ref, wd_ref, o_ref, h_ref, acc_ref):
    j = pl.program_id(1)

    @pl.when(j == 0)
    def _():
        h_ref[...] = _norm_mod(x_ref[...], g_ref[...], m_ref[0, 3:4, :], m_ref[0, 4:5, :]).astype(BF)
        acc_ref[...] = jnp.zeros_like(acc_ref)

    h = h_ref[...]
    a = (_silu(_dot(h, wg_ref[...])) * _dot(h, wu_ref[...])).astype(BF)
    acc_ref[...] += _dot(a, wd_ref[...])

    @pl.when(j == pl.num_programs(1) - 1)
    def _():
        o_ref[...] = x_ref[...] + m_ref[0, 5:6, :] * acc_ref[...]


def _ffn(x2, g, mpack, mod_of_tile, wg, wu, wd, tm, tf):
    rows = x2.shape[0]
    ff = wg.shape[1]
    return pl.pallas_call(
        _ffn_body,
        grid=(rows // tm, ff // tf),
        in_specs=[pl.BlockSpec((tm, D), lambda i, j: (i, 0)),
                  pl.BlockSpec((1, D), lambda i, j: (0, 0)),
                  pl.BlockSpec((1, MOD_ROWS, D), lambda i, j: (mod_of_tile(i), 0, 0)),
                  pl.BlockSpec((D, tf), lambda i, j: (0, j)),
                  pl.BlockSpec((D, tf), lambda i, j: (0, j)),
                  pl.BlockSpec((tf, D), lambda i, j: (j, 0))],
        out_specs=pl.BlockSpec((tm, D), lambda i, j: (i, 0)),
        out_shape=jax.ShapeDtypeStruct((rows, D), F32),
        scratch_shapes=[pltpu.VMEM((tm, D), BF), pltpu.VMEM((tm, D), F32)],
        compiler_params=_params(("arbitrary", "arbitrary")),
        name="ffn_swiglu",
    )(x2, g.reshape(1, D), mpack, wg, wu, wd)


NA_QROWS = 4
NA_KROWS = 12
NA_TQ = NA_QROWS * GRID_W
NA_HG = LANES // HD


def _na_body(q_ref, k0_ref, k1_ref, k2_ref, v0_ref, v1_ref, v2_ref, kc_ref, vc_ref, b_ref, o_ref):
    outs = []
    for hh in range(NA_HG):
        cols = slice(HD * hh, HD * (hh + 1))
        q = q_ref[:, cols]
        s_loc = jnp.concatenate([_dot_nt(q, kr[:, cols]) for kr in (k0_ref, k1_ref, k2_ref)], axis=1)
        s_loc = s_loc + b_ref[0, hh]
        s_ctx = _dot_nt(q, kc_ref[:, cols])
        m = jnp.maximum(jnp.max(s_loc, axis=1, keepdims=True), jnp.max(s_ctx, axis=1, keepdims=True))
        p_loc = jnp.exp(s_loc - m)
        p_ctx = jnp.exp(s_ctx - m)
        den = jnp.sum(p_loc, axis=1, keepdims=True) + jnp.sum(p_ctx, axis=1, keepdims=True)
        acc = _dot(p_ctx.astype(BF), vc_ref[:, cols])
        for t, vr in enumerate((v0_ref, v1_ref, v2_ref)):
            acc = acc + _dot(p_loc[:, t * NA_TQ:(t + 1) * NA_TQ].astype(BF), vr[:, cols])
        outs.append(acc / den)
    o_ref[...] = jnp.concatenate(outs, axis=1).astype(BF)


def _na_bias_tables(rows):
    nblk = rows // NA_QROWS
    kinds = (0, min(1, nblk - 1), nblk - 1)
    dr = np.zeros((3, NA_TQ, NA_KROWS * GRID_W), np.int32)
    dc = np.zeros_like(dr)
    ok = np.zeros(dr.shape, bool)
    qr, qc = np.divmod(np.arange(NA_TQ), GRID_W)
    kr, kc = np.divmod(np.arange(NA_KROWS * GRID_W), GRID_W)
    for t, j in enumerate(kinds):
        r = NA_QROWS * j + qr
        ws = NA_QROWS * int(np.clip(j - 1, 0, nblk - 3))
        ka = ws + kr
        rs = np.clip(r - NA_ROWS // 2, 0, rows - NA_ROWS)
        row_ok = (ka[None, :] >= rs[:, None]) & (ka[None, :] < rs[:, None] + NA_ROWS)
        cs = np.clip(qc - NA_COLS // 2, 0, GRID_W - NA_COLS)
        col_ok = (kc[None, :] >= cs[:, None]) & (kc[None, :] < cs[:, None] + NA_COLS)
        dr[t] = np.clip(ka[None, :] - r[:, None] + NA_ROWS - 1, 0, 2 * NA_ROWS - 2)
        dc[t] = np.clip(kc[None, :] - qc[:, None] + NA_COLS - 1, 0, 2 * NA_COLS - 2)
        ok[t] = row_ok & col_ok
    return dr, dc, ok


def _natten(q, k, v, k_ctx, v_ctx, bias, bsz):
    s = q.shape[0] // bsz
    nc = k_ctx.shape[0] // bsz
    nblk = s // NA_TQ
    ngrp = C_HEADS // NA_HG

    def kv_spec(t):
        return pl.BlockSpec((NA_TQ, LANES),
                            lambda j, g, b: (b * nblk + jnp.clip(j - 1, 0, nblk - 3) + t, g))

    def kind(j):
        return jnp.where(j == 0, 0, jnp.where(j == nblk - 1, 2, 1))

    return pl.pallas_call(
        _na_body,
        grid=(nblk, ngrp, bsz),
        in_specs=[pl.BlockSpec((NA_TQ, LANES), lambda j, g, b: (b * nblk + j, g)),
                  kv_spec(0), kv_spec(1), kv_spec(2), kv_spec(0), kv_spec(1), kv_spec(2),
                  pl.BlockSpec((nc, LANES), lambda j, g, b: (b, g)),
                  pl.BlockSpec((nc, LANES), lambda j, g, b: (b, g)),
                  pl.BlockSpec((1, NA_HG, NA_TQ, NA_KROWS * GRID_W), lambda j, g, b: (kind(j), g, 0, 0))],
        out_specs=pl.BlockSpec((NA_TQ, LANES), lambda j, g, b: (b * nblk + j, g)),
        out_shape=jax.ShapeDtypeStruct(q.shape, BF),
        compiler_params=_params(("arbitrary", "arbitrary", "arbitrary")),
        name="natten",
    )(q, k, k, k, v, v, v, k_ctx, v_ctx, bias)


def _router(h, wr_ref):
    logits = jnp.dot(h, wr_ref[...], preferred_element_type=F32, precision=lax.Precision.HIGHEST)
    lane = lax.broadcasted_iota(jnp.int32, logits.shape, 1)
    lg = jnp.where(lane < N_EXPERTS, logits, NEG_INF)
    m1 = jnp.max(lg, axis=1, keepdims=True)
    i1 = jnp.min(jnp.where(lg == m1, lane, LANES), axis=1, keepdims=True)
    lg2 = jnp.where(lane == i1, NEG_INF, lg)
    m2 = jnp.max(lg2, axis=1, keepdims=True)
    i2 = jnp.min(jnp.where(lg2 == m2, lane, LANES), axis=1, keepdims=True)
    e = jnp.exp(m2 - m1)
    g1 = 1.0 / (1.0 + e)
    g2 = e / (1.0 + e)
    return jnp.where(lane == i1, g1, 0.0) + jnp.where(lane == i2, g2, 0.0)


def _moe_body(x_ref, g_ref, m_ref, wr_ref, wg_ref, wu_ref, wd_ref, gf_ref, o_ref, h_ref, acc_ref, comb_ref):
    e = pl.program_id(1)
    j = pl.program_id(2)

    @pl.when((e == 0) & (j == 0))
    def _():
        h = _norm_mod(x_ref[...], g_ref[...], m_ref[0, 3:4, :], m_ref[0, 4:5, :])
        h_ref[...] = h.astype(BF)
        comb_ref[...] = _router(h, wr_ref)
        acc_ref[...] = jnp.zeros_like(acc_ref)

    h = h_ref[...]
    lane = lax.broadcasted_iota(jnp.int32, comb_ref.shape, 1)
    ce = jnp.sum(jnp.where(lane == e, comb_ref[...], 0.0), axis=1, keepdims=True)
    a = (_silu(_dot(h, wg_ref[0])) * _dot(h, wu_ref[0]) * ce).astype(BF)
    acc_ref[...] += _dot(a, wd_ref[0])

    @pl.when((e == pl.num_programs(1) - 1) & (j == pl.num_programs(2) - 1))
    def _():
        y = x_ref[...] + m_ref[0, 5:6, :] * acc_ref[...]
        ms = jnp.mean(y * y, axis=-1, keepdims=True)
        o_ref[...] = y * lax.rsqrt(ms + EPS) * gf_ref[...]


def _moe(x2, g, mpack, mod_of_tile, wr, wg, wu, wd, g_final, tm, tf):
    rows = x2.shape[0]
    ne, _, ff = wg.shape
    return pl.pallas_call(
        _moe_body,
        grid=(rows // tm, ne, ff // tf),
        in_specs=[pl.BlockSpec((tm, D), lambda i, e, j: (i, 0)),
                  pl.BlockSpec((1, D), lambda i, e, j: (0, 0)),
                  pl.BlockSpec((1, MOD_ROWS, D), lambda i, e, j: (mod_of_tile(i), 0, 0)),
                  pl.BlockSpec((D, LANES), lambda i, e, j: (0, 0)),
                  pl.BlockSpec((1, D, tf), lambda i, e, j: (e, 0, j)),
                  pl.BlockSpec((1, D, tf), lambda i, e, j: (e, 0, j)),
                  pl.BlockSpec((1, tf, D), lambda i, e, j: (e, j, 0)),
                  pl.BlockSpec((1, D), lambda i, e, j: (0, 0))],
        out_specs=pl.BlockSpec((tm, D), lambda i, e, j: (i, 0)),
        out_shape=jax.ShapeDtypeStruct((rows, D), F32),
        scratch_shapes=[pltpu.VMEM((tm, D), BF), pltpu.VMEM((tm, D), F32), pltpu.VMEM((tm, LANES), F32)],
        compiler_params=_params(("arbitrary", "arbitrary", "arbitrary")),
        name="moe_dense_sweep",
    )(x2, g.reshape(1, D), mpack, wr, wg, wu, wd, g_final.reshape(1, D))


def _rope_tables(seq_len):
    t = np.arange(seq_len)
    row = (t // GRID_W).astype(np.float32)
    col = (t % GRID_W).astype(np.float32)
    n_axis = HD // 4
    inv = jnp.power(ROPE_THETA, -jnp.arange(n_axis, dtype=F32) / n_axis)
    ang = jnp.concatenate([jnp.asarray(row)[:, None] * inv, jnp.asarray(col)[:, None] * inv], axis=-1)
    cos, sin = jnp.cos(ang), jnp.sin(ang)
    ct = jnp.tile(jnp.concatenate([cos, cos], axis=-1), (1, LANES // HD))
    st = jnp.tile(jnp.concatenate([-sin, sin], axis=-1), (1, LANES // HD))
    return ct, st


def kernel(x, c, ctx, c_ctx, w_mod, b_mod, g_norm1, g_norm2, w_in_ab, g_q, g_k, w_a2_f, b_a_f, w_a2_b,
           b_a_b, g_gla, w_out_ab, w_ff_gate, w_ff_up, w_ff_down, w_in_c, rpb_c, w_out_c, w_router,
           w_moe_gate, w_moe_up, w_moe_down, g_final):
    bsz, seq, _ = x.shape
    nc = ctx.shape[1]
    depth = w_mod.shape[0]
    assert depth == 2 and seq % GRID_W == 0

    cond_rows = -(-(bsz + 1) // 8) * 8
    cond = jnp.zeros((cond_rows, D), F32).at[:bsz].set(c).at[bsz].set(c_ctx)
    mods = _adaln(cond, w_mod, b_mod).reshape(depth, cond_rows, N_MOD, D)
    mods = jnp.pad(mods, ((0, 0), (0, 0), (0, MOD_ROWS - N_MOD), (0, 0)))

    xl = x.reshape(bsz * seq, D)
    xc = ctx.reshape(bsz * nc, D)
    tm_l = min(1024, seq)
    tm_c = min(1024, bsz * nc)
    tpb = seq // tm_l
    lat_mod = lambda i: i // tpb
    ctx_mod = lambda i: bsz

    mp = mods[0]
    w_in = w_in_ab[0]
    perm = np.concatenate([np.arange(0, HD, 2), np.arange(1, HD, 2)])
    nq, nkv = A_HEADS * HD, A_KV * HD
    o_q, o_k, o_v = 0, nq, nq + nkv
    o_bq = o_v + nkv
    o_bk = o_bq + B_HEADS * B_DK
    o_bv = o_bk + B_HEADS * B_DK
    o_br = o_bv + B_HEADS * B_DV
    o_zf = o_br + B_HEADS * B_DV
    o_end = o_zf + 2 * B_RANK
    wq = w_in[:, o_q:o_k].reshape(D, A_HEADS, HD)[:, :, perm].reshape(D, nq)
    wk = w_in[:, o_k:o_v].reshape(D, A_KV, HD)[:, :, perm].reshape(D, nkv)
    wz = jnp.pad(w_in[:, o_zf:o_end], ((0, 0), (0, LANES - 2 * B_RANK)))
    w0 = jnp.concatenate([wq, wk, w_in[:, o_v:o_bq], w_in[:, o_bq:o_bk] * (B_DK ** -0.5),
                          w_in[:, o_bk:o_zf], wz], axis=1).astype(BF)
    widths0 = (nq + nkv, nkv, GLA_W, LANES)
    outw0 = (nq, nkv, nkv, GLA_W, LANES)
    gqk = jnp.concatenate([jnp.tile(g_q[0][perm] * (HD ** -0.5), A_HEADS),
                           jnp.tile(g_k[0][perm], A_KV)]).reshape(1, nq + nkv)
    nh = A_HEADS + A_KV
    bd = jnp.asarray(np.kron(np.eye(nh), np.ones((HD, HD))), BF)
    ct, st = _rope_tables(seq)
    ct_c = jnp.ones((tm_c, LANES), F32)
    st_c = jnp.zeros((tm_c, LANES), F32)

    ql, kl, vl, gl, zl = _proj(xl, g_norm1[0], mp, lat_mod, w0, widths0, outw0, tm_l, (ct, st, gqk, bd))
    qc, kc, vc, gc, zc = _proj(xc, g_norm1[0], mp, ctx_mod, w0, widths0, outw0, tm_c, (ct_c, st_c, gqk, bd))

    oa_l = _gqa(ql, kc, vc, kl, vl, bsz, 128)
    oa_c = _gqa(qc, kc, vc, None, None, bsz, min(128, nc))

    w2f = jnp.zeros((LANES, B_HEADS * B_DK), F32).at[:B_RANK].set(w_a2_f[0]).astype(BF)
    w2b = jnp.zeros((LANES, B_HEADS * B_DK), F32).at[B_RANK:2 * B_RANK].set(w_a2_b[0]).astype(BF)
    ob_c, ob_l = _gla(gc, gl, zc, zl, w2f, w2b, b_a_f[0].reshape(1, -1), b_a_b[0].reshape(1, -1),
                      jnp.tile(g_gla[0], B_HEADS).reshape(1, -1), bsz)

    wo = w_out_ab[0].astype(BF)
    xl = _outproj(xl, mp, lat_mod, [(oa_l, wo[:nq]), (ob_l, wo[nq:])], tm_l)
    xc = _outproj(xc, mp, ctx_mod, [(oa_c, wo[:nq]), (ob_c, wo[nq:])], tm_c)

    wg, wu, wd = w_ff_gate[0].astype(BF), w_ff_up[0].astype(BF), w_ff_down[0].astype(BF)
    xl = _ffn(xl, g_norm2[0], mp, lat_mod, wg, wu, wd, tm_l, 512)
    xc = _ffn(xc, g_norm2[0], mp, ctx_mod, wg, wu, wd, tm_c, 512)

    mp = mods[1]
    cw = C_HEADS * HD
    w1 = jnp.concatenate([w_in_c[0][:, :cw] * (HD ** -0.5), w_in_c[0][:, cw:]], axis=1).astype(BF)
    q1, k1, v1 = _proj(xl, g_norm1[1], mp, lat_mod, w1, (cw, cw, cw), (cw, cw, cw), tm_l)
    k1c, v1c = _proj(xc, g_norm1[1], mp, ctx_mod, w1[:, cw:], (cw, cw), (cw, cw), tm_c)

    dr, dc, ok = _na_bias_tables(seq // GRID_W)
    bias = jnp.where(ok[None], rpb_c[0].astype(F32)[:, dr, dc], NEG_INF).transpose(1, 0, 2, 3)
    o1 = _natten(q1, k1, v1, k1c, v1c, bias, bsz)
    xl = _outproj(xl, mp, lat_mod, [(o1, w_out_c[0].astype(BF))], tm_l)

    wr = jnp.pad(w_router[0], ((0, 0), (0, LANES - N_EXPERTS)))
    out = _moe(xl, g_norm2[1], mp, lat_mod, wr, w_moe_gate[0].astype(BF), w_moe_up[0].astype(BF),
               w_moe_down[0].astype(BF), g_final, tm_l, 512)
    return out.reshape(bsz, seq, D)
```

```python
import functools

import numpy as np
import jax
import jax.numpy as jnp
from jax import lax
from jax.experimental import pallas as pl
from jax.experimental.pallas import tpu as pltpu

F32 = jnp.float32
BF = jnp.bfloat16

D = 1024
EPS = 1e-6
N_MOD = 6
MOD_ROWS = 8
GRID_W = 64
HD = 64
A_HEADS = 8
A_KV = 2
A_GROUP = A_HEADS // A_KV
ROPE_THETA = 10000.0
B_HEADS = 4
B_DK = 64
B_DV = 128
B_RANK = 16
B_TAU = 16.0
B_CHUNK = 64
C_HEADS = 16
NA_ROWS = 8
NA_COLS = 16
N_EXPERTS = 8
LANES = 128
VMEM_LIMIT = 56 * 2 ** 20

NEG_INF = float("-inf")


def _params(sem):
    return pltpu.CompilerParams(dimension_semantics=sem, vmem_limit_bytes=VMEM_LIMIT)


def _dot(a, b):
    return jnp.dot(a, b, preferred_element_type=F32)


def _dot_nt(a, b):
    return lax.dot_general(a, b, (((1,), (1,)), ((), ())), preferred_element_type=F32)


def _dot_tn(a, b):
    return lax.dot_general(a, b, (((0,), (0,)), ((), ())), preferred_element_type=F32)


def _silu(x):
    return x / (1.0 + jnp.exp(-x))


def _split_dot(lhs_bf_exact, x):
    hi = x.astype(BF)
    lo = (x - hi.astype(F32)).astype(BF)
    return _dot(lhs_bf_exact, hi) + _dot(lhs_bf_exact, lo)


def _norm_mod(x, g, shift, scale):
    ms = jnp.mean(x * x, axis=-1, keepdims=True)
    return (x * lax.rsqrt(ms + EPS) * g) * (1.0 + scale) + shift


def _adaln_body(c_ref, w_ref, b_ref, o_ref):
    s = _silu(c_ref[...])
    o_ref[0] = _dot(s.astype(BF), w_ref[0].astype(BF)) + b_ref[0]


def _adaln(cond, w_mod, b_mod):
    depth, _, n = w_mod.shape
    rows = cond.shape[0]
    tn = 1536
    return pl.pallas_call(
        _adaln_body,
        grid=(depth, n // tn),
        in_specs=[pl.BlockSpec((rows, D), lambda l, j: (0, 0)),
                  pl.BlockSpec((1, D, tn), lambda l, j: (l, 0, j)),
                  pl.BlockSpec((1, 1, tn), lambda l, j: (l, 0, j))],
        out_specs=pl.BlockSpec((1, rows, tn), lambda l, j: (l, 0, j)),
        out_shape=jax.ShapeDtypeStruct((depth, rows, n), F32),
        compiler_params=_params(("arbitrary", "arbitrary")),
        name="adaln",
    )(cond, w_mod, b_mod.reshape(depth, 1, n))


def _qk_norm_rope(p, gqk, bd, ct, st):
    ss = _split_dot_rhs(p * p, bd)
    y = p * lax.rsqrt(ss * (1.0 / HD) + EPS) * gqk
    lane = lax.broadcasted_iota(jnp.int32, ct.shape, 1)
    first = (lane % HD) < (HD // 2)
    outs = []
    for c in range(p.shape[1] // LANES):
        yc = y[:, c * LANES:(c + 1) * LANES]
        partner = jnp.where(first, pltpu.roll(yc, LANES - HD // 2, 1), pltpu.roll(yc, HD // 2, 1))
        outs.append(yc * ct + partner * st)
    return jnp.concatenate(outs, axis=1)


def _split_dot_rhs(x, rhs_bf_exact):
    hi = x.astype(BF)
    lo = (x - hi.astype(F32)).astype(BF)
    return _dot(hi, rhs_bf_exact) + _dot(lo, rhs_bf_exact)


def _proj_body(x_ref, g_ref, m_ref, w_ref, *rest, widths, rope):
    if rope:
        ct_ref, st_ref, gqk_ref, bd_ref = rest[:4]
        o_refs = rest[4:]
    else:
        o_refs = rest
    h = _norm_mod(x_ref[...], g_ref[...], m_ref[0, 0:1, :], m_ref[0, 1:2, :]).astype(BF)
    off = 0
    oi = 0
    for gi, wd in enumerate(widths):
        p = _dot(h, w_ref[:, off:off + wd])
        off += wd
        if rope and gi == 0:
            p = _qk_norm_rope(p, gqk_ref[...], bd_ref[...], ct_ref[...], st_ref[...])
            nq = A_HEADS * HD
            o_refs[0][...] = p[:, :nq].astype(BF)
            o_refs[1][...] = p[:, nq:].astype(BF)
            oi = 2
        else:
            o_refs[oi][...] = p.astype(BF)
            oi += 1


def _proj(x2, g, mpack, mod_of_tile, w, widths, out_widths, tm, rope_args=None):
    rows = x2.shape[0]
    n = w.shape[1]
    in_specs = [pl.BlockSpec((tm, D), lambda i: (i, 0)),
                pl.BlockSpec((1, D), lambda i: (0, 0)),
                pl.BlockSpec((1, MOD_ROWS, D), lambda i: (mod_of_tile(i), 0, 0)),
                pl.BlockSpec((D, n), lambda i: (0, 0))]
    args = [x2, g.reshape(1, D), mpack, w]
    if rope_args is not None:
        ct, st, gqk, bd = rope_args
        nt = ct.shape[0] // tm
        in_specs += [pl.BlockSpec((tm, LANES), lambda i: (i % nt, 0)),
                     pl.BlockSpec((tm, LANES), lambda i: (i % nt, 0)),
                     pl.BlockSpec(gqk.shape, lambda i: (0, 0)),
                     pl.BlockSpec(bd.shape, lambda i: (0, 0))]
        args += [ct, st, gqk, bd]
    return pl.pallas_call(
        functools.partial(_proj_body, widths=widths, rope=rope_args is not None),
        grid=(rows // tm,),
        in_specs=in_specs,
        out_specs=[pl.BlockSpec((tm, ow), lambda i: (i, 0)) for ow in out_widths],
        out_shape=[jax.ShapeDtypeStruct((rows, ow), BF) for ow in out_widths],
        compiler_params=_params(("arbitrary",)),
        name="norm_mod_proj",
    )(*args)


def _gqa_body(q_ref, kc_ref, vc_ref, *rest, with_lat, tq):
    if with_lat:
        kl_ref, vl_ref, o_ref = rest
    else:
        (o_ref,) = rest
    for j in range(A_KV):
        cols = slice(HD * j, HD * (j + 1))
        qg = jnp.concatenate(
            [q_ref[:, HD * (A_GROUP * j + g):HD * (A_GROUP * j + g + 1)] for g in range(A_GROUP)], axis=0)
        s_c = _dot_nt(qg, kc_ref[:, cols])
        m = jnp.max(s_c, axis=1, keepdims=True)
        if with_lat:
            s_l = _dot_nt(qg, kl_ref[:, cols])
            m = jnp.maximum(m, jnp.max(s_l, axis=1, keepdims=True))
        p_c = jnp.exp(s_c - m)
        den = jnp.sum(p_c, axis=1, keepdims=True)
        acc = _dot(p_c.astype(BF), vc_ref[:, cols])
        if with_lat:
            p_l = jnp.exp(s_l - m)
            den = den + jnp.sum(p_l, axis=1, keepdims=True)
            acc = acc + _dot(p_l.astype(BF), vl_ref[:, cols])
        o = acc / den
        for g in range(A_GROUP):
            h = A_GROUP * j + g
            o_ref[:, HD * h:HD * (h + 1)] = o[g * tq:(g + 1) * tq].astype(BF)


def _gqa(q, k_ctx, v_ctx, k_lat, v_lat, bsz, tq):
    sq = q.shape[0] // bsz
    nc = k_ctx.shape[0] // bsz
    nq = sq // tq
    with_lat = k_lat is not None
    in_specs = [pl.BlockSpec((tq, A_HEADS * HD), lambda b, i: (b * nq + i, 0)),
                pl.BlockSpec((nc, A_KV * HD), lambda b, i: (b, 0)),
                pl.BlockSpec((nc, A_KV * HD), lambda b, i: (b, 0))]
    args = [q, k_ctx, v_ctx]
    if with_lat:
        sl = k_lat.shape[0] // bsz
        in_specs += [pl.BlockSpec((sl, A_KV * HD), lambda b, i: (b, 0)),
                     pl.BlockSpec((sl, A_KV * HD), lambda b, i: (b, 0))]
        args += [k_lat, v_lat]
    return pl.pallas_call(
        functools.partial(_gqa_body, with_lat=with_lat, tq=tq),
        grid=(bsz, nq),
        in_specs=in_specs,
        out_specs=pl.BlockSpec((tq, A_HEADS * HD), lambda b, i: (b * nq + i, 0)),
        out_shape=jax.ShapeDtypeStruct(q.shape, BF),
        compiler_params=_params(("arbitrary", "arbitrary")),
        name="gqa_attention",
    )(*args)


GLA_Q = 0
GLA_K = B_HEADS * B_DK
GLA_V = 2 * B_HEADS * B_DK
GLA_R = GLA_V + B_HEADS * B_DV
GLA_W = GLA_R + B_HEADS * B_DV


def _gla_body(gc_ref, gl_ref, zc_ref, zl_ref, w2f_ref, w2b_ref, bf_ref, bb_ref, gg_ref,
              oc_ref, ol_ref, ofc_ref, ofl_ref, st_ref):
    nk = B_HEADS * B_DK
    ch = B_CHUNK
    row = lax.broadcasted_iota(jnp.int32, (ch, ch), 0)
    col = lax.broadcasted_iota(jnp.int32, (ch, ch), 1)
    lane = lax.broadcasted_iota(jnp.int32, (ch, nk), 1)
    head_masks = [(lane // B_DK) == h for h in range(B_HEADS)]

    def chunk_out(g_ref, z_ref, r0, fwd):
        blk = g_ref[pl.ds(r0, ch), :]
        q = blk[:, GLA_Q:GLA_Q + nk].astype(F32)
        k = blk[:, GLA_K:GLA_K + nk].astype(F32)
        v = blk[:, GLA_V:GLA_V + B_HEADS * B_DV]
        z = z_ref[pl.ds(r0, ch), :]
        pre = _dot(z, (w2f_ref if fwd else w2b_ref)[...]) + (bf_ref if fwd else bb_ref)[...]
        la = -(jnp.maximum(-pre, 0.0) + jnp.log(1.0 + jnp.exp(-jnp.abs(pre)))) * (1.0 / B_TAU)
        keep = (row >= col) if fwd else (row <= col)
        bcum = _split_dot(keep.astype(BF), la)
        b_last = bcum[ch - 1:ch, :] if fwd else bcum[0:1, :]
        q_dec = q * jnp.exp(bcum)
        k_intra = (k * jnp.exp(-bcum)).astype(BF)
        k_state = k * jnp.exp(b_last - bcum)
        s_t = st_ref[...]
        s_b = s_t.astype(BF)
        d_s = jnp.zeros_like(s_t)
        outs = []
        for h in range(B_HEADS):
            qm = jnp.where(head_masks[h], q_dec, 0.0).astype(BF)
            att = jnp.where(keep, _dot_nt(qm, k_intra), 0.0)
            vh = v[:, B_DV * h:B_DV * (h + 1)]
            outs.append(_dot(att.astype(BF), vh) + _dot_nt(qm, s_b))
            ksm = jnp.where(head_masks[h], k_state, 0.0).astype(BF)
            d_s = d_s + _dot_tn(vh, ksm)
        st_ref[...] = jnp.exp(b_last) * s_t + d_s
        return jnp.concatenate(outs, axis=1), blk

    def fwd_pass(g_ref, z_ref, of_ref, n):
        def body(i, carry):
            r0 = pl.multiple_of(i * ch, ch)
            o, _ = chunk_out(g_ref, z_ref, r0, True)
            of_ref[pl.ds(r0, ch), :] = o
            return carry
        lax.fori_loop(0, n, body, 0)

    def bwd_pass(g_ref, z_ref, of_ref, o_ref, n):
        def body(i, carry):
            r0 = pl.multiple_of((n - 1 - i) * ch, ch)
            o, blk = chunk_out(g_ref, z_ref, r0, False)
            o = o + of_ref[pl.ds(r0, ch), :]
            parts = []
            for h in range(B_HEADS):
                oh = o[:, B_DV * h:B_DV * (h + 1)]
                ms = jnp.mean(oh * oh, axis=-1, keepdims=True)
                parts.append(oh * lax.rsqrt(ms + EPS))
            y = jnp.concatenate(parts, axis=1) * gg_ref[...]
            r = blk[:, GLA_R:GLA_R + B_HEADS * B_DV].astype(F32)
            o_ref[pl.ds(r0, ch), :] = (y * _silu(r)).astype(BF)
            return carry
        lax.fori_loop(0, n, body, 0)

    nc = gc_ref.shape[0] // ch
    nl = gl_ref.shape[0] // ch
    st_ref[...] = jnp.zeros_like(st_ref)
    fwd_pass(gc_ref, zc_ref, ofc_ref, nc)
    fwd_pass(gl_ref, zl_ref, ofl_ref, nl)
    st_ref[...] = jnp.zeros_like(st_ref)
    bwd_pass(gc_ref, zc_ref, ofc_ref, oc_ref, nc)
    bwd_pass(gl_ref, zl_ref, ofl_ref, ol_ref, nl)


def _gla(g_ctx, g_lat, z_ctx, z_lat, w2f, w2b, b_f, b_b, gg, bsz):
    nc = g_ctx.shape[0] // bsz
    sl = g_lat.shape[0] // bsz
    nk = B_HEADS * B_DK
    nv = B_HEADS * B_DV
    full = lambda shape: pl.BlockSpec(shape, lambda b: (0, 0))
    return pl.pallas_call(
        _gla_body,
        grid=(bsz,),
        in_specs=[pl.BlockSpec((nc, GLA_W), lambda b: (b, 0)),
                  pl.BlockSpec((sl, GLA_W), lambda b: (b, 0)),
                  pl.BlockSpec((nc, LANES), lambda b: (b, 0)),
                  pl.BlockSpec((sl, LANES), lambda b: (b, 0)),
                  full((LANES, nk)), full((LANES, nk)), full((1, nk)), full((1, nk)), full((1, nv))],
        out_specs=[pl.BlockSpec((nc, nv), lambda b: (b, 0)),
                   pl.BlockSpec((sl, nv), lambda b: (b, 0))],
        out_shape=[jax.ShapeDtypeStruct((g_ctx.shape[0], nv), BF),
                   jax.ShapeDtypeStruct((g_lat.shape[0], nv), BF)],
        scratch_shapes=[pltpu.VMEM((nc, nv), F32), pltpu.VMEM((sl, nv), F32),
                        pltpu.VMEM((B_DV, nk), F32)],
        compiler_params=_params(("arbitrary",)),
        name="gla_bidir",
    )(g_ctx, g_lat, z_ctx, z_lat, w2f, w2b, b_f, b_b, gg)


def _outproj_body(x_ref, m_ref, *rest, n_in):
    o_ref = rest[-1]
    acc = _dot(rest[0][...], rest[1][...])
    for t in range(1, n_in):
        acc = acc + _dot(rest[2 * t][...], rest[2 * t + 1][...])
    o_ref[...] = x_ref[...] + m_ref[0, 2:3, :] * acc


def _outproj(x2, mpack, mod_of_tile, pairs, tm):
    rows = x2.shape[0]
    in_specs = [pl.BlockSpec((tm, D), lambda i: (i, 0)),
                pl.BlockSpec((1, MOD_ROWS, D), lambda i: (mod_of_tile(i), 0, 0))]
    args = [x2, mpack]
    for a, w in pairs:
        in_specs += [pl.BlockSpec((tm, a.shape[1]), lambda i: (i, 0)),
                     pl.BlockSpec(w.shape, lambda i: (0, 0))]
        args += [a, w]
    return pl.pallas_call(
        functools.partial(_outproj_body, n_in=len(pairs)),
        grid=(rows // tm,),
        in_specs=in_specs,
        out_specs=pl.BlockSpec((tm, D), lambda i: (i, 0)),
        out_shape=jax.ShapeDtypeStruct((rows, D), F32),
        compiler_params=_params(("arbitrary",)),
        name="outproj_residual",
    )(*args)


def _ffn_body(x_ref, g_ref, m_ref, wg_ref, wu_ref, wd_ref, o_ref, h_ref, acc_ref):
    j = pl.program_id(1)

    @pl.when(j == 0)
    def _():
        h_ref[...] = _norm_mod(x_ref[...], g_ref[...], m_ref[0, 3:4, :], m_ref[0, 4:5, :]).astype(BF)
        acc_ref[...] = jnp.zeros_like(acc_ref)

    h = h_ref[...]
    a = (_silu(_dot(h, wg_ref[...])) * _dot(h, wu_ref[...])).astype(BF)
    acc_ref[...] += _dot(a, wd_ref[...])

    @pl.when(j == pl.num_programs(1) - 1)
    def _():
        o_ref[...] = x_ref[...] + m_ref[0, 5:6, :] * acc_ref[...]


def _ffn(x2, g, mpack, mod_of_tile, wg, wu, wd, tm, tf):
    rows = x2.shape[0]
    ff = wg.shape[1]
    return pl.pallas_call(
        _ffn_body,
        grid=(rows // tm, ff // tf),
        in_specs=[pl.BlockSpec((tm, D), lambda i, j: (i, 0)),
                  pl.BlockSpec((1, D), lambda i, j: (0, 0)),
                  pl.BlockSpec((1, MOD_ROWS, D), lambda i, j: (mod_of_tile(i), 0, 0)),
                  pl.BlockSpec((D, tf), lambda i, j: (0, j)),
                  pl.BlockSpec((D, tf), lambda i, j: (0, j)),
                  pl.BlockSpec((tf, D), lambda i, j: (j, 0))],
        out_specs=pl.BlockSpec((tm, D), lambda i, j: (i, 0)),
        out_shape=jax.ShapeDtypeStruct((rows, D), F32),
        scratch_shapes=[pltpu.VMEM((tm, D), BF), pltpu.VMEM((tm, D), F32)],
        compiler_params=_params(("arbitrary", "arbitrary")),
        name="ffn_swiglu",
    )(x2, g.reshape(1, D), mpack, wg, wu, wd)


NA_QROWS = 4
NA_KROWS = 12
NA_TQ = NA_QROWS * GRID_W
NA_HG = LANES // HD


def _na_body(q_ref, k0_ref, k1_ref, k2_ref, v0_ref, v1_ref, v2_ref, kc_ref, vc_ref, b_ref, o_ref):
    outs = []
    for hh in range(NA_HG):
        cols = slice(HD * hh, HD * (hh + 1))
        q = q_ref[:, cols]
        s_loc = jnp.concatenate([_dot_nt(q, kr[:, cols]) for kr in (k0_ref, k1_ref, k2_ref)], axis=1)
        s_loc = s_loc + b_ref[0, hh]
        s_ctx = _dot_nt(q, kc_ref[:, cols])
        m = jnp.maximum(jnp.max(s_loc, axis=1, keepdims=True), jnp.max(s_ctx, axis=1, keepdims=True))
        p_loc = jnp.exp(s_loc - m)
        p_ctx = jnp.exp(s_ctx - m)
        den = jnp.sum(p_loc, axis=1, keepdims=True) + jnp.sum(p_ctx, axis=1, keepdims=True)
        acc = _dot(p_ctx.astype(BF), vc_ref[:, cols])
        for t, vr in enumerate((v0_ref, v1_ref, v2_ref)):
            acc = acc + _dot(p_loc[:, t * NA_TQ:(t + 1) * NA_TQ].astype(BF), vr[:, cols])
        outs.append(acc / den)
    o_ref[...] = jnp.concatenate(outs, axis=1).astype(BF)


def _na_bias(rpb, rows):
    heads, n_dr, n_dc = rpb.shape
    nblk = rows // NA_QROWS
    pad_l = GRID_W - NA_COLS
    rext = jnp.pad(rpb.astype(F32), ((0, 0), (0, 0), (pad_l, 2 * GRID_W - n_dc - pad_l)))
    flat = jnp.tile(rext, (1, 1, GRID_W))[:, :, :GRID_W * (2 * GRID_W - 1)]
    toe = flat.reshape(heads, n_dr, GRID_W, 2 * GRID_W - 1)[:, :, :, GRID_W - 1:]
    cols = np.arange(GRID_W)
    cs = np.clip(cols - NA_COLS // 2, 0, GRID_W - NA_COLS)
    col_ok = (cols[None, :] >= cs[:, None]) & (cols[None, :] < cs[:, None] + NA_COLS)
    toe = jnp.where(col_ok, toe, NEG_INF)
    dead = jnp.full((heads, GRID_W, GRID_W), NEG_INF, F32)
    kinds = []
    for j in (0, min(1, nblk - 1), nblk - 1):
        ws = NA_QROWS * int(np.clip(j - 1, 0, nblk - 3))
        q_rows = []
        for qr in range(NA_QROWS):
            r = NA_QROWS * j + qr
            rs = int(np.clip(r - NA_ROWS // 2, 0, rows - NA_ROWS))
            blocks = [toe[:, ws + kr - r + NA_ROWS - 1] if rs <= ws + kr < rs + NA_ROWS else dead
                      for kr in range(NA_KROWS)]
            q_rows.append(jnp.concatenate(blocks, axis=-1))
        kinds.append(jnp.concatenate(q_rows, axis=-2))
    return jnp.stack(kinds)


def _natten(q, k, v, k_ctx, v_ctx, bias, bsz):
    s = q.shape[0] // bsz
    nc = k_ctx.shape[0] // bsz
    nblk = s // NA_TQ
    ngrp = C_HEADS // NA_HG

    def kv_spec(t):
        return pl.BlockSpec((NA_TQ, LANES),
                            lambda j, g, b: (b * nblk + jnp.clip(j - 1, 0, nblk - 3) + t, g))

    def kind(j):
        return jnp.where(j == 0, 0, jnp.where(j == nblk - 1, 2, 1))

    return pl.pallas_call(
        _na_body,
        grid=(nblk, ngrp, bsz),
        in_specs=[pl.BlockSpec((NA_TQ, LANES), lambda j, g, b: (b * nblk + j, g)),
                  kv_spec(0), kv_spec(1), kv_spec(2), kv_spec(0), kv_spec(1), kv_spec(2),
                  pl.BlockSpec((nc, LANES), lambda j, g, b: (b, g)),
                  pl.BlockSpec((nc, LANES), lambda j, g, b: (b, g)),
                  pl.BlockSpec((1, NA_HG, NA_TQ, NA_KROWS * GRID_W), lambda j, g, b: (kind(j), g, 0, 0))],
        out_specs=pl.BlockSpec((NA_TQ, LANES), lambda j, g, b: (b * nblk + j, g)),
        out_shape=jax.ShapeDtypeStruct(q.shape, BF),
        compiler_params=_params(("arbitrary", "arbitrary", "arbitrary")),
        name="natten",
    )(q, k, k, k, v, v, v, k_ctx, v_ctx, bias)


MOE_TM = 512
MOE_TF = 512
META_E1, META_E2, META_R1, META_R2, META_G1, META_G2 = range(6)


def _route_body(x_ref, g_ref, m_ref, wr_ref, h_ref, meta_ref, cnt_ref, tri_ref, carry_ref):
    tm = x_ref.shape[0]

    @pl.when(pl.program_id(0) == 0)
    def _():
        row = lax.broadcasted_iota(jnp.int32, (tm, tm), 0)
        col = lax.broadcasted_iota(jnp.int32, (tm, tm), 1)
        tri_ref[...] = jnp.where(row >= col, 1.0, 0.0).astype(BF)
        carry_ref[...] = jnp.zeros_like(carry_ref)

    h = _norm_mod(x_ref[...], g_ref[...], m_ref[0, 3:4, :], m_ref[0, 4:5, :])
    h_ref[...] = h
    logits = jnp.dot(h, wr_ref[...], preferred_element_type=F32, precision=lax.Precision.HIGHEST)
    lane = lax.broadcasted_iota(jnp.int32, logits.shape, 1)
    lg = jnp.where(lane < N_EXPERTS, logits, NEG_INF)
    m1 = jnp.max(lg, axis=1, keepdims=True)
    i1 = jnp.min(jnp.where(lg == m1, lane, LANES), axis=1, keepdims=True)
    lg2 = jnp.where(lane == i1, NEG_INF, lg)
    m2 = jnp.max(lg2, axis=1, keepdims=True)
    i2 = jnp.min(jnp.where(lg2 == m2, lane, LANES), axis=1, keepdims=True)
    e = jnp.exp(m2 - m1)
    g1 = 1.0 / (1.0 + e)
    g2 = e / (1.0 + e)
    oh = jnp.where(lane == i1, 1.0, 0.0) + jnp.where(lane == i2, 1.0, 0.0)
    cum = _dot(tri_ref[...], oh.astype(BF))
    base = carry_ref[0:1, :] + cum - oh
    r1 = jnp.sum(jnp.where(lane == i1, base, 0.0), axis=1, keepdims=True)
    r2 = jnp.sum(jnp.where(lane == i2, base, 0.0), axis=1, keepdims=True)
    carry_ref[...] = carry_ref[...] + cum[tm - 1:tm, :]
    cnt_ref[...] = carry_ref[...]
    meta = jnp.zeros(logits.shape, F32)
    for ln, val in ((META_E1, i1.astype(F32)), (META_E2, i2.astype(F32)), (META_R1, r1), (META_R2, r2),
                    (META_G1, g1), (META_G2, g2)):
        meta = jnp.where(lane == ln, val, meta)
    meta_ref[...] = meta


def _route(x2, g, mpack, mod_of_tile, wr, tm):
    rows = x2.shape[0]
    return pl.pallas_call(
        _route_body,
        grid=(rows // tm,),
        in_specs=[pl.BlockSpec((tm, D), lambda i: (i, 0)),
                  pl.BlockSpec((1, D), lambda i: (0, 0)),
                  pl.BlockSpec((1, MOD_ROWS, D), lambda i: (mod_of_tile(i), 0, 0)),
                  pl.BlockSpec((D, LANES), lambda i: (0, 0))],
        out_specs=[pl.BlockSpec((tm, D), lambda i: (i, 0)),
                   pl.BlockSpec((tm, LANES), lambda i: (i, 0)),
                   pl.BlockSpec((8, LANES), lambda i: (0, 0))],
        out_shape=[jax.ShapeDtypeStruct((rows, D), F32),
                   jax.ShapeDtypeStruct((rows, LANES), F32),
                   jax.ShapeDtypeStruct((8, LANES), F32)],
        scratch_shapes=[pltpu.VMEM((tm, tm), BF), pltpu.VMEM((8, LANES), F32)],
        compiler_params=_params(("arbitrary",)),
        name="moe_route",
    )(x2, g.reshape(1, D), mpack, wr)


def _row_copy(src_ref, src_row, dst_ref, dst_row, sem):
    return pltpu.make_async_copy(src_ref.at[pl.ds(src_row, 1), :], dst_ref.at[pl.ds(dst_row, 1), :], sem)


def _dispatch_body(idx_ref, src_ref, o_ref, sem):
    n = o_ref.shape[0]

    def start(r, c):
        _row_copy(src_ref, idx_ref[0, 0, r], o_ref, r, sem).start()
        return c

    def wait(r, c):
        _row_copy(src_ref, 0, o_ref, r, sem).wait()
        return c

    lax.fori_loop(0, n, start, 0)
    lax.fori_loop(0, n, wait, 0)


def _dispatch(h, tok_of_slot, tm):
    slots = tok_of_slot.shape[0]
    nt = slots // tm
    return pl.pallas_call(
        _dispatch_body,
        grid=(nt,),
        in_specs=[pl.BlockSpec((1, 1, tm), lambda i: (i, 0, 0), memory_space=pltpu.SMEM),
                  pl.BlockSpec(memory_space=pl.ANY)],
        out_specs=pl.BlockSpec((tm, D), lambda i: (i, 0)),
        out_shape=jax.ShapeDtypeStruct((slots, D), F32),
        scratch_shapes=[pltpu.SemaphoreType.DMA(())],
        compiler_params=_params(("arbitrary",)),
        name="moe_dispatch",
    )(tok_of_slot.reshape(nt, 1, tm), h)


def _experts_body(te_ref, nu_ref, xs_ref, wg_ref, wu_ref, wd_ref, o_ref, h_ref, acc_ref):
    i = pl.program_id(0)
    j = pl.program_id(1)
    used = i < nu_ref[0]

    @pl.when(used & (j == 0))
    def _():
        h_ref[...] = xs_ref[...].astype(BF)
        acc_ref[...] = jnp.zeros_like(acc_ref)

    @pl.when(used)
    def _():
        h = h_ref[...]
        a = (_silu(_dot(h, wg_ref[0])) * _dot(h, wu_ref[0])).astype(BF)
        acc_ref[...] += _dot(a, wd_ref[0])

    @pl.when(j == pl.num_programs(1) - 1)
    def _():
        o_ref[...] = jnp.where(used, acc_ref[...], 0.0)


def _experts(xs, tile_expert, n_used, wg, wu, wd):
    slots = xs.shape[0]
    ff = wg.shape[2]
    nf = ff // MOE_TF

    def jj(i, j, nu):
        return jnp.where(i < nu[0], j, nf - 1)

    grid_spec = pltpu.PrefetchScalarGridSpec(
        num_scalar_prefetch=2,
        grid=(slots // MOE_TM, nf),
        in_specs=[pl.BlockSpec((MOE_TM, D), lambda i, j, te, nu: (i, 0)),
                  pl.BlockSpec((1, D, MOE_TF), lambda i, j, te, nu: (te[i], 0, jj(i, j, nu))),
                  pl.BlockSpec((1, D, MOE_TF), lambda i, j, te, nu: (te[i], 0, jj(i, j, nu))),
                  pl.BlockSpec((1, MOE_TF, D), lambda i, j, te, nu: (te[i], jj(i, j, nu), 0))],
        out_specs=pl.BlockSpec((MOE_TM, D), lambda i, j, te, nu: (i, 0)),
        scratch_shapes=[pltpu.VMEM((MOE_TM, D), BF), pltpu.VMEM((MOE_TM, D), F32)])
    return pl.pallas_call(
        _experts_body,
        grid_spec=grid_spec,
        out_shape=jax.ShapeDtypeStruct((slots, D), F32),
        compiler_params=_params(("arbitrary", "arbitrary")),
        name="moe_experts",
    )(tile_expert, n_used, xs, wg, wu, wd)


def _combine_body(s1_ref, s2_ref, ys_ref, x_ref, m_ref, meta_ref, gf_ref, o_ref, y1_ref, y2_ref, sem):
    n = x_ref.shape[0]

    def start(r, c):
        _row_copy(ys_ref, s1_ref[0, 0, r], y1_ref, r, sem).start()
        _row_copy(ys_ref, s2_ref[0, 0, r], y2_ref, r, sem).start()
        return c

    def wait(r, c):
        _row_copy(ys_ref, 0, y1_ref, r, sem).wait()
        _row_copy(ys_ref, 0, y2_ref, r, sem).wait()
        return c

    lax.fori_loop(0, n, start, 0)
    lax.fori_loop(0, n, wait, 0)
    meta = meta_ref[...]
    lane = lax.broadcasted_iota(jnp.int32, meta.shape, 1)
    g1 = jnp.sum(jnp.where(lane == META_G1, meta, 0.0), axis=1, keepdims=True)
    g2 = jnp.sum(jnp.where(lane == META_G2, meta, 0.0), axis=1, keepdims=True)
    y = x_ref[...] + m_ref[0, 5:6, :] * (g1 * y1_ref[...] + g2 * y2_ref[...])
    ms = jnp.mean(y * y, axis=-1, keepdims=True)
    o_ref[...] = y * lax.rsqrt(ms + EPS) * gf_ref[...]


def _combine(x2, mpack, mod_of_row, meta, slot1, slot2, ys, g_final, tm):
    rows = x2.shape[0]
    nt = rows // tm
    smem = lambda: pl.BlockSpec((1, 1, tm), lambda i: (i, 0, 0), memory_space=pltpu.SMEM)
    return pl.pallas_call(
        _combine_body,
        grid=(nt,),
        in_specs=[smem(), smem(),
                  pl.BlockSpec(memory_space=pl.ANY),
                  pl.BlockSpec((tm, D), lambda i: (i, 0)),
                  pl.BlockSpec((1, MOD_ROWS, D), lambda i: (mod_of_row(i * tm), 0, 0)),
                  pl.BlockSpec((tm, LANES), lambda i: (i, 0)),
                  pl.BlockSpec((1, D), lambda i: (0, 0))],
        out_specs=pl.BlockSpec((tm, D), lambda i: (i, 0)),
        out_shape=jax.ShapeDtypeStruct((rows, D), F32),
        scratch_shapes=[pltpu.VMEM((tm, D), F32), pltpu.VMEM((tm, D), F32), pltpu.SemaphoreType.DMA(())],
        compiler_params=_params(("arbitrary",)),
        name="moe_combine",
    )(slot1.reshape(nt, 1, tm), slot2.reshape(nt, 1, tm), ys, x2, mpack, meta, g_final.reshape(1, D))


def _moe(x2, g, mpack, mod_of_tile, mod_of_row, wr, wg, wu, wd, g_final, tm):
    rows = x2.shape[0]
    ne = wg.shape[0]
    h, meta, cnt = _route(x2, g, mpack, mod_of_tile, wr, tm)
    counts = cnt[0, :ne].astype(jnp.int32)
    tiles_e = (counts + MOE_TM - 1) // MOE_TM
    tile_end = jnp.cumsum(tiles_e)
    group_off = (tile_end - tiles_e) * MOE_TM
    n_tiles = 2 * rows // MOE_TM + ne
    tile_expert = jnp.minimum(
        jnp.sum(jnp.arange(n_tiles)[:, None] >= tile_end[None, :], axis=1), ne - 1).astype(jnp.int32)
    n_used = tile_end[-1:].astype(jnp.int32)
    eids = jnp.arange(ne)

    def slot_of(lane_e, lane_r):
        e = meta[:, lane_e].astype(jnp.int32)
        off = jnp.sum(jnp.where(e[:, None] == eids[None, :], group_off[None, :], 0), axis=1)
        return off + meta[:, lane_r].astype(jnp.int32)

    slot1 = slot_of(META_E1, META_R1)
    slot2 = slot_of(META_E2, META_R2)
    tok = jnp.arange(rows, dtype=jnp.int32)
    tok_of_slot = jnp.zeros((n_tiles * MOE_TM,), jnp.int32).at[slot1].set(tok).at[slot2].set(tok)
    xs = _dispatch(h, tok_of_slot, MOE_TM)
    ys = _experts(xs, tile_expert, n_used, wg, wu, wd)
    return _combine(x2, mpack, mod_of_row, meta, slot1, slot2, ys, g_final, MOE_TM)


def _rope_tables(seq_len):
    t = np.arange(seq_len)
    row = (t // GRID_W).astype(np.float32)
    col = (t % GRID_W).astype(np.float32)
    n_axis = HD // 4
    inv = jnp.power(ROPE_THETA, -jnp.arange(n_axis, dtype=F32) / n_axis)
    ang = jnp.concatenate([jnp.asarray(row)[:, None] * inv, jnp.asarray(col)[:, None] * inv], axis=-1)
    cos, sin = jnp.cos(ang), jnp.sin(ang)
    ct = jnp.tile(jnp.concatenate([cos, cos], axis=-1), (1, LANES // HD))
    st = jnp.tile(jnp.concatenate([-sin, sin], axis=-1), (1, LANES // HD))
    return ct, st


def kernel(x, c, ctx, c_ctx, w_mod, b_mod, g_norm1, g_norm2, w_in_ab, g_q, g_k, w_a2_f, b_a_f, w_a2_b,
           b_a_b, g_gla, w_out_ab, w_ff_gate, w_ff_up, w_ff_down, w_in_c, rpb_c, w_out_c, w_router,
           w_moe_gate, w_moe_up, w_moe_down, g_final):
    bsz, seq, _ = x.shape
    nc = ctx.shape[1]
    depth = w_mod.shape[0]
    assert depth == 2 and seq % GRID_W == 0

    cond_rows = -(-(bsz + 1) // 8) * 8
    cond = jnp.zeros((cond_rows, D), F32).at[:bsz].set(c).at[bsz].set(c_ctx)
    mods = _adaln(cond, w_mod, b_mod).reshape(depth, cond_rows, N_MOD, D)
    mods = jnp.pad(mods, ((0, 0), (0, 0), (0, MOD_ROWS - N_MOD), (0, 0)))

    xl = x.reshape(bsz * seq, D)
    xc = ctx.reshape(bsz * nc, D)
    tm_l = min(1024, seq)
    tm_c = min(1024, bsz * nc)
    tpb = seq // tm_l
    lat_mod = lambda i: i // tpb
    ctx_mod = lambda i: bsz

    mp = mods[0]
    w_in = w_in_ab[0]
    perm = np.concatenate([np.arange(0, HD, 2), np.arange(1, HD, 2)])
    nq, nkv = A_HEADS * HD, A_KV * HD
    o_q, o_k, o_v = 0, nq, nq + nkv
    o_bq = o_v + nkv
    o_bk = o_bq + B_HEADS * B_DK
    o_bv = o_bk + B_HEADS * B_DK
    o_br = o_bv + B_HEADS * B_DV
    o_zf = o_br + B_HEADS * B_DV
    o_end = o_zf + 2 * B_RANK
    wq = w_in[:, o_q:o_k].reshape(D, A_HEADS, HD)[:, :, perm].reshape(D, nq)
    wk = w_in[:, o_k:o_v].reshape(D, A_KV, HD)[:, :, perm].reshape(D, nkv)
    wz = jnp.pad(w_in[:, o_zf:o_end], ((0, 0), (0, LANES - 2 * B_RANK)))
    w0 = jnp.concatenate([wq, wk, w_in[:, o_v:o_bq], w_in[:, o_bq:o_bk] * (B_DK ** -0.5),
                          w_in[:, o_bk:o_zf], wz], axis=1).astype(BF)
    widths0 = (nq + nkv, nkv, GLA_W, LANES)
    outw0 = (nq, nkv, nkv, GLA_W, LANES)
    gqk = jnp.concatenate([jnp.tile(g_q[0][perm] * (HD ** -0.5), A_HEADS),
                           jnp.tile(g_k[0][perm], A_KV)]).reshape(1, nq + nkv)
    nh = A_HEADS + A_KV
    bd = jnp.asarray(np.kron(np.eye(nh), np.ones((HD, HD))), BF)
    ct, st = _rope_tables(seq)
    ct_c = jnp.ones((tm_c, LANES), F32)
    st_c = jnp.zeros((tm_c, LANES), F32)

    ql, kl, vl, gl, zl = _proj(xl, g_norm1[0], mp, lat_mod, w0, widths0, outw0, tm_l, (ct, st, gqk, bd))
    qc, kc, vc, gc, zc = _proj(xc, g_norm1[0], mp, ctx_mod, w0, widths0, outw0, tm_c, (ct_c, st_c, gqk, bd))

    oa_l = _gqa(ql, kc, vc, kl, vl, bsz, 128)
    oa_c = _gqa(qc, kc, vc, None, None, bsz, min(128, nc))

    w2f = jnp.zeros((LANES, B_HEADS * B_DK), F32).at[:B_RANK].set(w_a2_f[0]).astype(BF)
    w2b = jnp.zeros((LANES, B_HEADS * B_DK), F32).at[B_RANK:2 * B_RANK].set(w_a2_b[0]).astype(BF)
    ob_c, ob_l = _gla(gc, gl, zc, zl, w2f, w2b, b_a_f[0].reshape(1, -1), b_a_b[0].reshape(1, -1),
                      jnp.tile(g_gla[0], B_HEADS).reshape(1, -1), bsz)

    wo = w_out_ab[0].astype(BF)
    xl = _outproj(xl, mp, lat_mod, [(oa_l, wo[:nq]), (ob_l, wo[nq:])], tm_l)
    xc = _outproj(xc, mp, ctx_mod, [(oa_c, wo[:nq]), (ob_c, wo[nq:])], tm_c)

    wg, wu, wd = w_ff_gate[0].astype(BF), w_ff_up[0].astype(BF), w_ff_down[0].astype(BF)
    xl = _ffn(xl, g_norm2[0], mp, lat_mod, wg, wu, wd, tm_l, 512)
    xc = _ffn(xc, g_norm2[0], mp, ctx_mod, wg, wu, wd, tm_c, 512)

    mp = mods[1]
    cw = C_HEADS * HD
    w1 = jnp.concatenate([w_in_c[0][:, :cw] * (HD ** -0.5), w_in_c[0][:, cw:]], axis=1).astype(BF)
    q1, k1, v1 = _proj(xl, g_norm1[1], mp, lat_mod, w1, (cw, cw, cw), (cw, cw, cw), tm_l)
    k1c, v1c = _proj(xc, g_norm1[1], mp, ctx_mod, w1[:, cw:], (cw, cw), (cw, cw), tm_c)

    bias = _na_bias(rpb_c[0], seq // GRID_W)
    o1 = _natten(q1, k1, v1, k1c, v1c, bias, bsz)
    xl = _outproj(xl, mp, lat_mod, [(o1, w_out_c[0].astype(BF))], tm_l)

    wr = jnp.pad(w_router[0], ((0, 0), (0, LANES - N_EXPERTS)))
    out = _moe(xl, g_norm2[1], mp, lat_mod, lambda r: r // seq, wr, w_moe_gate[0].astype(BF),
               w_moe_up[0].astype(BF), w_moe_down[0].astype(BF), g_final, tm_l)
    return out.reshape(bsz, seq, D)
```

```python
import functools

import numpy as np
import jax
import jax.numpy as jnp
from jax import lax
from jax.experimental import pallas as pl
from jax.experimental.pallas import tpu as pltpu

F32 = jnp.float32
BF = jnp.bfloat16

D = 1024
EPS = 1e-6
N_MOD = 6
MOD_ROWS = 8
GRID_W = 64
HD = 64
A_HEADS = 8
A_KV = 2
A_GROUP = A_HEADS // A_KV
ROPE_THETA = 10000.0
LOG2E = 1.4426950408889634
B_HEADS = 4
B_DK = 64
B_DV = 128
B_RANK = 16
B_TAU = 16.0
B_CHUNK = 64
C_HEADS = 16
NA_ROWS = 8
NA_COLS = 16
N_EXPERTS = 8
LANES = 128
VMEM_LIMIT = 56 * 2 ** 20

NEG_INF = float("-inf")


def _params(sem):
    return pltpu.CompilerParams(dimension_semantics=sem, vmem_limit_bytes=VMEM_LIMIT)


def _dot(a, b):
    return jnp.dot(a, b, preferred_element_type=F32)


def _dot_nt(a, b):
    return lax.dot_general(a, b, (((1,), (1,)), ((), ())), preferred_element_type=F32)


def _dot_tn(a, b):
    return lax.dot_general(a, b, (((0,), (0,)), ((), ())), preferred_element_type=F32)


def _silu(x):
    return x / (1.0 + jnp.exp(-x))


def _split_dot(lhs_bf_exact, x):
    hi = x.astype(BF)
    lo = (x - hi.astype(F32)).astype(BF)
    return _dot(lhs_bf_exact, hi) + _dot(lhs_bf_exact, lo)


def _norm_mod(x, g, shift, scale):
    ms = jnp.mean(x * x, axis=-1, keepdims=True)
    return (x * lax.rsqrt(ms + EPS) * g) * (1.0 + scale) + shift


def _adaln_body(c_ref, w_ref, b_ref, o_ref):
    s = _silu(c_ref[...])
    o_ref[0] = _dot(s.astype(BF), w_ref[0].astype(BF)) + b_ref[0]


def _adaln(cond, w_mod, b_mod):
    depth, _, n = w_mod.shape
    rows = cond.shape[0]
    tn = 1536
    return pl.pallas_call(
        _adaln_body,
        grid=(depth, n // tn),
        in_specs=[pl.BlockSpec((rows, D), lambda l, j: (0, 0)),
                  pl.BlockSpec((1, D, tn), lambda l, j: (l, 0, j)),
                  pl.BlockSpec((1, 1, tn), lambda l, j: (l, 0, j))],
        out_specs=pl.BlockSpec((1, rows, tn), lambda l, j: (l, 0, j)),
        out_shape=jax.ShapeDtypeStruct((depth, rows, n), F32),
        compiler_params=_params(("arbitrary", "arbitrary")),
        name="adaln",
    )(cond, w_mod, b_mod.reshape(depth, 1, n))


def _qk_norm_rope(p, gqk, bd, ct, st):
    ss = _split_dot_rhs(p * p, bd)
    y = p * lax.rsqrt(ss * (1.0 / HD) + EPS) * gqk
    lane = lax.broadcasted_iota(jnp.int32, ct.shape, 1)
    first = (lane % HD) < (HD // 2)
    outs = []
    for c in range(p.shape[1] // LANES):
        yc = y[:, c * LANES:(c + 1) * LANES]
        partner = jnp.where(first, pltpu.roll(yc, LANES - HD // 2, 1), pltpu.roll(yc, HD // 2, 1))
        outs.append(yc * ct + partner * st)
    return jnp.concatenate(outs, axis=1)


def _split_dot_rhs(x, rhs_bf_exact):
    hi = x.astype(BF)
    lo = (x - hi.astype(F32)).astype(BF)
    return _dot(hi, rhs_bf_exact) + _dot(lo, rhs_bf_exact)


def _proj_body(x_ref, g_ref, m_ref, w_ref, *rest, widths, rope, scales):
    if rope:
        ct_ref, st_ref, gqk_ref, bd_ref = rest[:4]
        o_refs = rest[4:]
    else:
        o_refs = rest
    h = _norm_mod(x_ref[...], g_ref[...], m_ref[0, 0:1, :], m_ref[0, 1:2, :]).astype(BF)
    off = 0
    oi = 0
    for gi, wd in enumerate(widths):
        p = _dot(h, w_ref[:, off:off + wd])
        off += wd
        if scales[gi] != 1.0:
            p = p * scales[gi]
        if rope and gi == 0:
            p = _qk_norm_rope(p, gqk_ref[...], bd_ref[...], ct_ref[...], st_ref[...])
            nq = A_HEADS * HD
            o_refs[0][...] = p[:, :nq].astype(BF)
            o_refs[1][...] = p[:, nq:].astype(BF)
            oi = 2
        elif rope and gi == 1:
            ones = jnp.ones((p.shape[0], HD), F32)
            parts = []
            for j in range(A_KV):
                parts += [p[:, HD * j:HD * (j + 1)], ones]
            o_refs[oi][...] = jnp.concatenate(parts, axis=1).astype(BF)
            oi += 1
        else:
            o_refs[oi][...] = p.astype(BF)
            oi += 1


def _proj(x2, g, mpack, mod_of_tile, w, widths, out_widths, tm, rope_args=None, scales=None):
    rows = x2.shape[0]
    n = w.shape[1]
    in_specs = [pl.BlockSpec((tm, D), lambda i: (i, 0)),
                pl.BlockSpec((1, D), lambda i: (0, 0)),
                pl.BlockSpec((1, MOD_ROWS, D), lambda i: (mod_of_tile(i), 0, 0)),
                pl.BlockSpec((D, n), lambda i: (0, 0))]
    args = [x2, g.reshape(1, D), mpack, w]
    if rope_args is not None:
        ct, st, gqk, bd = rope_args
        nt = ct.shape[0] // tm
        in_specs += [pl.BlockSpec((tm, LANES), lambda i: (i % nt, 0)),
                     pl.BlockSpec((tm, LANES), lambda i: (i % nt, 0)),
                     pl.BlockSpec(gqk.shape, lambda i: (0, 0)),
                     pl.BlockSpec(bd.shape, lambda i: (0, 0))]
        args += [ct, st, gqk, bd]
    return pl.pallas_call(
        functools.partial(_proj_body, widths=widths, rope=rope_args is not None,
                          scales=scales or (1.0,) * len(widths)),
        grid=(rows // tm,),
        in_specs=in_specs,
        out_specs=[pl.BlockSpec((tm, ow), lambda i: (i, 0)) for ow in out_widths],
        out_shape=[jax.ShapeDtypeStruct((rows, ow), BF) for ow in out_widths],
        compiler_params=_params(("arbitrary",)),
        name="norm_mod_proj",
    )(*args)


def _gqa_body(q_ref, kc_ref, vc_ref, *rest, with_lat, tq):
    if with_lat:
        kl_ref, vl_ref, o_ref = rest
    else:
        (o_ref,) = rest
    for j in range(A_KV):
        cols = slice(HD * j, HD * (j + 1))
        vcols = slice(2 * HD * j, 2 * HD * (j + 1))
        qg = jnp.concatenate(
            [q_ref[:, HD * (A_GROUP * j + g):HD * (A_GROUP * j + g + 1)] for g in range(A_GROUP)], axis=0)
        s_c = _dot_nt(qg, kc_ref[:, cols])
        m = jnp.max(s_c, axis=1, keepdims=True)
        if with_lat:
            s_l = _dot_nt(qg, kl_ref[:, cols])
            m = jnp.maximum(m, jnp.max(s_l, axis=1, keepdims=True))
        acc = _dot(jnp.exp2(s_c - m).astype(BF), vc_ref[:, vcols])
        if with_lat:
            acc = acc + _dot(jnp.exp2(s_l - m).astype(BF), vl_ref[:, vcols])
        o = acc[:, :HD] / acc[:, HD:HD + 1]
        for g in range(A_GROUP):
            h = A_GROUP * j + g
            o_ref[:, HD * h:HD * (h + 1)] = o[g * tq:(g + 1) * tq].astype(BF)


def _gqa(q, k_ctx, v_ctx, k_lat, v_lat, bsz, tq):
    sq = q.shape[0] // bsz
    nc = k_ctx.shape[0] // bsz
    nq = sq // tq
    with_lat = k_lat is not None
    in_specs = [pl.BlockSpec((tq, A_HEADS * HD), lambda b, i: (b * nq + i, 0)),
                pl.BlockSpec((nc, A_KV * HD), lambda b, i: (b, 0)),
                pl.BlockSpec((nc, 2 * A_KV * HD), lambda b, i: (b, 0))]
    args = [q, k_ctx, v_ctx]
    if with_lat:
        sl = k_lat.shape[0] // bsz
        in_specs += [pl.BlockSpec((sl, A_KV * HD), lambda b, i: (b, 0)),
                     pl.BlockSpec((sl, 2 * A_KV * HD), lambda b, i: (b, 0))]
        args += [k_lat, v_lat]
    return pl.pallas_call(
        functools.partial(_gqa_body, with_lat=with_lat, tq=tq),
        grid=(bsz, nq),
        in_specs=in_specs,
        out_specs=pl.BlockSpec((tq, A_HEADS * HD), lambda b, i: (b * nq + i, 0)),
        out_shape=jax.ShapeDtypeStruct(q.shape, BF),
        compiler_params=_params(("arbitrary", "arbitrary")),
        name="gqa_attention",
    )(*args)


GLA_Q = 0
GLA_K = B_HEADS * B_DK
GLA_V = 2 * B_HEADS * B_DK
GLA_R = GLA_V + B_HEADS * B_DV
GLA_W = GLA_R + B_HEADS * B_DV
GLA_GROUP = 4


def _gla_body(gc_ref, gl_ref, zc_ref, zl_ref, w2f_ref, w2b_ref, bf_ref, bb_ref, gg_ref,
              oc_ref, ol_ref, ofc_ref, ofl_ref, st_ref):
    nk = B_HEADS * B_DK
    ch = B_CHUNK
    row = lax.broadcasted_iota(jnp.int32, (ch, ch), 0)
    col = lax.broadcasted_iota(jnp.int32, (ch, ch), 1)
    lane = lax.broadcasted_iota(jnp.int32, (ch, nk), 1)
    head_masks = [(lane // B_DK) == h for h in range(B_HEADS)]

    def chunk_out(g_ref, z_ref, r0, fwd, s_t):
        blk = g_ref[pl.ds(r0, ch), :]
        q = blk[:, GLA_Q:GLA_Q + nk].astype(F32)
        k = blk[:, GLA_K:GLA_K + nk].astype(F32)
        v = blk[:, GLA_V:GLA_V + B_HEADS * B_DV]
        z = z_ref[pl.ds(r0, ch), :]
        pre = _dot(z, (w2f_ref if fwd else w2b_ref)[...]) + (bf_ref if fwd else bb_ref)[...]
        la = -(jnp.maximum(-pre, 0.0) + jnp.log(1.0 + jnp.exp(-jnp.abs(pre)))) * (1.0 / B_TAU)
        keep = (row >= col) if fwd else (row <= col)
        bcum = _split_dot(keep.astype(BF), la)
        b_last = bcum[ch - 1:ch, :] if fwd else bcum[0:1, :]
        q_dec = q * jnp.exp(bcum)
        k_intra = (k * jnp.exp(-bcum)).astype(BF)
        k_state = k * jnp.exp(b_last - bcum)
        s_b = s_t.astype(BF)
        d_s = jnp.zeros_like(s_t)
        outs = []
        for h in range(B_HEADS):
            qm = jnp.where(head_masks[h], q_dec, 0.0).astype(BF)
            att = jnp.where(keep, _dot_nt(qm, k_intra), 0.0)
            vh = v[:, B_DV * h:B_DV * (h + 1)]
            outs.append(_dot(att.astype(BF), vh) + _dot_nt(qm, s_b))
            ksm = jnp.where(head_masks[h], k_state, 0.0).astype(BF)
            d_s = d_s + _dot_tn(vh, ksm)
        return jnp.concatenate(outs, axis=1), blk, jnp.exp(b_last) * s_t + d_s

    def fwd_pass(g_ref, z_ref, of_ref, n):
        def body(i, carry):
            s_t = st_ref[...]
            for u in range(GLA_GROUP):
                r0 = pl.multiple_of((i * GLA_GROUP + u) * ch, ch)
                o, _, s_t = chunk_out(g_ref, z_ref, r0, True, s_t)
                of_ref[pl.ds(r0, ch), :] = o
            st_ref[...] = s_t
            return carry
        lax.fori_loop(0, n // GLA_GROUP, body, 0)

    def bwd_pass(g_ref, z_ref, of_ref, o_ref, n):
        def body(i, carry):
            s_t = st_ref[...]
            for u in range(GLA_GROUP):
                r0 = pl.multiple_of((n - 1 - (i * GLA_GROUP + u)) * ch, ch)
                o, blk, s_t = chunk_out(g_ref, z_ref, r0, False, s_t)
                o = o + of_ref[pl.ds(r0, ch), :]
                parts = []
                for h in range(B_HEADS):
                    oh = o[:, B_DV * h:B_DV * (h + 1)]
                    ms = jnp.mean(oh * oh, axis=-1, keepdims=True)
                    parts.append(oh * lax.rsqrt(ms + EPS))
                y = jnp.concatenate(parts, axis=1) * gg_ref[...]
                r = blk[:, GLA_R:GLA_R + B_HEADS * B_DV].astype(F32)
                o_ref[pl.ds(r0, ch), :] = (y * _silu(r)).astype(BF)
            st_ref[...] = s_t
            return carry
        lax.fori_loop(0, n // GLA_GROUP, body, 0)

    nc = gc_ref.shape[0] // ch
    nl = gl_ref.shape[0] // ch
    assert nc % GLA_GROUP == 0 and nl % GLA_GROUP == 0
    st_ref[...] = jnp.zeros_like(st_ref)
    fwd_pass(gc_ref, zc_ref, ofc_ref, nc)
    fwd_pass(gl_ref, zl_ref, ofl_ref, nl)
    st_ref[...] = jnp.zeros_like(st_ref)
    bwd_pass(gc_ref, zc_ref, ofc_ref, oc_ref, nc)
    bwd_pass(gl_ref, zl_ref, ofl_ref, ol_ref, nl)


def _gla(g_ctx, g_lat, z_ctx, z_lat, w2f, w2b, b_f, b_b, gg, bsz):
    nc = g_ctx.shape[0] // bsz
    sl = g_lat.shape[0] // bsz
    nk = B_HEADS * B_DK
    nv = B_HEADS * B_DV
    full = lambda shape: pl.BlockSpec(shape, lambda b: (0, 0))
    return pl.pallas_call(
        _gla_body,
        grid=(bsz,),
        in_specs=[pl.BlockSpec((nc, GLA_W), lambda b: (b, 0)),
                  pl.BlockSpec((sl, GLA_W), lambda b: (b, 0)),
                  pl.BlockSpec((nc, LANES), lambda b: (b, 0)),
                  pl.BlockSpec((sl, LANES), lambda b: (b, 0)),
                  full((LANES, nk)), full((LANES, nk)), full((1, nk)), full((1, nk)), full((1, nv))],
        out_specs=[pl.BlockSpec((nc, nv), lambda b: (b, 0)),
                   pl.BlockSpec((sl, nv), lambda b: (b, 0))],
        out_shape=[jax.ShapeDtypeStruct((g_ctx.shape[0], nv), BF),
                   jax.ShapeDtypeStruct((g_lat.shape[0], nv), BF)],
        scratch_shapes=[pltpu.VMEM((nc, nv), F32), pltpu.VMEM((sl, nv), F32),
                        pltpu.VMEM((B_DV, nk), F32)],
        compiler_params=_params(("arbitrary",)),
        name="gla_bidir",
    )(g_ctx, g_lat, z_ctx, z_lat, w2f, w2b, b_f, b_b, gg)


def _outproj_body(x_ref, m_ref, *rest, n_in):
    o_ref = rest[-1]
    acc = _dot(rest[0][...], rest[1][...])
    for t in range(1, n_in):
        acc = acc + _dot(rest[2 * t][...], rest[2 * t + 1][...])
    o_ref[...] = x_ref[...] + m_ref[0, 2:3, :] * acc


def _outproj(x2, mpack, mod_of_tile, pairs, tm):
    rows = x2.shape[0]
    in_specs = [pl.BlockSpec((tm, D), lambda i: (i, 0)),
                pl.BlockSpec((1, MOD_ROWS, D), lambda i: (mod_of_tile(i), 0, 0))]
    args = [x2, mpack]
    for a, w in pairs:
        in_specs += [pl.BlockSpec((tm, a.shape[1]), lambda i: (i, 0)),
                     pl.BlockSpec(w.shape, lambda i: (0, 0))]
        args += [a, w]
    return pl.pallas_call(
        functools.partial(_outproj_body, n_in=len(pairs)),
        grid=(rows // tm,),
        in_specs=in_specs,
        out_specs=pl.BlockSpec((tm, D), lambda i: (i, 0)),
        out_shape=jax.ShapeDtypeStruct((rows, D), F32),
        compiler_params=_params(("arbitrary",)),
        name="outproj_residual",
    )(*args)


def _ffn_body(x_ref, g_ref, m_ref, wg_ref, wu_ref, wd_ref, o_ref, h_ref, acc_ref):
    j = pl.program_id(1)

    @pl.when(j == 0)
    def _():
        h_ref[...] = _norm_mod(x_ref[...], g_ref[...], m_ref[0, 3:4, :], m_ref[0, 4:5, :]).astype(BF)
        acc_ref[...] = jnp.zeros_like(acc_ref)

    h = h_ref[...]
    a = (_silu(_dot(h, wg_ref[...])) * _dot(h, wu_ref[...])).astype(BF)
    acc_ref[...] += _dot(a, wd_ref[...])

    @pl.when(j == pl.num_programs(1) - 1)
    def _():
        o_ref[...] = x_ref[...] + m_ref[0, 5:6, :] * acc_ref[...]


def _ffn(x2, g, mpack, mod_of_tile, wg, wu, wd, tm, tf):
    rows = x2.shape[0]
    ff = wg.shape[1]
    return pl.pallas_call(
        _ffn_body,
        grid=(rows // tm, ff // tf),
        in_specs=[pl.BlockSpec((tm, D), lambda i, j: (i, 0)),
                  pl.BlockSpec((1, D), lambda i, j: (0, 0)),
                  pl.BlockSpec((1, MOD_ROWS, D), lambda i, j: (mod_of_tile(i), 0, 0)),
                  pl.BlockSpec((D, tf), lambda i, j: (0, j)),
                  pl.BlockSpec((D, tf), lambda i, j: (0, j)),
                  pl.BlockSpec((tf, D), lambda i, j: (j, 0))],
        out_specs=pl.BlockSpec((tm, D), lambda i, j: (i, 0)),
        out_shape=jax.ShapeDtypeStruct((rows, D), F32),
        scratch_shapes=[pltpu.VMEM((tm, D), BF), pltpu.VMEM((tm, D), F32)],
        compiler_params=_params(("arbitrary", "arbitrary")),
        name="ffn_swiglu",
    )(x2, g.reshape(1, D), mpack, wg, wu, wd)


NA_QROWS = 4
NA_KROWS = 12
NA_TQ = NA_QROWS * GRID_W
NA_HG = 8
NA_W = NA_HG * HD


def _na_body(q_ref, k0_ref, k1_ref, k2_ref, v0_ref, v1_ref, v2_ref, kc_ref, vc_ref, b_ref, o_ref):
    pair = LANES // HD
    for h0 in range(0, NA_HG, pair):
        outs = []
        for hh in range(h0, h0 + pair):
            cols = slice(HD * hh, HD * (hh + 1))
            q = q_ref[:, cols]
            s_loc = jnp.concatenate([_dot_nt(q, kr[:, cols]) for kr in (k0_ref, k1_ref, k2_ref)], axis=1)
            s_loc = s_loc + b_ref[0, hh]
            s_ctx = _dot_nt(q, kc_ref[:, cols])
            m = jnp.maximum(jnp.max(s_loc, axis=1, keepdims=True), jnp.max(s_ctx, axis=1, keepdims=True))
            p_loc = jnp.exp2(s_loc - m)
            p_ctx = jnp.exp2(s_ctx - m)
            den = jnp.sum(p_loc, axis=1, keepdims=True) + jnp.sum(p_ctx, axis=1, keepdims=True)
            acc = _dot(p_ctx.astype(BF), vc_ref[:, cols])
            for t, vr in enumerate((v0_ref, v1_ref, v2_ref)):
                acc = acc + _dot(p_loc[:, t * NA_TQ:(t + 1) * NA_TQ].astype(BF), vr[:, cols])
            outs.append(acc / den)
        o_ref[:, HD * h0:HD * (h0 + pair)] = jnp.concatenate(outs, axis=1).astype(BF)


def _na_bias(rpb, rows):
    heads, n_dr, n_dc = rpb.shape
    nblk = rows // NA_QROWS
    pad_l = GRID_W - NA_COLS
    rext = jnp.pad(rpb.astype(F32), ((0, 0), (0, 0), (pad_l, 2 * GRID_W - n_dc - pad_l)))
    flat = jnp.tile(rext, (1, 1, GRID_W))[:, :, :GRID_W * (2 * GRID_W - 1)]
    toe = flat.reshape(heads, n_dr, GRID_W, 2 * GRID_W - 1)[:, :, :, GRID_W - 1:]
    cols = np.arange(GRID_W)
    cs = np.clip(cols - NA_COLS // 2, 0, GRID_W - NA_COLS)
    col_ok = (cols[None, :] >= cs[:, None]) & (cols[None, :] < cs[:, None] + NA_COLS)
    toe = jnp.where(col_ok, toe, NEG_INF)
    dead = jnp.full((heads, GRID_W, GRID_W), NEG_INF, F32)
    kinds = []
    for j in (0, min(1, nblk - 1), nblk - 1):
        ws = NA_QROWS * int(np.clip(j - 1, 0, nblk - 3))
        q_rows = []
        for qr in range(NA_QROWS):
            r = NA_QROWS * j + qr
            rs = int(np.clip(r - NA_ROWS // 2, 0, rows - NA_ROWS))
            blocks = [toe[:, ws + kr - r + NA_ROWS - 1] if rs <= ws + kr < rs + NA_ROWS else dead
                      for kr in range(NA_KROWS)]
            q_rows.append(jnp.concatenate(blocks, axis=-1))
        kinds.append(jnp.concatenate(q_rows, axis=-2))
    return jnp.stack(kinds)


def _natten(q, k, v, k_ctx, v_ctx, bias, bsz):
    s = q.shape[0] // bsz
    nc = k_ctx.shape[0] // bsz
    nblk = s // NA_TQ
    ngrp = C_HEADS // NA_HG

    def kv_spec(t):
        return pl.BlockSpec((NA_TQ, NA_W),
                            lambda j, g, b: (b * nblk + jnp.clip(j - 1, 0, nblk - 3) + t, g))

    def kind(j):
        return jnp.where(j == 0, 0, jnp.where(j == nblk - 1, 2, 1))

    return pl.pallas_call(
        _na_body,
        grid=(nblk, ngrp, bsz),
        in_specs=[pl.BlockSpec((NA_TQ, NA_W), lambda j, g, b: (b * nblk + j, g)),
                  kv_spec(0), kv_spec(1), kv_spec(2), kv_spec(0), kv_spec(1), kv_spec(2),
                  pl.BlockSpec((nc, NA_W), lambda j, g, b: (b, g)),
                  pl.BlockSpec((nc, NA_W), lambda j, g, b: (b, g)),
                  pl.BlockSpec((1, NA_HG, NA_TQ, NA_KROWS * GRID_W), lambda j, g, b: (kind(j), g, 0, 0))],
        out_specs=pl.BlockSpec((NA_TQ, NA_W), lambda j, g, b: (b * nblk + j, g)),
        out_shape=jax.ShapeDtypeStruct(q.shape, BF),
        compiler_params=_params(("arbitrary", "arbitrary", "arbitrary")),
        name="natten",
    )(q, k, k, k, v, v, v, k_ctx, v_ctx, bias)


MOE_TM = 512
MOE_TF = 512
META_E1, META_E2, META_R1, META_R2, META_G1, META_G2 = range(6)


def _route_body(x_ref, g_ref, m_ref, wr_ref, h_ref, meta_ref, cnt_ref, tri_ref, carry_ref):
    tm = x_ref.shape[0]

    @pl.when(pl.program_id(0) == 0)
    def _():
        row = lax.broadcasted_iota(jnp.int32, (tm, tm), 0)
        col = lax.broadcasted_iota(jnp.int32, (tm, tm), 1)
        tri_ref[...] = jnp.where(row >= col, 1.0, 0.0).astype(BF)
        carry_ref[...] = jnp.zeros_like(carry_ref)

    h = _norm_mod(x_ref[...], g_ref[...], m_ref[0, 3:4, :], m_ref[0, 4:5, :])
    h_ref[...] = h
    logits = jnp.dot(h, wr_ref[...], preferred_element_type=F32, precision=lax.Precision.HIGHEST)
    lane = lax.broadcasted_iota(jnp.int32, logits.shape, 1)
    lg = jnp.where(lane < N_EXPERTS, logits, NEG_INF)
    m1 = jnp.max(lg, axis=1, keepdims=True)
    i1 = jnp.min(jnp.where(lg == m1, lane, LANES), axis=1, keepdims=True)
    lg2 = jnp.where(lane == i1, NEG_INF, lg)
    m2 = jnp.max(lg2, axis=1, keepdims=True)
    i2 = jnp.min(jnp.where(lg2 == m2, lane, LANES), axis=1, keepdims=True)
    e = jnp.exp(m2 - m1)
    g1 = 1.0 / (1.0 + e)
    g2 = e / (1.0 + e)
    oh = jnp.where(lane == i1, 1.0, 0.0) + jnp.where(lane == i2, 1.0, 0.0)
    cum = _dot(tri_ref[...], oh.astype(BF))
    base = carry_ref[0:1, :] + cum - oh
    r1 = jnp.sum(jnp.where(lane == i1, base, 0.0), axis=1, keepdims=True)
    r2 = jnp.sum(jnp.where(lane == i2, base, 0.0), axis=1, keepdims=True)
    carry_ref[...] = carry_ref[...] + cum[tm - 1:tm, :]
    cnt_ref[...] = carry_ref[...]
    meta = jnp.zeros(logits.shape, F32)
    for ln, val in ((META_E1, i1.astype(F32)), (META_E2, i2.astype(F32)), (META_R1, r1), (META_R2, r2),
                    (META_G1, g1), (META_G2, g2)):
        meta = jnp.where(lane == ln, val, meta)
    meta_ref[...] = meta


def _route(x2, g, mpack, mod_of_tile, wr, tm):
    rows = x2.shape[0]
    return pl.pallas_call(
        _route_body,
        grid=(rows // tm,),
        in_specs=[pl.BlockSpec((tm, D), lambda i: (i, 0)),
                  pl.BlockSpec((1, D), lambda i: (0, 0)),
                  pl.BlockSpec((1, MOD_ROWS, D), lambda i: (mod_of_tile(i), 0, 0)),
                  pl.BlockSpec((D, LANES), lambda i: (0, 0))],
        out_specs=[pl.BlockSpec((tm, D), lambda i: (i, 0)),
                   pl.BlockSpec((tm, LANES), lambda i: (i, 0)),
                   pl.BlockSpec((8, LANES), lambda i: (0, 0))],
        out_shape=[jax.ShapeDtypeStruct((rows, D), F32),
                   jax.ShapeDtypeStruct((rows, LANES), F32),
                   jax.ShapeDtypeStruct((8, LANES), F32)],
        scratch_shapes=[pltpu.VMEM((tm, tm), BF), pltpu.VMEM((8, LANES), F32)],
        compiler_params=_params(("arbitrary",)),
        name="moe_route",
    )(x2, g.reshape(1, D), mpack, wr)


def _row_copy(src_ref, src_row, dst_ref, dst_row, sem):
    return pltpu.make_async_copy(src_ref.at[pl.ds(src_row, 1), :], dst_ref.at[pl.ds(dst_row, 1), :], sem)


def _dispatch_body(idx_ref, src_ref, o_ref, sem):
    n = o_ref.shape[0]

    def start(r, c):
        _row_copy(src_ref, idx_ref[0, 0, r], o_ref, r, sem).start()
        return c

    def wait(r, c):
        _row_copy(src_ref, 0, o_ref, r, sem).wait()
        return c

    lax.fori_loop(0, n, start, 0)
    lax.fori_loop(0, n, wait, 0)


def _dispatch(h, tok_of_slot, tm):
    slots = tok_of_slot.shape[0]
    nt = slots // tm
    return pl.pallas_call(
        _dispatch_body,
        grid=(nt,),
        in_specs=[pl.BlockSpec((1, 1, tm), lambda i: (i, 0, 0), memory_space=pltpu.SMEM),
                  pl.BlockSpec(memory_space=pl.ANY)],
        out_specs=pl.BlockSpec((tm, D), lambda i: (i, 0)),
        out_shape=jax.ShapeDtypeStruct((slots, D), F32),
        scratch_shapes=[pltpu.SemaphoreType.DMA(())],
        compiler_params=_params(("arbitrary",)),
        name="moe_dispatch",
    )(tok_of_slot.reshape(nt, 1, tm), h)


def _experts_body(te_ref, nu_ref, xs_ref, wg_ref, wu_ref, wd_ref, o_ref, h_ref, acc_ref):
    i = pl.program_id(0)
    j = pl.program_id(1)
    used = i < nu_ref[0]

    @pl.when(used & (j == 0))
    def _():
        h_ref[...] = xs_ref[...].astype(BF)
        acc_ref[...] = jnp.zeros_like(acc_ref)

    @pl.when(used)
    def _():
        h = h_ref[...]
        a = (_silu(_dot(h, wg_ref[0])) * _dot(h, wu_ref[0])).astype(BF)
        acc_ref[...] += _dot(a, wd_ref[0])

    @pl.when(j == pl.num_programs(1) - 1)
    def _():
        o_ref[...] = jnp.where(used, acc_ref[...], 0.0)


def _experts(xs, tile_expert, n_used, wg, wu, wd):
    slots = xs.shape[0]
    ff = wg.shape[2]
    nf = ff // MOE_TF

    def jj(i, j, nu):
        return jnp.where(i < nu[0], j, nf - 1)

    grid_spec = pltpu.PrefetchScalarGridSpec(
        num_scalar_prefetch=2,
        grid=(slots // MOE_TM, nf),
        in_specs=[pl.BlockSpec((MOE_TM, D), lambda i, j, te, nu: (i, 0)),
                  pl.BlockSpec((1, D, MOE_TF), lambda i, j, te, nu: (te[i], 0, jj(i, j, nu))),
                  pl.BlockSpec((1, D, MOE_TF), lambda i, j, te, nu: (te[i], 0, jj(i, j, nu))),
                  pl.BlockSpec((1, MOE_TF, D), lambda i, j, te, nu: (te[i], jj(i, j, nu), 0))],
        out_specs=pl.BlockSpec((MOE_TM, D), lambda i, j, te, nu: (i, 0)),
        scratch_shapes=[pltpu.VMEM((MOE_TM, D), BF), pltpu.VMEM((MOE_TM, D), F32)])
    return pl.pallas_call(
        _experts_body,
        grid_spec=grid_spec,
        out_shape=jax.ShapeDtypeStruct((slots, D), F32),
        compiler_params=_params(("arbitrary", "arbitrary")),
        name="moe_experts",
    )(tile_expert, n_used, xs, wg, wu, wd)


def _combine_body(s1_ref, s2_ref, ys_ref, x_ref, m_ref, meta_ref, gf_ref, o_ref, y1_ref, y2_ref, sem):
    n = x_ref.shape[0]

    def start(r, c):
        _row_copy(ys_ref, s1_ref[0, 0, r], y1_ref, r, sem).start()
        _row_copy(ys_ref, s2_ref[0, 0, r], y2_ref, r, sem).start()
        return c

    def wait(r, c):
        _row_copy(ys_ref, 0, y1_ref, r, sem).wait()
        _row_copy(ys_ref, 0, y2_ref, r, sem).wait()
        return c

    lax.fori_loop(0, n, start, 0)
    lax.fori_loop(0, n, wait, 0)
    meta = meta_ref[...]
    lane = lax.broadcasted_iota(jnp.int32, meta.shape, 1)
    g1 = jnp.sum(jnp.where(lane == META_G1, meta, 0.0), axis=1, keepdims=True)
    g2 = jnp.sum(jnp.where(lane == META_G2, meta, 0.0), axis=1, keepdims=True)
    y = x_ref[...] + m_ref[0, 5:6, :] * (g1 * y1_ref[...] + g2 * y2_ref[...])
    ms = jnp.mean(y * y, axis=-1, keepdims=True)
    o_ref[...] = y * lax.rsqrt(ms + EPS) * gf_ref[...]


def _combine(x2, mpack, mod_of_row, meta, slot1, slot2, ys, g_final, tm):
    rows = x2.shape[0]
    nt = rows // tm
    smem = lambda: pl.BlockSpec((1, 1, tm), lambda i: (i, 0, 0), memory_space=pltpu.SMEM)
    return pl.pallas_call(
        _combine_body,
        grid=(nt,),
        in_specs=[smem(), smem(),
                  pl.BlockSpec(memory_space=pl.ANY),
                  pl.BlockSpec((tm, D), lambda i: (i, 0)),
                  pl.BlockSpec((1, MOD_ROWS, D), lambda i: (mod_of_row(i * tm), 0, 0)),
                  pl.BlockSpec((tm, LANES), lambda i: (i, 0)),
                  pl.BlockSpec((1, D), lambda i: (0, 0))],
        out_specs=pl.BlockSpec((tm, D), lambda i: (i, 0)),
        out_shape=jax.ShapeDtypeStruct((rows, D), F32),
        scratch_shapes=[pltpu.VMEM((tm, D), F32), pltpu.VMEM((tm, D), F32), pltpu.SemaphoreType.DMA(())],
        compiler_params=_params(("arbitrary",)),
        name="moe_combine",
    )(slot1.reshape(nt, 1, tm), slot2.reshape(nt, 1, tm), ys, x2, mpack, meta, g_final.reshape(1, D))


def _moe(x2, g, mpack, mod_of_tile, mod_of_row, wr, wg, wu, wd, g_final, tm):
    rows = x2.shape[0]
    ne = wg.shape[0]
    h, meta, cnt = _route(x2, g, mpack, mod_of_tile, wr, tm)
    counts = cnt[0, :ne].astype(jnp.int32)
    tiles_e = (counts + MOE_TM - 1) // MOE_TM
    tile_end = jnp.cumsum(tiles_e)
    group_off = (tile_end - tiles_e) * MOE_TM
    n_tiles = 2 * rows // MOE_TM + ne
    tile_expert = jnp.minimum(
        jnp.sum(jnp.arange(n_tiles)[:, None] >= tile_end[None, :], axis=1), ne - 1).astype(jnp.int32)
    n_used = tile_end[-1:].astype(jnp.int32)
    eids = jnp.arange(ne)

    def slot_of(lane_e, lane_r):
        e = meta[:, lane_e].astype(jnp.int32)
        off = jnp.sum(jnp.where(e[:, None] == eids[None, :], group_off[None, :], 0), axis=1)
        return off + meta[:, lane_r].astype(jnp.int32)

    slot1 = slot_of(META_E1, META_R1)
    slot2 = slot_of(META_E2, META_R2)
    tok = jnp.arange(rows, dtype=jnp.int32)
    tok_of_slot = jnp.zeros((n_tiles * MOE_TM,), jnp.int32).at[slot1].set(tok).at[slot2].set(tok)
    xs = _dispatch(h, tok_of_slot, MOE_TM)
    ys = _experts(xs, tile_expert, n_used, wg, wu, wd)
    return _combine(x2, mpack, mod_of_row, meta, slot1, slot2, ys, g_final, MOE_TM)


def _rope_tables(seq_len):
    t = np.arange(seq_len)
    row = (t // GRID_W).astype(np.float32)
    col = (t % GRID_W).astype(np.float32)
    n_axis = HD // 4
    inv = jnp.power(ROPE_THETA, -jnp.arange(n_axis, dtype=F32) / n_axis)
    ang = jnp.concatenate([jnp.asarray(row)[:, None] * inv, jnp.asarray(col)[:, None] * inv], axis=-1)
    cos, sin = jnp.cos(ang), jnp.sin(ang)
    ct = jnp.tile(jnp.concatenate([cos, cos], axis=-1), (1, LANES // HD))
    st = jnp.tile(jnp.concatenate([-sin, sin], axis=-1), (1, LANES // HD))
    return ct, st


def kernel(x, c, ctx, c_ctx, w_mod, b_mod, g_norm1, g_norm2, w_in_ab, g_q, g_k, w_a2_f, b_a_f, w_a2_b,
           b_a_b, g_gla, w_out_ab, w_ff_gate, w_ff_up, w_ff_down, w_in_c, rpb_c, w_out_c, w_router,
           w_moe_gate, w_moe_up, w_moe_down, g_final):
    bsz, seq, _ = x.shape
    nc = ctx.shape[1]
    depth = w_mod.shape[0]
    assert depth == 2 and seq % GRID_W == 0

    cond_rows = -(-(bsz + 1) // 8) * 8
    cond = jnp.zeros((cond_rows, D), F32).at[:bsz].set(c).at[bsz].set(c_ctx)
    mods = _adaln(cond, w_mod, b_mod).reshape(depth, cond_rows, N_MOD, D)
    mods = jnp.pad(mods, ((0, 0), (0, 0), (0, MOD_ROWS - N_MOD), (0, 0)))

    xl = x.reshape(bsz * seq, D)
    xc = ctx.reshape(bsz * nc, D)
    tm_l = min(1024, seq)
    tm_c = min(1024, bsz * nc)
    tpb = seq // tm_l
    lat_mod = lambda i: i // tpb
    ctx_mod = lambda i: bsz

    mp = mods[0]
    w_in = w_in_ab[0]
    perm = np.concatenate([np.arange(0, HD, 2), np.arange(1, HD, 2)])
    nq, nkv = A_HEADS * HD, A_KV * HD
    o_q, o_k, o_v = 0, nq, nq + nkv
    o_bq = o_v + nkv
    o_bk = o_bq + B_HEADS * B_DK
    o_bv = o_bk + B_HEADS * B_DK
    o_br = o_bv + B_HEADS * B_DV
    o_zf = o_br + B_HEADS * B_DV
    o_end = o_zf + 2 * B_RANK
    wq = w_in[:, o_q:o_k].reshape(D, A_HEADS, HD)[:, :, perm].reshape(D, nq)
    wk = w_in[:, o_k:o_v].reshape(D, A_KV, HD)[:, :, perm].reshape(D, nkv)
    wz = jnp.pad(w_in[:, o_zf:o_end], ((0, 0), (0, LANES - 2 * B_RANK)))
    w0 = jnp.concatenate([wq, wk, w_in[:, o_v:o_bq], w_in[:, o_bq:o_bk] * (B_DK ** -0.5),
                          w_in[:, o_bk:o_zf], wz], axis=1).astype(BF)
    widths0 = (nq + nkv, nkv, GLA_W, LANES)
    outw0 = (nq, nkv, 2 * nkv, GLA_W, LANES)
    gqk = jnp.concatenate([jnp.tile(g_q[0][perm] * (HD ** -0.5 * LOG2E), A_HEADS),
                           jnp.tile(g_k[0][perm], A_KV)]).reshape(1, nq + nkv)
    nh = A_HEADS + A_KV
    bd = jnp.asarray(np.kron(np.eye(nh), np.ones((HD, HD))), BF)
    ct, st = _rope_tables(seq)
    ct_c = jnp.ones((tm_c, LANES), F32)
    st_c = jnp.zeros((tm_c, LANES), F32)

    ql, kl, vl, gl, zl = _proj(xl, g_norm1[0], mp, lat_mod, w0, widths0, outw0, tm_l, (ct, st, gqk, bd))
    qc, kc, vc, gc, zc = _proj(xc, g_norm1[0], mp, ctx_mod, w0, widths0, outw0, tm_c, (ct_c, st_c, gqk, bd))

    oa_l = _gqa(ql, kc, vc, kl, vl, bsz, 128)
    oa_c = _gqa(qc, kc, vc, None, None, bsz, min(128, nc))

    w2f = jnp.zeros((LANES, B_HEADS * B_DK), F32).at[:B_RANK].set(w_a2_f[0]).astype(BF)
    w2b = jnp.zeros((LANES, B_HEADS * B_DK), F32).at[B_RANK:2 * B_RANK].set(w_a2_b[0]).astype(BF)
    ob_c, ob_l = _gla(gc, gl, zc, zl, w2f, w2b, b_a_f[0].reshape(1, -1), b_a_b[0].reshape(1, -1),
                      jnp.tile(g_gla[0], B_HEADS).reshape(1, -1), bsz)

    wo = w_out_ab[0].astype(BF)
    xl = _outproj(xl, mp, lat_mod, [(oa_l, wo[:nq]), (ob_l, wo[nq:])], tm_l)
    xc = _outproj(xc, mp, ctx_mod, [(oa_c, wo[:nq]), (ob_c, wo[nq:])], tm_c)

    wg, wu, wd = w_ff_gate[0].astype(BF), w_ff_up[0].astype(BF), w_ff_down[0].astype(BF)
    xl = _ffn(xl, g_norm2[0], mp, lat_mod, wg, wu, wd, tm_l, 512)
    xc = _ffn(xc, g_norm2[0], mp, ctx_mod, wg, wu, wd, tm_c, 512)

    mp = mods[1]
    cw = C_HEADS * HD
    w1 = w_in_c[0].astype(BF)
    q1, k1, v1 = _proj(xl, g_norm1[1], mp, lat_mod, w1, (cw, cw, cw), (cw, cw, cw), tm_l,
                       scales=(HD ** -0.5 * LOG2E, 1.0, 1.0))
    k1c, v1c = _proj(xc, g_norm1[1], mp, ctx_mod, w1[:, cw:], (cw, cw), (cw, cw), tm_c)

    bias = _na_bias(rpb_c[0], seq // GRID_W) * LOG2E
    o1 = _natten(q1, k1, v1, k1c, v1c, bias, bsz)
    xl = _outproj(xl, mp, lat_mod, [(o1, w_out_c[0].astype(BF))], tm_l)

    wr = jnp.pad(w_router[0], ((0, 0), (0, LANES - N_EXPERTS)))
    out = _moe(xl, g_norm2[1], mp, lat_mod, lambda r: r // seq, wr, w_moe_gate[0].astype(BF),
               w_moe_up[0].astype(BF), w_moe_down[0].astype(BF), g_final, tm_l)
    return out.reshape(bsz, seq, D)
```

```python
import functools

import numpy as np
import jax
import jax.numpy as jnp
from jax import lax
from jax.experimental import pallas as pl
from jax.experimental.pallas import tpu as pltpu

F32 = jnp.float32
BF = jnp.bfloat16

D = 1024
EPS = 1e-6
N_MOD = 6
MOD_ROWS = 8
GRID_W = 64
HD = 64
A_HEADS = 8
A_KV = 2
A_GROUP = A_HEADS // A_KV
ROPE_THETA = 10000.0
LOG2E = 1.4426950408889634
B_HEADS = 4
B_DK = 64
B_DV = 128
B_RANK = 16
B_TAU = 16.0
B_CHUNK = 64
C_HEADS = 16
NA_ROWS = 8
NA_COLS = 16
N_EXPERTS = 8
LANES = 128
VMEM_LIMIT = 56 * 2 ** 20

NEG_INF = float("-inf")


def _params(sem):
    return pltpu.CompilerParams(dimension_semantics=sem, vmem_limit_bytes=VMEM_LIMIT)


def _dot(a, b):
    return jnp.dot(a, b, preferred_element_type=F32)


def _dot_nt(a, b):
    return lax.dot_general(a, b, (((1,), (1,)), ((), ())), preferred_element_type=F32)


def _dot_tn(a, b):
    return lax.dot_general(a, b, (((0,), (0,)), ((), ())), preferred_element_type=F32)


def _silu(x):
    return x / (1.0 + jnp.exp(-x))


def _split_dot(lhs_bf_exact, x):
    hi = x.astype(BF)
    lo = (x - hi.astype(F32)).astype(BF)
    return _dot(lhs_bf_exact, hi) + _dot(lhs_bf_exact, lo)


def _norm_mod(x, g, shift, scale):
    ms = jnp.mean(x * x, axis=-1, keepdims=True)
    return (x * lax.rsqrt(ms + EPS) * g) * (1.0 + scale) + shift


def _adaln_body(c_ref, w_ref, b_ref, o_ref):
    s = _silu(c_ref[...])
    o_ref[0] = _dot(s.astype(BF), w_ref[0].astype(BF)) + b_ref[0]


def _adaln(cond, w_mod, b_mod):
    depth, _, n = w_mod.shape
    rows = cond.shape[0]
    tn = 1536
    return pl.pallas_call(
        _adaln_body,
        grid=(depth, n // tn),
        in_specs=[pl.BlockSpec((rows, D), lambda l, j: (0, 0)),
                  pl.BlockSpec((1, D, tn), lambda l, j: (l, 0, j)),
                  pl.BlockSpec((1, 1, tn), lambda l, j: (l, 0, j))],
        out_specs=pl.BlockSpec((1, rows, tn), lambda l, j: (l, 0, j)),
        out_shape=jax.ShapeDtypeStruct((depth, rows, n), F32),
        compiler_params=_params(("arbitrary", "arbitrary")),
        name="adaln",
    )(cond, w_mod, b_mod.reshape(depth, 1, n))


def _qk_norm_rope(p, gqk, bd, ct, st):
    ss = _split_dot_rhs(p * p, bd)
    y = p * lax.rsqrt(ss * (1.0 / HD) + EPS) * gqk
    lane = lax.broadcasted_iota(jnp.int32, ct.shape, 1)
    first = (lane % HD) < (HD // 2)
    outs = []
    for c in range(p.shape[1] // LANES):
        yc = y[:, c * LANES:(c + 1) * LANES]
        partner = jnp.where(first, pltpu.roll(yc, LANES - HD // 2, 1), pltpu.roll(yc, HD // 2, 1))
        outs.append(yc * ct + partner * st)
    return jnp.concatenate(outs, axis=1)


def _split_dot_rhs(x, rhs_bf_exact):
    hi = x.astype(BF)
    lo = (x - hi.astype(F32)).astype(BF)
    return _dot(hi, rhs_bf_exact) + _dot(lo, rhs_bf_exact)


def _proj_body(x_ref, g_ref, m_ref, w_ref, *rest, widths, rope, scales):
    if rope:
        ct_ref, st_ref, gqk_ref, bd_ref = rest[:4]
        o_refs = rest[4:]
    else:
        o_refs = rest
    h = _norm_mod(x_ref[...], g_ref[...], m_ref[0, 0:1, :], m_ref[0, 1:2, :]).astype(BF)
    off = 0
    oi = 0
    for gi, wd in enumerate(widths):
        p = _dot(h, w_ref[:, off:off + wd])
        off += wd
        if scales[gi] != 1.0:
            p = p * scales[gi]
        if rope and gi == 0:
            p = _qk_norm_rope(p, gqk_ref[...], bd_ref[...], ct_ref[...], st_ref[...])
            nq = A_HEADS * HD
            o_refs[0][...] = p[:, :nq].astype(BF)
            o_refs[1][...] = p[:, nq:].astype(BF)
            oi = 2
        elif rope and gi == 1:
            ones = jnp.ones((p.shape[0], HD), F32)
            parts = []
            for j in range(A_KV):
                parts += [p[:, HD * j:HD * (j + 1)], ones]
            o_refs[oi][...] = jnp.concatenate(parts, axis=1).astype(BF)
            oi += 1
        else:
            o_refs[oi][...] = p.astype(BF)
            oi += 1


def _proj(x2, g, mpack, mod_of_tile, w, widths, out_widths, tm, rope_args=None, scales=None):
    rows = x2.shape[0]
    n = w.shape[1]
    in_specs = [pl.BlockSpec((tm, D), lambda i: (i, 0)),
                pl.BlockSpec((1, D), lambda i: (0, 0)),
                pl.BlockSpec((1, MOD_ROWS, D), lambda i: (mod_of_tile(i), 0, 0)),
                pl.BlockSpec((D, n), lambda i: (0, 0))]
    args = [x2, g.reshape(1, D), mpack, w]
    if rope_args is not None:
        ct, st, gqk, bd = rope_args
        nt = ct.shape[0] // tm
        in_specs += [pl.BlockSpec((tm, LANES), lambda i: (i % nt, 0)),
                     pl.BlockSpec((tm, LANES), lambda i: (i % nt, 0)),
                     pl.BlockSpec(gqk.shape, lambda i: (0, 0)),
                     pl.BlockSpec(bd.shape, lambda i: (0, 0))]
        args += [ct, st, gqk, bd]
    return pl.pallas_call(
        functools.partial(_proj_body, widths=widths, rope=rope_args is not None,
                          scales=scales or (1.0,) * len(widths)),
        grid=(rows // tm,),
        in_specs=in_specs,
        out_specs=[pl.BlockSpec((tm, ow), lambda i: (i, 0)) for ow in out_widths],
        out_shape=[jax.ShapeDtypeStruct((rows, ow), BF) for ow in out_widths],
        compiler_params=_params(("arbitrary",)),
        name="norm_mod_proj",
    )(*args)


def _gqa_body(q_ref, kc_ref, vc_ref, *rest, with_lat, tq):
    if with_lat:
        kl_ref, vl_ref, o_ref = rest
    else:
        (o_ref,) = rest
    for j in range(A_KV):
        cols = slice(HD * j, HD * (j + 1))
        vcols = slice(2 * HD * j, 2 * HD * (j + 1))
        qg = jnp.concatenate(
            [q_ref[:, HD * (A_GROUP * j + g):HD * (A_GROUP * j + g + 1)] for g in range(A_GROUP)], axis=0)
        s_c = _dot_nt(qg, kc_ref[:, cols])
        m = jnp.max(s_c, axis=1, keepdims=True)
        if with_lat:
            s_l = _dot_nt(qg, kl_ref[:, cols])
            m = jnp.maximum(m, jnp.max(s_l, axis=1, keepdims=True))
        acc = _dot(jnp.exp2(s_c - m).astype(BF), vc_ref[:, vcols])
        if with_lat:
            acc = acc + _dot(jnp.exp2(s_l - m).astype(BF), vl_ref[:, vcols])
        o = acc[:, :HD] / acc[:, HD:HD + 1]
        for g in range(A_GROUP):
            h = A_GROUP * j + g
            o_ref[:, HD * h:HD * (h + 1)] = o[g * tq:(g + 1) * tq].astype(BF)


def _gqa(q, k_ctx, v_ctx, k_lat, v_lat, bsz, tq):
    sq = q.shape[0] // bsz
    nc = k_ctx.shape[0] // bsz
    nq = sq // tq
    with_lat = k_lat is not None
    in_specs = [pl.BlockSpec((tq, A_HEADS * HD), lambda b, i: (b * nq + i, 0)),
                pl.BlockSpec((nc, A_KV * HD), lambda b, i: (b, 0)),
                pl.BlockSpec((nc, 2 * A_KV * HD), lambda b, i: (b, 0))]
    args = [q, k_ctx, v_ctx]
    if with_lat:
        sl = k_lat.shape[0] // bsz
        in_specs += [pl.BlockSpec((sl, A_KV * HD), lambda b, i: (b, 0)),
                     pl.BlockSpec((sl, 2 * A_KV * HD), lambda b, i: (b, 0))]
        args += [k_lat, v_lat]
    return pl.pallas_call(
        functools.partial(_gqa_body, with_lat=with_lat, tq=tq),
        grid=(bsz, nq),
        in_specs=in_specs,
        out_specs=pl.BlockSpec((tq, A_HEADS * HD), lambda b, i: (b * nq + i, 0)),
        out_shape=jax.ShapeDtypeStruct(q.shape, BF),
        compiler_params=_params(("arbitrary", "arbitrary")),
        name="gqa_attention",
    )(*args)


GLA_Q = 0
GLA_K = B_HEADS * B_DK
GLA_V = 2 * B_HEADS * B_DK
GLA_R = GLA_V + B_HEADS * B_DV
GLA_W = GLA_R + B_HEADS * B_DV
GLA_GROUP = 4


def _gla_body(gc_ref, gl_ref, zc_ref, zl_ref, w2f_ref, w2b_ref, bf_ref, bb_ref, gg_ref,
              oc_ref, ol_ref, ofc_ref, ofl_ref, st_ref):
    nk = B_HEADS * B_DK
    ch = B_CHUNK
    row = lax.broadcasted_iota(jnp.int32, (ch, ch), 0)
    col = lax.broadcasted_iota(jnp.int32, (ch, ch), 1)
    lane = lax.broadcasted_iota(jnp.int32, (ch, nk), 1)
    head_masks = [(lane // B_DK) == h for h in range(B_HEADS)]

    def chunk_out(g_ref, z_ref, r0, fwd, s_t):
        blk = g_ref[pl.ds(r0, ch), :]
        q = blk[:, GLA_Q:GLA_Q + nk].astype(F32)
        k = blk[:, GLA_K:GLA_K + nk].astype(F32)
        v = blk[:, GLA_V:GLA_V + B_HEADS * B_DV]
        z = z_ref[pl.ds(r0, ch), :]
        pre = _dot(z, (w2f_ref if fwd else w2b_ref)[...]) + (bf_ref if fwd else bb_ref)[...]
        la = -(jnp.maximum(-pre, 0.0) + jnp.log(1.0 + jnp.exp(-jnp.abs(pre)))) * (1.0 / B_TAU)
        keep = (row >= col) if fwd else (row <= col)
        bcum = _split_dot(keep.astype(BF), la)
        b_last = bcum[ch - 1:ch, :] if fwd else bcum[0:1, :]
        q_dec = q * jnp.exp(bcum)
        k_intra = (k * jnp.exp(-bcum)).astype(BF)
        k_state = k * jnp.exp(b_last - bcum)
        s_b = s_t.astype(BF)
        d_s = jnp.zeros_like(s_t)
        outs = []
        for h in range(B_HEADS):
            qm = jnp.where(head_masks[h], q_dec, 0.0).astype(BF)
            att = jnp.where(keep, _dot_nt(qm, k_intra), 0.0)
            vh = v[:, B_DV * h:B_DV * (h + 1)]
            outs.append(_dot(att.astype(BF), vh) + _dot_nt(qm, s_b))
            ksm = jnp.where(head_masks[h], k_state, 0.0).astype(BF)
            d_s = d_s + _dot_tn(vh, ksm)
        return jnp.concatenate(outs, axis=1), blk, jnp.exp(b_last) * s_t + d_s

    def fwd_pass(g_ref, z_ref, of_ref, n):
        def body(i, carry):
            s_t = st_ref[...]
            for u in range(GLA_GROUP):
                r0 = pl.multiple_of((i * GLA_GROUP + u) * ch, ch)
                o, _, s_t = chunk_out(g_ref, z_ref, r0, True, s_t)
                of_ref[pl.ds(r0, ch), :] = o
            st_ref[...] = s_t
            return carry
        lax.fori_loop(0, n // GLA_GROUP, body, 0)

    def bwd_pass(g_ref, z_ref, of_ref, o_ref, n):
        def body(i, carry):
            s_t = st_ref[...]
            for u in range(GLA_GROUP):
                r0 = pl.multiple_of((n - 1 - (i * GLA_GROUP + u)) * ch, ch)
                o, blk, s_t = chunk_out(g_ref, z_ref, r0, False, s_t)
                o = o + of_ref[pl.ds(r0, ch), :]
                parts = []
                for h in range(B_HEADS):
                    oh = o[:, B_DV * h:B_DV * (h + 1)]
                    ms = jnp.mean(oh * oh, axis=-1, keepdims=True)
                    parts.append(oh * lax.rsqrt(ms + EPS))
                y = jnp.concatenate(parts, axis=1) * gg_ref[...]
                r = blk[:, GLA_R:GLA_R + B_HEADS * B_DV].astype(F32)
                o_ref[pl.ds(r0, ch), :] = (y * _silu(r)).astype(BF)
            st_ref[...] = s_t
            return carry
        lax.fori_loop(0, n // GLA_GROUP, body, 0)

    nc = gc_ref.shape[0] // ch
    nl = gl_ref.shape[0] // ch
    assert nc % GLA_GROUP == 0 and nl % GLA_GROUP == 0
    st_ref[...] = jnp.zeros_like(st_ref)
    fwd_pass(gc_ref, zc_ref, ofc_ref, nc)
    fwd_pass(gl_ref, zl_ref, ofl_ref, nl)
    st_ref[...] = jnp.zeros_like(st_ref)
    bwd_pass(gc_ref, zc_ref, ofc_ref, oc_ref, nc)
    bwd_pass(gl_ref, zl_ref, ofl_ref, ol_ref, nl)


def _gla(g_ctx, g_lat, z_ctx, z_lat, w2f, w2b, b_f, b_b, gg, bsz):
    nc = g_ctx.shape[0] // bsz
    sl = g_lat.shape[0] // bsz
    nk = B_HEADS * B_DK
    nv = B_HEADS * B_DV
    full = lambda shape: pl.BlockSpec(shape, lambda b: (0, 0))
    return pl.pallas_call(
        _gla_body,
        grid=(bsz,),
        in_specs=[pl.BlockSpec((nc, GLA_W), lambda b: (b, 0)),
                  pl.BlockSpec((sl, GLA_W), lambda b: (b, 0)),
                  pl.BlockSpec((nc, LANES), lambda b: (b, 0)),
                  pl.BlockSpec((sl, LANES), lambda b: (b, 0)),
                  full((LANES, nk)), full((LANES, nk)), full((1, nk)), full((1, nk)), full((1, nv))],
        out_specs=[pl.BlockSpec((nc, nv), lambda b: (b, 0)),
                   pl.BlockSpec((sl, nv), lambda b: (b, 0))],
        out_shape=[jax.ShapeDtypeStruct((g_ctx.shape[0], nv), BF),
                   jax.ShapeDtypeStruct((g_lat.shape[0], nv), BF)],
        scratch_shapes=[pltpu.VMEM((nc, nv), F32), pltpu.VMEM((sl, nv), F32),
                        pltpu.VMEM((B_DV, nk), F32)],
        compiler_params=_params(("arbitrary",)),
        name="gla_bidir",
    )(g_ctx, g_lat, z_ctx, z_lat, w2f, w2b, b_f, b_b, gg)


def _outproj_body(x_ref, m_ref, *rest, n_in):
    o_ref = rest[-1]
    acc = _dot(rest[0][...], rest[1][...])
    for t in range(1, n_in):
        acc = acc + _dot(rest[2 * t][...], rest[2 * t + 1][...])
    o_ref[...] = x_ref[...] + m_ref[0, 2:3, :] * acc


def _outproj(x2, mpack, mod_of_tile, pairs, tm):
    rows = x2.shape[0]
    in_specs = [pl.BlockSpec((tm, D), lambda i: (i, 0)),
                pl.BlockSpec((1, MOD_ROWS, D), lambda i: (mod_of_tile(i), 0, 0))]
    args = [x2, mpack]
    for a, w in pairs:
        in_specs += [pl.BlockSpec((tm, a.shape[1]), lambda i: (i, 0)),
                     pl.BlockSpec(w.shape, lambda i: (0, 0))]
        args += [a, w]
    return pl.pallas_call(
        functools.partial(_outproj_body, n_in=len(pairs)),
        grid=(rows // tm,),
        in_specs=in_specs,
        out_specs=pl.BlockSpec((tm, D), lambda i: (i, 0)),
        out_shape=jax.ShapeDtypeStruct((rows, D), F32),
        compiler_params=_params(("arbitrary",)),
        name="outproj_residual",
    )(*args)


def _ffn_body(x_ref, g_ref, m_ref, wg_ref, wu_ref, wd_ref, o_ref, h_ref, acc_ref):
    j = pl.program_id(1)

    @pl.when(j == 0)
    def _():
        h_ref[...] = _norm_mod(x_ref[...], g_ref[...], m_ref[0, 3:4, :], m_ref[0, 4:5, :]).astype(BF)
        acc_ref[...] = jnp.zeros_like(acc_ref)

    h = h_ref[...]
    a = (_silu(_dot(h, wg_ref[...])) * _dot(h, wu_ref[...])).astype(BF)
    acc_ref[...] += _dot(a, wd_ref[...])

    @pl.when(j == pl.num_programs(1) - 1)
    def _():
        o_ref[...] = x_ref[...] + m_ref[0, 5:6, :] * acc_ref[...]


def _ffn(x2, g, mpack, mod_of_tile, wg, wu, wd, tm, tf):
    rows = x2.shape[0]
    ff = wg.shape[1]
    return pl.pallas_call(
        _ffn_body,
        grid=(rows // tm, ff // tf),
        in_specs=[pl.BlockSpec((tm, D), lambda i, j: (i, 0)),
                  pl.BlockSpec((1, D), lambda i, j: (0, 0)),
                  pl.BlockSpec((1, MOD_ROWS, D), lambda i, j: (mod_of_tile(i), 0, 0)),
                  pl.BlockSpec((D, tf), lambda i, j: (0, j)),
                  pl.BlockSpec((D, tf), lambda i, j: (0, j)),
                  pl.BlockSpec((tf, D), lambda i, j: (j, 0))],
        out_specs=pl.BlockSpec((tm, D), lambda i, j: (i, 0)),
        out_shape=jax.ShapeDtypeStruct((rows, D), F32),
        scratch_shapes=[pltpu.VMEM((tm, D), BF), pltpu.VMEM((tm, D), F32)],
        compiler_params=_params(("arbitrary", "arbitrary")),
        name="ffn_swiglu",
    )(x2, g.reshape(1, D), mpack, wg, wu, wd)


NA_QROWS = 4
NA_KROWS = 12
NA_TQ = NA_QROWS * GRID_W
NA_HG = 8
NA_W = NA_HG * HD


def _na_body(q_ref, k0_ref, k1_ref, k2_ref, v0_ref, v1_ref, v2_ref, kc_ref, vc_ref, b_ref, o_ref):
    pair = LANES // HD
    for h0 in range(0, NA_HG, pair):
        outs = []
        for hh in range(h0, h0 + pair):
            cols = slice(HD * hh, HD * (hh + 1))
            q = q_ref[:, cols]
            s_loc = jnp.concatenate([_dot_nt(q, kr[:, cols]) for kr in (k0_ref, k1_ref, k2_ref)], axis=1)
            s_loc = s_loc + b_ref[0, hh]
            s_ctx = _dot_nt(q, kc_ref[:, cols])
            m = jnp.maximum(jnp.max(s_loc, axis=1, keepdims=True), jnp.max(s_ctx, axis=1, keepdims=True))
            p_loc = jnp.exp2(s_loc - m)
            p_ctx = jnp.exp2(s_ctx - m)
            den = jnp.sum(p_loc, axis=1, keepdims=True) + jnp.sum(p_ctx, axis=1, keepdims=True)
            acc = _dot(p_ctx.astype(BF), vc_ref[:, cols])
            for t, vr in enumerate((v0_ref, v1_ref, v2_ref)):
                acc = acc + _dot(p_loc[:, t * NA_TQ:(t + 1) * NA_TQ].astype(BF), vr[:, cols])
            outs.append(acc / den)
        o_ref[:, HD * h0:HD * (h0 + pair)] = jnp.concatenate(outs, axis=1).astype(BF)


def _na_bias(rpb, rows):
    heads, n_dr, n_dc = rpb.shape
    nblk = rows // NA_QROWS
    pad_l = GRID_W - NA_COLS
    rext = jnp.pad(rpb.astype(F32), ((0, 0), (0, 0), (pad_l, 2 * GRID_W - n_dc - pad_l)))
    flat = jnp.tile(rext, (1, 1, GRID_W))[:, :, :GRID_W * (2 * GRID_W - 1)]
    toe = flat.reshape(heads, n_dr, GRID_W, 2 * GRID_W - 1)[:, :, :, GRID_W - 1:]
    cols = np.arange(GRID_W)
    cs = np.clip(cols - NA_COLS // 2, 0, GRID_W - NA_COLS)
    col_ok = (cols[None, :] >= cs[:, None]) & (cols[None, :] < cs[:, None] + NA_COLS)
    toe = jnp.where(col_ok, toe, NEG_INF)
    dead = jnp.full((heads, GRID_W, GRID_W), NEG_INF, F32)
    kinds = []
    for j in (0, min(1, nblk - 1), nblk - 1):
        ws = NA_QROWS * int(np.clip(j - 1, 0, nblk - 3))
        q_rows = []
        for qr in range(NA_QROWS):
            r = NA_QROWS * j + qr
            rs = int(np.clip(r - NA_ROWS // 2, 0, rows - NA_ROWS))
            blocks = [toe[:, ws + kr - r + NA_ROWS - 1] if rs <= ws + kr < rs + NA_ROWS else dead
                      for kr in range(NA_KROWS)]
            q_rows.append(jnp.concatenate(blocks, axis=-1))
        kinds.append(jnp.concatenate(q_rows, axis=-2))
    return jnp.stack(kinds)


def _natten(q, k, v, k_ctx, v_ctx, bias, bsz):
    s = q.shape[0] // bsz
    nc = k_ctx.shape[0] // bsz
    nblk = s // NA_TQ
    ngrp = C_HEADS // NA_HG

    def kv_spec(t):
        return pl.BlockSpec((NA_TQ, NA_W),
                            lambda j, g, b: (b * nblk + jnp.clip(j - 1, 0, nblk - 3) + t, g))

    def kind(j):
        return jnp.where(j == 0, 0, jnp.where(j == nblk - 1, 2, 1))

    return pl.pallas_call(
        _na_body,
        grid=(nblk, ngrp, bsz),
        in_specs=[pl.BlockSpec((NA_TQ, NA_W), lambda j, g, b: (b * nblk + j, g)),
                  kv_spec(0), kv_spec(1), kv_spec(2), kv_spec(0), kv_spec(1), kv_spec(2),
                  pl.BlockSpec((nc, NA_W), lambda j, g, b: (b, g)),
                  pl.BlockSpec((nc, NA_W), lambda j, g, b: (b, g)),
                  pl.BlockSpec((1, NA_HG, NA_TQ, NA_KROWS * GRID_W), lambda j, g, b: (kind(j), g, 0, 0))],
        out_specs=pl.BlockSpec((NA_TQ, NA_W), lambda j, g, b: (b * nblk + j, g)),
        out_shape=jax.ShapeDtypeStruct(q.shape, BF),
        compiler_params=_params(("arbitrary", "arbitrary", "arbitrary")),
        name="natten",
    )(q, k, k, k, v, v, v, k_ctx, v_ctx, bias)


MOE_TM = 512
MOE_TF = 512
META_E1, META_E2, META_R1, META_R2, META_G1, META_G2 = range(6)
ROW_TILE = D // LANES


def _to_token_tiles(dst_ref, value):
    n = value.shape[0]
    for s in range(ROW_TILE):
        dst_ref[pl.ds(s, n, stride=ROW_TILE), :] = value[:, LANES * s:LANES * (s + 1)]


def _from_token_tiles(src_ref, n):
    return jnp.concatenate([src_ref[pl.ds(s, n, stride=ROW_TILE), :] for s in range(ROW_TILE)], axis=1)


def _route_body(x_ref, g_ref, m_ref, wr_ref, h_ref, meta_ref, cnt_ref, tri_ref, carry_ref):
    tm = x_ref.shape[0]

    @pl.when(pl.program_id(0) == 0)
    def _():
        row = lax.broadcasted_iota(jnp.int32, (tm, tm), 0)
        col = lax.broadcasted_iota(jnp.int32, (tm, tm), 1)
        tri_ref[...] = jnp.where(row >= col, 1.0, 0.0).astype(BF)
        carry_ref[...] = jnp.zeros_like(carry_ref)

    h = _norm_mod(x_ref[...], g_ref[...], m_ref[0, 3:4, :], m_ref[0, 4:5, :])
    _to_token_tiles(h_ref, h)
    logits = jnp.dot(h, wr_ref[...], preferred_element_type=F32, precision=lax.Precision.HIGHEST)
    lane = lax.broadcasted_iota(jnp.int32, logits.shape, 1)
    lg = jnp.where(lane < N_EXPERTS, logits, NEG_INF)
    m1 = jnp.max(lg, axis=1, keepdims=True)
    i1 = jnp.min(jnp.where(lg == m1, lane, LANES), axis=1, keepdims=True)
    lg2 = jnp.where(lane == i1, NEG_INF, lg)
    m2 = jnp.max(lg2, axis=1, keepdims=True)
    i2 = jnp.min(jnp.where(lg2 == m2, lane, LANES), axis=1, keepdims=True)
    e = jnp.exp(m2 - m1)
    g1 = 1.0 / (1.0 + e)
    g2 = e / (1.0 + e)
    oh = jnp.where(lane == i1, 1.0, 0.0) + jnp.where(lane == i2, 1.0, 0.0)
    cum = _dot(tri_ref[...], oh.astype(BF))
    base = carry_ref[0:1, :] + cum - oh
    r1 = jnp.sum(jnp.where(lane == i1, base, 0.0), axis=1, keepdims=True)
    r2 = jnp.sum(jnp.where(lane == i2, base, 0.0), axis=1, keepdims=True)
    carry_ref[...] = carry_ref[...] + cum[tm - 1:tm, :]
    cnt_ref[...] = carry_ref[...]
    meta = jnp.zeros(logits.shape, F32)
    for ln, val in ((META_E1, i1.astype(F32)), (META_E2, i2.astype(F32)), (META_R1, r1), (META_R2, r2),
                    (META_G1, g1), (META_G2, g2)):
        meta = jnp.where(lane == ln, val, meta)
    meta_ref[...] = meta


def _route(x2, g, mpack, mod_of_tile, wr, tm):
    rows = x2.shape[0]
    return pl.pallas_call(
        _route_body,
        grid=(rows // tm,),
        in_specs=[pl.BlockSpec((tm, D), lambda i: (i, 0)),
                  pl.BlockSpec((1, D), lambda i: (0, 0)),
                  pl.BlockSpec((1, MOD_ROWS, D), lambda i: (mod_of_tile(i), 0, 0)),
                  pl.BlockSpec((D, LANES), lambda i: (0, 0))],
        out_specs=[pl.BlockSpec((tm * ROW_TILE, LANES), lambda i: (i, 0)),
                   pl.BlockSpec((tm, LANES), lambda i: (i, 0)),
                   pl.BlockSpec((8, LANES), lambda i: (0, 0))],
        out_shape=[jax.ShapeDtypeStruct((rows * ROW_TILE, LANES), F32),
                   jax.ShapeDtypeStruct((rows, LANES), F32),
                   jax.ShapeDtypeStruct((8, LANES), F32)],
        scratch_shapes=[pltpu.VMEM((tm, tm), BF), pltpu.VMEM((8, LANES), F32)],
        compiler_params=_params(("arbitrary",)),
        name="moe_route",
    )(x2, g.reshape(1, D), mpack, wr)


def _token_copy(src_ref, src_tok, dst_ref, dst_tok, sem):
    src = src_ref.at[pl.ds(pl.multiple_of(src_tok * ROW_TILE, ROW_TILE), ROW_TILE), :]
    dst = dst_ref.at[pl.ds(pl.multiple_of(dst_tok * ROW_TILE, ROW_TILE), ROW_TILE), :]
    return pltpu.make_async_copy(src, dst, sem)


def _experts_body(te_ref, nu_ref, cur_ref, nxt_ref, dst_ref, h_hbm, wg_ref, wu_ref, wd_ref, y_hbm,
                  xbuf, h_ref, acc_ref, ybuf, gsem, ssem):
    i = pl.program_id(0)
    j = pl.program_id(1)
    last_j = pl.num_programs(1) - 1
    n_used = nu_ref[0]
    used = i < n_used
    slot = i % 2

    def start_gather(idx_ref, sl):
        def body(r, c):
            _token_copy(h_hbm, idx_ref[0, 0, r], xbuf.at[sl], r, gsem.at[sl]).start()
            return c
        lax.fori_loop(0, MOE_TM, body, 0, unroll=8)

    def wait_gather(sl):
        pltpu.make_async_copy(xbuf.at[sl], xbuf.at[sl], gsem.at[sl]).wait()

    def start_scatter():
        def body(r, c):
            _token_copy(ybuf, r, y_hbm, dst_ref[0, 0, r], ssem).start()
            return c
        lax.fori_loop(0, MOE_TM, body, 0, unroll=8)

    def wait_scatter():
        pltpu.make_async_copy(ybuf, ybuf, ssem).wait()

    @pl.when((j == 0) & (i == 0))
    def _():
        ybuf[...] = jnp.zeros_like(ybuf)
        n_spare = MOE_TM * ROW_TILE
        spare = pltpu.make_async_copy(ybuf, y_hbm.at[pl.ds(y_hbm.shape[0] - n_spare, n_spare), :], ssem)
        spare.start()
        spare.wait()
        start_gather(cur_ref, 0)

    @pl.when((j == 0) & used)
    def _():
        wait_gather(slot)
        h_ref[...] = _from_token_tiles(xbuf.at[slot], MOE_TM).astype(BF)
        acc_ref[...] = jnp.zeros_like(acc_ref)

    @pl.when((j == 0) & (i + 1 < n_used))
    def _():
        start_gather(nxt_ref, 1 - slot)

    @pl.when(used)
    def _():
        h = h_ref[...]
        a = (_silu(_dot(h, wg_ref[0])) * _dot(h, wu_ref[0])).astype(BF)
        acc_ref[...] += _dot(a, wd_ref[0])

    @pl.when((j == last_j) & used)
    def _():
        @pl.when(i > 0)
        def _():
            wait_scatter()
        _to_token_tiles(ybuf, acc_ref[...])
        start_scatter()

    @pl.when((j == last_j) & (i == pl.num_programs(0) - 1))
    def _():
        wait_scatter()


def _experts(h_tiles, tok_of_slot, dest_of_slot, tile_expert, n_used, wg, wu, wd, out_tokens):
    n_tiles = tile_expert.shape[0]
    ff = wg.shape[2]
    nf = ff // MOE_TF

    def jj(i, j, nu):
        return jnp.where(i < nu[0], j, nf - 1)

    def smem_rows(index):
        return pl.BlockSpec((1, 1, MOE_TM), lambda i, j, te, nu: (index(i), 0, 0), memory_space=pltpu.SMEM)

    grid_spec = pltpu.PrefetchScalarGridSpec(
        num_scalar_prefetch=2,
        grid=(n_tiles, nf),
        in_specs=[smem_rows(lambda i: i),
                  smem_rows(lambda i: jnp.minimum(i + 1, n_tiles - 1)),
                  smem_rows(lambda i: i),
                  pl.BlockSpec(memory_space=pl.ANY),
                  pl.BlockSpec((1, D, MOE_TF), lambda i, j, te, nu: (te[i], 0, jj(i, j, nu))),
                  pl.BlockSpec((1, D, MOE_TF), lambda i, j, te, nu: (te[i], 0, jj(i, j, nu))),
                  pl.BlockSpec((1, MOE_TF, D), lambda i, j, te, nu: (te[i], jj(i, j, nu), 0))],
        out_specs=pl.BlockSpec(memory_space=pl.ANY),
        scratch_shapes=[pltpu.VMEM((2, MOE_TM * ROW_TILE, LANES), F32),
                        pltpu.VMEM((MOE_TM, D), BF),
                        pltpu.VMEM((MOE_TM, D), F32),
                        pltpu.VMEM((MOE_TM * ROW_TILE, LANES), F32),
                        pltpu.SemaphoreType.DMA((2,)),
                        pltpu.SemaphoreType.DMA(())])
    tok3 = tok_of_slot.reshape(n_tiles, 1, MOE_TM)
    return pl.pallas_call(
        _experts_body,
        grid_spec=grid_spec,
        out_shape=jax.ShapeDtypeStruct((out_tokens * ROW_TILE, LANES), F32),
        compiler_params=_params(("arbitrary", "arbitrary")),
        name="moe_experts",
    )(tile_expert, n_used, tok3, tok3, dest_of_slot.reshape(n_tiles, 1, MOE_TM), h_tiles, wg, wu, wd)


def _combine_body(y1_ref, y2_ref, x_ref, m_ref, meta_ref, gf_ref, o_ref):
    n = x_ref.shape[0]
    meta = meta_ref[...]
    lane = lax.broadcasted_iota(jnp.int32, meta.shape, 1)
    g1 = jnp.sum(jnp.where(lane == META_G1, meta, 0.0), axis=1, keepdims=True)
    g2 = jnp.sum(jnp.where(lane == META_G2, meta, 0.0), axis=1, keepdims=True)
    moe = g1 * _from_token_tiles(y1_ref, n) + g2 * _from_token_tiles(y2_ref, n)
    y = x_ref[...] + m_ref[0, 5:6, :] * moe
    ms = jnp.mean(y * y, axis=-1, keepdims=True)
    o_ref[...] = y * lax.rsqrt(ms + EPS) * gf_ref[...]


def _combine(x2, mpack, mod_of_row, meta, y_tiles, g_final, tm):
    rows = x2.shape[0]
    nt = rows // tm
    return pl.pallas_call(
        _combine_body,
        grid=(nt,),
        in_specs=[pl.BlockSpec((tm * ROW_TILE, LANES), lambda i: (i, 0)),
                  pl.BlockSpec((tm * ROW_TILE, LANES), lambda i: (i + nt, 0)),
                  pl.BlockSpec((tm, D), lambda i: (i, 0)),
                  pl.BlockSpec((1, MOD_ROWS, D), lambda i: (mod_of_row(i * tm), 0, 0)),
                  pl.BlockSpec((tm, LANES), lambda i: (i, 0)),
                  pl.BlockSpec((1, D), lambda i: (0, 0))],
        out_specs=pl.BlockSpec((tm, D), lambda i: (i, 0)),
        out_shape=jax.ShapeDtypeStruct((rows, D), F32),
        compiler_params=_params(("arbitrary",)),
        name="moe_combine",
    )(y_tiles, y_tiles, x2, mpack, meta, g_final.reshape(1, D))


def _moe(x2, g, mpack, mod_of_tile, mod_of_row, wr, wg, wu, wd, g_final, tm):
    rows = x2.shape[0]
    ne = wg.shape[0]
    h_tiles, meta, cnt = _route(x2, g, mpack, mod_of_tile, wr, tm)
    counts = cnt[0, :ne].astype(jnp.int32)
    tiles_e = (counts + MOE_TM - 1) // MOE_TM
    tile_end = jnp.cumsum(tiles_e)
    group_off = (tile_end - tiles_e) * MOE_TM
    n_tiles = 2 * rows // MOE_TM + ne
    n_slots = n_tiles * MOE_TM
    tile_expert = jnp.minimum(
        jnp.sum(jnp.arange(n_tiles)[:, None] >= tile_end[None, :], axis=1), ne - 1).astype(jnp.int32)
    n_used = tile_end[-1:].astype(jnp.int32)
    eids = jnp.arange(ne)

    def slot_of(lane_e, lane_r):
        e = meta[:, lane_e].astype(jnp.int32)
        off = jnp.sum(jnp.where(e[:, None] == eids[None, :], group_off[None, :], 0), axis=1)
        return off + meta[:, lane_r].astype(jnp.int32)

    tok = jnp.arange(rows, dtype=jnp.int32)
    dest = jnp.full((n_slots,), -1, jnp.int32).at[
        jnp.concatenate([slot_of(META_E1, META_R1), slot_of(META_E2, META_R2)])].set(
        jnp.concatenate([tok, rows + tok]))
    pad = dest < 0
    dest_of_slot = jnp.where(pad, 2 * rows + jnp.arange(n_slots, dtype=jnp.int32) % MOE_TM, dest)
    tok_of_slot = jnp.where(pad, 0, jnp.where(dest >= rows, dest - rows, dest))
    y_tiles = _experts(h_tiles, tok_of_slot, dest_of_slot, tile_expert, n_used, wg, wu, wd, 2 * rows + MOE_TM)
    return _combine(x2, mpack, mod_of_row, meta, y_tiles, g_final, MOE_TM)


def _rope_tables(seq_len):
    t = np.arange(seq_len)
    row = (t // GRID_W).astype(np.float32)
    col = (t % GRID_W).astype(np.float32)
    n_axis = HD // 4
    inv = jnp.power(ROPE_THETA, -jnp.arange(n_axis, dtype=F32) / n_axis)
    ang = jnp.concatenate([jnp.asarray(row)[:, None] * inv, jnp.asarray(col)[:, None] * inv], axis=-1)
    cos, sin = jnp.cos(ang), jnp.sin(ang)
    ct = jnp.tile(jnp.concatenate([cos, cos], axis=-1), (1, LANES // HD))
    st = jnp.tile(jnp.concatenate([-sin, sin], axis=-1), (1, LANES // HD))
    return ct, st


def kernel(x, c, ctx, c_ctx, w_mod, b_mod, g_norm1, g_norm2, w_in_ab, g_q, g_k, w_a2_f, b_a_f, w_a2_b,
           b_a_b, g_gla, w_out_ab, w_ff_gate, w_ff_up, w_ff_down, w_in_c, rpb_c, w_out_c, w_router,
           w_moe_gate, w_moe_up, w_moe_down, g_final):
    bsz, seq, _ = x.shape
    nc = ctx.shape[1]
    depth = w_mod.shape[0]
    assert depth == 2 and seq % GRID_W == 0

    cond_rows = -(-(bsz + 1) // 8) * 8
    cond = jnp.zeros((cond_rows, D), F32).at[:bsz].set(c).at[bsz].set(c_ctx)
    mods = _adaln(cond, w_mod, b_mod).reshape(depth, cond_rows, N_MOD, D)
    mods = jnp.pad(mods, ((0, 0), (0, 0), (0, MOD_ROWS - N_MOD), (0, 0)))

    xl = x.reshape(bsz * seq, D)
    xc = ctx.reshape(bsz * nc, D)
    tm_l = min(1024, seq)
    tm_c = min(1024, bsz * nc)
    tpb = seq // tm_l
    lat_mod = lambda i: i // tpb
    ctx_mod = lambda i: bsz

    mp = mods[0]
    w_in = w_in_ab[0]
    perm = np.concatenate([np.arange(0, HD, 2), np.arange(1, HD, 2)])
    nq, nkv = A_HEADS * HD, A_KV * HD
    o_q, o_k, o_v = 0, nq, nq + nkv
    o_bq = o_v + nkv
    o_bk = o_bq + B_HEADS * B_DK
    o_bv = o_bk + B_HEADS * B_DK
    o_br = o_bv + B_HEADS * B_DV
    o_zf = o_br + B_HEADS * B_DV
    o_end = o_zf + 2 * B_RANK
    wq = w_in[:, o_q:o_k].reshape(D, A_HEADS, HD)[:, :, perm].reshape(D, nq)
    wk = w_in[:, o_k:o_v].reshape(D, A_KV, HD)[:, :, perm].reshape(D, nkv)
    wz = jnp.pad(w_in[:, o_zf:o_end], ((0, 0), (0, LANES - 2 * B_RANK)))
    w0 = jnp.concatenate([wq, wk, w_in[:, o_v:o_bq], w_in[:, o_bq:o_bk] * (B_DK ** -0.5),
                          w_in[:, o_bk:o_zf], wz], axis=1).astype(BF)
    widths0 = (nq + nkv, nkv, GLA_W, LANES)
    outw0 = (nq, nkv, 2 * nkv, GLA_W, LANES)
    gqk = jnp.concatenate([jnp.tile(g_q[0][perm] * (HD ** -0.5 * LOG2E), A_HEADS),
                           jnp.tile(g_k[0][perm], A_KV)]).reshape(1, nq + nkv)
    nh = A_HEADS + A_KV
    bd = jnp.asarray(np.kron(np.eye(nh), np.ones((HD, HD))), BF)
    ct, st = _rope_tables(seq)
    ct_c = jnp.ones((tm_c, LANES), F32)
    st_c = jnp.zeros((tm_c, LANES), F32)

    ql, kl, vl, gl, zl = _proj(xl, g_norm1[0], mp, lat_mod, w0, widths0, outw0, tm_l, (ct, st, gqk, bd))
    qc, kc, vc, gc, zc = _proj(xc, g_norm1[0], mp, ctx_mod, w0, widths0, outw0, tm_c, (ct_c, st_c, gqk, bd))

    oa_l = _gqa(ql, kc, vc, kl, vl, bsz, 128)
    oa_c = _gqa(qc, kc, vc, None, None, bsz, min(128, nc))

    w2f = jnp.zeros((LANES, B_HEADS * B_DK), F32).at[:B_RANK].set(w_a2_f[0]).astype(BF)
    w2b = jnp.zeros((LANES, B_HEADS * B_DK), F32).at[B_RANK:2 * B_RANK].set(w_a2_b[0]).astype(BF)
    ob_c, ob_l = _gla(gc, gl, zc, zl, w2f, w2b, b_a_f[0].reshape(1, -1), b_a_b[0].reshape(1, -1),
                      jnp.tile(g_gla[0], B_HEADS).reshape(1, -1), bsz)

    wo = w_out_ab[0].astype(BF)
    xl = _outproj(xl, mp, lat_mod, [(oa_l, wo[:nq]), (ob_l, wo[nq:])], tm_l)
    xc = _outproj(xc, mp, ctx_mod, [(oa_c, wo[:nq]), (ob_c, wo[nq:])], tm_c)

    wg, wu, wd = w_ff_gate[0].astype(BF), w_ff_up[0].astype(BF), w_ff_down[0].astype(BF)
    xl = _ffn(xl, g_norm2[0], mp, lat_mod, wg, wu, wd, tm_l, 512)
    xc = _ffn(xc, g_norm2[0], mp, ctx_mod, wg, wu, wd, tm_c, 512)

    mp = mods[1]
    cw = C_HEADS * HD
    w1 = w_in_c[0].astype(BF)
    q1, k1, v1 = _proj(xl, g_norm1[1], mp, lat_mod, w1, (cw, cw, cw), (cw, cw, cw), tm_l,
                       scales=(HD ** -0.5 * LOG2E, 1.0, 1.0))
    k1c, v1c = _proj(xc, g_norm1[1], mp, ctx_mod, w1[:, cw:], (cw, cw), (cw, cw), tm_c)

    bias = _na_bias(rpb_c[0], seq // GRID_W) * LOG2E
    o1 = _natten(q1, k1, v1, k1c, v1c, bias, bsz)
    xl = _outproj(xl, mp, lat_mod, [(o1, w_out_c[0].astype(BF))], tm_l)

    wr = jnp.pad(w_router[0], ((0, 0), (0, LANES - N_EXPERTS)))
    out = _moe(xl, g_norm2[1], mp, lat_mod, lambda r: r // seq, wr, w_moe_gate[0].astype(BF),
               w_moe_up[0].astype(BF), w_moe_down[0].astype(BF), g_final, tm_l)
    return out.reshape(bsz, seq, D)
```

```python
import functools

import numpy as np
import jax
import jax.numpy as jnp
from jax import lax
from jax.experimental import pallas as pl
from jax.experimental.pallas import tpu as pltpu

F32 = jnp.float32
BF = jnp.bfloat16

D = 1024
EPS = 1e-6
N_MOD = 6
MOD_ROWS = 8
GRID_W = 64
HD = 64
A_HEADS = 8
A_KV = 2
A_GROUP = A_HEADS // A_KV
ROPE_THETA = 10000.0
LOG2E = 1.4426950408889634
B_HEADS = 4
B_DK = 64
B_DV = 128
B_RANK = 16
B_TAU = 16.0
B_CHUNK = 64
C_HEADS = 16
NA_ROWS = 8
NA_COLS = 16
N_EXPERTS = 8
LANES = 128
VMEM_LIMIT = 56 * 2 ** 20

NEG_INF = float("-inf")


def _params(sem):
    return pltpu.CompilerParams(dimension_semantics=sem, vmem_limit_bytes=VMEM_LIMIT)


def _dot(a, b):
    return jnp.dot(a, b, preferred_element_type=F32)


def _dot_nt(a, b):
    return lax.dot_general(a, b, (((1,), (1,)), ((), ())), preferred_element_type=F32)


def _dot_tn(a, b):
    return lax.dot_general(a, b, (((0,), (0,)), ((), ())), preferred_element_type=F32)


def _silu(x):
    return x / (1.0 + jnp.exp(-x))


def _split_dot(lhs_bf_exact, x):
    hi = x.astype(BF)
    lo = (x - hi.astype(F32)).astype(BF)
    return _dot(lhs_bf_exact, hi) + _dot(lhs_bf_exact, lo)


def _norm_mod(x, g, shift, scale):
    ms = jnp.mean(x * x, axis=-1, keepdims=True)
    return (x * lax.rsqrt(ms + EPS) * g) * (1.0 + scale) + shift


def _adaln_body(c_ref, w_ref, b_ref, o_ref):
    s = _silu(c_ref[...])
    o_ref[0] = _dot(s.astype(BF), w_ref[0].astype(BF)) + b_ref[0]


def _adaln(cond, w_mod, b_mod):
    depth, _, n = w_mod.shape
    rows = cond.shape[0]
    tn = 1536
    return pl.pallas_call(
        _adaln_body,
        grid=(depth, n // tn),
        in_specs=[pl.BlockSpec((rows, D), lambda l, j: (0, 0)),
                  pl.BlockSpec((1, D, tn), lambda l, j: (l, 0, j)),
                  pl.BlockSpec((1, 1, tn), lambda l, j: (l, 0, j))],
        out_specs=pl.BlockSpec((1, rows, tn), lambda l, j: (l, 0, j)),
        out_shape=jax.ShapeDtypeStruct((depth, rows, n), F32),
        compiler_params=_params(("arbitrary", "arbitrary")),
        name="adaln",
    )(cond, w_mod, b_mod.reshape(depth, 1, n))


def _qk_norm_rope(p, gqk, bd, ct, st):
    ss = _split_dot_rhs(p * p, bd)
    y = p * lax.rsqrt(ss * (1.0 / HD) + EPS) * gqk
    lane = lax.broadcasted_iota(jnp.int32, ct.shape, 1)
    first = (lane % HD) < (HD // 2)
    outs = []
    for c in range(p.shape[1] // LANES):
        yc = y[:, c * LANES:(c + 1) * LANES]
        partner = jnp.where(first, pltpu.roll(yc, LANES - HD // 2, 1), pltpu.roll(yc, HD // 2, 1))
        outs.append(yc * ct + partner * st)
    return jnp.concatenate(outs, axis=1)


def _split_dot_rhs(x, rhs_bf_exact):
    hi = x.astype(BF)
    lo = (x - hi.astype(F32)).astype(BF)
    return _dot(hi, rhs_bf_exact) + _dot(lo, rhs_bf_exact)


def _proj_body(x_ref, g_ref, m_ref, w_ref, *rest, widths, rope, scales):
    if rope:
        ct_ref, st_ref, gqk_ref, bd_ref = rest[:4]
        o_refs = rest[4:]
    else:
        o_refs = rest
    h = _norm_mod(x_ref[...], g_ref[...], m_ref[0, 0:1, :], m_ref[0, 1:2, :]).astype(BF)
    off = 0
    oi = 0
    for gi, wd in enumerate(widths):
        p = _dot(h, w_ref[:, off:off + wd])
        off += wd
        if scales[gi] != 1.0:
            p = p * scales[gi]
        if rope and gi == 0:
            p = _qk_norm_rope(p, gqk_ref[...], bd_ref[...], ct_ref[...], st_ref[...])
            nq = A_HEADS * HD
            o_refs[0][...] = p[:, :nq].astype(BF)
            o_refs[1][...] = p[:, nq:].astype(BF)
            oi = 2
        elif rope and gi == 1:
            ones = jnp.ones((p.shape[0], HD), F32)
            parts = []
            for j in range(A_KV):
                parts += [p[:, HD * j:HD * (j + 1)], ones]
            o_refs[oi][...] = jnp.concatenate(parts, axis=1).astype(BF)
            oi += 1
        else:
            o_refs[oi][...] = p.astype(BF)
            oi += 1


def _proj(x2, g, mpack, mod_of_tile, w, widths, out_widths, tm, rope_args=None, scales=None):
    rows = x2.shape[0]
    n = w.shape[1]
    in_specs = [pl.BlockSpec((tm, D), lambda i: (i, 0)),
                pl.BlockSpec((1, D), lambda i: (0, 0)),
                pl.BlockSpec((1, MOD_ROWS, D), lambda i: (mod_of_tile(i), 0, 0)),
                pl.BlockSpec((D, n), lambda i: (0, 0))]
    args = [x2, g.reshape(1, D), mpack, w]
    if rope_args is not None:
        ct, st, gqk, bd = rope_args
        nt = ct.shape[0] // tm
        in_specs += [pl.BlockSpec((tm, LANES), lambda i: (i % nt, 0)),
                     pl.BlockSpec((tm, LANES), lambda i: (i % nt, 0)),
                     pl.BlockSpec(gqk.shape, lambda i: (0, 0)),
                     pl.BlockSpec(bd.shape, lambda i: (0, 0))]
        args += [ct, st, gqk, bd]
    return pl.pallas_call(
        functools.partial(_proj_body, widths=widths, rope=rope_args is not None,
                          scales=scales or (1.0,) * len(widths)),
        grid=(rows // tm,),
        in_specs=in_specs,
        out_specs=[pl.BlockSpec((tm, ow), lambda i: (i, 0)) for ow in out_widths],
        out_shape=[jax.ShapeDtypeStruct((rows, ow), BF) for ow in out_widths],
        compiler_params=_params(("arbitrary",)),
        name="norm_mod_proj",
    )(*args)


GQA_UNIT = 2

def _gqa_body(q_ref, kc_ref, vc_ref, *rest, with_lat, tq):
    if with_lat:
        kl_ref, vl_ref, o_ref = rest
    else:
        (o_ref,) = rest
    units = [(j, g0) for j in range(A_KV) for g0 in range(0, A_GROUP, GQA_UNIT)]

    def scores(unit):
        j, g0 = unit
        cols = slice(HD * j, HD * (j + 1))
        heads = [A_GROUP * j + g0 + g for g in range(GQA_UNIT)]
        qg = jnp.concatenate([q_ref[:, HD * h:HD * (h + 1)] for h in heads], axis=0)
        s_c = _dot_nt(qg, kc_ref[:, cols])
        s_l = _dot_nt(qg, kl_ref[:, cols]) if with_lat else None
        return s_c, s_l

    nxt = scores(units[0])
    for u, (j, g0) in enumerate(units):
        s_c, s_l = nxt
        if u + 1 < len(units):
            nxt = scores(units[u + 1])
        vcols = slice(2 * HD * j, 2 * HD * (j + 1))
        m = jnp.max(s_c, axis=1, keepdims=True)
        if with_lat:
            m = jnp.maximum(m, jnp.max(s_l, axis=1, keepdims=True))
        acc = _dot(jnp.exp2(s_c - m).astype(BF), vc_ref[:, vcols])
        if with_lat:
            acc = acc + _dot(jnp.exp2(s_l - m).astype(BF), vl_ref[:, vcols])
        o = acc[:, :HD] / acc[:, HD:HD + 1]
        for g in range(GQA_UNIT):
            h = A_GROUP * j + g0 + g
            o_ref[:, HD * h:HD * (h + 1)] = o[g * tq:(g + 1) * tq].astype(BF)


def _gqa(q, k_ctx, v_ctx, k_lat, v_lat, bsz, tq):
    sq = q.shape[0] // bsz
    nc = k_ctx.shape[0] // bsz
    nq = sq // tq
    with_lat = k_lat is not None
    in_specs = [pl.BlockSpec((tq, A_HEADS * HD), lambda b, i: (b * nq + i, 0)),
                pl.BlockSpec((nc, A_KV * HD), lambda b, i: (b, 0)),
                pl.BlockSpec((nc, 2 * A_KV * HD), lambda b, i: (b, 0))]
    args = [q, k_ctx, v_ctx]
    if with_lat:
        sl = k_lat.shape[0] // bsz
        in_specs += [pl.BlockSpec((sl, A_KV * HD), lambda b, i: (b, 0)),
                     pl.BlockSpec((sl, 2 * A_KV * HD), lambda b, i: (b, 0))]
        args += [k_lat, v_lat]
    return pl.pallas_call(
        functools.partial(_gqa_body, with_lat=with_lat, tq=tq),
        grid=(bsz, nq),
        in_specs=in_specs,
        out_specs=pl.BlockSpec((tq, A_HEADS * HD), lambda b, i: (b * nq + i, 0)),
        out_shape=jax.ShapeDtypeStruct(q.shape, BF),
        compiler_params=_params(("arbitrary", "arbitrary")),
        name="gqa_attention",
    )(*args)


GLA_Q = 0
GLA_K = B_HEADS * B_DK
GLA_V = 2 * B_HEADS * B_DK
GLA_R = GLA_V + B_HEADS * B_DV
GLA_W = GLA_R + B_HEADS * B_DV
GLA_GROUP = 8


def _gla_body(gc_ref, gl_ref, zc_ref, zl_ref, w2f_ref, w2b_ref, bf_ref, bb_ref, gg_ref,
              oc_ref, ol_ref, ofc_ref, ofl_ref, st_ref):
    nk = B_HEADS * B_DK
    ch = B_CHUNK
    row = lax.broadcasted_iota(jnp.int32, (ch, ch), 0)
    col = lax.broadcasted_iota(jnp.int32, (ch, ch), 1)
    lane = lax.broadcasted_iota(jnp.int32, (ch, nk), 1)
    head_masks = [(lane // B_DK) == h for h in range(B_HEADS)]

    def group_out(g_ref, z_ref, r0s, fwd, s_t):
        w2 = (w2f_ref if fwd else w2b_ref)[...]
        bias = (bf_ref if fwd else bb_ref)[...]
        keep = (row >= col) if fwd else (row <= col)
        tri = jnp.where(keep, 1.0, 0.0).astype(BF)
        blks = [g_ref[pl.ds(r0, ch), :] for r0 in r0s]
        pres = [_dot(z_ref[pl.ds(r0, ch), :], w2) + bias for r0 in r0s]
        las = [-(jnp.maximum(-p, 0.0) + jnp.log(1.0 + jnp.exp(-jnp.abs(p)))) * (1.0 / B_TAU) for p in pres]
        bcums = [_split_dot(tri, la) for la in las]
        qms, kis, kss, decs = [], [], [], []
        for blk, bcum in zip(blks, bcums):
            q = blk[:, GLA_Q:GLA_Q + nk].astype(F32)
            k = blk[:, GLA_K:GLA_K + nk].astype(F32)
            b_last = bcum[ch - 1:ch, :] if fwd else bcum[0:1, :]
            q_dec = q * jnp.exp(bcum)
            k_state = k * jnp.exp(b_last - bcum)
            qms.append([jnp.where(m, q_dec, 0.0).astype(BF) for m in head_masks])
            kis.append((k * jnp.exp(-bcum)).astype(BF))
            kss.append([jnp.where(m, k_state, 0.0).astype(BF) for m in head_masks])
            decs.append(jnp.exp(b_last))
        vs = [[blk[:, GLA_V + B_DV * h:GLA_V + B_DV * (h + 1)] for h in range(B_HEADS)] for blk in blks]
        atts = [[jnp.where(keep, _dot_nt(qm, ki), 0.0).astype(BF) for qm in qmu] for qmu, ki in zip(qms, kis)]
        o_intra = [[_dot(a, v) for a, v in zip(au, vu)] for au, vu in zip(atts, vs)]
        d_ss = []
        for vu, ksu in zip(vs, kss):
            d_s = _dot_tn(vu[0], ksu[0])
            for h in range(1, B_HEADS):
                d_s = d_s + _dot_tn(vu[h], ksu[h])
            d_ss.append(d_s)
        outs = []
        for u in range(len(r0s)):
            s_b = s_t.astype(BF)
            outs.append(jnp.concatenate(
                [o_intra[u][h] + _dot_nt(qms[u][h], s_b) for h in range(B_HEADS)], axis=1))
            s_t = decs[u] * s_t + d_ss[u]
        return outs, blks, s_t

    def fwd_pass(g_ref, z_ref, of_ref, n):
        grp = min(GLA_GROUP, n)
        assert n % grp == 0

        def body(i, carry):
            r0s = [pl.multiple_of((i * grp + u) * ch, ch) for u in range(grp)]
            outs, _, s_t = group_out(g_ref, z_ref, r0s, True, st_ref[...])
            for r0, o in zip(r0s, outs):
                of_ref[pl.ds(r0, ch), :] = o
            st_ref[...] = s_t
            return carry
        lax.fori_loop(0, n // grp, body, 0)

    def bwd_pass(g_ref, z_ref, of_ref, o_ref, n):
        grp = min(GLA_GROUP, n)
        assert n % grp == 0

        def body(i, carry):
            r0s = [pl.multiple_of((n - 1 - (i * grp + u)) * ch, ch) for u in range(grp)]
            outs, blks, s_t = group_out(g_ref, z_ref, r0s, False, st_ref[...])
            for r0, o, blk in zip(r0s, outs, blks):
                o = o + of_ref[pl.ds(r0, ch), :]
                parts = []
                for h in range(B_HEADS):
                    oh = o[:, B_DV * h:B_DV * (h + 1)]
                    ms = jnp.mean(oh * oh, axis=-1, keepdims=True)
                    parts.append(oh * lax.rsqrt(ms + EPS))
                y = jnp.concatenate(parts, axis=1) * gg_ref[...]
                r = blk[:, GLA_R:GLA_R + B_HEADS * B_DV].astype(F32)
                o_ref[pl.ds(r0, ch), :] = (y * _silu(r)).astype(BF)
            st_ref[...] = s_t
            return carry
        lax.fori_loop(0, n // grp, body, 0)

    nc = gc_ref.shape[0] // ch
    nl = gl_ref.shape[0] // ch
    st_ref[...] = jnp.zeros_like(st_ref)
    fwd_pass(gc_ref, zc_ref, ofc_ref, nc)
    fwd_pass(gl_ref, zl_ref, ofl_ref, nl)
    st_ref[...] = jnp.zeros_like(st_ref)
    bwd_pass(gc_ref, zc_ref, ofc_ref, oc_ref, nc)
    bwd_pass(gl_ref, zl_ref, ofl_ref, ol_ref, nl)


def _gla(g_ctx, g_lat, z_ctx, z_lat, w2f, w2b, b_f, b_b, gg, bsz):
    nc = g_ctx.shape[0] // bsz
    sl = g_lat.shape[0] // bsz
    nk = B_HEADS * B_DK
    nv = B_HEADS * B_DV
    full = lambda shape: pl.BlockSpec(shape, lambda b: (0, 0))
    return pl.pallas_call(
        _gla_body,
        grid=(bsz,),
        in_specs=[pl.BlockSpec((nc, GLA_W), lambda b: (b, 0)),
                  pl.BlockSpec((sl, GLA_W), lambda b: (b, 0)),
                  pl.BlockSpec((nc, LANES), lambda b: (b, 0)),
                  pl.BlockSpec((sl, LANES), lambda b: (b, 0)),
                  full((LANES, nk)), full((LANES, nk)), full((1, nk)), full((1, nk)), full((1, nv))],
        out_specs=[pl.BlockSpec((nc, nv), lambda b: (b, 0)),
                   pl.BlockSpec((sl, nv), lambda b: (b, 0))],
        out_shape=[jax.ShapeDtypeStruct((g_ctx.shape[0], nv), BF),
                   jax.ShapeDtypeStruct((g_lat.shape[0], nv), BF)],
        scratch_shapes=[pltpu.VMEM((nc, nv), F32), pltpu.VMEM((sl, nv), F32),
                        pltpu.VMEM((B_DV, nk), F32)],
        compiler_params=_params(("arbitrary",)),
        name="gla_bidir",
    )(g_ctx, g_lat, z_ctx, z_lat, w2f, w2b, b_f, b_b, gg)


def _outproj_body(x_ref, m_ref, *rest, n_in):
    o_ref = rest[-1]
    acc = _dot(rest[0][...], rest[1][...])
    for t in range(1, n_in):
        acc = acc + _dot(rest[2 * t][...], rest[2 * t + 1][...])
    o_ref[...] = x_ref[...] + m_ref[0, 2:3, :] * acc


def _outproj(x2, mpack, mod_of_tile, pairs, tm):
    rows = x2.shape[0]
    in_specs = [pl.BlockSpec((tm, D), lambda i: (i, 0)),
                pl.BlockSpec((1, MOD_ROWS, D), lambda i: (mod_of_tile(i), 0, 0))]
    args = [x2, mpack]
    for a, w in pairs:
        in_specs += [pl.BlockSpec((tm, a.shape[1]), lambda i: (i, 0)),
                     pl.BlockSpec(w.shape, lambda i: (0, 0))]
        args += [a, w]
    return pl.pallas_call(
        functools.partial(_outproj_body, n_in=len(pairs)),
        grid=(rows // tm,),
        in_specs=in_specs,
        out_specs=pl.BlockSpec((tm, D), lambda i: (i, 0)),
        out_shape=jax.ShapeDtypeStruct((rows, D), F32),
        compiler_params=_params(("arbitrary",)),
        name="outproj_residual",
    )(*args)


def _ffn_body(x_ref, g_ref, m_ref, wg_ref, wu_ref, wd_ref, o_ref, h_ref, acc_ref):
    j = pl.program_id(1)

    @pl.when(j == 0)
    def _():
        h_ref[...] = _norm_mod(x_ref[...], g_ref[...], m_ref[0, 3:4, :], m_ref[0, 4:5, :]).astype(BF)
        acc_ref[...] = jnp.zeros_like(acc_ref)

    h = h_ref[...]
    a = (_silu(_dot(h, wg_ref[...])) * _dot(h, wu_ref[...])).astype(BF)
    acc_ref[...] += _dot(a, wd_ref[...])

    @pl.when(j == pl.num_programs(1) - 1)
    def _():
        o_ref[...] = x_ref[...] + m_ref[0, 5:6, :] * acc_ref[...]


def _ffn(x2, g, mpack, mod_of_tile, wg, wu, wd, tm, tf):
    rows = x2.shape[0]
    ff = wg.shape[1]
    return pl.pallas_call(
        _ffn_body,
        grid=(rows // tm, ff // tf),
        in_specs=[pl.BlockSpec((tm, D), lambda i, j: (i, 0)),
                  pl.BlockSpec((1, D), lambda i, j: (0, 0)),
                  pl.BlockSpec((1, MOD_ROWS, D), lambda i, j: (mod_of_tile(i), 0, 0)),
                  pl.BlockSpec((D, tf), lambda i, j: (0, j)),
                  pl.BlockSpec((D, tf), lambda i, j: (0, j)),
                  pl.BlockSpec((tf, D), lambda i, j: (j, 0))],
        out_specs=pl.BlockSpec((tm, D), lambda i, j: (i, 0)),
        out_shape=jax.ShapeDtypeStruct((rows, D), F32),
        scratch_shapes=[pltpu.VMEM((tm, D), BF), pltpu.VMEM((tm, D), F32)],
        compiler_params=_params(("arbitrary", "arbitrary")),
        name="ffn_swiglu",
    )(x2, g.reshape(1, D), mpack, wg, wu, wd)


NA_QROWS = 4
NA_KROWS = 12
NA_TQ = NA_QROWS * GRID_W
NA_HG = 8
NA_W = NA_HG * HD


def _na_body(q_ref, k0_ref, k1_ref, k2_ref, v0_ref, v1_ref, v2_ref, kc_ref, vc_ref, b_ref, o_ref):
    pair = LANES // HD

    def scores(hh):
        cols = slice(HD * hh, HD * (hh + 1))
        q = q_ref[:, cols]
        s_loc = jnp.concatenate([_dot_nt(q, kr[:, cols]) for kr in (k0_ref, k1_ref, k2_ref)], axis=1)
        return s_loc + b_ref[0, hh], _dot_nt(q, kc_ref[:, cols])

    nxt = scores(0)
    outs = []
    for hh in range(NA_HG):
        s_loc, s_ctx = nxt
        if hh + 1 < NA_HG:
            nxt = scores(hh + 1)
        cols = slice(HD * hh, HD * (hh + 1))
        m = jnp.maximum(jnp.max(s_loc, axis=1, keepdims=True), jnp.max(s_ctx, axis=1, keepdims=True))
        p_loc = jnp.exp2(s_loc - m)
        p_ctx = jnp.exp2(s_ctx - m)
        den = jnp.sum(p_loc, axis=1, keepdims=True) + jnp.sum(p_ctx, axis=1, keepdims=True)
        acc = _dot(p_ctx.astype(BF), vc_ref[:, cols])
        for t, vr in enumerate((v0_ref, v1_ref, v2_ref)):
            acc = acc + _dot(p_loc[:, t * NA_TQ:(t + 1) * NA_TQ].astype(BF), vr[:, cols])
        outs.append(acc / den)
        if len(outs) == pair:
            h0 = hh + 1 - pair
            o_ref[:, HD * h0:HD * (h0 + pair)] = jnp.concatenate(outs, axis=1).astype(BF)
            outs = []


def _na_bias(rpb, rows):
    heads, n_dr, n_dc = rpb.shape
    nblk = rows // NA_QROWS
    pad_l = GRID_W - NA_COLS
    rext = jnp.pad(rpb.astype(F32), ((0, 0), (0, 0), (pad_l, 2 * GRID_W - n_dc - pad_l)))
    flat = jnp.tile(rext, (1, 1, GRID_W))[:, :, :GRID_W * (2 * GRID_W - 1)]
    toe = flat.reshape(heads, n_dr, GRID_W, 2 * GRID_W - 1)[:, :, :, GRID_W - 1:]
    cols = np.arange(GRID_W)
    cs = np.clip(cols - NA_COLS // 2, 0, GRID_W - NA_COLS)
    col_ok = (cols[None, :] >= cs[:, None]) & (cols[None, :] < cs[:, None] + NA_COLS)
    toe = jnp.where(col_ok, toe, NEG_INF)
    dead = jnp.full((heads, GRID_W, GRID_W), NEG_INF, F32)
    kinds = []
    for j in (0, min(1, nblk - 1), nblk - 1):
        ws = NA_QROWS * int(np.clip(j - 1, 0, nblk - 3))
        q_rows = []
        for qr in range(NA_QROWS):
            r = NA_QROWS * j + qr
            rs = int(np.clip(r - NA_ROWS // 2, 0, rows - NA_ROWS))
            blocks = [toe[:, ws + kr - r + NA_ROWS - 1] if rs <= ws + kr < rs + NA_ROWS else dead
                      for kr in range(NA_KROWS)]
            q_rows.append(jnp.concatenate(blocks, axis=-1))
        kinds.append(jnp.concatenate(q_rows, axis=-2))
    return jnp.stack(kinds)


def _natten(q, k, v, k_ctx, v_ctx, bias, bsz):
    s = q.shape[0] // bsz
    nc = k_ctx.shape[0] // bsz
    nblk = s // NA_TQ
    ngrp = C_HEADS // NA_HG

    def kv_spec(t):
        return pl.BlockSpec((NA_TQ, NA_W),
                            lambda j, g, b: (b * nblk + jnp.clip(j - 1, 0, nblk - 3) + t, g))

    def kind(j):
        return jnp.where(j == 0, 0, jnp.where(j == nblk - 1, 2, 1))

    return pl.pallas_call(
        _na_body,
        grid=(nblk, ngrp, bsz),
        in_specs=[pl.BlockSpec((NA_TQ, NA_W), lambda j, g, b: (b * nblk + j, g)),
                  kv_spec(0), kv_spec(1), kv_spec(2), kv_spec(0), kv_spec(1), kv_spec(2),
                  pl.BlockSpec((nc, NA_W), lambda j, g, b: (b, g)),
                  pl.BlockSpec((nc, NA_W), lambda j, g, b: (b, g)),
                  pl.BlockSpec((1, NA_HG, NA_TQ, NA_KROWS * GRID_W), lambda j, g, b: (kind(j), g, 0, 0))],
        out_specs=pl.BlockSpec((NA_TQ, NA_W), lambda j, g, b: (b * nblk + j, g)),
        out_shape=jax.ShapeDtypeStruct(q.shape, BF),
        compiler_params=_params(("arbitrary", "arbitrary", "arbitrary")),
        name="natten",
    )(q, k, k, k, v, v, v, k_ctx, v_ctx, bias)


MOE_TM = 512
MOE_TF = 512
META_E1, META_E2, META_R1, META_R2, META_G1, META_G2 = range(6)
ROW_TILE = D // LANES


def _to_token_tiles(dst_ref, value):
    n = value.shape[0]
    for s in range(ROW_TILE):
        dst_ref[pl.ds(s, n, stride=ROW_TILE), :] = value[:, LANES * s:LANES * (s + 1)]


def _from_token_tiles(src_ref, n):
    return jnp.concatenate([src_ref[pl.ds(s, n, stride=ROW_TILE), :] for s in range(ROW_TILE)], axis=1)


def _route_body(x_ref, g_ref, m_ref, wr_ref, h_ref, meta_ref, cnt_ref, tri_ref, carry_ref):
    tm = x_ref.shape[0]

    @pl.when(pl.program_id(0) == 0)
    def _():
        row = lax.broadcasted_iota(jnp.int32, (tm, tm), 0)
        col = lax.broadcasted_iota(jnp.int32, (tm, tm), 1)
        tri_ref[...] = jnp.where(row >= col, 1.0, 0.0).astype(BF)
        carry_ref[...] = jnp.zeros_like(carry_ref)

    h = _norm_mod(x_ref[...], g_ref[...], m_ref[0, 3:4, :], m_ref[0, 4:5, :])
    _to_token_tiles(h_ref, h)
    logits = jnp.dot(h, wr_ref[...], preferred_element_type=F32, precision=lax.Precision.HIGHEST)
    lane = lax.broadcasted_iota(jnp.int32, logits.shape, 1)
    lg = jnp.where(lane < N_EXPERTS, logits, NEG_INF)
    m1 = jnp.max(lg, axis=1, keepdims=True)
    i1 = jnp.min(jnp.where(lg == m1, lane, LANES), axis=1, keepdims=True)
    lg2 = jnp.where(lane == i1, NEG_INF, lg)
    m2 = jnp.max(lg2, axis=1, keepdims=True)
    i2 = jnp.min(jnp.where(lg2 == m2, lane, LANES), axis=1, keepdims=True)
    e = jnp.exp(m2 - m1)
    g1 = 1.0 / (1.0 + e)
    g2 = e / (1.0 + e)
    oh = jnp.where(lane == i1, 1.0, 0.0) + jnp.where(lane == i2, 1.0, 0.0)
    cum = _dot(tri_ref[...], oh.astype(BF))
    base = carry_ref[0:1, :] + cum - oh
    r1 = jnp.sum(jnp.where(lane == i1, base, 0.0), axis=1, keepdims=True)
    r2 = jnp.sum(jnp.where(lane == i2, base, 0.0), axis=1, keepdims=True)
    carry_ref[...] = carry_ref[...] + cum[tm - 1:tm, :]
    cnt_ref[...] = carry_ref[...]
    meta = jnp.zeros(logits.shape, F32)
    for ln, val in ((META_E1, i1.astype(F32)), (META_E2, i2.astype(F32)), (META_R1, r1), (META_R2, r2),
                    (META_G1, g1), (META_G2, g2)):
        meta = jnp.where(lane == ln, val, meta)
    meta_ref[...] = meta


def _route(x2, g, mpack, mod_of_tile, wr, tm):
    rows = x2.shape[0]
    return pl.pallas_call(
        _route_body,
        grid=(rows // tm,),
        in_specs=[pl.BlockSpec((tm, D), lambda i: (i, 0)),
                  pl.BlockSpec((1, D), lambda i: (0, 0)),
                  pl.BlockSpec((1, MOD_ROWS, D), lambda i: (mod_of_tile(i), 0, 0)),
                  pl.BlockSpec((D, LANES), lambda i: (0, 0))],
        out_specs=[pl.BlockSpec((tm * ROW_TILE, LANES), lambda i: (i, 0)),
                   pl.BlockSpec((tm, LANES), lambda i: (i, 0)),
                   pl.BlockSpec((8, LANES), lambda i: (0, 0))],
        out_shape=[jax.ShapeDtypeStruct((rows * ROW_TILE, LANES), F32),
                   jax.ShapeDtypeStruct((rows, LANES), F32),
                   jax.ShapeDtypeStruct((8, LANES), F32)],
        scratch_shapes=[pltpu.VMEM((tm, tm), BF), pltpu.VMEM((8, LANES), F32)],
        compiler_params=_params(("arbitrary",)),
        name="moe_route",
    )(x2, g.reshape(1, D), mpack, wr)


def _token_copy(src_ref, src_tok, dst_ref, dst_tok, sem):
    src = src_ref.at[pl.ds(pl.multiple_of(src_tok * ROW_TILE, ROW_TILE), ROW_TILE), :]
    dst = dst_ref.at[pl.ds(pl.multiple_of(dst_tok * ROW_TILE, ROW_TILE), ROW_TILE), :]
    return pltpu.make_async_copy(src, dst, sem)


def _experts_body(te_ref, nu_ref, cur_ref, nxt_ref, dprev_ref, h_hbm, wg_ref, wu_ref, wd_ref, y_hbm,
                  xbuf, h_ref, acc_ref, ybuf, gsem, ssem, *, n_steps):
    i = pl.program_id(0)
    j = pl.program_id(1)
    used = i < nu_ref[0]
    slot = i % 2
    other = 1 - slot
    per_step = MOE_TM // n_steps

    def gather_row(idx_ref, r, sl):
        _token_copy(h_hbm, idx_ref[0, 0, r], xbuf.at[sl], r, gsem.at[sl]).start()

    def scatter_row(r):
        _token_copy(ybuf.at[other], r, y_hbm, dprev_ref[0, 0, r], ssem.at[other]).start()

    def issue_streams():
        base = j * per_step
        for u in range(per_step):
            gather_row(nxt_ref, base + u, other)
            scatter_row(base + u)

    def wait_rows(buf, sem):
        pltpu.make_async_copy(buf, buf, sem).wait()

    def swiglu_step():
        h = h_ref[...]
        a = (_silu(_dot(h, wg_ref[0])) * _dot(h, wu_ref[0])).astype(BF)
        return _dot(a, wd_ref[0])

    @pl.when((j == 0) & (i == 0))
    def _():
        ybuf[...] = jnp.zeros_like(ybuf)
        n_spare = MOE_TM * ROW_TILE
        for sl in range(2):
            spare = pltpu.make_async_copy(
                ybuf.at[sl], y_hbm.at[pl.ds(y_hbm.shape[0] - (2 - sl) * n_spare, n_spare), :], ssem.at[sl])
            spare.start()
            spare.wait()

        def body(r, c):
            gather_row(cur_ref, r, 0)
            return c
        lax.fori_loop(0, MOE_TM, body, 0, unroll=8)

    @pl.when(j == 0)
    def _():
        wait_rows(xbuf.at[slot], gsem.at[slot])
        for r in range(per_step * n_steps, MOE_TM):
            gather_row(nxt_ref, r, other)
            scatter_row(r)

    @pl.when(used & (j == 0))
    def _():
        h_ref[...] = _from_token_tiles(xbuf.at[slot], MOE_TM).astype(BF)
        acc_ref[...] = swiglu_step()
        issue_streams()

    @pl.when(used & (j > 0))
    def _():
        acc_ref[...] += swiglu_step()
        issue_streams()

    @pl.when(jnp.logical_not(used))
    def _():
        issue_streams()

    @pl.when(j == n_steps - 1)
    def _():
        @pl.when(i > 0)
        def _():
            wait_rows(ybuf.at[slot], ssem.at[slot])

        @pl.when(used)
        def _():
            _to_token_tiles(ybuf.at[slot], acc_ref[...])

        @pl.when(i == pl.num_programs(0) - 1)
        def _():
            wait_rows(ybuf.at[other], ssem.at[other])
            wait_rows(xbuf.at[other], gsem.at[other])


def _experts(h_tiles, tok_of_slot, dest_of_slot, tile_expert, n_used, wg, wu, wd, out_tokens):
    n_tiles = tile_expert.shape[0]
    ff = wg.shape[2]
    nf = ff // MOE_TF
    spare_rows = out_tokens - MOE_TM + jnp.arange(MOE_TM, dtype=jnp.int32)
    dest_prev = jnp.concatenate([spare_rows, dest_of_slot[:-MOE_TM]])

    def jj(i, j, nu):
        return jnp.where(i < nu[0], j, nf - 1)

    def smem_rows(index):
        return pl.BlockSpec((1, 1, MOE_TM), lambda i, j, te, nu: (index(i), 0, 0), memory_space=pltpu.SMEM)

    grid_spec = pltpu.PrefetchScalarGridSpec(
        num_scalar_prefetch=2,
        grid=(n_tiles, nf),
        in_specs=[smem_rows(lambda i: i),
                  smem_rows(lambda i: jnp.minimum(i + 1, n_tiles - 1)),
                  smem_rows(lambda i: i),
                  pl.BlockSpec(memory_space=pl.ANY),
                  pl.BlockSpec((1, D, MOE_TF), lambda i, j, te, nu: (te[i], 0, jj(i, j, nu))),
                  pl.BlockSpec((1, D, MOE_TF), lambda i, j, te, nu: (te[i], 0, jj(i, j, nu))),
                  pl.BlockSpec((1, MOE_TF, D), lambda i, j, te, nu: (te[i], jj(i, j, nu), 0))],
        out_specs=pl.BlockSpec(memory_space=pl.ANY),
        scratch_shapes=[pltpu.VMEM((2, MOE_TM * ROW_TILE, LANES), F32),
                        pltpu.VMEM((MOE_TM, D), BF),
                        pltpu.VMEM((MOE_TM, D), F32),
                        pltpu.VMEM((2, MOE_TM * ROW_TILE, LANES), F32),
                        pltpu.SemaphoreType.DMA((2,)),
                        pltpu.SemaphoreType.DMA((2,))])
    tok3 = tok_of_slot.reshape(n_tiles, 1, MOE_TM)
    return pl.pallas_call(
        functools.partial(_experts_body, n_steps=nf),
        grid_spec=grid_spec,
        out_shape=jax.ShapeDtypeStruct((out_tokens * ROW_TILE, LANES), F32),
        compiler_params=_params(("arbitrary", "arbitrary")),
        name="moe_experts",
    )(tile_expert, n_used, tok3, tok3, dest_prev.reshape(n_tiles, 1, MOE_TM), h_tiles, wg, wu, wd)


def _combine_body(y1_ref, y2_ref, x_ref, m_ref, meta_ref, gf_ref, o_ref):
    n = x_ref.shape[0]
    meta = meta_ref[...]
    lane = lax.broadcasted_iota(jnp.int32, meta.shape, 1)
    g1 = jnp.sum(jnp.where(lane == META_G1, meta, 0.0), axis=1, keepdims=True)
    g2 = jnp.sum(jnp.where(lane == META_G2, meta, 0.0), axis=1, keepdims=True)
    moe = g1 * _from_token_tiles(y1_ref, n) + g2 * _from_token_tiles(y2_ref, n)
    y = x_ref[...] + m_ref[0, 5:6, :] * moe
    ms = jnp.mean(y * y, axis=-1, keepdims=True)
    o_ref[...] = y * lax.rsqrt(ms + EPS) * gf_ref[...]


def _combine(x2, mpack, mod_of_row, meta, y_tiles, g_final, tm):
    rows = x2.shape[0]
    nt = rows // tm
    return pl.pallas_call(
        _combine_body,
        grid=(nt,),
        in_specs=[pl.BlockSpec((tm * ROW_TILE, LANES), lambda i: (i, 0)),
                  pl.BlockSpec((tm * ROW_TILE, LANES), lambda i: (i + nt, 0)),
                  pl.BlockSpec((tm, D), lambda i: (i, 0)),
                  pl.BlockSpec((1, MOD_ROWS, D), lambda i: (mod_of_row(i * tm), 0, 0)),
                  pl.BlockSpec((tm, LANES), lambda i: (i, 0)),
                  pl.BlockSpec((1, D), lambda i: (0, 0))],
        out_specs=pl.BlockSpec((tm, D), lambda i: (i, 0)),
        out_shape=jax.ShapeDtypeStruct((rows, D), F32),
        compiler_params=_params(("arbitrary",)),
        name="moe_combine",
    )(y_tiles, y_tiles, x2, mpack, meta, g_final.reshape(1, D))


def _moe(x2, g, mpack, mod_of_tile, mod_of_row, wr, wg, wu, wd, g_final, tm):
    rows = x2.shape[0]
    ne = wg.shape[0]
    h_tiles, meta, cnt = _route(x2, g, mpack, mod_of_tile, wr, tm)
    counts = cnt[0, :ne].astype(jnp.int32)
    tiles_e = (counts + MOE_TM - 1) // MOE_TM
    tile_end = jnp.cumsum(tiles_e)
    group_off = (tile_end - tiles_e) * MOE_TM
    n_tiles = 2 * rows // MOE_TM + ne
    n_slots = n_tiles * MOE_TM
    tile_expert = jnp.minimum(
        jnp.sum(jnp.arange(n_tiles)[:, None] >= tile_end[None, :], axis=1), ne - 1).astype(jnp.int32)
    n_used = tile_end[-1:].astype(jnp.int32)
    eids = jnp.arange(ne)

    def slot_of(lane_e, lane_r):
        e = meta[:, lane_e].astype(jnp.int32)
        off = jnp.sum(jnp.where(e[:, None] == eids[None, :], group_off[None, :], 0), axis=1)
        return off + meta[:, lane_r].astype(jnp.int32)

    tok = jnp.arange(rows, dtype=jnp.int32)
    dest = jnp.full((n_slots,), -1, jnp.int32).at[
        jnp.concatenate([slot_of(META_E1, META_R1), slot_of(META_E2, META_R2)])].set(
        jnp.concatenate([tok, rows + tok]), unique_indices=True, mode="promise_in_bounds")
    pad = dest < 0
    dest_of_slot = jnp.where(pad, 2 * rows + jnp.arange(n_slots, dtype=jnp.int32) % (2 * MOE_TM), dest)
    tok_of_slot = jnp.where(pad, 0, jnp.where(dest >= rows, dest - rows, dest))
    y_tiles = _experts(h_tiles, tok_of_slot, dest_of_slot, tile_expert, n_used, wg, wu, wd, 2 * rows + 2 * MOE_TM)
    return _combine(x2, mpack, mod_of_row, meta, y_tiles, g_final, MOE_TM)


def _rope_tables(seq_len):
    t = np.arange(seq_len)
    row = (t // GRID_W).astype(np.float32)
    col = (t % GRID_W).astype(np.float32)
    n_axis = HD // 4
    inv = jnp.power(ROPE_THETA, -jnp.arange(n_axis, dtype=F32) / n_axis)
    ang = jnp.concatenate([jnp.asarray(row)[:, None] * inv, jnp.asarray(col)[:, None] * inv], axis=-1)
    cos, sin = jnp.cos(ang), jnp.sin(ang)
    ct = jnp.tile(jnp.concatenate([cos, cos], axis=-1), (1, LANES // HD))
    st = jnp.tile(jnp.concatenate([-sin, sin], axis=-1), (1, LANES // HD))
    return ct, st


def kernel(x, c, ctx, c_ctx, w_mod, b_mod, g_norm1, g_norm2, w_in_ab, g_q, g_k, w_a2_f, b_a_f, w_a2_b,
           b_a_b, g_gla, w_out_ab, w_ff_gate, w_ff_up, w_ff_down, w_in_c, rpb_c, w_out_c, w_router,
           w_moe_gate, w_moe_up, w_moe_down, g_final):
    bsz, seq, _ = x.shape
    nc = ctx.shape[1]
    depth = w_mod.shape[0]
    assert depth == 2 and seq % GRID_W == 0

    cond_rows = -(-(bsz + 1) // 8) * 8
    cond = jnp.zeros((cond_rows, D), F32).at[:bsz].set(c).at[bsz].set(c_ctx)
    mods = _adaln(cond, w_mod, b_mod).reshape(depth, cond_rows, N_MOD, D)
    mods = jnp.pad(mods, ((0, 0), (0, 0), (0, MOD_ROWS - N_MOD), (0, 0)))

    xl = x.reshape(bsz * seq, D)
    xc = ctx.reshape(bsz * nc, D)
    tm_l = min(1024, seq)
    tm_c = min(1024, bsz * nc)
    tpb = seq // tm_l
    lat_mod = lambda i: i // tpb
    ctx_mod = lambda i: bsz

    mp = mods[0]
    w_in = w_in_ab[0]
    perm = np.concatenate([np.arange(0, HD, 2), np.arange(1, HD, 2)])
    nq, nkv = A_HEADS * HD, A_KV * HD
    o_q, o_k, o_v = 0, nq, nq + nkv
    o_bq = o_v + nkv
    o_bk = o_bq + B_HEADS * B_DK
    o_bv = o_bk + B_HEADS * B_DK
    o_br = o_bv + B_HEADS * B_DV
    o_zf = o_br + B_HEADS * B_DV
    o_end = o_zf + 2 * B_RANK
    wq = w_in[:, o_q:o_k].reshape(D, A_HEADS, HD)[:, :, perm].reshape(D, nq)
    wk = w_in[:, o_k:o_v].reshape(D, A_KV, HD)[:, :, perm].reshape(D, nkv)
    wz = jnp.pad(w_in[:, o_zf:o_end], ((0, 0), (0, LANES - 2 * B_RANK)))
    w0 = jnp.concatenate([wq, wk, w_in[:, o_v:o_bq], w_in[:, o_bq:o_bk] * (B_DK ** -0.5),
                          w_in[:, o_bk:o_zf], wz], axis=1).astype(BF)
    widths0 = (nq + nkv, nkv, GLA_W, LANES)
    outw0 = (nq, nkv, 2 * nkv, GLA_W, LANES)
    gqk = jnp.concatenate([jnp.tile(g_q[0][perm] * (HD ** -0.5 * LOG2E), A_HEADS),
                           jnp.tile(g_k[0][perm], A_KV)]).reshape(1, nq + nkv)
    nh = A_HEADS + A_KV
    bd = jnp.asarray(np.kron(np.eye(nh), np.ones((HD, HD))), BF)
    ct, st = _rope_tables(seq)
    ct_c = jnp.ones((tm_c, LANES), F32)
    st_c = jnp.zeros((tm_c, LANES), F32)

    ql, kl, vl, gl, zl = _proj(xl, g_norm1[0], mp, lat_mod, w0, widths0, outw0, tm_l, (ct, st, gqk, bd))
    qc, kc, vc, gc, zc = _proj(xc, g_norm1[0], mp, ctx_mod, w0, widths0, outw0, tm_c, (ct_c, st_c, gqk, bd))

    oa_l = _gqa(ql, kc, vc, kl, vl, bsz, 128)
    oa_c = _gqa(qc, kc, vc, None, None, bsz, min(128, nc))

    w2f = jnp.zeros((LANES, B_HEADS * B_DK), F32).at[:B_RANK].set(w_a2_f[0]).astype(BF)
    w2b = jnp.zeros((LANES, B_HEADS * B_DK), F32).at[B_RANK:2 * B_RANK].set(w_a2_b[0]).astype(BF)
    ob_c, ob_l = _gla(gc, gl, zc, zl, w2f, w2b, b_a_f[0].reshape(1, -1), b_a_b[0].reshape(1, -1),
                      jnp.tile(g_gla[0], B_HEADS).reshape(1, -1), bsz)

    wo = w_out_ab[0].astype(BF)
    xl = _outproj(xl, mp, lat_mod, [(oa_l, wo[:nq]), (ob_l, wo[nq:])], tm_l)
    xc = _outproj(xc, mp, ctx_mod, [(oa_c, wo[:nq]), (ob_c, wo[nq:])], tm_c)

    wg, wu, wd = w_ff_gate[0].astype(BF), w_ff_up[0].astype(BF), w_ff_down[0].astype(BF)
    xl = _ffn(xl, g_norm2[0], mp, lat_mod, wg, wu, wd, tm_l, 512)
    xc = _ffn(xc, g_norm2[0], mp, ctx_mod, wg, wu, wd, tm_c, 512)

    mp = mods[1]
    cw = C_HEADS * HD
    w1 = w_in_c[0].astype(BF)
    q1, k1, v1 = _proj(xl, g_norm1[1], mp, lat_mod, w1, (cw, cw, cw), (cw, cw, cw), tm_l,
                       scales=(HD ** -0.5 * LOG2E, 1.0, 1.0))
    k1c, v1c = _proj(xc, g_norm1[1], mp, ctx_mod, w1[:, cw:], (cw, cw), (cw, cw), tm_c)

    bias = _na_bias(rpb_c[0], seq // GRID_W) * LOG2E
    o1 = _natten(q1, k1, v1, k1c, v1c, bias, bsz)
    xl = _outproj(xl, mp, lat_mod, [(o1, w_out_c[0].astype(BF))], tm_l)

    wr = jnp.pad(w_router[0], ((0, 0), (0, LANES - N_EXPERTS)))
    out = _moe(xl, g_norm2[1], mp, lat_mod, lambda r: r // seq, wr, w_moe_gate[0].astype(BF),
               w_moe_up[0].astype(BF), w_moe_down[0].astype(BF), g_final, tm_l)
    return out.reshape(bsz, seq, D)
```

```python
import functools

import numpy as np
import jax
import jax.numpy as jnp
from jax import lax
from jax.experimental import pallas as pl
from jax.experimental.pallas import tpu as pltpu

F32 = jnp.float32
BF = jnp.bfloat16

D = 1024
EPS = 1e-6
N_MOD = 6
MOD_ROWS = 8
GRID_W = 64
HD = 64
A_HEADS = 8
A_KV = 2
A_GROUP = A_HEADS // A_KV
ROPE_THETA = 10000.0
LOG2E = 1.4426950408889634
B_HEADS = 4
B_DK = 64
B_DV = 128
B_RANK = 16
B_TAU = 16.0
B_CHUNK = 64
C_HEADS = 16
NA_ROWS = 8
NA_COLS = 16
N_EXPERTS = 8
LANES = 128
VMEM_LIMIT = 56 * 2 ** 20

NEG_INF = float("-inf")


def _params(sem):
    return pltpu.CompilerParams(dimension_semantics=sem, vmem_limit_bytes=VMEM_LIMIT)


def _dot(a, b):
    return jnp.dot(a, b, preferred_element_type=F32)


def _dot_nt(a, b):
    return lax.dot_general(a, b, (((1,), (1,)), ((), ())), preferred_element_type=F32)


def _dot_tn(a, b):
    return lax.dot_general(a, b, (((0,), (0,)), ((), ())), preferred_element_type=F32)


def _silu(x):
    return x / (1.0 + jnp.exp(-x))


def _split_dot(lhs_bf_exact, x):
    hi = x.astype(BF)
    lo = (x - hi.astype(F32)).astype(BF)
    return _dot(lhs_bf_exact, hi) + _dot(lhs_bf_exact, lo)


def _norm_mod(x, g, shift, scale):
    ms = jnp.mean(x * x, axis=-1, keepdims=True)
    return (x * lax.rsqrt(ms + EPS) * g) * (1.0 + scale) + shift


def _adaln_body(c_ref, w_ref, b_ref, o_ref):
    s = _silu(c_ref[...])
    o_ref[0] = _dot(s.astype(BF), w_ref[0].astype(BF)) + b_ref[0]


def _adaln(cond, w_mod, b_mod):
    depth, _, n = w_mod.shape
    rows = cond.shape[0]
    tn = 1536
    return pl.pallas_call(
        _adaln_body,
        grid=(depth, n // tn),
        in_specs=[pl.BlockSpec((rows, D), lambda l, j: (0, 0)),
                  pl.BlockSpec((1, D, tn), lambda l, j: (l, 0, j)),
                  pl.BlockSpec((1, 1, tn), lambda l, j: (l, 0, j))],
        out_specs=pl.BlockSpec((1, rows, tn), lambda l, j: (l, 0, j)),
        out_shape=jax.ShapeDtypeStruct((depth, rows, n), F32),
        compiler_params=_params(("arbitrary", "arbitrary")),
        name="adaln",
    )(cond, w_mod, b_mod.reshape(depth, 1, n))


def _qk_norm_rope(p, gqk, bd, ct, st):
    pp = p * p
    sums = []
    for c in range(p.shape[1] // LANES):
        x = pp[:, c * LANES:(c + 1) * LANES]
        hi = x.astype(BF)
        lo = (x - hi.astype(F32)).astype(BF)
        sums.append(_dot(jnp.concatenate([hi, lo], axis=1), bd))
    ss = jnp.concatenate(sums, axis=1)
    y = p * lax.rsqrt(ss * (1.0 / HD) + EPS) * gqk
    lane = lax.broadcasted_iota(jnp.int32, ct.shape, 1)
    first = (lane % HD) < (HD // 2)
    outs = []
    for c in range(p.shape[1] // LANES):
        yc = y[:, c * LANES:(c + 1) * LANES]
        partner = jnp.where(first, pltpu.roll(yc, LANES - HD // 2, 1), pltpu.roll(yc, HD // 2, 1))
        outs.append(yc * ct + partner * st)
    return jnp.concatenate(outs, axis=1)


def _proj_body(x_ref, g_ref, m_ref, w_ref, *rest, widths, rope, scales):
    if rope:
        ct_ref, st_ref, gqk_ref, bd_ref = rest[:4]
        o_refs = rest[4:]
    else:
        o_refs = rest
    h = _norm_mod(x_ref[...], g_ref[...], m_ref[0, 0:1, :], m_ref[0, 1:2, :]).astype(BF)
    off = 0
    oi = 0
    for gi, wd in enumerate(widths):
        p = _dot(h, w_ref[:, off:off + wd])
        off += wd
        if scales[gi] != 1.0:
            p = p * scales[gi]
        if rope and gi == 0:
            p = _qk_norm_rope(p, gqk_ref[...], bd_ref[...], ct_ref[...], st_ref[...])
            nq = A_HEADS * HD
            o_refs[0][...] = p[:, :nq].astype(BF)
            o_refs[1][...] = p[:, nq:].astype(BF)
            oi = 2
        elif rope and gi == 1:
            ones = jnp.ones((p.shape[0], HD), F32)
            parts = []
            for j in range(A_KV):
                parts += [p[:, HD * j:HD * (j + 1)], ones]
            o_refs[oi][...] = jnp.concatenate(parts, axis=1).astype(BF)
            oi += 1
        else:
            o_refs[oi][...] = p.astype(BF)
            oi += 1


def _proj(x2, g, mpack, mod_of_tile, w, widths, out_widths, tm, rope_args=None, scales=None):
    rows = x2.shape[0]
    n = w.shape[1]
    in_specs = [pl.BlockSpec((tm, D), lambda i: (i, 0)),
                pl.BlockSpec((1, D), lambda i: (0, 0)),
                pl.BlockSpec((1, MOD_ROWS, D), lambda i: (mod_of_tile(i), 0, 0)),
                pl.BlockSpec((D, n), lambda i: (0, 0))]
    args = [x2, g.reshape(1, D), mpack, w]
    if rope_args is not None:
        ct, st, gqk, bd = rope_args
        nt = ct.shape[0] // tm
        in_specs += [pl.BlockSpec((tm, LANES), lambda i: (i % nt, 0)),
                     pl.BlockSpec((tm, LANES), lambda i: (i % nt, 0)),
                     pl.BlockSpec(gqk.shape, lambda i: (0, 0)),
                     pl.BlockSpec(bd.shape, lambda i: (0, 0))]
        args += [ct, st, gqk, bd]
    return pl.pallas_call(
        functools.partial(_proj_body, widths=widths, rope=rope_args is not None,
                          scales=scales or (1.0,) * len(widths)),
        grid=(rows // tm,),
        in_specs=in_specs,
        out_specs=[pl.BlockSpec((tm, ow), lambda i: (i, 0)) for ow in out_widths],
        out_shape=[jax.ShapeDtypeStruct((rows, ow), BF) for ow in out_widths],
        compiler_params=_params(("arbitrary",)),
        name="norm_mod_proj",
    )(*args)


GQA_UNIT = 2

def _gqa_body(q_ref, kc_ref, vc_ref, *rest, with_lat, tq):
    if with_lat:
        kl_ref, vl_ref, o_ref = rest
    else:
        (o_ref,) = rest
    units = [(j, g0) for j in range(A_KV) for g0 in range(0, A_GROUP, GQA_UNIT)]

    def scores(unit):
        j, g0 = unit
        cols = slice(HD * j, HD * (j + 1))
        heads = [A_GROUP * j + g0 + g for g in range(GQA_UNIT)]
        qg = jnp.concatenate([q_ref[:, HD * h:HD * (h + 1)] for h in heads], axis=0)
        s_c = _dot_nt(qg, kc_ref[:, cols])
        s_l = _dot_nt(qg, kl_ref[:, cols]) if with_lat else None
        return s_c, s_l

    nxt = scores(units[0])
    for u, (j, g0) in enumerate(units):
        s_c, s_l = nxt
        if u + 1 < len(units):
            nxt = scores(units[u + 1])
        vcols = slice(2 * HD * j, 2 * HD * (j + 1))
        m = jnp.max(s_c, axis=1, keepdims=True)
        if with_lat:
            m = jnp.maximum(m, jnp.max(s_l, axis=1, keepdims=True))
        acc = _dot(jnp.exp2(s_c - m).astype(BF), vc_ref[:, vcols])
        if with_lat:
            acc = acc + _dot(jnp.exp2(s_l - m).astype(BF), vl_ref[:, vcols])
        o = acc[:, :HD] / acc[:, HD:HD + 1]
        for g in range(GQA_UNIT):
            h = A_GROUP * j + g0 + g
            o_ref[:, HD * h:HD * (h + 1)] = o[g * tq:(g + 1) * tq].astype(BF)


def _gqa(q, k_ctx, v_ctx, k_lat, v_lat, bsz, tq):
    sq = q.shape[0] // bsz
    nc = k_ctx.shape[0] // bsz
    nq = sq // tq
    with_lat = k_lat is not None
    in_specs = [pl.BlockSpec((tq, A_HEADS * HD), lambda b, i: (b * nq + i, 0)),
                pl.BlockSpec((nc, A_KV * HD), lambda b, i: (b, 0)),
                pl.BlockSpec((nc, 2 * A_KV * HD), lambda b, i: (b, 0))]
    args = [q, k_ctx, v_ctx]
    if with_lat:
        sl = k_lat.shape[0] // bsz
        in_specs += [pl.BlockSpec((sl, A_KV * HD), lambda b, i: (b, 0)),
                     pl.BlockSpec((sl, 2 * A_KV * HD), lambda b, i: (b, 0))]
        args += [k_lat, v_lat]
    return pl.pallas_call(
        functools.partial(_gqa_body, with_lat=with_lat, tq=tq),
        grid=(bsz, nq),
        in_specs=in_specs,
        out_specs=pl.BlockSpec((tq, A_HEADS * HD), lambda b, i: (b * nq + i, 0)),
        out_shape=jax.ShapeDtypeStruct(q.shape, BF),
        compiler_params=_params(("arbitrary", "arbitrary")),
        name="gqa_attention",
    )(*args)


GLA_Q = 0
GLA_K = B_HEADS * B_DK
GLA_V = 2 * B_HEADS * B_DK
GLA_R = GLA_V + B_HEADS * B_DV
GLA_W = GLA_R + B_HEADS * B_DV
GLA_GROUP = 8


def _gla_body(gc_ref, gl_ref, zc_ref, zl_ref, w2f_ref, w2b_ref, bf_ref, bb_ref, gg_ref,
              oc_ref, ol_ref, ofc_ref, ofl_ref, st_ref):
    nk = B_HEADS * B_DK
    ch = B_CHUNK
    row = lax.broadcasted_iota(jnp.int32, (ch, ch), 0)
    col = lax.broadcasted_iota(jnp.int32, (ch, ch), 1)
    lane = lax.broadcasted_iota(jnp.int32, (ch, nk), 1)
    head_masks = [(lane // B_DK) == h for h in range(B_HEADS)]

    def group_out(g_ref, z_ref, r0s, fwd, s_t):
        w2 = (w2f_ref if fwd else w2b_ref)[...]
        bias = (bf_ref if fwd else bb_ref)[...]
        keep = (row >= col) if fwd else (row <= col)
        tri = jnp.where(keep, 1.0, 0.0).astype(BF)
        blks = [g_ref[pl.ds(r0, ch), :] for r0 in r0s]
        pres = [_dot(z_ref[pl.ds(r0, ch), :], w2) + bias for r0 in r0s]
        las = [-(jnp.maximum(-p, 0.0) + jnp.log(1.0 + jnp.exp(-jnp.abs(p)))) * (1.0 / B_TAU) for p in pres]
        bcums = [_split_dot(tri, la) for la in las]
        qms, kis, kss, decs = [], [], [], []
        for blk, bcum in zip(blks, bcums):
            q = blk[:, GLA_Q:GLA_Q + nk].astype(F32)
            k = blk[:, GLA_K:GLA_K + nk].astype(F32)
            b_last = bcum[ch - 1:ch, :] if fwd else bcum[0:1, :]
            q_dec = q * jnp.exp(bcum)
            k_state = k * jnp.exp(b_last - bcum)
            qms.append([jnp.where(m, q_dec, 0.0).astype(BF) for m in head_masks])
            kis.append((k * jnp.exp(-bcum)).astype(BF))
            kss.append([jnp.where(m, k_state, 0.0).astype(BF) for m in head_masks])
            decs.append(jnp.exp(b_last))
        vs = [[blk[:, GLA_V + B_DV * h:GLA_V + B_DV * (h + 1)] for h in range(B_HEADS)] for blk in blks]
        atts = [[jnp.where(keep, _dot_nt(qm, ki), 0.0).astype(BF) for qm in qmu] for qmu, ki in zip(qms, kis)]
        o_intra = [[_dot(a, v) for a, v in zip(au, vu)] for au, vu in zip(atts, vs)]
        d_ss = []
        for vu, ksu in zip(vs, kss):
            d_s = _dot_tn(vu[0], ksu[0])
            for h in range(1, B_HEADS):
                d_s = d_s + _dot_tn(vu[h], ksu[h])
            d_ss.append(d_s)
        outs = []
        for u in range(len(r0s)):
            s_b = s_t.astype(BF)
            outs.append(jnp.concatenate(
                [o_intra[u][h] + _dot_nt(qms[u][h], s_b) for h in range(B_HEADS)], axis=1))
            s_t = decs[u] * s_t + d_ss[u]
        return outs, blks, s_t

    def fwd_pass(g_ref, z_ref, of_ref, n):
        grp = min(GLA_GROUP, n)
        assert n % grp == 0

        def body(i, carry):
            r0s = [pl.multiple_of((i * grp + u) * ch, ch) for u in range(grp)]
            outs, _, s_t = group_out(g_ref, z_ref, r0s, True, st_ref[...])
            for r0, o in zip(r0s, outs):
                of_ref[pl.ds(r0, ch), :] = o
            st_ref[...] = s_t
            return carry
        lax.fori_loop(0, n // grp, body, 0)

    def bwd_pass(g_ref, z_ref, of_ref, o_ref, n):
        grp = min(GLA_GROUP, n)
        assert n % grp == 0

        def body(i, carry):
            r0s = [pl.multiple_of((n - 1 - (i * grp + u)) * ch, ch) for u in range(grp)]
            outs, blks, s_t = group_out(g_ref, z_ref, r0s, False, st_ref[...])
            for r0, o, blk in zip(r0s, outs, blks):
                o = o + of_ref[pl.ds(r0, ch), :]
                parts = []
                for h in range(B_HEADS):
                    oh = o[:, B_DV * h:B_DV * (h + 1)]
                    ms = jnp.mean(oh * oh, axis=-1, keepdims=True)
                    parts.append(oh * lax.rsqrt(ms + EPS))
                y = jnp.concatenate(parts, axis=1) * gg_ref[...]
                r = blk[:, GLA_R:GLA_R + B_HEADS * B_DV].astype(F32)
                o_ref[pl.ds(r0, ch), :] = (y * _silu(r)).astype(BF)
            st_ref[...] = s_t
            return carry
        lax.fori_loop(0, n // grp, body, 0)

    nc = gc_ref.shape[0] // ch
    nl = gl_ref.shape[0] // ch
    st_ref[...] = jnp.zeros_like(st_ref)
    fwd_pass(gc_ref, zc_ref, ofc_ref, nc)
    fwd_pass(gl_ref, zl_ref, ofl_ref, nl)
    st_ref[...] = jnp.zeros_like(st_ref)
    bwd_pass(gc_ref, zc_ref, ofc_ref, oc_ref, nc)
    bwd_pass(gl_ref, zl_ref, ofl_ref, ol_ref, nl)


def _gla(g_ctx, g_lat, z_ctx, z_lat, w2f, w2b, b_f, b_b, gg, bsz):
    nc = g_ctx.shape[0] // bsz
    sl = g_lat.shape[0] // bsz
    nk = B_HEADS * B_DK
    nv = B_HEADS * B_DV
    full = lambda shape: pl.BlockSpec(shape, lambda b: (0, 0))
    return pl.pallas_call(
        _gla_body,
        grid=(bsz,),
        in_specs=[pl.BlockSpec((nc, GLA_W), lambda b: (b, 0)),
                  pl.BlockSpec((sl, GLA_W), lambda b: (b, 0)),
                  pl.BlockSpec((nc, LANES), lambda b: (b, 0)),
                  pl.BlockSpec((sl, LANES), lambda b: (b, 0)),
                  full((LANES, nk)), full((LANES, nk)), full((1, nk)), full((1, nk)), full((1, nv))],
        out_specs=[pl.BlockSpec((nc, nv), lambda b: (b, 0)),
                   pl.BlockSpec((sl, nv), lambda b: (b, 0))],
        out_shape=[jax.ShapeDtypeStruct((g_ctx.shape[0], nv), BF),
                   jax.ShapeDtypeStruct((g_lat.shape[0], nv), BF)],
        scratch_shapes=[pltpu.VMEM((nc, nv), F32), pltpu.VMEM((sl, nv), F32),
                        pltpu.VMEM((B_DV, nk), F32)],
        compiler_params=_params(("arbitrary",)),
        name="gla_bidir",
    )(g_ctx, g_lat, z_ctx, z_lat, w2f, w2b, b_f, b_b, gg)


def _mixer_residual(x_ref, m_ref, pair_refs):
    acc = _dot(pair_refs[0][...], pair_refs[1][...])
    for t in range(1, len(pair_refs) // 2):
        acc = acc + _dot(pair_refs[2 * t][...], pair_refs[2 * t + 1][...])
    return x_ref[...] + m_ref[0, 2:3, :] * acc


def _pair_specs(pairs, tm, index):
    in_specs, args = [], []
    for a, w in pairs:
        in_specs += [pl.BlockSpec((tm, a.shape[1]), index), pl.BlockSpec(w.shape, lambda *_: (0, 0))]
        args += [a, w]
    return in_specs, args


def _ffn_body(x_ref, g_ref, m_ref, *rest, n_pairs):
    pair_refs = rest[:2 * n_pairs]
    wg_ref, wu_ref, wd_ref, o_ref, xn_ref, h_ref, acc_ref = rest[2 * n_pairs:]
    j = pl.program_id(1)

    @pl.when(j == 0)
    def _():
        xn = _mixer_residual(x_ref, m_ref, pair_refs)
        xn_ref[...] = xn
        h_ref[...] = _norm_mod(xn, g_ref[...], m_ref[0, 3:4, :], m_ref[0, 4:5, :]).astype(BF)
        acc_ref[...] = jnp.zeros_like(acc_ref)

    h = h_ref[...]
    a = (_silu(_dot(h, wg_ref[...])) * _dot(h, wu_ref[...])).astype(BF)
    acc_ref[...] += _dot(a, wd_ref[...])

    @pl.when(j == pl.num_programs(1) - 1)
    def _():
        o_ref[...] = xn_ref[...] + m_ref[0, 5:6, :] * acc_ref[...]


def _ffn(x2, g, mpack, mod_of_tile, pairs, wg, wu, wd, tm, tf):
    rows = x2.shape[0]
    ff = wg.shape[1]
    pair_specs, pair_args = _pair_specs(pairs, tm, lambda i, j: (i, 0))
    return pl.pallas_call(
        functools.partial(_ffn_body, n_pairs=len(pairs)),
        grid=(rows // tm, ff // tf),
        in_specs=[pl.BlockSpec((tm, D), lambda i, j: (i, 0)),
                  pl.BlockSpec((1, D), lambda i, j: (0, 0)),
                  pl.BlockSpec((1, MOD_ROWS, D), lambda i, j: (mod_of_tile(i), 0, 0)),
                  *pair_specs,
                  pl.BlockSpec((D, tf), lambda i, j: (0, j)),
                  pl.BlockSpec((D, tf), lambda i, j: (0, j)),
                  pl.BlockSpec((tf, D), lambda i, j: (j, 0))],
        out_specs=pl.BlockSpec((tm, D), lambda i, j: (i, 0)),
        out_shape=jax.ShapeDtypeStruct((rows, D), F32),
        scratch_shapes=[pltpu.VMEM((tm, D), F32), pltpu.VMEM((tm, D), BF), pltpu.VMEM((tm, D), F32)],
        compiler_params=_params(("arbitrary", "arbitrary")),
        name="ffn_swiglu",
    )(x2, g.reshape(1, D), mpack, *pair_args, wg, wu, wd)


NA_QROWS = 4
NA_KROWS = 12
NA_TQ = NA_QROWS * GRID_W
NA_HG = 8
NA_W = NA_HG * HD


def _na_body(q_ref, k0_ref, k1_ref, k2_ref, v0_ref, v1_ref, v2_ref, kc_ref, vc_ref, b_ref, o_ref):
    pair = LANES // HD

    def scores(hh):
        cols = slice(HD * hh, HD * (hh + 1))
        q = q_ref[:, cols]
        s_loc = jnp.concatenate([_dot_nt(q, kr[:, cols]) for kr in (k0_ref, k1_ref, k2_ref)], axis=1)
        return s_loc + b_ref[0, hh], _dot_nt(q, kc_ref[:, cols])

    nxt = scores(0)
    outs = []
    for hh in range(NA_HG):
        s_loc, s_ctx = nxt
        if hh + 1 < NA_HG:
            nxt = scores(hh + 1)
        cols = slice(HD * hh, HD * (hh + 1))
        m = jnp.maximum(jnp.max(s_loc, axis=1, keepdims=True), jnp.max(s_ctx, axis=1, keepdims=True))
        p_loc = jnp.exp2(s_loc - m)
        p_ctx = jnp.exp2(s_ctx - m)
        den = jnp.sum(p_loc, axis=1, keepdims=True) + jnp.sum(p_ctx, axis=1, keepdims=True)
        acc = _dot(p_ctx.astype(BF), vc_ref[:, cols])
        for t, vr in enumerate((v0_ref, v1_ref, v2_ref)):
            acc = acc + _dot(p_loc[:, t * NA_TQ:(t + 1) * NA_TQ].astype(BF), vr[:, cols])
        outs.append(acc / den)
        if len(outs) == pair:
            h0 = hh + 1 - pair
            o_ref[:, HD * h0:HD * (h0 + pair)] = jnp.concatenate(outs, axis=1).astype(BF)
            outs = []


def _na_bias(rpb, rows):
    heads, n_dr, n_dc = rpb.shape
    nblk = rows // NA_QROWS
    pad_l = GRID_W - NA_COLS
    rext = jnp.pad(rpb.astype(F32), ((0, 0), (0, 0), (pad_l, 2 * GRID_W - n_dc - pad_l)))
    flat = jnp.tile(rext, (1, 1, GRID_W))[:, :, :GRID_W * (2 * GRID_W - 1)]
    toe = flat.reshape(heads, n_dr, GRID_W, 2 * GRID_W - 1)[:, :, :, GRID_W - 1:]
    cols = np.arange(GRID_W)
    cs = np.clip(cols - NA_COLS // 2, 0, GRID_W - NA_COLS)
    col_ok = (cols[None, :] >= cs[:, None]) & (cols[None, :] < cs[:, None] + NA_COLS)
    toe = jnp.where(col_ok, toe, NEG_INF)
    dead = jnp.full((heads, GRID_W, GRID_W), NEG_INF, F32)
    kinds = []
    for j in (0, min(1, nblk - 1), nblk - 1):
        ws = NA_QROWS * int(np.clip(j - 1, 0, nblk - 3))
        q_rows = []
        for qr in range(NA_QROWS):
            r = NA_QROWS * j + qr
            rs = int(np.clip(r - NA_ROWS // 2, 0, rows - NA_ROWS))
            blocks = [toe[:, ws + kr - r + NA_ROWS - 1] if rs <= ws + kr < rs + NA_ROWS else dead
                      for kr in range(NA_KROWS)]
            q_rows.append(jnp.concatenate(blocks, axis=-1))
        kinds.append(jnp.concatenate(q_rows, axis=-2))
    return jnp.stack(kinds)


def _natten(q, k, v, k_ctx, v_ctx, bias, bsz):
    s = q.shape[0] // bsz
    nc = k_ctx.shape[0] // bsz
    nblk = s // NA_TQ
    ngrp = C_HEADS // NA_HG

    def kv_spec(t):
        return pl.BlockSpec((NA_TQ, NA_W),
                            lambda j, g, b: (b * nblk + jnp.clip(j - 1, 0, nblk - 3) + t, g))

    def kind(j):
        return jnp.where(j == 0, 0, jnp.where(j == nblk - 1, 2, 1))

    return pl.pallas_call(
        _na_body,
        grid=(nblk, ngrp, bsz),
        in_specs=[pl.BlockSpec((NA_TQ, NA_W), lambda j, g, b: (b * nblk + j, g)),
                  kv_spec(0), kv_spec(1), kv_spec(2), kv_spec(0), kv_spec(1), kv_spec(2),
                  pl.BlockSpec((nc, NA_W), lambda j, g, b: (b, g)),
                  pl.BlockSpec((nc, NA_W), lambda j, g, b: (b, g)),
                  pl.BlockSpec((1, NA_HG, NA_TQ, NA_KROWS * GRID_W), lambda j, g, b: (kind(j), g, 0, 0))],
        out_specs=pl.BlockSpec((NA_TQ, NA_W), lambda j, g, b: (b * nblk + j, g)),
        out_shape=jax.ShapeDtypeStruct(q.shape, BF),
        compiler_params=_params(("arbitrary", "arbitrary", "arbitrary")),
        name="natten",
    )(q, k, k, k, v, v, v, k_ctx, v_ctx, bias)


MOE_TM = 512
MOE_TF = 1792
META_E1, META_E2, META_R1, META_R2, META_G1, META_G2 = range(6)
ROW_TILE = D // LANES


def _to_token_tiles(dst_ref, value):
    n = value.shape[0]
    for s in range(ROW_TILE):
        dst_ref[pl.ds(s, n, stride=ROW_TILE), :] = value[:, LANES * s:LANES * (s + 1)]


def _from_token_tiles(src_ref, n):
    return jnp.concatenate([src_ref[pl.ds(s, n, stride=ROW_TILE), :] for s in range(ROW_TILE)], axis=1)


def _route_body(x_ref, g_ref, m_ref, *rest, n_pairs):
    pair_refs = rest[:2 * n_pairs]
    wr_ref, xn_ref, h_ref, meta_ref, cnt_ref, tri_ref, carry_ref = rest[2 * n_pairs:]
    tm = x_ref.shape[0]

    @pl.when(pl.program_id(0) == 0)
    def _():
        row = lax.broadcasted_iota(jnp.int32, (tm, tm), 0)
        col = lax.broadcasted_iota(jnp.int32, (tm, tm), 1)
        tri_ref[...] = jnp.where(row >= col, 1.0, 0.0).astype(BF)
        carry_ref[...] = jnp.zeros_like(carry_ref)

    xn = _mixer_residual(x_ref, m_ref, pair_refs)
    xn_ref[...] = xn
    h = _norm_mod(xn, g_ref[...], m_ref[0, 3:4, :], m_ref[0, 4:5, :])
    _to_token_tiles(h_ref, h)
    h_hi = h.astype(BF)
    h_lo = (h - h_hi.astype(F32)).astype(BF)
    w = wr_ref[...]
    w_hi = w.astype(BF)
    w_lo = (w - w_hi.astype(F32)).astype(BF)
    logits = _dot(h_hi, w_hi) + (_dot(h_lo, w_hi) + _dot(h_hi, w_lo))
    lane = lax.broadcasted_iota(jnp.int32, logits.shape, 1)
    lg = jnp.where(lane < N_EXPERTS, logits, NEG_INF)
    m1 = jnp.max(lg, axis=1, keepdims=True)
    i1 = jnp.min(jnp.where(lg == m1, lane, LANES), axis=1, keepdims=True)
    lg2 = jnp.where(lane == i1, NEG_INF, lg)
    m2 = jnp.max(lg2, axis=1, keepdims=True)
    i2 = jnp.min(jnp.where(lg2 == m2, lane, LANES), axis=1, keepdims=True)
    e = jnp.exp(m2 - m1)
    g1 = 1.0 / (1.0 + e)
    g2 = e / (1.0 + e)
    oh = jnp.where(lane == i1, 1.0, 0.0) + jnp.where(lane == i2, 1.0, 0.0)
    cum = _dot(tri_ref[...], oh.astype(BF))
    base = carry_ref[0:1, :] + cum - oh
    r1 = jnp.sum(jnp.where(lane == i1, base, 0.0), axis=1, keepdims=True)
    r2 = jnp.sum(jnp.where(lane == i2, base, 0.0), axis=1, keepdims=True)
    carry_ref[...] = carry_ref[...] + cum[tm - 1:tm, :]
    cnt_ref[...] = carry_ref[...]
    meta = jnp.zeros(logits.shape, F32)
    for ln, val in ((META_E1, i1.astype(F32)), (META_E2, i2.astype(F32)), (META_R1, r1), (META_R2, r2),
                    (META_G1, g1), (META_G2, g2)):
        meta = jnp.where(lane == ln, val, meta)
    meta_ref[...] = meta


def _route(x2, g, mpack, mod_of_tile, pairs, wr, tm):
    rows = x2.shape[0]
    pair_specs, pair_args = _pair_specs(pairs, tm, lambda i: (i, 0))
    return pl.pallas_call(
        functools.partial(_route_body, n_pairs=len(pairs)),
        grid=(rows // tm,),
        in_specs=[pl.BlockSpec((tm, D), lambda i: (i, 0)),
                  pl.BlockSpec((1, D), lambda i: (0, 0)),
                  pl.BlockSpec((1, MOD_ROWS, D), lambda i: (mod_of_tile(i), 0, 0)),
                  *pair_specs,
                  pl.BlockSpec((D, LANES), lambda i: (0, 0))],
        out_specs=[pl.BlockSpec((tm, D), lambda i: (i, 0)),
                   pl.BlockSpec((tm * ROW_TILE, LANES), lambda i: (i, 0)),
                   pl.BlockSpec((tm, LANES), lambda i: (i, 0)),
                   pl.BlockSpec((8, LANES), lambda i: (0, 0))],
        out_shape=[jax.ShapeDtypeStruct((rows, D), F32),
                   jax.ShapeDtypeStruct((rows * ROW_TILE, LANES), F32),
                   jax.ShapeDtypeStruct((rows, LANES), F32),
                   jax.ShapeDtypeStruct((8, LANES), F32)],
        scratch_shapes=[pltpu.VMEM((tm, tm), BF), pltpu.VMEM((8, LANES), F32)],
        compiler_params=_params(("arbitrary",)),
        name="moe_route",
    )(x2, g.reshape(1, D), mpack, *pair_args, wr)


def _token_copy(src_ref, src_tok, dst_ref, dst_tok, sem):
    src = src_ref.at[pl.ds(pl.multiple_of(src_tok * ROW_TILE, ROW_TILE), ROW_TILE), :]
    dst = dst_ref.at[pl.ds(pl.multiple_of(dst_tok * ROW_TILE, ROW_TILE), ROW_TILE), :]
    return pltpu.make_async_copy(src, dst, sem)


def _experts_body(te_ref, nu_ref, cur_ref, nxt_ref, dprev_ref, h_hbm, wg_ref, wu_ref, wd_ref, y_hbm,
                  xbuf, h_ref, acc_ref, ybuf, gsem, ssem, *, n_steps):
    i = pl.program_id(0)
    j = pl.program_id(1)
    used = i < nu_ref[0]
    slot = i % 2
    other = 1 - slot
    per_step = MOE_TM // n_steps

    def gather_row(idx_ref, r, sl):
        _token_copy(h_hbm, idx_ref[0, 0, r], xbuf.at[sl], r, gsem.at[sl]).start()

    def scatter_row(r):
        _token_copy(ybuf.at[other], r, y_hbm, dprev_ref[0, 0, r], ssem.at[other]).start()

    def issue_streams():
        base = j * per_step
        for u in range(per_step):
            gather_row(nxt_ref, base + u, other)
            scatter_row(base + u)

    def wait_rows(buf, sem):
        pltpu.make_async_copy(buf, buf, sem).wait()

    def swiglu_step():
        h = h_ref[...]
        a = (_silu(_dot(h, wg_ref[0])) * _dot(h, wu_ref[0])).astype(BF)
        return _dot(a, wd_ref[0])

    @pl.when((j == 0) & (i == 0))
    def _():
        ybuf[...] = jnp.zeros_like(ybuf)
        n_spare = MOE_TM * ROW_TILE
        for sl in range(2):
            spare = pltpu.make_async_copy(
                ybuf.at[sl], y_hbm.at[pl.ds(y_hbm.shape[0] - (2 - sl) * n_spare, n_spare), :], ssem.at[sl])
            spare.start()
            spare.wait()

        def body(r, c):
            gather_row(cur_ref, r, 0)
            return c
        lax.fori_loop(0, MOE_TM, body, 0, unroll=8)

    @pl.when(j == 0)
    def _():
        wait_rows(xbuf.at[slot], gsem.at[slot])
        for r in range(per_step * n_steps, MOE_TM):
            gather_row(nxt_ref, r, other)
            scatter_row(r)

    @pl.when(used & (j == 0))
    def _():
        h_ref[...] = _from_token_tiles(xbuf.at[slot], MOE_TM).astype(BF)
        acc_ref[...] = swiglu_step()
        issue_streams()

    @pl.when(used & (j > 0))
    def _():
        acc_ref[...] += swiglu_step()
        issue_streams()

    @pl.when(jnp.logical_not(used))
    def _():
        issue_streams()

    @pl.when(j == n_steps - 1)
    def _():
        @pl.when(i > 0)
        def _():
            wait_rows(ybuf.at[slot], ssem.at[slot])

        @pl.when(used)
        def _():
            _to_token_tiles(ybuf.at[slot], acc_ref[...])

        @pl.when(i == pl.num_programs(0) - 1)
        def _():
            wait_rows(ybuf.at[other], ssem.at[other])
            wait_rows(xbuf.at[other], gsem.at[other])


def _experts(h_tiles, tok_of_slot, dest_of_slot, tile_expert, n_used, wg, wu, wd, out_tokens):
    n_tiles = tile_expert.shape[0]
    ff = wg.shape[2]
    nf = ff // MOE_TF
    spare_rows = out_tokens - MOE_TM + jnp.arange(MOE_TM, dtype=jnp.int32)
    dest_prev = jnp.concatenate([spare_rows, dest_of_slot[:-MOE_TM]])

    def jj(i, j, nu):
        return jnp.where(i < nu[0], j, nf - 1)

    def smem_rows(index):
        return pl.BlockSpec((1, 1, MOE_TM), lambda i, j, te, nu: (index(i), 0, 0), memory_space=pltpu.SMEM)

    grid_spec = pltpu.PrefetchScalarGridSpec(
        num_scalar_prefetch=2,
        grid=(n_tiles, nf),
        in_specs=[smem_rows(lambda i: i),
                  smem_rows(lambda i: jnp.minimum(i + 1, n_tiles - 1)),
                  smem_rows(lambda i: i),
                  pl.BlockSpec(memory_space=pl.ANY),
                  pl.BlockSpec((1, D, MOE_TF), lambda i, j, te, nu: (te[i], 0, jj(i, j, nu))),
                  pl.BlockSpec((1, D, MOE_TF), lambda i, j, te, nu: (te[i], 0, jj(i, j, nu))),
                  pl.BlockSpec((1, MOE_TF, D), lambda i, j, te, nu: (te[i], jj(i, j, nu), 0))],
        out_specs=pl.BlockSpec(memory_space=pl.ANY),
        scratch_shapes=[pltpu.VMEM((2, MOE_TM * ROW_TILE, LANES), F32),
                        pltpu.VMEM((MOE_TM, D), BF),
                        pltpu.VMEM((MOE_TM, D), F32),
                        pltpu.VMEM((2, MOE_TM * ROW_TILE, LANES), F32),
                        pltpu.SemaphoreType.DMA((2,)),
                        pltpu.SemaphoreType.DMA((2,))])
    tok3 = tok_of_slot.reshape(n_tiles, 1, MOE_TM)
    return pl.pallas_call(
        functools.partial(_experts_body, n_steps=nf),
        grid_spec=grid_spec,
        out_shape=jax.ShapeDtypeStruct((out_tokens * ROW_TILE, LANES), F32),
        compiler_params=_params(("arbitrary", "arbitrary")),
        name="moe_experts",
    )(tile_expert, n_used, tok3, tok3, dest_prev.reshape(n_tiles, 1, MOE_TM), h_tiles, wg, wu, wd)


def _combine_body(y1_ref, y2_ref, x_ref, m_ref, meta_ref, gf_ref, o_ref):
    n = x_ref.shape[0]
    meta = meta_ref[...]
    lane = lax.broadcasted_iota(jnp.int32, meta.shape, 1)
    g1 = jnp.sum(jnp.where(lane == META_G1, meta, 0.0), axis=1, keepdims=True)
    g2 = jnp.sum(jnp.where(lane == META_G2, meta, 0.0), axis=1, keepdims=True)
    moe = g1 * _from_token_tiles(y1_ref, n) + g2 * _from_token_tiles(y2_ref, n)
    y = x_ref[...] + m_ref[0, 5:6, :] * moe
    ms = jnp.mean(y * y, axis=-1, keepdims=True)
    o_ref[...] = y * lax.rsqrt(ms + EPS) * gf_ref[...]


def _combine(x2, mpack, mod_of_row, meta, y_tiles, g_final, tm):
    rows = x2.shape[0]
    nt = rows // tm
    return pl.pallas_call(
        _combine_body,
        grid=(nt,),
        in_specs=[pl.BlockSpec((tm * ROW_TILE, LANES), lambda i: (i, 0)),
                  pl.BlockSpec((tm * ROW_TILE, LANES), lambda i: (i + nt, 0)),
                  pl.BlockSpec((tm, D), lambda i: (i, 0)),
                  pl.BlockSpec((1, MOD_ROWS, D), lambda i: (mod_of_row(i * tm), 0, 0)),
                  pl.BlockSpec((tm, LANES), lambda i: (i, 0)),
                  pl.BlockSpec((1, D), lambda i: (0, 0))],
        out_specs=pl.BlockSpec((tm, D), lambda i: (i, 0)),
        out_shape=jax.ShapeDtypeStruct((rows, D), F32),
        compiler_params=_params(("arbitrary",)),
        name="moe_combine",
    )(y_tiles, y_tiles, x2, mpack, meta, g_final.reshape(1, D))


def _moe(x2, g, mpack, mod_of_tile, mod_of_row, pairs, wr, wg, wu, wd, g_final, tm):
    rows = x2.shape[0]
    ne = wg.shape[0]
    x2, h_tiles, meta, cnt = _route(x2, g, mpack, mod_of_tile, pairs, wr, tm)
    counts = cnt[0, :ne].astype(jnp.int32)
    tiles_e = (counts + MOE_TM - 1) // MOE_TM
    tile_end = jnp.cumsum(tiles_e)
    group_off = (tile_end - tiles_e) * MOE_TM
    n_tiles = 2 * rows // MOE_TM + ne
    n_slots = n_tiles * MOE_TM
    tile_expert = jnp.minimum(
        jnp.sum(jnp.arange(n_tiles)[:, None] >= tile_end[None, :], axis=1), ne - 1).astype(jnp.int32)
    n_used = tile_end[-1:].astype(jnp.int32)
    eids = jnp.arange(ne)

    def slot_of(lane_e, lane_r):
        e = meta[:, lane_e].astype(jnp.int32)
        off = jnp.sum(jnp.where(e[:, None] == eids[None, :], group_off[None, :], 0), axis=1)
        return off + meta[:, lane_r].astype(jnp.int32)

    tok = jnp.arange(rows, dtype=jnp.int32)
    dest = jnp.full((n_slots,), -1, jnp.int32).at[
        jnp.concatenate([slot_of(META_E1, META_R1), slot_of(META_E2, META_R2)])].set(
        jnp.concatenate([tok, rows + tok]), unique_indices=True, mode="promise_in_bounds")
    pad = dest < 0
    dest_of_slot = jnp.where(pad, 2 * rows + jnp.arange(n_slots, dtype=jnp.int32) % (2 * MOE_TM), dest)
    tok_of_slot = jnp.where(pad, 0, jnp.where(dest >= rows, dest - rows, dest))
    y_tiles = _experts(h_tiles, tok_of_slot, dest_of_slot, tile_expert, n_used, wg, wu, wd, 2 * rows + 2 * MOE_TM)
    return _combine(x2, mpack, mod_of_row, meta, y_tiles, g_final, MOE_TM)


def _rope_tables(seq_len):
    t = np.arange(seq_len)
    row = (t // GRID_W).astype(np.float32)
    col = (t % GRID_W).astype(np.float32)
    n_axis = HD // 4
    inv = jnp.power(ROPE_THETA, -jnp.arange(n_axis, dtype=F32) / n_axis)
    ang = jnp.concatenate([jnp.asarray(row)[:, None] * inv, jnp.asarray(col)[:, None] * inv], axis=-1)
    cos, sin = jnp.cos(ang), jnp.sin(ang)
    ct = jnp.tile(jnp.concatenate([cos, cos], axis=-1), (1, LANES // HD))
    st = jnp.tile(jnp.concatenate([-sin, sin], axis=-1), (1, LANES // HD))
    return ct, st


def kernel(x, c, ctx, c_ctx, w_mod, b_mod, g_norm1, g_norm2, w_in_ab, g_q, g_k, w_a2_f, b_a_f, w_a2_b,
           b_a_b, g_gla, w_out_ab, w_ff_gate, w_ff_up, w_ff_down, w_in_c, rpb_c, w_out_c, w_router,
           w_moe_gate, w_moe_up, w_moe_down, g_final):
    bsz, seq, _ = x.shape
    nc = ctx.shape[1]
    depth = w_mod.shape[0]
    assert depth == 2 and seq % GRID_W == 0

    cond_rows = -(-(bsz + 1) // 8) * 8
    cond = jnp.zeros((cond_rows, D), F32).at[:bsz].set(c).at[bsz].set(c_ctx)
    mods = _adaln(cond, w_mod, b_mod).reshape(depth, cond_rows, N_MOD, D)
    mods = jnp.pad(mods, ((0, 0), (0, 0), (0, MOD_ROWS - N_MOD), (0, 0)))

    xl = x.reshape(bsz * seq, D)
    xc = ctx.reshape(bsz * nc, D)
    tm_l = min(1024, seq)
    tm_c = min(1024, bsz * nc)
    tpb = seq // tm_l
    lat_mod = lambda i: i // tpb
    ctx_mod = lambda i: bsz

    mp = mods[0]
    w_in = w_in_ab[0]
    perm = np.concatenate([np.arange(0, HD, 2), np.arange(1, HD, 2)])
    nq, nkv = A_HEADS * HD, A_KV * HD
    o_q, o_k, o_v = 0, nq, nq + nkv
    o_bq = o_v + nkv
    o_bk = o_bq + B_HEADS * B_DK
    o_bv = o_bk + B_HEADS * B_DK
    o_br = o_bv + B_HEADS * B_DV
    o_zf = o_br + B_HEADS * B_DV
    o_end = o_zf + 2 * B_RANK
    wq = w_in[:, o_q:o_k].reshape(D, A_HEADS, HD)[:, :, perm].reshape(D, nq)
    wk = w_in[:, o_k:o_v].reshape(D, A_KV, HD)[:, :, perm].reshape(D, nkv)
    wz = jnp.pad(w_in[:, o_zf:o_end], ((0, 0), (0, LANES - 2 * B_RANK)))
    w0 = jnp.concatenate([wq, wk, w_in[:, o_v:o_bq], w_in[:, o_bq:o_bk] * (B_DK ** -0.5),
                          w_in[:, o_bk:o_zf], wz], axis=1).astype(BF)
    widths0 = (nq + nkv, nkv, GLA_W, LANES)
    outw0 = (nq, nkv, 2 * nkv, GLA_W, LANES)
    gqk = jnp.concatenate([jnp.tile(g_q[0][perm] * (HD ** -0.5 * LOG2E), A_HEADS),
                           jnp.tile(g_k[0][perm], A_KV)]).reshape(1, nq + nkv)
    bd = jnp.asarray(np.tile(np.kron(np.eye(LANES // HD), np.ones((HD, HD))), (2, 1)), BF)
    ct, st = _rope_tables(seq)
    ct_c = jnp.ones((tm_c, LANES), F32)
    st_c = jnp.zeros((tm_c, LANES), F32)

    ql, kl, vl, gl, zl = _proj(xl, g_norm1[0], mp, lat_mod, w0, widths0, outw0, tm_l, (ct, st, gqk, bd))
    qc, kc, vc, gc, zc = _proj(xc, g_norm1[0], mp, ctx_mod, w0, widths0, outw0, tm_c, (ct_c, st_c, gqk, bd))

    oa_l = _gqa(ql, kc, vc, kl, vl, bsz, 128)
    oa_c = _gqa(qc, kc, vc, None, None, bsz, min(128, nc))

    w2f = jnp.zeros((LANES, B_HEADS * B_DK), F32).at[:B_RANK].set(w_a2_f[0]).astype(BF)
    w2b = jnp.zeros((LANES, B_HEADS * B_DK), F32).at[B_RANK:2 * B_RANK].set(w_a2_b[0]).astype(BF)
    ob_c, ob_l = _gla(gc, gl, zc, zl, w2f, w2b, b_a_f[0].reshape(1, -1), b_a_b[0].reshape(1, -1),
                      jnp.tile(g_gla[0], B_HEADS).reshape(1, -1), bsz)

    wo = w_out_ab[0].astype(BF)
    wg, wu, wd = w_ff_gate[0].astype(BF), w_ff_up[0].astype(BF), w_ff_down[0].astype(BF)
    xl = _ffn(xl, g_norm2[0], mp, lat_mod, [(oa_l, wo[:nq]), (ob_l, wo[nq:])], wg, wu, wd, tm_l, 512)
    xc = _ffn(xc, g_norm2[0], mp, ctx_mod, [(oa_c, wo[:nq]), (ob_c, wo[nq:])], wg, wu, wd, tm_c, 512)

    mp = mods[1]
    cw = C_HEADS * HD
    w1 = w_in_c[0].astype(BF)
    q1, k1, v1 = _proj(xl, g_norm1[1], mp, lat_mod, w1, (cw, cw, cw), (cw, cw, cw), tm_l,
                       scales=(HD ** -0.5 * LOG2E, 1.0, 1.0))
    k1c, v1c = _proj(xc, g_norm1[1], mp, ctx_mod, w1[:, cw:], (cw, cw), (cw, cw), tm_c)

    bias = _na_bias(rpb_c[0], seq // GRID_W) * LOG2E
    o1 = _natten(q1, k1, v1, k1c, v1c, bias, bsz)

    wr = jnp.pad(w_router[0], ((0, 0), (0, LANES - N_EXPERTS)))
    out = _moe(xl, g_norm2[1], mp, lat_mod, lambda r: r // seq, [(o1, w_out_c[0].astype(BF))], wr,
               w_moe_gate[0].astype(BF), w_moe_up[0].astype(BF), w_moe_down[0].astype(BF), g_final, tm_l)
    return out.reshape(bsz, seq, D)
```

```python
import functools

import numpy as np
import jax
import jax.numpy as jnp
from jax import lax
from jax.experimental import pallas as pl
from jax.experimental.pallas import tpu as pltpu

F32 = jnp.float32
BF = jnp.bfloat16

D = 1024
EPS = 1e-6
N_MOD = 6
MOD_ROWS = 8
GRID_W = 64
HD = 64
A_HEADS = 8
A_KV = 2
A_GROUP = A_HEADS // A_KV
ROPE_THETA = 10000.0
LOG2E = 1.4426950408889634
B_HEADS = 4
B_DK = 64
B_DV = 128
B_RANK = 16
B_TAU = 16.0
B_CHUNK = 64
C_HEADS = 16
NA_ROWS = 8
NA_COLS = 16
N_EXPERTS = 8
LANES = 128
VMEM_LIMIT = 56 * 2 ** 20

NEG_INF = float("-inf")


def _params(sem):
    return pltpu.CompilerParams(dimension_semantics=sem, vmem_limit_bytes=VMEM_LIMIT)


def _dot(a, b):
    return jnp.dot(a, b, preferred_element_type=F32)


def _dot_nt(a, b):
    return lax.dot_general(a, b, (((1,), (1,)), ((), ())), preferred_element_type=F32)


def _dot_tn(a, b):
    return lax.dot_general(a, b, (((0,), (0,)), ((), ())), preferred_element_type=F32)


def _silu(x):
    return x / (1.0 + jnp.exp(-x))


def _split_dot(lhs_bf_exact, x):
    hi = x.astype(BF)
    lo = (x - hi.astype(F32)).astype(BF)
    return _dot(lhs_bf_exact, hi) + _dot(lhs_bf_exact, lo)


def _norm_mod(x, g, shift, scale):
    ms = jnp.mean(x * x, axis=-1, keepdims=True)
    return (x * lax.rsqrt(ms + EPS) * g) * (1.0 + scale) + shift


def _adaln_body(c_ref, w_ref, b_ref, o_ref):
    s = _silu(c_ref[...])
    o_ref[0] = _dot(s.astype(BF), w_ref[0].astype(BF)) + b_ref[0]


def _adaln(cond, w_mod, b_mod):
    depth, _, n = w_mod.shape
    rows = cond.shape[0]
    tn = 1536
    return pl.pallas_call(
        _adaln_body,
        grid=(depth, n // tn),
        in_specs=[pl.BlockSpec((rows, D), lambda l, j: (0, 0)),
                  pl.BlockSpec((1, D, tn), lambda l, j: (l, 0, j)),
                  pl.BlockSpec((1, 1, tn), lambda l, j: (l, 0, j))],
        out_specs=pl.BlockSpec((1, rows, tn), lambda l, j: (l, 0, j)),
        out_shape=jax.ShapeDtypeStruct((depth, rows, n), F32),
        compiler_params=_params(("arbitrary", "arbitrary")),
        name="adaln",
    )(cond, w_mod, b_mod.reshape(depth, 1, n))


def _qk_norm_rope(p, gqk, bd, ct, st):
    pp = p * p
    sums = []
    for c in range(p.shape[1] // LANES):
        x = pp[:, c * LANES:(c + 1) * LANES]
        hi = x.astype(BF)
        lo = (x - hi.astype(F32)).astype(BF)
        sums.append(_dot(jnp.concatenate([hi, lo], axis=1), bd))
    ss = jnp.concatenate(sums, axis=1)
    y = p * lax.rsqrt(ss * (1.0 / HD) + EPS) * gqk
    lane = lax.broadcasted_iota(jnp.int32, ct.shape, 1)
    first = (lane % HD) < (HD // 2)
    outs = []
    for c in range(p.shape[1] // LANES):
        yc = y[:, c * LANES:(c + 1) * LANES]
        partner = jnp.where(first, pltpu.roll(yc, LANES - HD // 2, 1), pltpu.roll(yc, HD // 2, 1))
        outs.append(yc * ct + partner * st)
    return jnp.concatenate(outs, axis=1)


def _values_with_ones(p):
    lane = lax.broadcasted_iota(jnp.int32, (p.shape[0], LANES), 1)
    low = lane < HD
    tiles = []
    for c in range(p.shape[1] // LANES):
        x = p[:, c * LANES:(c + 1) * LANES]
        tiles += [jnp.where(low, x, 1.0), jnp.where(low, 1.0, x)]
    return jnp.concatenate(tiles, axis=1)


def _proj_body(x_ref, g_ref, m_ref, w_ref, *rest, widths, kinds, scales):
    if "qk" in kinds:
        ct_ref, st_ref, gqk_ref, bd_ref = rest[:4]
        o_refs = rest[4:]
    else:
        o_refs = rest
    h = _norm_mod(x_ref[...], g_ref[...], m_ref[0, 0:1, :], m_ref[0, 1:2, :]).astype(BF)
    off = 0
    oi = 0
    for wd, kind, scale in zip(widths, kinds, scales):
        p = _dot(h, w_ref[:, off:off + wd])
        off += wd
        if scale != 1.0:
            p = p * scale
        if kind == "qk":
            p = _qk_norm_rope(p, gqk_ref[...], bd_ref[...], ct_ref[...], st_ref[...])
            nq = A_HEADS * HD
            o_refs[oi][...] = p[:, :nq].astype(BF)
            o_refs[oi + 1][...] = p[:, nq:].astype(BF)
            oi += 2
        else:
            if kind == "values":
                p = _values_with_ones(p)
            o_refs[oi][...] = p.astype(BF)
            oi += 1


def _proj(x2, g, mpack, mod_of_tile, w, widths, kinds, out_widths, tm, rope_args=None, scales=None):
    rows = x2.shape[0]
    n = w.shape[1]
    in_specs = [pl.BlockSpec((tm, D), lambda i: (i, 0)),
                pl.BlockSpec((1, D), lambda i: (0, 0)),
                pl.BlockSpec((1, MOD_ROWS, D), lambda i: (mod_of_tile(i), 0, 0)),
                pl.BlockSpec((D, n), lambda i: (0, 0))]
    args = [x2, g.reshape(1, D), mpack, w]
    if rope_args is not None:
        ct, st, gqk, bd = rope_args
        nt = ct.shape[0] // tm
        in_specs += [pl.BlockSpec((tm, LANES), lambda i: (i % nt, 0)),
                     pl.BlockSpec((tm, LANES), lambda i: (i % nt, 0)),
                     pl.BlockSpec(gqk.shape, lambda i: (0, 0)),
                     pl.BlockSpec(bd.shape, lambda i: (0, 0))]
        args += [ct, st, gqk, bd]
    return pl.pallas_call(
        functools.partial(_proj_body, widths=widths, kinds=kinds, scales=scales or (1.0,) * len(widths)),
        grid=(rows // tm,),
        in_specs=in_specs,
        out_specs=[pl.BlockSpec((tm, ow), lambda i: (i, 0)) for ow in out_widths],
        out_shape=[jax.ShapeDtypeStruct((rows, ow), BF) for ow in out_widths],
        compiler_params=_params(("arbitrary",)),
        name="norm_mod_proj",
    )(*args)


GQA_UNIT = 2

def _gqa_body(q_ref, kc_ref, vc_ref, *rest, with_lat, tq):
    if with_lat:
        kl_ref, vl_ref, o_ref = rest
    else:
        (o_ref,) = rest
    units = [(j, t0) for j in range(A_KV) for t0 in range(0, A_GROUP, GQA_UNIT)]
    lane = lax.broadcasted_iota(jnp.int32, (tq, LANES), 1)
    own = [(lane // HD) == j for j in range(A_KV)]

    def scores(unit):
        j, t0 = unit
        qg = jnp.concatenate(
            [jnp.where(own[j], q_ref[:, LANES * t:LANES * (t + 1)], 0.0) for t in range(t0, t0 + GQA_UNIT)],
            axis=0)
        s_c = _dot_nt(qg, kc_ref[...])
        s_l = _dot_nt(qg, kl_ref[...]) if with_lat else None
        return s_c, s_l

    res = {}
    nxt = scores(units[0])
    for u, (j, t0) in enumerate(units):
        s_c, s_l = nxt
        if u + 1 < len(units):
            nxt = scores(units[u + 1])
        vcols = slice(LANES * j, LANES * (j + 1))
        m = jnp.max(s_c, axis=1, keepdims=True)
        if with_lat:
            m = jnp.maximum(m, jnp.max(s_l, axis=1, keepdims=True))
        acc = _dot(jnp.exp2(s_c - m).astype(BF), vc_ref[:, vcols])
        if with_lat:
            acc = acc + _dot(jnp.exp2(s_l - m).astype(BF), vl_ref[:, vcols])
        den_lane = HD * (1 - j)
        o = acc / acc[:, den_lane:den_lane + 1]
        for g in range(GQA_UNIT):
            res[(j, t0 + g)] = o[g * tq:(g + 1) * tq]
    for t in range(A_GROUP):
        o_ref[:, LANES * t:LANES * (t + 1)] = jnp.where(own[0], res[(0, t)], res[(1, t)]).astype(BF)


def _gqa(q, k_ctx, v_ctx, k_lat, v_lat, bsz, tq):
    sq = q.shape[0] // bsz
    nc = k_ctx.shape[0] // bsz
    nq = sq // tq
    with_lat = k_lat is not None
    in_specs = [pl.BlockSpec((tq, A_HEADS * HD), lambda b, i: (b * nq + i, 0)),
                pl.BlockSpec((nc, A_KV * HD), lambda b, i: (b, 0)),
                pl.BlockSpec((nc, 2 * A_KV * HD), lambda b, i: (b, 0))]
    args = [q, k_ctx, v_ctx]
    if with_lat:
        sl = k_lat.shape[0] // bsz
        in_specs += [pl.BlockSpec((sl, A_KV * HD), lambda b, i: (b, 0)),
                     pl.BlockSpec((sl, 2 * A_KV * HD), lambda b, i: (b, 0))]
        args += [k_lat, v_lat]
    return pl.pallas_call(
        functools.partial(_gqa_body, with_lat=with_lat, tq=tq),
        grid=(bsz, nq),
        in_specs=in_specs,
        out_specs=pl.BlockSpec((tq, A_HEADS * HD), lambda b, i: (b * nq + i, 0)),
        out_shape=jax.ShapeDtypeStruct(q.shape, BF),
        compiler_params=_params(("arbitrary", "arbitrary")),
        name="gqa_attention",
    )(*args)


GLA_Q = 0
GLA_K = B_HEADS * B_DK
GLA_V = 2 * B_HEADS * B_DK
GLA_R = GLA_V + B_HEADS * B_DV
GLA_W = GLA_R + B_HEADS * B_DV
GLA_GROUP = 8


def _gla_body(gc_ref, gl_ref, zc_ref, zl_ref, w2f_ref, w2b_ref, bf_ref, bb_ref, gg_ref,
              oc_ref, ol_ref, ofc_ref, ofl_ref, st_ref):
    nk = B_HEADS * B_DK
    ch = B_CHUNK
    row = lax.broadcasted_iota(jnp.int32, (ch, ch), 0)
    col = lax.broadcasted_iota(jnp.int32, (ch, ch), 1)
    lane = lax.broadcasted_iota(jnp.int32, (ch, nk), 1)
    head_masks = [(lane // B_DK) == h for h in range(B_HEADS)]

    def group_out(g_ref, z_ref, r0s, fwd, s_t):
        w2 = (w2f_ref if fwd else w2b_ref)[...]
        bias = (bf_ref if fwd else bb_ref)[...]
        keep = (row >= col) if fwd else (row <= col)
        tri = jnp.where(keep, 1.0, 0.0).astype(BF)
        blks = [g_ref[pl.ds(r0, ch), :] for r0 in r0s]
        pres = [_dot(z_ref[pl.ds(r0, ch), :], w2) + bias for r0 in r0s]
        las = [-(jnp.maximum(-p, 0.0) + jnp.log(1.0 + jnp.exp(-jnp.abs(p)))) * (1.0 / B_TAU) for p in pres]
        bcums = [_split_dot(tri, la) for la in las]
        qms, kis, kss, decs = [], [], [], []
        for blk, bcum in zip(blks, bcums):
            q = blk[:, GLA_Q:GLA_Q + nk].astype(F32)
            k = blk[:, GLA_K:GLA_K + nk].astype(F32)
            b_last = bcum[ch - 1:ch, :] if fwd else bcum[0:1, :]
            q_dec = q * jnp.exp(bcum)
            k_state = k * jnp.exp(b_last - bcum)
            qms.append([jnp.where(m, q_dec, 0.0).astype(BF) for m in head_masks])
            kis.append((k * jnp.exp(-bcum)).astype(BF))
            kss.append([jnp.where(m, k_state, 0.0).astype(BF) for m in head_masks])
            decs.append(jnp.exp(b_last))
        vs = [[blk[:, GLA_V + B_DV * h:GLA_V + B_DV * (h + 1)] for h in range(B_HEADS)] for blk in blks]
        atts = [[jnp.where(keep, _dot_nt(qm, ki), 0.0).astype(BF) for qm in qmu] for qmu, ki in zip(qms, kis)]
        o_intra = [[_dot(a, v) for a, v in zip(au, vu)] for au, vu in zip(atts, vs)]
        d_ss = []
        for vu, ksu in zip(vs, kss):
            d_s = _dot_tn(vu[0], ksu[0])
            for h in range(1, B_HEADS):
                d_s = d_s + _dot_tn(vu[h], ksu[h])
            d_ss.append(d_s)
        outs = []
        for u in range(len(r0s)):
            s_b = s_t.astype(BF)
            outs.append(jnp.concatenate(
                [o_intra[u][h] + _dot_nt(qms[u][h], s_b) for h in range(B_HEADS)], axis=1))
            s_t = decs[u] * s_t + d_ss[u]
        return outs, blks, s_t

    def fwd_pass(g_ref, z_ref, of_ref, n):
        grp = min(GLA_GROUP, n)
        assert n % grp == 0

        def body(i, carry):
            r0s = [pl.multiple_of((i * grp + u) * ch, ch) for u in range(grp)]
            outs, _, s_t = group_out(g_ref, z_ref, r0s, True, st_ref[...])
            for r0, o in zip(r0s, outs):
                of_ref[pl.ds(r0, ch), :] = o
            st_ref[...] = s_t
            return carry
        lax.fori_loop(0, n // grp, body, 0)

    def bwd_pass(g_ref, z_ref, of_ref, o_ref, n):
        grp = min(GLA_GROUP, n)
        assert n % grp == 0

        def body(i, carry):
            r0s = [pl.multiple_of((n - 1 - (i * grp + u)) * ch, ch) for u in range(grp)]
            outs, blks, s_t = group_out(g_ref, z_ref, r0s, False, st_ref[...])
            for r0, o, blk in zip(r0s, outs, blks):
                o = o + of_ref[pl.ds(r0, ch), :]
                parts = []
                for h in range(B_HEADS):
                    oh = o[:, B_DV * h:B_DV * (h + 1)]
                    ms = jnp.mean(oh * oh, axis=-1, keepdims=True)
                    parts.append(oh * lax.rsqrt(ms + EPS))
                y = jnp.concatenate(parts, axis=1) * gg_ref[...]
                r = blk[:, GLA_R:GLA_R + B_HEADS * B_DV].astype(F32)
                o_ref[pl.ds(r0, ch), :] = (y * _silu(r)).astype(BF)
            st_ref[...] = s_t
            return carry
        lax.fori_loop(0, n // grp, body, 0)

    nc = gc_ref.shape[0] // ch
    nl = gl_ref.shape[0] // ch
    st_ref[...] = jnp.zeros_like(st_ref)
    fwd_pass(gc_ref, zc_ref, ofc_ref, nc)
    fwd_pass(gl_ref, zl_ref, ofl_ref, nl)
    st_ref[...] = jnp.zeros_like(st_ref)
    bwd_pass(gc_ref, zc_ref, ofc_ref, oc_ref, nc)
    bwd_pass(gl_ref, zl_ref, ofl_ref, ol_ref, nl)


def _gla(g_ctx, g_lat, z_ctx, z_lat, w2f, w2b, b_f, b_b, gg, bsz):
    nc = g_ctx.shape[0] // bsz
    sl = g_lat.shape[0] // bsz
    nk = B_HEADS * B_DK
    nv = B_HEADS * B_DV
    full = lambda shape: pl.BlockSpec(shape, lambda b: (0, 0))
    return pl.pallas_call(
        _gla_body,
        grid=(bsz,),
        in_specs=[pl.BlockSpec((nc, GLA_W), lambda b: (b, 0)),
                  pl.BlockSpec((sl, GLA_W), lambda b: (b, 0)),
                  pl.BlockSpec((nc, LANES), lambda b: (b, 0)),
                  pl.BlockSpec((sl, LANES), lambda b: (b, 0)),
                  full((LANES, nk)), full((LANES, nk)), full((1, nk)), full((1, nk)), full((1, nv))],
        out_specs=[pl.BlockSpec((nc, nv), lambda b: (b, 0)),
                   pl.BlockSpec((sl, nv), lambda b: (b, 0))],
        out_shape=[jax.ShapeDtypeStruct((g_ctx.shape[0], nv), BF),
                   jax.ShapeDtypeStruct((g_lat.shape[0], nv), BF)],
        scratch_shapes=[pltpu.VMEM((nc, nv), F32), pltpu.VMEM((sl, nv), F32),
                        pltpu.VMEM((B_DV, nk), F32)],
        compiler_params=_params(("arbitrary",)),
        name="gla_bidir",
    )(g_ctx, g_lat, z_ctx, z_lat, w2f, w2b, b_f, b_b, gg)


def _mixer_residual(x_ref, m_ref, pair_refs):
    acc = _dot(pair_refs[0][...], pair_refs[1][...])
    for t in range(1, len(pair_refs) // 2):
        acc = acc + _dot(pair_refs[2 * t][...], pair_refs[2 * t + 1][...])
    return x_ref[...] + m_ref[0, 2:3, :] * acc


def _pair_specs(pairs, tm, index):
    in_specs, args = [], []
    for a, w in pairs:
        in_specs += [pl.BlockSpec((tm, a.shape[1]), index), pl.BlockSpec(w.shape, lambda *_: (0, 0))]
        args += [a, w]
    return in_specs, args


def _ffn_body(x_ref, g_ref, m_ref, *rest, n_pairs):
    pair_refs = rest[:2 * n_pairs]
    wg_ref, wu_ref, wd_ref, o_ref, xn_ref, h_ref, acc_ref = rest[2 * n_pairs:]
    j = pl.program_id(1)

    @pl.when(j == 0)
    def _():
        xn = _mixer_residual(x_ref, m_ref, pair_refs)
        xn_ref[...] = xn
        h_ref[...] = _norm_mod(xn, g_ref[...], m_ref[0, 3:4, :], m_ref[0, 4:5, :]).astype(BF)
        acc_ref[...] = jnp.zeros_like(acc_ref)

    h = h_ref[...]
    a = (_silu(_dot(h, wg_ref[...])) * _dot(h, wu_ref[...])).astype(BF)
    acc_ref[...] += _dot(a, wd_ref[...])

    @pl.when(j == pl.num_programs(1) - 1)
    def _():
        o_ref[...] = xn_ref[...] + m_ref[0, 5:6, :] * acc_ref[...]


def _ffn(x2, g, mpack, mod_of_tile, pairs, wg, wu, wd, tm, tf):
    rows = x2.shape[0]
    ff = wg.shape[1]
    pair_specs, pair_args = _pair_specs(pairs, tm, lambda i, j: (i, 0))
    return pl.pallas_call(
        functools.partial(_ffn_body, n_pairs=len(pairs)),
        grid=(rows // tm, ff // tf),
        in_specs=[pl.BlockSpec((tm, D), lambda i, j: (i, 0)),
                  pl.BlockSpec((1, D), lambda i, j: (0, 0)),
                  pl.BlockSpec((1, MOD_ROWS, D), lambda i, j: (mod_of_tile(i), 0, 0)),
                  *pair_specs,
                  pl.BlockSpec((D, tf), lambda i, j: (0, j)),
                  pl.BlockSpec((D, tf), lambda i, j: (0, j)),
                  pl.BlockSpec((tf, D), lambda i, j: (j, 0))],
        out_specs=pl.BlockSpec((tm, D), lambda i, j: (i, 0)),
        out_shape=jax.ShapeDtypeStruct((rows, D), F32),
        scratch_shapes=[pltpu.VMEM((tm, D), F32), pltpu.VMEM((tm, D), BF), pltpu.VMEM((tm, D), F32)],
        compiler_params=_params(("arbitrary", "arbitrary")),
        name="ffn_swiglu",
    )(x2, g.reshape(1, D), mpack, *pair_args, wg, wu, wd)


NA_QROWS = 4
NA_KROWS = 12
NA_TQ = NA_QROWS * GRID_W
NA_HG = 8
NA_W = NA_HG * HD


def _na_body(q_ref, k0_ref, k1_ref, k2_ref, v0_ref, v1_ref, v2_ref, kc_ref, vc_ref, b_ref, o_ref):
    pair = LANES // HD
    lane = lax.broadcasted_iota(jnp.int32, (NA_TQ, LANES), 1)
    own = [(lane // HD) == half for half in range(pair)]

    def scores(hh):
        tile = slice(LANES * (hh // pair), LANES * (hh // pair + 1))
        q = jnp.where(own[hh % pair], q_ref[:, tile], 0.0)
        s_loc = jnp.concatenate([_dot_nt(q, kr[:, tile]) for kr in (k0_ref, k1_ref, k2_ref)], axis=1)
        return s_loc + b_ref[0, hh], _dot_nt(q, kc_ref[:, tile])

    nxt = scores(0)
    outs = []
    for hh in range(NA_HG):
        s_loc, s_ctx = nxt
        if hh + 1 < NA_HG:
            nxt = scores(hh + 1)
        vcols = slice(LANES * hh, LANES * (hh + 1))
        m = jnp.maximum(jnp.max(s_loc, axis=1, keepdims=True), jnp.max(s_ctx, axis=1, keepdims=True))
        p_loc = jnp.exp2(s_loc - m).astype(BF)
        acc = _dot(jnp.exp2(s_ctx - m).astype(BF), vc_ref[:, vcols])
        for t, vr in enumerate((v0_ref, v1_ref, v2_ref)):
            acc = acc + _dot(p_loc[:, t * NA_TQ:(t + 1) * NA_TQ], vr[:, vcols])
        den_lane = HD * (1 - hh % pair)
        outs.append(acc / acc[:, den_lane:den_lane + 1])
        if len(outs) == pair:
            t0 = hh // pair
            o_ref[:, LANES * t0:LANES * (t0 + 1)] = jnp.where(own[0], outs[0], outs[1]).astype(BF)
            outs = []


def _na_bias(rpb, rows):
    heads, n_dr, n_dc = rpb.shape
    nblk = rows // NA_QROWS
    pad_l = GRID_W - NA_COLS
    rext = jnp.pad(rpb.astype(F32), ((0, 0), (0, 0), (pad_l, 2 * GRID_W - n_dc - pad_l)))
    flat = jnp.tile(rext, (1, 1, GRID_W))[:, :, :GRID_W * (2 * GRID_W - 1)]
    toe = flat.reshape(heads, n_dr, GRID_W, 2 * GRID_W - 1)[:, :, :, GRID_W - 1:]
    cols = np.arange(GRID_W)
    cs = np.clip(cols - NA_COLS // 2, 0, GRID_W - NA_COLS)
    col_ok = (cols[None, :] >= cs[:, None]) & (cols[None, :] < cs[:, None] + NA_COLS)
    toe = jnp.where(col_ok, toe, NEG_INF)
    dead = jnp.full((heads, GRID_W, GRID_W), NEG_INF, F32)
    kinds = []
    for j in (0, min(1, nblk - 1), nblk - 1):
        ws = NA_QROWS * int(np.clip(j - 1, 0, nblk - 3))
        q_rows = []
        for qr in range(NA_QROWS):
            r = NA_QROWS * j + qr
            rs = int(np.clip(r - NA_ROWS // 2, 0, rows - NA_ROWS))
            blocks = [toe[:, ws + kr - r + NA_ROWS - 1] if rs <= ws + kr < rs + NA_ROWS else dead
                      for kr in range(NA_KROWS)]
            q_rows.append(jnp.concatenate(blocks, axis=-1))
        kinds.append(jnp.concatenate(q_rows, axis=-2))
    return jnp.stack(kinds)


def _natten(q, k, v, k_ctx, v_ctx, bias, bsz):
    s = q.shape[0] // bsz
    nc = k_ctx.shape[0] // bsz
    nblk = s // NA_TQ
    ngrp = C_HEADS // NA_HG
    v_w = NA_HG * LANES

    def kv_spec(t, width):
        return pl.BlockSpec((NA_TQ, width),
                            lambda j, g, b: (b * nblk + jnp.clip(j - 1, 0, nblk - 3) + t, g))

    def kind(j):
        return jnp.where(j == 0, 0, jnp.where(j == nblk - 1, 2, 1))

    return pl.pallas_call(
        _na_body,
        grid=(nblk, ngrp, bsz),
        in_specs=[pl.BlockSpec((NA_TQ, NA_W), lambda j, g, b: (b * nblk + j, g)),
                  kv_spec(0, NA_W), kv_spec(1, NA_W), kv_spec(2, NA_W),
                  kv_spec(0, v_w), kv_spec(1, v_w), kv_spec(2, v_w),
                  pl.BlockSpec((nc, NA_W), lambda j, g, b: (b, g)),
                  pl.BlockSpec((nc, v_w), lambda j, g, b: (b, g)),
                  pl.BlockSpec((1, NA_HG, NA_TQ, NA_KROWS * GRID_W), lambda j, g, b: (kind(j), g, 0, 0))],
        out_specs=pl.BlockSpec((NA_TQ, NA_W), lambda j, g, b: (b * nblk + j, g)),
        out_shape=jax.ShapeDtypeStruct(q.shape, BF),
        compiler_params=_params(("arbitrary", "arbitrary", "arbitrary")),
        name="natten",
    )(q, k, k, k, v, v, v, k_ctx, v_ctx, bias)


MOE_TM = 512
MOE_TF = 1792
META_E1, META_E2, META_R1, META_R2, META_G1, META_G2 = range(6)
ROW_TILE = D // LANES


def _to_token_tiles(dst_ref, value):
    n = value.shape[0]
    for s in range(ROW_TILE):
        dst_ref[pl.ds(s, n, stride=ROW_TILE), :] = value[:, LANES * s:LANES * (s + 1)]


def _from_token_tiles(src_ref, n):
    return jnp.concatenate([src_ref[pl.ds(s, n, stride=ROW_TILE), :] for s in range(ROW_TILE)], axis=1)


def _route_body(x_ref, g_ref, m_ref, *rest, n_pairs):
    pair_refs = rest[:2 * n_pairs]
    wr_ref, xn_ref, h_ref, meta_ref, cnt_ref, tri_ref, carry_ref = rest[2 * n_pairs:]
    tm = x_ref.shape[0]

    @pl.when(pl.program_id(0) == 0)
    def _():
        row = lax.broadcasted_iota(jnp.int32, (tm, tm), 0)
        col = lax.broadcasted_iota(jnp.int32, (tm, tm), 1)
        tri_ref[...] = jnp.where(row >= col, 1.0, 0.0).astype(BF)
        carry_ref[...] = jnp.zeros_like(carry_ref)

    xn = _mixer_residual(x_ref, m_ref, pair_refs)
    xn_ref[...] = xn
    h = _norm_mod(xn, g_ref[...], m_ref[0, 3:4, :], m_ref[0, 4:5, :])
    _to_token_tiles(h_ref, h)
    h_hi = h.astype(BF)
    h_lo = (h - h_hi.astype(F32)).astype(BF)
    w = wr_ref[...]
    w_hi = w.astype(BF)
    w_lo = (w - w_hi.astype(F32)).astype(BF)
    logits = _dot(h_hi, w_hi) + (_dot(h_lo, w_hi) + _dot(h_hi, w_lo))
    lane = lax.broadcasted_iota(jnp.int32, logits.shape, 1)
    lg = jnp.where(lane < N_EXPERTS, logits, NEG_INF)
    m1 = jnp.max(lg, axis=1, keepdims=True)
    i1 = jnp.min(jnp.where(lg == m1, lane, LANES), axis=1, keepdims=True)
    lg2 = jnp.where(lane == i1, NEG_INF, lg)
    m2 = jnp.max(lg2, axis=1, keepdims=True)
    i2 = jnp.min(jnp.where(lg2 == m2, lane, LANES), axis=1, keepdims=True)
    e = jnp.exp(m2 - m1)
    g1 = 1.0 / (1.0 + e)
    g2 = e / (1.0 + e)
    oh = jnp.where(lane == i1, 1.0, 0.0) + jnp.where(lane == i2, 1.0, 0.0)
    cum = _dot(tri_ref[...], oh.astype(BF))
    base = carry_ref[0:1, :] + cum - oh
    r1 = jnp.sum(jnp.where(lane == i1, base, 0.0), axis=1, keepdims=True)
    r2 = jnp.sum(jnp.where(lane == i2, base, 0.0), axis=1, keepdims=True)
    carry_ref[...] = carry_ref[...] + cum[tm - 1:tm, :]
    cnt_ref[...] = carry_ref[...]
    meta = jnp.zeros(logits.shape, F32)
    for ln, val in ((META_E1, i1.astype(F32)), (META_E2, i2.astype(F32)), (META_R1, r1), (META_R2, r2),
                    (META_G1, g1), (META_G2, g2)):
        meta = jnp.where(lane == ln, val, meta)
    meta_ref[...] = meta


def _route(x2, g, mpack, mod_of_tile, pairs, wr, tm):
    rows = x2.shape[0]
    pair_specs, pair_args = _pair_specs(pairs, tm, lambda i: (i, 0))
    return pl.pallas_call(
        functools.partial(_route_body, n_pairs=len(pairs)),
        grid=(rows // tm,),
        in_specs=[pl.BlockSpec((tm, D), lambda i: (i, 0)),
                  pl.BlockSpec((1, D), lambda i: (0, 0)),
                  pl.BlockSpec((1, MOD_ROWS, D), lambda i: (mod_of_tile(i), 0, 0)),
                  *pair_specs,
                  pl.BlockSpec((D, LANES), lambda i: (0, 0))],
        out_specs=[pl.BlockSpec((tm, D), lambda i: (i, 0)),
                   pl.BlockSpec((tm * ROW_TILE, LANES), lambda i: (i, 0)),
                   pl.BlockSpec((tm, LANES), lambda i: (i, 0)),
                   pl.BlockSpec((8, LANES), lambda i: (0, 0))],
        out_shape=[jax.ShapeDtypeStruct((rows, D), F32),
                   jax.ShapeDtypeStruct((rows * ROW_TILE, LANES), F32),
                   jax.ShapeDtypeStruct((rows, LANES), F32),
                   jax.ShapeDtypeStruct((8, LANES), F32)],
        scratch_shapes=[pltpu.VMEM((tm, tm), BF), pltpu.VMEM((8, LANES), F32)],
        compiler_params=_params(("arbitrary",)),
        name="moe_route",
    )(x2, g.reshape(1, D), mpack, *pair_args, wr)


def _token_copy(src_ref, src_tok, dst_ref, dst_tok, sem):
    src = src_ref.at[pl.ds(pl.multiple_of(src_tok * ROW_TILE, ROW_TILE), ROW_TILE), :]
    dst = dst_ref.at[pl.ds(pl.multiple_of(dst_tok * ROW_TILE, ROW_TILE), ROW_TILE), :]
    return pltpu.make_async_copy(src, dst, sem)


def _experts_body(te_ref, nu_ref, cur_ref, nxt_ref, dprev_ref, h_hbm, wg_ref, wu_ref, wd_ref, y_hbm,
                  xbuf, h_ref, acc_ref, ybuf, gsem, ssem, *, n_steps):
    assert n_steps >= 2
    i = pl.program_id(0)
    j = pl.program_id(1)
    used = i < nu_ref[0]
    slot = i % 2
    other = 1 - slot

    def gather_row(idx_ref, r, sl):
        _token_copy(h_hbm, idx_ref[0, 0, r], xbuf.at[sl], r, gsem.at[sl]).start()

    def issue_gathers():
        for r in range(MOE_TM):
            gather_row(nxt_ref, r, other)

    def issue_scatters():
        for r in range(MOE_TM):
            _token_copy(ybuf.at[other], r, y_hbm, dprev_ref[0, 0, r], ssem.at[other]).start()

    def wait_rows(buf, sem):
        pltpu.make_async_copy(buf, buf, sem).wait()

    def swiglu_step():
        h = h_ref[...]
        a = (_silu(_dot(h, wg_ref[0])) * _dot(h, wu_ref[0])).astype(BF)
        return _dot(a, wd_ref[0])

    @pl.when((j == 0) & (i == 0))
    def _():
        ybuf[...] = jnp.zeros_like(ybuf)
        n_spare = MOE_TM * ROW_TILE
        for sl in range(2):
            spare = pltpu.make_async_copy(
                ybuf.at[sl], y_hbm.at[pl.ds(y_hbm.shape[0] - (2 - sl) * n_spare, n_spare), :], ssem.at[sl])
            spare.start()
            spare.wait()

        def body(r, c):
            gather_row(cur_ref, r, 0)
            return c
        lax.fori_loop(0, MOE_TM, body, 0, unroll=8)

    @pl.when(j == 0)
    def _():
        wait_rows(xbuf.at[slot], gsem.at[slot])

    @pl.when(used & (j == 0))
    def _():
        h_ref[...] = _from_token_tiles(xbuf.at[slot], MOE_TM).astype(BF)
        acc_ref[...] = swiglu_step()
        issue_gathers()

    if n_steps > 2:
        @pl.when(used & (j > 0) & (j < n_steps - 1))
        def _():
            acc_ref[...] += swiglu_step()

    @pl.when(used & (j == n_steps - 1))
    def _():
        acc_ref[...] += swiglu_step()
        issue_scatters()

    @pl.when(jnp.logical_not(used) & (j == 0))
    def _():
        issue_gathers()

    @pl.when(jnp.logical_not(used) & (j == n_steps - 1))
    def _():
        issue_scatters()

    @pl.when(j == n_steps - 1)
    def _():
        @pl.when(i > 0)
        def _():
            wait_rows(ybuf.at[slot], ssem.at[slot])

        @pl.when(used)
        def _():
            _to_token_tiles(ybuf.at[slot], acc_ref[...])

        @pl.when(i == pl.num_programs(0) - 1)
        def _():
            wait_rows(ybuf.at[other], ssem.at[other])
            wait_rows(xbuf.at[other], gsem.at[other])


def _experts(h_tiles, tok_of_slot, dest_of_slot, tile_expert, n_used, wg, wu, wd, out_tokens):
    n_tiles = tile_expert.shape[0]
    ff = wg.shape[2]
    nf = ff // MOE_TF
    spare_rows = out_tokens - MOE_TM + jnp.arange(MOE_TM, dtype=jnp.int32)
    dest_prev = jnp.concatenate([spare_rows, dest_of_slot[:-MOE_TM]])

    def jj(i, j, nu):
        return jnp.where(i < nu[0], j, nf - 1)

    def smem_rows(index):
        return pl.BlockSpec((1, 1, MOE_TM), lambda i, j, te, nu: (index(i), 0, 0), memory_space=pltpu.SMEM)

    grid_spec = pltpu.PrefetchScalarGridSpec(
        num_scalar_prefetch=2,
        grid=(n_tiles, nf),
        in_specs=[smem_rows(lambda i: i),
                  smem_rows(lambda i: jnp.minimum(i + 1, n_tiles - 1)),
                  smem_rows(lambda i: i),
                  pl.BlockSpec(memory_space=pl.ANY),
                  pl.BlockSpec((1, D, MOE_TF), lambda i, j, te, nu: (te[i], 0, jj(i, j, nu))),
                  pl.BlockSpec((1, D, MOE_TF), lambda i, j, te, nu: (te[i], 0, jj(i, j, nu))),
                  pl.BlockSpec((1, MOE_TF, D), lambda i, j, te, nu: (te[i], jj(i, j, nu), 0))],
        out_specs=pl.BlockSpec(memory_space=pl.ANY),
        scratch_shapes=[pltpu.VMEM((2, MOE_TM * ROW_TILE, LANES), F32),
                        pltpu.VMEM((MOE_TM, D), BF),
                        pltpu.VMEM((MOE_TM, D), F32),
                        pltpu.VMEM((2, MOE_TM * ROW_TILE, LANES), F32),
                        pltpu.SemaphoreType.DMA((2,)),
                        pltpu.SemaphoreType.DMA((2,))])
    tok3 = tok_of_slot.reshape(n_tiles, 1, MOE_TM)
    return pl.pallas_call(
        functools.partial(_experts_body, n_steps=nf),
        grid_spec=grid_spec,
        out_shape=jax.ShapeDtypeStruct((out_tokens * ROW_TILE, LANES), F32),
        compiler_params=_params(("arbitrary", "arbitrary")),
        name="moe_experts",
    )(tile_expert, n_used, tok3, tok3, dest_prev.reshape(n_tiles, 1, MOE_TM), h_tiles, wg, wu, wd)


def _combine_body(y1_ref, y2_ref, x_ref, m_ref, meta_ref, gf_ref, o_ref):
    n = x_ref.shape[0]
    meta = meta_ref[...]
    lane = lax.broadcasted_iota(jnp.int32, meta.shape, 1)
    g1 = jnp.sum(jnp.where(lane == META_G1, meta, 0.0), axis=1, keepdims=True)
    g2 = jnp.sum(jnp.where(lane == META_G2, meta, 0.0), axis=1, keepdims=True)
    moe = g1 * _from_token_tiles(y1_ref, n) + g2 * _from_token_tiles(y2_ref, n)
    y = x_ref[...] + m_ref[0, 5:6, :] * moe
    ms = jnp.mean(y * y, axis=-1, keepdims=True)
    o_ref[...] = y * lax.rsqrt(ms + EPS) * gf_ref[...]


def _combine(x2, mpack, mod_of_row, meta, y_tiles, g_final, tm):
    rows = x2.shape[0]
    nt = rows // tm
    return pl.pallas_call(
        _combine_body,
        grid=(nt,),
        in_specs=[pl.BlockSpec((tm * ROW_TILE, LANES), lambda i: (i, 0)),
                  pl.BlockSpec((tm * ROW_TILE, LANES), lambda i: (i + nt, 0)),
                  pl.BlockSpec((tm, D), lambda i: (i, 0)),
                  pl.BlockSpec((1, MOD_ROWS, D), lambda i: (mod_of_row(i * tm), 0, 0)),
                  pl.BlockSpec((tm, LANES), lambda i: (i, 0)),
                  pl.BlockSpec((1, D), lambda i: (0, 0))],
        out_specs=pl.BlockSpec((tm, D), lambda i: (i, 0)),
        out_shape=jax.ShapeDtypeStruct((rows, D), F32),
        compiler_params=_params(("arbitrary",)),
        name="moe_combine",
    )(y_tiles, y_tiles, x2, mpack, meta, g_final.reshape(1, D))


def _moe(x2, g, mpack, mod_of_tile, mod_of_row, pairs, wr, wg, wu, wd, g_final, tm):
    rows = x2.shape[0]
    ne = wg.shape[0]
    x2, h_tiles, meta, cnt = _route(x2, g, mpack, mod_of_tile, pairs, wr, tm)
    counts = cnt[0, :ne].astype(jnp.int32)
    tiles_e = (counts + MOE_TM - 1) // MOE_TM
    tile_end = jnp.cumsum(tiles_e)
    group_off = (tile_end - tiles_e) * MOE_TM
    n_tiles = 2 * rows // MOE_TM + ne
    n_slots = n_tiles * MOE_TM
    tile_expert = jnp.minimum(
        jnp.sum(jnp.arange(n_tiles)[:, None] >= tile_end[None, :], axis=1), ne - 1).astype(jnp.int32)
    n_used = tile_end[-1:].astype(jnp.int32)
    eids = jnp.arange(ne)

    def slot_of(lane_e, lane_r):
        e = meta[:, lane_e].astype(jnp.int32)
        off = jnp.sum(jnp.where(e[:, None] == eids[None, :], group_off[None, :], 0), axis=1)
        return off + meta[:, lane_r].astype(jnp.int32)

    tok = jnp.arange(rows, dtype=jnp.int32)
    dest = jnp.full((n_slots,), -1, jnp.int32).at[
        jnp.concatenate([slot_of(META_E1, META_R1), slot_of(META_E2, META_R2)])].set(
        jnp.concatenate([tok, rows + tok]), unique_indices=True, mode="promise_in_bounds")
    pad = dest < 0
    dest_of_slot = jnp.where(pad, 2 * rows + jnp.arange(n_slots, dtype=jnp.int32) % (2 * MOE_TM), dest)
    tok_of_slot = jnp.where(pad, 0, jnp.where(dest >= rows, dest - rows, dest))
    y_tiles = _experts(h_tiles, tok_of_slot, dest_of_slot, tile_expert, n_used, wg, wu, wd, 2 * rows + 2 * MOE_TM)
    return _combine(x2, mpack, mod_of_row, meta, y_tiles, g_final, MOE_TM)


def _rope_tables(seq_len):
    t = np.arange(seq_len)
    row = (t // GRID_W).astype(np.float32)
    col = (t % GRID_W).astype(np.float32)
    n_axis = HD // 4
    inv = jnp.power(ROPE_THETA, -jnp.arange(n_axis, dtype=F32) / n_axis)
    ang = jnp.concatenate([jnp.asarray(row)[:, None] * inv, jnp.asarray(col)[:, None] * inv], axis=-1)
    cos, sin = jnp.cos(ang), jnp.sin(ang)
    ct = jnp.tile(jnp.concatenate([cos, cos], axis=-1), (1, LANES // HD))
    st = jnp.tile(jnp.concatenate([-sin, sin], axis=-1), (1, LANES // HD))
    return ct, st


def kernel(x, c, ctx, c_ctx, w_mod, b_mod, g_norm1, g_norm2, w_in_ab, g_q, g_k, w_a2_f, b_a_f, w_a2_b,
           b_a_b, g_gla, w_out_ab, w_ff_gate, w_ff_up, w_ff_down, w_in_c, rpb_c, w_out_c, w_router,
           w_moe_gate, w_moe_up, w_moe_down, g_final):
    bsz, seq, _ = x.shape
    nc = ctx.shape[1]
    depth = w_mod.shape[0]
    assert depth == 2 and seq % GRID_W == 0

    cond_rows = -(-(bsz + 1) // 8) * 8
    cond = jnp.zeros((cond_rows, D), F32).at[:bsz].set(c).at[bsz].set(c_ctx)
    mods = _adaln(cond, w_mod, b_mod).reshape(depth, cond_rows, N_MOD, D)
    mods = jnp.pad(mods, ((0, 0), (0, 0), (0, MOD_ROWS - N_MOD), (0, 0)))

    xl = x.reshape(bsz * seq, D)
    xc = ctx.reshape(bsz * nc, D)
    tm_l = min(1024, seq)
    tm_c = min(1024, bsz * nc)
    tpb = seq // tm_l
    lat_mod = lambda i: i // tpb
    ctx_mod = lambda i: bsz

    mp = mods[0]
    w_in = w_in_ab[0]
    perm = np.concatenate([np.arange(0, HD, 2), np.arange(1, HD, 2)])
    nq, nkv = A_HEADS * HD, A_KV * HD
    o_q, o_k, o_v = 0, nq, nq + nkv
    o_bq = o_v + nkv
    o_bk = o_bq + B_HEADS * B_DK
    o_bv = o_bk + B_HEADS * B_DK
    o_br = o_bv + B_HEADS * B_DV
    o_zf = o_br + B_HEADS * B_DV
    o_end = o_zf + 2 * B_RANK
    assert A_KV * HD == LANES
    q_order = np.arange(A_HEADS).reshape(A_KV, A_GROUP).T.reshape(-1)
    wq = w_in[:, o_q:o_k].reshape(D, A_HEADS, HD)[:, q_order][:, :, perm].reshape(D, nq)
    wk = w_in[:, o_k:o_v].reshape(D, A_KV, HD)[:, :, perm].reshape(D, nkv)
    wz = jnp.pad(w_in[:, o_zf:o_end], ((0, 0), (0, LANES - 2 * B_RANK)))
    w0 = jnp.concatenate([wq, wk, w_in[:, o_v:o_bq], w_in[:, o_bq:o_bk] * (B_DK ** -0.5),
                          w_in[:, o_bk:o_zf], wz], axis=1).astype(BF)
    widths0 = (nq + nkv, nkv, GLA_W, LANES)
    outw0 = (nq, nkv, 2 * nkv, GLA_W, LANES)
    gqk = jnp.concatenate([jnp.tile(g_q[0][perm] * (HD ** -0.5 * LOG2E), A_HEADS),
                           jnp.tile(g_k[0][perm], A_KV)]).reshape(1, nq + nkv)
    bd = jnp.asarray(np.tile(np.kron(np.eye(LANES // HD), np.ones((HD, HD))), (2, 1)), BF)
    ct, st = _rope_tables(seq)
    ct_c = jnp.ones((tm_c, LANES), F32)
    st_c = jnp.zeros((tm_c, LANES), F32)

    kinds0 = ("qk", "values", "plain", "plain")
    ql, kl, vl, gl, zl = _proj(xl, g_norm1[0], mp, lat_mod, w0, widths0, kinds0, outw0, tm_l, (ct, st, gqk, bd))
    qc, kc, vc, gc, zc = _proj(xc, g_norm1[0], mp, ctx_mod, w0, widths0, kinds0, outw0, tm_c,
                               (ct_c, st_c, gqk, bd))

    oa_l = _gqa(ql, kc, vc, kl, vl, bsz, 256)
    oa_c = _gqa(qc, kc, vc, None, None, bsz, min(128, nc))

    w2f = jnp.zeros((LANES, B_HEADS * B_DK), F32).at[:B_RANK].set(w_a2_f[0]).astype(BF)
    w2b = jnp.zeros((LANES, B_HEADS * B_DK), F32).at[B_RANK:2 * B_RANK].set(w_a2_b[0]).astype(BF)
    ob_c, ob_l = _gla(gc, gl, zc, zl, w2f, w2b, b_a_f[0].reshape(1, -1), b_a_b[0].reshape(1, -1),
                      jnp.tile(g_gla[0], B_HEADS).reshape(1, -1), bsz)

    wo = w_out_ab[0].astype(BF)
    wo_a = wo[:nq].reshape(A_HEADS, HD, D)[q_order].reshape(nq, D)
    wo_b = wo[nq:]
    wg, wu, wd = w_ff_gate[0].astype(BF), w_ff_up[0].astype(BF), w_ff_down[0].astype(BF)
    xl = _ffn(xl, g_norm2[0], mp, lat_mod, [(oa_l, wo_a), (ob_l, wo_b)], wg, wu, wd, tm_l, 512)
    xc = _ffn(xc, g_norm2[0], mp, ctx_mod, [(oa_c, wo_a), (ob_c, wo_b)], wg, wu, wd, tm_c, 512)

    mp = mods[1]
    cw = C_HEADS * HD
    w1 = w_in_c[0].astype(BF)
    q1, k1, v1 = _proj(xl, g_norm1[1], mp, lat_mod, w1, (cw, cw, cw), ("plain", "plain", "values"),
                       (cw, cw, 2 * cw), tm_l, scales=(HD ** -0.5 * LOG2E, 1.0, 1.0))
    k1c, v1c = _proj(xc, g_norm1[1], mp, ctx_mod, w1[:, cw:], (cw, cw), ("plain", "values"), (cw, 2 * cw), tm_c)

    bias = _na_bias(rpb_c[0], seq // GRID_W) * LOG2E
    o1 = _natten(q1, k1, v1, k1c, v1c, bias, bsz)

    wr = jnp.pad(w_router[0], ((0, 0), (0, LANES - N_EXPERTS)))
    out = _moe(xl, g_norm2[1], mp, lat_mod, lambda r: r // seq, [(o1, w_out_c[0].astype(BF))], wr,
               w_moe_gate[0].astype(BF), w_moe_up[0].astype(BF), w_moe_down[0].astype(BF), g_final, tm_l)
    return out.reshape(bsz, seq, D)
```

```python
import functools

import numpy as np
import jax
import jax.numpy as jnp
from jax import lax
from jax.experimental import pallas as pl
from jax.experimental.pallas import tpu as pltpu

F32 = jnp.float32
BF = jnp.bfloat16

D = 1024
EPS = 1e-6
N_MOD = 6
MOD_ROWS = 8
GRID_W = 64
HD = 64
A_HEADS = 8
A_KV = 2
A_GROUP = A_HEADS // A_KV
ROPE_THETA = 10000.0
LOG2E = 1.4426950408889634
B_HEADS = 4
B_DK = 64
B_DV = 128
B_RANK = 16
B_TAU = 16.0
B_CHUNK = 64
C_HEADS = 16
NA_ROWS = 8
NA_COLS = 16
N_EXPERTS = 8
LANES = 128
VMEM_LIMIT = 56 * 2 ** 20

NEG_INF = float("-inf")


def _params(sem):
    return pltpu.CompilerParams(dimension_semantics=sem, vmem_limit_bytes=VMEM_LIMIT)


def _dot(a, b):
    return jnp.dot(a, b, preferred_element_type=F32)


def _dot_nt(a, b):
    return lax.dot_general(a, b, (((1,), (1,)), ((), ())), preferred_element_type=F32)


def _dot_tn(a, b):
    return lax.dot_general(a, b, (((0,), (0,)), ((), ())), preferred_element_type=F32)


def _silu(x):
    return x / (1.0 + jnp.exp(-x))


def _split_dot(lhs_bf_exact, x):
    hi = x.astype(BF)
    lo = (x - hi.astype(F32)).astype(BF)
    return _dot(lhs_bf_exact, hi) + _dot(lhs_bf_exact, lo)


def _norm_mod(x, g, shift, scale):
    ms = jnp.mean(x * x, axis=-1, keepdims=True)
    return (x * lax.rsqrt(ms + EPS) * g) * (1.0 + scale) + shift


def _adaln_body(c_ref, w_ref, b_ref, o_ref):
    s = _silu(c_ref[...])
    o_ref[0] = _dot(s.astype(BF), w_ref[0].astype(BF)) + b_ref[0]


def _adaln(cond, w_mod, b_mod):
    depth, _, n = w_mod.shape
    rows = cond.shape[0]
    tn = 1536
    return pl.pallas_call(
        _adaln_body,
        grid=(depth, n // tn),
        in_specs=[pl.BlockSpec((rows, D), lambda l, j: (0, 0)),
                  pl.BlockSpec((1, D, tn), lambda l, j: (l, 0, j)),
                  pl.BlockSpec((1, 1, tn), lambda l, j: (l, 0, j))],
        out_specs=pl.BlockSpec((1, rows, tn), lambda l, j: (l, 0, j)),
        out_shape=jax.ShapeDtypeStruct((depth, rows, n), F32),
        compiler_params=_params(("arbitrary", "arbitrary")),
        name="adaln",
    )(cond, w_mod, b_mod.reshape(depth, 1, n))


def _qk_norm_rope(p, gqk, bd, ct, st):
    pp = p * p
    sums = []
    for c in range(p.shape[1] // LANES):
        x = pp[:, c * LANES:(c + 1) * LANES]
        hi = x.astype(BF)
        lo = (x - hi.astype(F32)).astype(BF)
        sums.append(_dot(jnp.concatenate([hi, lo], axis=1), bd))
    ss = jnp.concatenate(sums, axis=1)
    y = p * lax.rsqrt(ss * (1.0 / HD) + EPS) * gqk
    lane = lax.broadcasted_iota(jnp.int32, ct.shape, 1)
    first = (lane % HD) < (HD // 2)
    outs = []
    for c in range(p.shape[1] // LANES):
        yc = y[:, c * LANES:(c + 1) * LANES]
        partner = jnp.where(first, pltpu.roll(yc, LANES - HD // 2, 1), pltpu.roll(yc, HD // 2, 1))
        outs.append(yc * ct + partner * st)
    return jnp.concatenate(outs, axis=1)


def _values_with_ones(p):
    lane = lax.broadcasted_iota(jnp.int32, (p.shape[0], LANES), 1)
    low = lane < HD
    tiles = []
    for c in range(p.shape[1] // LANES):
        x = p[:, c * LANES:(c + 1) * LANES]
        tiles += [jnp.where(low, x, 1.0), jnp.where(low, 1.0, x)]
    return jnp.concatenate(tiles, axis=1)


def _proj_body(x_ref, g_ref, m_ref, w_ref, *rest, widths, kinds, scales):
    if "qk" in kinds:
        ct_ref, st_ref, gqk_ref, bd_ref = rest[:4]
        o_refs = rest[4:]
    else:
        o_refs = rest
    h = _norm_mod(x_ref[...], g_ref[...], m_ref[0, 0:1, :], m_ref[0, 1:2, :]).astype(BF)
    off = 0
    oi = 0
    for wd, kind, scale in zip(widths, kinds, scales):
        p = _dot(h, w_ref[:, off:off + wd])
        off += wd
        if scale != 1.0:
            p = p * scale
        if kind == "qk":
            p = _qk_norm_rope(p, gqk_ref[...], bd_ref[...], ct_ref[...], st_ref[...])
            nq = A_HEADS * HD
            o_refs[oi][...] = p[:, :nq].astype(BF)
            o_refs[oi + 1][...] = p[:, nq:].astype(BF)
            oi += 2
        else:
            if kind == "values":
                p = _values_with_ones(p)
            o_refs[oi][...] = p.astype(BF)
            oi += 1


def _proj(x2, g, mpack, mod_of_tile, w, widths, kinds, out_widths, tm, rope_args=None, scales=None):
    rows = x2.shape[0]
    n = w.shape[1]
    in_specs = [pl.BlockSpec((tm, D), lambda i: (i, 0)),
                pl.BlockSpec((1, D), lambda i: (0, 0)),
                pl.BlockSpec((1, MOD_ROWS, D), lambda i: (mod_of_tile(i), 0, 0)),
                pl.BlockSpec((D, n), lambda i: (0, 0))]
    args = [x2, g.reshape(1, D), mpack, w]
    if rope_args is not None:
        ct, st, gqk, bd = rope_args
        nt = ct.shape[0] // tm
        in_specs += [pl.BlockSpec((tm, LANES), lambda i: (i % nt, 0)),
                     pl.BlockSpec((tm, LANES), lambda i: (i % nt, 0)),
                     pl.BlockSpec(gqk.shape, lambda i: (0, 0)),
                     pl.BlockSpec(bd.shape, lambda i: (0, 0))]
        args += [ct, st, gqk, bd]
    return pl.pallas_call(
        functools.partial(_proj_body, widths=widths, kinds=kinds, scales=scales or (1.0,) * len(widths)),
        grid=(rows // tm,),
        in_specs=in_specs,
        out_specs=[pl.BlockSpec((tm, ow), lambda i: (i, 0)) for ow in out_widths],
        out_shape=[jax.ShapeDtypeStruct((rows, ow), BF) for ow in out_widths],
        compiler_params=_params(("arbitrary",)),
        name="norm_mod_proj",
    )(*args)


GQA_UNIT = 2

def _gqa_body(q_ref, kc_ref, vc_ref, *rest, with_lat, tq):
    if with_lat:
        kl_ref, vl_ref, o_ref = rest
    else:
        (o_ref,) = rest
    units = [(j, t0) for j in range(A_KV) for t0 in range(0, A_GROUP, GQA_UNIT)]
    lane = lax.broadcasted_iota(jnp.int32, (tq, LANES), 1)
    own = [(lane // HD) == j for j in range(A_KV)]

    def scores(unit):
        j, t0 = unit
        qg = jnp.concatenate(
            [jnp.where(own[j], q_ref[:, LANES * t:LANES * (t + 1)], 0.0) for t in range(t0, t0 + GQA_UNIT)],
            axis=0)
        s_c = _dot_nt(qg, kc_ref[...])
        s_l = _dot_nt(qg, kl_ref[...]) if with_lat else None
        return s_c, s_l

    res = {}
    nxt = scores(units[0])
    for u, (j, t0) in enumerate(units):
        s_c, s_l = nxt
        if u + 1 < len(units):
            nxt = scores(units[u + 1])
        vcols = slice(LANES * j, LANES * (j + 1))
        m = jnp.max(s_c, axis=1, keepdims=True)
        if with_lat:
            m = jnp.maximum(m, jnp.max(s_l, axis=1, keepdims=True))
        acc = _dot(jnp.exp2(s_c - m).astype(BF), vc_ref[:, vcols])
        if with_lat:
            acc = acc + _dot(jnp.exp2(s_l - m).astype(BF), vl_ref[:, vcols])
        den_lane = HD * (1 - j)
        o = acc / acc[:, den_lane:den_lane + 1]
        for g in range(GQA_UNIT):
            res[(j, t0 + g)] = o[g * tq:(g + 1) * tq]
    for t in range(A_GROUP):
        o_ref[:, LANES * t:LANES * (t + 1)] = jnp.where(own[0], res[(0, t)], res[(1, t)]).astype(BF)


def _gqa(q, k_ctx, v_ctx, k_lat, v_lat, bsz, tq):
    sq = q.shape[0] // bsz
    nc = k_ctx.shape[0] // bsz
    nq = sq // tq
    with_lat = k_lat is not None
    in_specs = [pl.BlockSpec((tq, A_HEADS * HD), lambda b, i: (b * nq + i, 0)),
                pl.BlockSpec((nc, A_KV * HD), lambda b, i: (b, 0)),
                pl.BlockSpec((nc, 2 * A_KV * HD), lambda b, i: (b, 0))]
    args = [q, k_ctx, v_ctx]
    if with_lat:
        sl = k_lat.shape[0] // bsz
        in_specs += [pl.BlockSpec((sl, A_KV * HD), lambda b, i: (b, 0)),
                     pl.BlockSpec((sl, 2 * A_KV * HD), lambda b, i: (b, 0))]
        args += [k_lat, v_lat]
    return pl.pallas_call(
        functools.partial(_gqa_body, with_lat=with_lat, tq=tq),
        grid=(bsz, nq),
        in_specs=in_specs,
        out_specs=pl.BlockSpec((tq, A_HEADS * HD), lambda b, i: (b * nq + i, 0)),
        out_shape=jax.ShapeDtypeStruct(q.shape, BF),
        compiler_params=_params(("arbitrary", "arbitrary")),
        name="gqa_attention",
    )(*args)


GLA_Q = 0
GLA_K = B_HEADS * B_DK
GLA_V = 2 * B_HEADS * B_DK
GLA_R = GLA_V + B_HEADS * B_DV
GLA_W = GLA_R + B_HEADS * B_DV
GLA_GROUP = 8


def _gla_body(gc_ref, gl_ref, zc_ref, zl_ref, w2f_ref, w2b_ref, bf_ref, bb_ref, gg_ref,
              oc_ref, ol_ref, ofc_ref, ofl_ref, st_ref):
    nk = B_HEADS * B_DK
    ch = B_CHUNK
    row = lax.broadcasted_iota(jnp.int32, (ch, ch), 0)
    col = lax.broadcasted_iota(jnp.int32, (ch, ch), 1)
    lane = lax.broadcasted_iota(jnp.int32, (ch, nk), 1)
    head_masks = [(lane // B_DK) == h for h in range(B_HEADS)]

    def group_out(g_ref, z_ref, r0s, fwd, s_t):
        w2 = (w2f_ref if fwd else w2b_ref)[...]
        bias = (bf_ref if fwd else bb_ref)[...]
        keep = (row >= col) if fwd else (row <= col)
        tri = jnp.where(keep, 1.0, 0.0).astype(BF)
        blks = [g_ref[pl.ds(r0, ch), :] for r0 in r0s]
        pres = [_dot(z_ref[pl.ds(r0, ch), :], w2) + bias for r0 in r0s]
        las = [-(jnp.maximum(-p, 0.0) + jnp.log(1.0 + jnp.exp(-jnp.abs(p)))) * (1.0 / B_TAU) for p in pres]
        bcums = [_split_dot(tri, la) for la in las]
        qms, kis, kss, decs = [], [], [], []
        for blk, bcum in zip(blks, bcums):
            q = blk[:, GLA_Q:GLA_Q + nk].astype(F32)
            k = blk[:, GLA_K:GLA_K + nk].astype(F32)
            b_last = bcum[ch - 1:ch, :] if fwd else bcum[0:1, :]
            q_dec = q * jnp.exp(bcum)
            k_state = k * jnp.exp(b_last - bcum)
            qms.append([jnp.where(m, q_dec, 0.0).astype(BF) for m in head_masks])
            kis.append((k * jnp.exp(-bcum)).astype(BF))
            kss.append([jnp.where(m, k_state, 0.0).astype(BF) for m in head_masks])
            decs.append(jnp.exp(b_last))
        vs = [[blk[:, GLA_V + B_DV * h:GLA_V + B_DV * (h + 1)] for h in range(B_HEADS)] for blk in blks]
        atts = [[jnp.where(keep, _dot_nt(qm, ki), 0.0).astype(BF) for qm in qmu] for qmu, ki in zip(qms, kis)]
        o_intra = [[_dot(a, v) for a, v in zip(au, vu)] for au, vu in zip(atts, vs)]
        d_ss = []
        for vu, ksu in zip(vs, kss):
            d_s = _dot_tn(vu[0], ksu[0])
            for h in range(1, B_HEADS):
                d_s = d_s + _dot_tn(vu[h], ksu[h])
            d_ss.append(d_s)
        outs = []
        for u in range(len(r0s)):
            s_b = s_t.astype(BF)
            outs.append(jnp.concatenate(
                [o_intra[u][h] + _dot_nt(qms[u][h], s_b) for h in range(B_HEADS)], axis=1))
            s_t = decs[u] * s_t + d_ss[u]
        return outs, blks, s_t

    def fwd_pass(g_ref, z_ref, of_ref, n):
        grp = min(GLA_GROUP, n)
        assert n % grp == 0

        def body(i, carry):
            r0s = [pl.multiple_of((i * grp + u) * ch, ch) for u in range(grp)]
            outs, _, s_t = group_out(g_ref, z_ref, r0s, True, st_ref[...])
            for r0, o in zip(r0s, outs):
                of_ref[pl.ds(r0, ch), :] = o
            st_ref[...] = s_t
            return carry
        lax.fori_loop(0, n // grp, body, 0)

    def bwd_pass(g_ref, z_ref, of_ref, o_ref, n):
        grp = min(GLA_GROUP, n)
        assert n % grp == 0

        def body(i, carry):
            r0s = [pl.multiple_of((n - 1 - (i * grp + u)) * ch, ch) for u in range(grp)]
            outs, blks, s_t = group_out(g_ref, z_ref, r0s, False, st_ref[...])
            for r0, o, blk in zip(r0s, outs, blks):
                o = o + of_ref[pl.ds(r0, ch), :]
                parts = []
                for h in range(B_HEADS):
                    oh = o[:, B_DV * h:B_DV * (h + 1)]
                    ms = jnp.mean(oh * oh, axis=-1, keepdims=True)
                    parts.append(oh * lax.rsqrt(ms + EPS))
                y = jnp.concatenate(parts, axis=1) * gg_ref[...]
                r = blk[:, GLA_R:GLA_R + B_HEADS * B_DV].astype(F32)
                o_ref[pl.ds(r0, ch), :] = (y * _silu(r)).astype(BF)
            st_ref[...] = s_t
            return carry
        lax.fori_loop(0, n // grp, body, 0)

    nc = gc_ref.shape[0] // ch
    nl = gl_ref.shape[0] // ch
    st_ref[...] = jnp.zeros_like(st_ref)
    fwd_pass(gc_ref, zc_ref, ofc_ref, nc)
    fwd_pass(gl_ref, zl_ref, ofl_ref, nl)
    st_ref[...] = jnp.zeros_like(st_ref)
    bwd_pass(gc_ref, zc_ref, ofc_ref, oc_ref, nc)
    bwd_pass(gl_ref, zl_ref, ofl_ref, ol_ref, nl)


def _gla(g_ctx, g_lat, z_ctx, z_lat, w2f, w2b, b_f, b_b, gg, bsz):
    nc = g_ctx.shape[0] // bsz
    sl = g_lat.shape[0] // bsz
    nk = B_HEADS * B_DK
    nv = B_HEADS * B_DV
    full = lambda shape: pl.BlockSpec(shape, lambda b: (0, 0))
    return pl.pallas_call(
        _gla_body,
        grid=(bsz,),
        in_specs=[pl.BlockSpec((nc, GLA_W), lambda b: (b, 0)),
                  pl.BlockSpec((sl, GLA_W), lambda b: (b, 0)),
                  pl.BlockSpec((nc, LANES), lambda b: (b, 0)),
                  pl.BlockSpec((sl, LANES), lambda b: (b, 0)),
                  full((LANES, nk)), full((LANES, nk)), full((1, nk)), full((1, nk)), full((1, nv))],
        out_specs=[pl.BlockSpec((nc, nv), lambda b: (b, 0)),
                   pl.BlockSpec((sl, nv), lambda b: (b, 0))],
        out_shape=[jax.ShapeDtypeStruct((g_ctx.shape[0], nv), BF),
                   jax.ShapeDtypeStruct((g_lat.shape[0], nv), BF)],
        scratch_shapes=[pltpu.VMEM((nc, nv), F32), pltpu.VMEM((sl, nv), F32),
                        pltpu.VMEM((B_DV, nk), F32)],
        compiler_params=_params(("arbitrary",)),
        name="gla_bidir",
    )(g_ctx, g_lat, z_ctx, z_lat, w2f, w2b, b_f, b_b, gg)


def _mixer_residual(x_ref, m_ref, pair_refs):
    acc = _dot(pair_refs[0][...], pair_refs[1][...])
    for t in range(1, len(pair_refs) // 2):
        acc = acc + _dot(pair_refs[2 * t][...], pair_refs[2 * t + 1][...])
    return x_ref[...] + m_ref[0, 2:3, :] * acc


def _pair_specs(pairs, tm, index):
    in_specs, args = [], []
    for a, w in pairs:
        in_specs += [pl.BlockSpec((tm, a.shape[1]), index), pl.BlockSpec(w.shape, lambda *_: (0, 0))]
        args += [a, w]
    return in_specs, args


def _ffn_body(x_ref, g_ref, m_ref, *rest, n_pairs):
    pair_refs = rest[:2 * n_pairs]
    wg_ref, wu_ref, wd_ref, o_ref, xn_ref, h_ref, acc_ref = rest[2 * n_pairs:]
    j = pl.program_id(1)

    @pl.when(j == 0)
    def _():
        xn = _mixer_residual(x_ref, m_ref, pair_refs)
        xn_ref[...] = xn
        h_ref[...] = _norm_mod(xn, g_ref[...], m_ref[0, 3:4, :], m_ref[0, 4:5, :]).astype(BF)
        acc_ref[...] = jnp.zeros_like(acc_ref)

    h = h_ref[...]
    a = (_silu(_dot(h, wg_ref[...])) * _dot(h, wu_ref[...])).astype(BF)
    acc_ref[...] += _dot(a, wd_ref[...])

    @pl.when(j == pl.num_programs(1) - 1)
    def _():
        o_ref[...] = xn_ref[...] + m_ref[0, 5:6, :] * acc_ref[...]


def _ffn(x2, g, mpack, mod_of_tile, pairs, wg, wu, wd, tm, tf):
    rows = x2.shape[0]
    ff = wg.shape[1]
    pair_specs, pair_args = _pair_specs(pairs, tm, lambda i, j: (i, 0))
    return pl.pallas_call(
        functools.partial(_ffn_body, n_pairs=len(pairs)),
        grid=(rows // tm, ff // tf),
        in_specs=[pl.BlockSpec((tm, D), lambda i, j: (i, 0)),
                  pl.BlockSpec((1, D), lambda i, j: (0, 0)),
                  pl.BlockSpec((1, MOD_ROWS, D), lambda i, j: (mod_of_tile(i), 0, 0)),
                  *pair_specs,
                  pl.BlockSpec((D, tf), lambda i, j: (0, j)),
                  pl.BlockSpec((D, tf), lambda i, j: (0, j)),
                  pl.BlockSpec((tf, D), lambda i, j: (j, 0))],
        out_specs=pl.BlockSpec((tm, D), lambda i, j: (i, 0)),
        out_shape=jax.ShapeDtypeStruct((rows, D), F32),
        scratch_shapes=[pltpu.VMEM((tm, D), F32), pltpu.VMEM((tm, D), BF), pltpu.VMEM((tm, D), F32)],
        compiler_params=_params(("arbitrary", "arbitrary")),
        name="ffn_swiglu",
    )(x2, g.reshape(1, D), mpack, *pair_args, wg, wu, wd)


NA_QROWS = 4
NA_KROWS = 12
NA_TQ = NA_QROWS * GRID_W
NA_HG = 8
NA_W = NA_HG * HD


def _na_body(q_ref, k0_ref, k1_ref, k2_ref, v0_ref, v1_ref, v2_ref, kc_ref, vc_ref, b_ref, o_ref):
    pair = LANES // HD
    lane = lax.broadcasted_iota(jnp.int32, (NA_TQ, LANES), 1)
    own = [(lane // HD) == half for half in range(pair)]

    def scores(hh):
        tile = slice(LANES * (hh // pair), LANES * (hh // pair + 1))
        q = jnp.where(own[hh % pair], q_ref[:, tile], 0.0)
        s_loc = jnp.concatenate([_dot_nt(q, kr[:, tile]) for kr in (k0_ref, k1_ref, k2_ref)], axis=1)
        bias = b_ref[0, :, hh].reshape(NA_TQ, NA_KROWS * GRID_W)
        return s_loc + bias, _dot_nt(q, kc_ref[:, tile])

    nxt = scores(0)
    outs = []
    for hh in range(NA_HG):
        s_loc, s_ctx = nxt
        if hh + 1 < NA_HG:
            nxt = scores(hh + 1)
        vcols = slice(LANES * hh, LANES * (hh + 1))
        m = jnp.maximum(jnp.max(s_loc, axis=1, keepdims=True), jnp.max(s_ctx, axis=1, keepdims=True))
        p_loc = jnp.exp2(s_loc - m).astype(BF)
        acc = _dot(jnp.exp2(s_ctx - m).astype(BF), vc_ref[:, vcols])
        for t, vr in enumerate((v0_ref, v1_ref, v2_ref)):
            acc = acc + _dot(p_loc[:, t * NA_TQ:(t + 1) * NA_TQ], vr[:, vcols])
        den_lane = HD * (1 - hh % pair)
        outs.append(acc / acc[:, den_lane:den_lane + 1])
        if len(outs) == pair:
            t0 = hh // pair
            o_ref[:, LANES * t0:LANES * (t0 + 1)] = jnp.where(own[0], outs[0], outs[1]).astype(BF)
            outs = []


def _na_bias(rpb, rows):
    heads, n_dr, n_dc = rpb.shape
    nblk = rows // NA_QROWS
    pad_l = GRID_W - NA_COLS
    rext = jnp.pad(rpb.astype(F32) * LOG2E, ((0, 0), (0, 0), (pad_l, 2 * GRID_W - n_dc - pad_l)))
    flat = jnp.tile(rext, (1, 1, GRID_W))[:, :, :GRID_W * (2 * GRID_W - 1)]
    toe = flat.reshape(heads, n_dr, GRID_W, 2 * GRID_W - 1)[:, :, :, GRID_W - 1:]
    cols = np.arange(GRID_W)
    cs = np.clip(cols - NA_COLS // 2, 0, GRID_W - NA_COLS)
    col_ok = (cols[None, :] >= cs[:, None]) & (cols[None, :] < cs[:, None] + NA_COLS)
    toe = jnp.where(col_ok, toe, NEG_INF)
    toe_t = jnp.pad(toe.transpose(0, 2, 1, 3), ((0, 0), (0, 0), (NA_KROWS, NA_KROWS), (0, 0)),
                    constant_values=NEG_INF)
    blocks = []
    for j in (0, min(1, nblk - 1), nblk - 1):
        ws = NA_QROWS * int(np.clip(j - 1, 0, nblk - 3))
        for qr in range(NA_QROWS):
            r = NA_QROWS * j + qr
            rs = int(np.clip(r - NA_ROWS // 2, 0, rows - NA_ROWS))
            in_rows = np.array([rs <= ws + kr < rs + NA_ROWS for kr in range(NA_KROWS)])
            d0 = NA_KROWS + ws - r + NA_ROWS - 1
            blk = jnp.where(in_rows[None, None, :, None], toe_t[:, :, d0:d0 + NA_KROWS, :], NEG_INF)
            blocks.append(blk.reshape(heads, GRID_W, NA_KROWS * GRID_W))
    return jnp.stack(blocks).reshape(3, NA_QROWS, heads, GRID_W, NA_KROWS * GRID_W)


def _natten(q, k, v, k_ctx, v_ctx, bias, bsz):
    s = q.shape[0] // bsz
    nc = k_ctx.shape[0] // bsz
    nblk = s // NA_TQ
    ngrp = C_HEADS // NA_HG
    v_w = NA_HG * LANES

    def kv_spec(t, width):
        return pl.BlockSpec((NA_TQ, width),
                            lambda j, g, b: (b * nblk + jnp.clip(j - 1, 0, nblk - 3) + t, g))

    def kind(j):
        return jnp.where(j == 0, 0, jnp.where(j == nblk - 1, 2, 1))

    return pl.pallas_call(
        _na_body,
        grid=(nblk, ngrp, bsz),
        in_specs=[pl.BlockSpec((NA_TQ, NA_W), lambda j, g, b: (b * nblk + j, g)),
                  kv_spec(0, NA_W), kv_spec(1, NA_W), kv_spec(2, NA_W),
                  kv_spec(0, v_w), kv_spec(1, v_w), kv_spec(2, v_w),
                  pl.BlockSpec((nc, NA_W), lambda j, g, b: (b, g)),
                  pl.BlockSpec((nc, v_w), lambda j, g, b: (b, g)),
                  pl.BlockSpec((1, NA_QROWS, NA_HG, GRID_W, NA_KROWS * GRID_W),
                               lambda j, g, b: (kind(j), 0, g, 0, 0))],
        out_specs=pl.BlockSpec((NA_TQ, NA_W), lambda j, g, b: (b * nblk + j, g)),
        out_shape=jax.ShapeDtypeStruct(q.shape, BF),
        compiler_params=_params(("arbitrary", "arbitrary", "arbitrary")),
        name="natten",
    )(q, k, k, k, v, v, v, k_ctx, v_ctx, bias)


MOE_TM = 512
MOE_TF = 1792
META_E1, META_E2, META_R1, META_R2, META_G1, META_G2 = range(6)
ROW_TILE = D // LANES


def _to_token_tiles(dst_ref, value):
    n = value.shape[0]
    for s in range(ROW_TILE):
        dst_ref[pl.ds(s, n, stride=ROW_TILE), :] = value[:, LANES * s:LANES * (s + 1)]


def _from_token_tiles(src_ref, n):
    return jnp.concatenate([src_ref[pl.ds(s, n, stride=ROW_TILE), :] for s in range(ROW_TILE)], axis=1)


def _route_body(x_ref, g_ref, m_ref, *rest, n_pairs):
    pair_refs = rest[:2 * n_pairs]
    wr_ref, xn_ref, h_ref, meta_ref, rec_ref, cnt_ref, tri_ref, carry_ref = rest[2 * n_pairs:]
    tm = x_ref.shape[0]

    @pl.when(pl.program_id(0) == 0)
    def _():
        row = lax.broadcasted_iota(jnp.int32, (tm, tm), 0)
        col = lax.broadcasted_iota(jnp.int32, (tm, tm), 1)
        tri_ref[...] = jnp.where(row >= col, 1.0, 0.0).astype(BF)
        carry_ref[...] = jnp.zeros_like(carry_ref)

    xn = _mixer_residual(x_ref, m_ref, pair_refs)
    xn_ref[...] = xn
    h = _norm_mod(xn, g_ref[...], m_ref[0, 3:4, :], m_ref[0, 4:5, :])
    _to_token_tiles(h_ref, h)
    h_hi = h.astype(BF)
    h_lo = (h - h_hi.astype(F32)).astype(BF)
    w = wr_ref[...]
    w_hi = w.astype(BF)
    w_lo = (w - w_hi.astype(F32)).astype(BF)
    logits = _dot(h_hi, w_hi) + (_dot(h_lo, w_hi) + _dot(h_hi, w_lo))
    lane = lax.broadcasted_iota(jnp.int32, logits.shape, 1)
    lg = jnp.where(lane < N_EXPERTS, logits, NEG_INF)
    m1 = jnp.max(lg, axis=1, keepdims=True)
    i1 = jnp.min(jnp.where(lg == m1, lane, LANES), axis=1, keepdims=True)
    lg2 = jnp.where(lane == i1, NEG_INF, lg)
    m2 = jnp.max(lg2, axis=1, keepdims=True)
    i2 = jnp.min(jnp.where(lg2 == m2, lane, LANES), axis=1, keepdims=True)
    e = jnp.exp(m2 - m1)
    g1 = 1.0 / (1.0 + e)
    g2 = e / (1.0 + e)
    oh = jnp.where(lane == i1, 1.0, 0.0) + jnp.where(lane == i2, 1.0, 0.0)
    cum = _dot(tri_ref[...], oh.astype(BF))
    base = carry_ref[0:1, :] + cum - oh
    r1 = jnp.sum(jnp.where(lane == i1, base, 0.0), axis=1, keepdims=True)
    r2 = jnp.sum(jnp.where(lane == i2, base, 0.0), axis=1, keepdims=True)
    carry_ref[...] = carry_ref[...] + cum[tm - 1:tm, :]
    cnt_ref[...] = carry_ref[...]
    meta = jnp.zeros(logits.shape, F32)
    for ln, val in ((META_E1, i1.astype(F32)), (META_E2, i2.astype(F32)), (META_R1, r1), (META_R2, r2),
                    (META_G1, g1), (META_G2, g2)):
        meta = jnp.where(lane == ln, val, meta)
    meta_ref[...] = meta
    rec_ref[...] = meta.T[:rec_ref.shape[0], :]


def _route(x2, g, mpack, mod_of_tile, pairs, wr, tm):
    rows = x2.shape[0]
    pair_specs, pair_args = _pair_specs(pairs, tm, lambda i: (i, 0))
    return pl.pallas_call(
        functools.partial(_route_body, n_pairs=len(pairs)),
        grid=(rows // tm,),
        in_specs=[pl.BlockSpec((tm, D), lambda i: (i, 0)),
                  pl.BlockSpec((1, D), lambda i: (0, 0)),
                  pl.BlockSpec((1, MOD_ROWS, D), lambda i: (mod_of_tile(i), 0, 0)),
                  *pair_specs,
                  pl.BlockSpec((D, LANES), lambda i: (0, 0))],
        out_specs=[pl.BlockSpec((tm, D), lambda i: (i, 0)),
                   pl.BlockSpec((tm * ROW_TILE, LANES), lambda i: (i, 0)),
                   pl.BlockSpec((tm, LANES), lambda i: (i, 0)),
                   pl.BlockSpec((8, tm), lambda i: (0, i)),
                   pl.BlockSpec((8, LANES), lambda i: (0, 0))],
        out_shape=[jax.ShapeDtypeStruct((rows, D), F32),
                   jax.ShapeDtypeStruct((rows * ROW_TILE, LANES), F32),
                   jax.ShapeDtypeStruct((rows, LANES), F32),
                   jax.ShapeDtypeStruct((8, rows), F32),
                   jax.ShapeDtypeStruct((8, LANES), F32)],
        scratch_shapes=[pltpu.VMEM((tm, tm), BF), pltpu.VMEM((8, LANES), F32)],
        compiler_params=_params(("arbitrary",)),
        name="moe_route",
    )(x2, g.reshape(1, D), mpack, *pair_args, wr)


def _token_copy(src_ref, src_tok, dst_ref, dst_tok, sem):
    src = src_ref.at[pl.ds(pl.multiple_of(src_tok * ROW_TILE, ROW_TILE), ROW_TILE), :]
    dst = dst_ref.at[pl.ds(pl.multiple_of(dst_tok * ROW_TILE, ROW_TILE), ROW_TILE), :]
    return pltpu.make_async_copy(src, dst, sem)


def _experts_body(te_ref, nu_ref, cur_ref, nxt_ref, dprev_ref, h_hbm, wg_ref, wu_ref, wd_ref, y_hbm,
                  xbuf, h_ref, acc_ref, ybuf, gsem, ssem, *, n_steps):
    assert n_steps >= 2
    i = pl.program_id(0)
    j = pl.program_id(1)
    used = i < nu_ref[0]
    slot = i % 2
    other = 1 - slot

    def gather_row(idx_ref, r, sl):
        _token_copy(h_hbm, idx_ref[0, 0, r], xbuf.at[sl], r, gsem.at[sl]).start()

    def issue_gathers():
        for r in range(MOE_TM):
            gather_row(nxt_ref, r, other)

    def issue_scatters():
        for r in range(MOE_TM):
            _token_copy(ybuf.at[other], r, y_hbm, dprev_ref[0, 0, r], ssem.at[other]).start()

    def wait_rows(buf, sem):
        pltpu.make_async_copy(buf, buf, sem).wait()

    def swiglu_step():
        h = h_ref[...]
        a = (_silu(_dot(h, wg_ref[0])) * _dot(h, wu_ref[0])).astype(BF)
        return _dot(a, wd_ref[0])

    @pl.when((j == 0) & (i == 0))
    def _():
        ybuf[...] = jnp.zeros_like(ybuf)
        n_spare = MOE_TM * ROW_TILE
        for sl in range(2):
            spare = pltpu.make_async_copy(
                ybuf.at[sl], y_hbm.at[pl.ds(y_hbm.shape[0] - (2 - sl) * n_spare, n_spare), :], ssem.at[sl])
            spare.start()
            spare.wait()

        def body(r, c):
            gather_row(cur_ref, r, 0)
            return c
        lax.fori_loop(0, MOE_TM, body, 0, unroll=8)

    @pl.when(j == 0)
    def _():
        wait_rows(xbuf.at[slot], gsem.at[slot])

    @pl.when(used & (j == 0))
    def _():
        h_ref[...] = _from_token_tiles(xbuf.at[slot], MOE_TM).astype(BF)
        acc_ref[...] = swiglu_step()
        issue_gathers()

    if n_steps > 2:
        @pl.when(used & (j > 0) & (j < n_steps - 1))
        def _():
            acc_ref[...] += swiglu_step()

    @pl.when(used & (j == n_steps - 1))
    def _():
        acc_ref[...] += swiglu_step()
        issue_scatters()

    @pl.when(jnp.logical_not(used) & (j == 0))
    def _():
        issue_gathers()

    @pl.when(jnp.logical_not(used) & (j == n_steps - 1))
    def _():
        issue_scatters()

    @pl.when(j == n_steps - 1)
    def _():
        @pl.when(i > 0)
        def _():
            wait_rows(ybuf.at[slot], ssem.at[slot])

        @pl.when(used)
        def _():
            _to_token_tiles(ybuf.at[slot], acc_ref[...])

        @pl.when(i == pl.num_programs(0) - 1)
        def _():
            wait_rows(ybuf.at[other], ssem.at[other])
            wait_rows(xbuf.at[other], gsem.at[other])


def _experts(h_tiles, tok_of_slot, dest_of_slot, tile_expert, n_used, wg, wu, wd, out_tokens):
    n_tiles = tile_expert.shape[0]
    ff = wg.shape[2]
    nf = ff // MOE_TF
    spare_rows = out_tokens - MOE_TM + jnp.arange(MOE_TM, dtype=jnp.int32)
    dest_prev = jnp.concatenate([spare_rows, dest_of_slot[:-MOE_TM]])

    def jj(i, j, nu):
        return jnp.where(i < nu[0], j, nf - 1)

    def smem_rows(index):
        return pl.BlockSpec((1, 1, MOE_TM), lambda i, j, te, nu: (index(i), 0, 0), memory_space=pltpu.SMEM)

    grid_spec = pltpu.PrefetchScalarGridSpec(
        num_scalar_prefetch=2,
        grid=(n_tiles, nf),
        in_specs=[smem_rows(lambda i: i),
                  smem_rows(lambda i: jnp.minimum(i + 1, n_tiles - 1)),
                  smem_rows(lambda i: i),
                  pl.BlockSpec(memory_space=pl.ANY),
                  pl.BlockSpec((1, D, MOE_TF), lambda i, j, te, nu: (te[i], 0, jj(i, j, nu))),
                  pl.BlockSpec((1, D, MOE_TF), lambda i, j, te, nu: (te[i], 0, jj(i, j, nu))),
                  pl.BlockSpec((1, MOE_TF, D), lambda i, j, te, nu: (te[i], jj(i, j, nu), 0))],
        out_specs=pl.BlockSpec(memory_space=pl.ANY),
        scratch_shapes=[pltpu.VMEM((2, MOE_TM * ROW_TILE, LANES), F32),
                        pltpu.VMEM((MOE_TM, D), BF),
                        pltpu.VMEM((MOE_TM, D), F32),
                        pltpu.VMEM((2, MOE_TM * ROW_TILE, LANES), F32),
                        pltpu.SemaphoreType.DMA((2,)),
                        pltpu.SemaphoreType.DMA((2,))])
    tok3 = tok_of_slot.reshape(n_tiles, 1, MOE_TM)
    return pl.pallas_call(
        functools.partial(_experts_body, n_steps=nf),
        grid_spec=grid_spec,
        out_shape=jax.ShapeDtypeStruct((out_tokens * ROW_TILE, LANES), F32),
        compiler_params=_params(("arbitrary", "arbitrary")),
        name="moe_experts",
    )(tile_expert, n_used, tok3, tok3, dest_prev.reshape(n_tiles, 1, MOE_TM), h_tiles, wg, wu, wd)


def _combine_body(y1_ref, y2_ref, x_ref, m_ref, meta_ref, gf_ref, o_ref):
    n = x_ref.shape[0]
    meta = meta_ref[...]
    lane = lax.broadcasted_iota(jnp.int32, meta.shape, 1)
    g1 = jnp.sum(jnp.where(lane == META_G1, meta, 0.0), axis=1, keepdims=True)
    g2 = jnp.sum(jnp.where(lane == META_G2, meta, 0.0), axis=1, keepdims=True)
    moe = g1 * _from_token_tiles(y1_ref, n) + g2 * _from_token_tiles(y2_ref, n)
    y = x_ref[...] + m_ref[0, 5:6, :] * moe
    ms = jnp.mean(y * y, axis=-1, keepdims=True)
    o_ref[...] = y * lax.rsqrt(ms + EPS) * gf_ref[...]


def _combine(x2, mpack, mod_of_row, meta, y_tiles, g_final, tm):
    rows = x2.shape[0]
    nt = rows // tm
    return pl.pallas_call(
        _combine_body,
        grid=(nt,),
        in_specs=[pl.BlockSpec((tm * ROW_TILE, LANES), lambda i: (i, 0)),
                  pl.BlockSpec((tm * ROW_TILE, LANES), lambda i: (i + nt, 0)),
                  pl.BlockSpec((tm, D), lambda i: (i, 0)),
                  pl.BlockSpec((1, MOD_ROWS, D), lambda i: (mod_of_row(i * tm), 0, 0)),
                  pl.BlockSpec((tm, LANES), lambda i: (i, 0)),
                  pl.BlockSpec((1, D), lambda i: (0, 0))],
        out_specs=pl.BlockSpec((tm, D), lambda i: (i, 0)),
        out_shape=jax.ShapeDtypeStruct((rows, D), F32),
        compiler_params=_params(("arbitrary",)),
        name="moe_combine",
    )(y_tiles, y_tiles, x2, mpack, meta, g_final.reshape(1, D))


def _moe(x2, g, mpack, mod_of_tile, mod_of_row, pairs, wr, wg, wu, wd, g_final, tm):
    rows = x2.shape[0]
    ne = wg.shape[0]
    x2, h_tiles, meta, rec, cnt = _route(x2, g, mpack, mod_of_tile, pairs, wr, tm)
    counts = cnt[0, :ne].astype(jnp.int32)
    tiles_e = (counts + MOE_TM - 1) // MOE_TM
    tile_end = jnp.cumsum(tiles_e)
    group_off = (tile_end - tiles_e) * MOE_TM
    n_tiles = 2 * rows // MOE_TM + ne
    n_slots = n_tiles * MOE_TM
    tile_expert = jnp.minimum(
        jnp.sum(jnp.arange(n_tiles)[:, None] >= tile_end[None, :], axis=1), ne - 1).astype(jnp.int32)
    n_used = tile_end[-1:].astype(jnp.int32)
    eids = jnp.arange(ne)

    def slot_of(lane_e, lane_r):
        e = rec[lane_e].astype(jnp.int32)
        off = jnp.sum(jnp.where(e[None, :] == eids[:, None], group_off[:, None], 0), axis=0)
        return off + rec[lane_r].astype(jnp.int32)

    tok = jnp.arange(rows, dtype=jnp.int32)
    dest = jnp.full((n_slots,), -1, jnp.int32).at[
        jnp.concatenate([slot_of(META_E1, META_R1), slot_of(META_E2, META_R2)])].set(
        jnp.concatenate([tok, rows + tok]), unique_indices=True, mode="promise_in_bounds")
    pad = dest < 0
    dest_of_slot = jnp.where(pad, 2 * rows + jnp.arange(n_slots, dtype=jnp.int32) % (2 * MOE_TM), dest)
    tok_of_slot = jnp.where(pad, 0, jnp.where(dest >= rows, dest - rows, dest))
    y_tiles = _experts(h_tiles, tok_of_slot, dest_of_slot, tile_expert, n_used, wg, wu, wd, 2 * rows + 2 * MOE_TM)
    return _combine(x2, mpack, mod_of_row, meta, y_tiles, g_final, MOE_TM)


def _rope_tables(seq_len):
    t = np.arange(seq_len)
    row = (t // GRID_W).astype(np.float32)
    col = (t % GRID_W).astype(np.float32)
    n_axis = HD // 4
    inv = jnp.power(ROPE_THETA, -jnp.arange(n_axis, dtype=F32) / n_axis)
    ang = jnp.concatenate([jnp.asarray(row)[:, None] * inv, jnp.asarray(col)[:, None] * inv], axis=-1)
    cos, sin = jnp.cos(ang), jnp.sin(ang)
    ct = jnp.tile(jnp.concatenate([cos, cos], axis=-1), (1, LANES // HD))
    st = jnp.tile(jnp.concatenate([-sin, sin], axis=-1), (1, LANES // HD))
    return ct, st


def kernel(x, c, ctx, c_ctx, w_mod, b_mod, g_norm1, g_norm2, w_in_ab, g_q, g_k, w_a2_f, b_a_f, w_a2_b,
           b_a_b, g_gla, w_out_ab, w_ff_gate, w_ff_up, w_ff_down, w_in_c, rpb_c, w_out_c, w_router,
           w_moe_gate, w_moe_up, w_moe_down, g_final):
    bsz, seq, _ = x.shape
    nc = ctx.shape[1]
    depth = w_mod.shape[0]
    assert depth == 2 and seq % GRID_W == 0

    cond_rows = -(-(bsz + 1) // 8) * 8
    cond = jnp.zeros((cond_rows, D), F32).at[:bsz].set(c).at[bsz].set(c_ctx)
    mods = _adaln(cond, w_mod, b_mod).reshape(depth, cond_rows, N_MOD, D)
    mods = jnp.pad(mods, ((0, 0), (0, 0), (0, MOD_ROWS - N_MOD), (0, 0)))

    xl = x.reshape(bsz * seq, D)
    xc = ctx.reshape(bsz * nc, D)
    tm_l = min(1024, seq)
    tm_c = min(1024, bsz * nc)
    tpb = seq // tm_l
    lat_mod = lambda i: i // tpb
    ctx_mod = lambda i: bsz

    mp = mods[0]
    w_in = w_in_ab[0]
    perm = np.concatenate([np.arange(0, HD, 2), np.arange(1, HD, 2)])
    nq, nkv = A_HEADS * HD, A_KV * HD
    o_q, o_k, o_v = 0, nq, nq + nkv
    o_bq = o_v + nkv
    o_bk = o_bq + B_HEADS * B_DK
    o_bv = o_bk + B_HEADS * B_DK
    o_br = o_bv + B_HEADS * B_DV
    o_zf = o_br + B_HEADS * B_DV
    o_end = o_zf + 2 * B_RANK
    assert A_KV * HD == LANES
    q_order = np.arange(A_HEADS).reshape(A_KV, A_GROUP).T.reshape(-1)
    wq = w_in[:, o_q:o_k].reshape(D, A_HEADS, HD)[:, q_order][:, :, perm].reshape(D, nq)
    wk = w_in[:, o_k:o_v].reshape(D, A_KV, HD)[:, :, perm].reshape(D, nkv)
    wz = jnp.pad(w_in[:, o_zf:o_end], ((0, 0), (0, LANES - 2 * B_RANK)))
    w0 = jnp.concatenate([wq, wk, w_in[:, o_v:o_bq], w_in[:, o_bq:o_bk] * (B_DK ** -0.5),
                          w_in[:, o_bk:o_zf], wz], axis=1).astype(BF)
    widths0 = (nq + nkv, nkv, GLA_W, LANES)
    outw0 = (nq, nkv, 2 * nkv, GLA_W, LANES)
    gqk = jnp.concatenate([jnp.tile(g_q[0][perm] * (HD ** -0.5 * LOG2E), A_HEADS),
                           jnp.tile(g_k[0][perm], A_KV)]).reshape(1, nq + nkv)
    bd = jnp.asarray(np.tile(np.kron(np.eye(LANES // HD), np.ones((HD, HD))), (2, 1)), BF)
    ct, st = _rope_tables(seq)
    ct_c = jnp.ones((tm_c, LANES), F32)
    st_c = jnp.zeros((tm_c, LANES), F32)

    kinds0 = ("qk", "values", "plain", "plain")
    ql, kl, vl, gl, zl = _proj(xl, g_norm1[0], mp, lat_mod, w0, widths0, kinds0, outw0, tm_l, (ct, st, gqk, bd))
    qc, kc, vc, gc, zc = _proj(xc, g_norm1[0], mp, ctx_mod, w0, widths0, kinds0, outw0, tm_c,
                               (ct_c, st_c, gqk, bd))

    oa_l = _gqa(ql, kc, vc, kl, vl, bsz, 256)
    oa_c = _gqa(qc, kc, vc, None, None, bsz, min(128, nc))

    w2f = jnp.zeros((LANES, B_HEADS * B_DK), F32).at[:B_RANK].set(w_a2_f[0]).astype(BF)
    w2b = jnp.zeros((LANES, B_HEADS * B_DK), F32).at[B_RANK:2 * B_RANK].set(w_a2_b[0]).astype(BF)
    ob_c, ob_l = _gla(gc, gl, zc, zl, w2f, w2b, b_a_f[0].reshape(1, -1), b_a_b[0].reshape(1, -1),
                      jnp.tile(g_gla[0], B_HEADS).reshape(1, -1), bsz)

    wo = w_out_ab[0].astype(BF)
    wo_a = wo[:nq].reshape(A_HEADS, HD, D)[q_order].reshape(nq, D)
    wo_b = wo[nq:]
    wg, wu, wd = w_ff_gate[0].astype(BF), w_ff_up[0].astype(BF), w_ff_down[0].astype(BF)
    xl = _ffn(xl, g_norm2[0], mp, lat_mod, [(oa_l, wo_a), (ob_l, wo_b)], wg, wu, wd, tm_l, 512)
    xc = _ffn(xc, g_norm2[0], mp, ctx_mod, [(oa_c, wo_a), (ob_c, wo_b)], wg, wu, wd, tm_c, 512)

    mp = mods[1]
    cw = C_HEADS * HD
    w1 = w_in_c[0].astype(BF)
    q1, k1, v1 = _proj(xl, g_norm1[1], mp, lat_mod, w1, (cw, cw, cw), ("plain", "plain", "values"),
                       (cw, cw, 2 * cw), tm_l, scales=(HD ** -0.5 * LOG2E, 1.0, 1.0))
    k1c, v1c = _proj(xc, g_norm1[1], mp, ctx_mod, w1[:, cw:], (cw, cw), ("plain", "values"), (cw, 2 * cw), tm_c)

    bias = _na_bias(rpb_c[0], seq // GRID_W)
    o1 = _natten(q1, k1, v1, k1c, v1c, bias, bsz)

    wr = jnp.pad(w_router[0], ((0, 0), (0, LANES - N_EXPERTS)))
    out = _moe(xl, g_norm2[1], mp, lat_mod, lambda r: r // seq, [(o1, w_out_c[0].astype(BF))], wr,
               w_moe_gate[0].astype(BF), w_moe_up[0].astype(BF), w_moe_down[0].astype(BF), g_final, tm_l)
    return out.reshape(bsz, seq, D)
```

```python
import functools

import numpy as np
import jax
import jax.numpy as jnp
from jax import lax
from jax.experimental import pallas as pl
from jax.experimental.pallas import tpu as pltpu

F32 = jnp.float32
BF = jnp.bfloat16

D = 1024
EPS = 1e-6
N_MOD = 6
MOD_ROWS = 8
GRID_W = 64
HD = 64
A_HEADS = 8
A_KV = 2
A_GROUP = A_HEADS // A_KV
ROPE_THETA = 10000.0
LOG2E = 1.4426950408889634
B_HEADS = 4
B_DK = 64
B_DV = 128
B_RANK = 16
B_TAU = 16.0
B_CHUNK = 64
C_HEADS = 16
NA_ROWS = 8
NA_COLS = 16
N_EXPERTS = 8
LANES = 128
VMEM_LIMIT = 56 * 2 ** 20

NEG_INF = float("-inf")


def _params(sem):
    return pltpu.CompilerParams(dimension_semantics=sem, vmem_limit_bytes=VMEM_LIMIT)


def _dot(a, b):
    return jnp.dot(a, b, preferred_element_type=F32)


def _dot_nt(a, b):
    return lax.dot_general(a, b, (((1,), (1,)), ((), ())), preferred_element_type=F32)


def _dot_tn(a, b):
    return lax.dot_general(a, b, (((0,), (0,)), ((), ())), preferred_element_type=F32)


def _silu(x):
    return x / (1.0 + jnp.exp(-x))


def _split_dot(lhs_bf_exact, x):
    hi = x.astype(BF)
    lo = (x - hi.astype(F32)).astype(BF)
    return _dot(lhs_bf_exact, hi) + _dot(lhs_bf_exact, lo)


def _norm_mod(x, g, shift, scale):
    ms = jnp.mean(x * x, axis=-1, keepdims=True)
    return (x * lax.rsqrt(ms + EPS) * g) * (1.0 + scale) + shift


def _adaln_body(c_ref, w_ref, b_ref, o_ref):
    s = _silu(c_ref[...])
    o_ref[0] = _dot(s.astype(BF), w_ref[0].astype(BF)) + b_ref[0]


def _adaln(cond, w_mod, b_mod):
    depth, _, n = w_mod.shape
    rows = cond.shape[0]
    tn = 1536
    return pl.pallas_call(
        _adaln_body,
        grid=(depth, n // tn),
        in_specs=[pl.BlockSpec((rows, D), lambda l, j: (0, 0)),
                  pl.BlockSpec((1, D, tn), lambda l, j: (l, 0, j)),
                  pl.BlockSpec((1, 1, tn), lambda l, j: (l, 0, j))],
        out_specs=pl.BlockSpec((1, rows, tn), lambda l, j: (l, 0, j)),
        out_shape=jax.ShapeDtypeStruct((depth, rows, n), F32),
        compiler_params=_params(("arbitrary", "arbitrary")),
        name="adaln",
    )(cond, w_mod, b_mod.reshape(depth, 1, n))


def _qk_norm_rope(p, gqk, bd, ct, st):
    pp = p * p
    sums = []
    for c in range(p.shape[1] // LANES):
        x = pp[:, c * LANES:(c + 1) * LANES]
        hi = x.astype(BF)
        lo = (x - hi.astype(F32)).astype(BF)
        sums.append(_dot(jnp.concatenate([hi, lo], axis=1), bd))
    ss = jnp.concatenate(sums, axis=1)
    y = p * lax.rsqrt(ss * (1.0 / HD) + EPS) * gqk
    lane = lax.broadcasted_iota(jnp.int32, ct.shape, 1)
    first = (lane % HD) < (HD // 2)
    outs = []
    for c in range(p.shape[1] // LANES):
        yc = y[:, c * LANES:(c + 1) * LANES]
        partner = jnp.where(first, pltpu.roll(yc, LANES - HD // 2, 1), pltpu.roll(yc, HD // 2, 1))
        outs.append(yc * ct + partner * st)
    return jnp.concatenate(outs, axis=1)


def _values_with_ones(p):
    lane = lax.broadcasted_iota(jnp.int32, (p.shape[0], LANES), 1)
    low = lane < HD
    tiles = []
    for c in range(p.shape[1] // LANES):
        x = p[:, c * LANES:(c + 1) * LANES]
        tiles += [jnp.where(low, x, 1.0), jnp.where(low, 1.0, x)]
    return jnp.concatenate(tiles, axis=1)


def _proj_body(x_ref, g_ref, m_ref, w_ref, *rest, widths, kinds, scales):
    if "qk" in kinds:
        ct_ref, st_ref, gqk_ref, bd_ref = rest[:4]
        o_refs = rest[4:]
    else:
        o_refs = rest
    h = _norm_mod(x_ref[...], g_ref[...], m_ref[0, 0:1, :], m_ref[0, 1:2, :]).astype(BF)
    off = 0
    oi = 0
    for wd, kind, scale in zip(widths, kinds, scales):
        p = _dot(h, w_ref[:, off:off + wd])
        off += wd
        if scale != 1.0:
            p = p * scale
        if kind == "qk":
            p = _qk_norm_rope(p, gqk_ref[...], bd_ref[...], ct_ref[...], st_ref[...])
            nq = A_HEADS * HD
            o_refs[oi][...] = p[:, :nq].astype(BF)
            o_refs[oi + 1][...] = p[:, nq:].astype(BF)
            oi += 2
        else:
            if kind == "values":
                p = _values_with_ones(p)
            o_refs[oi][...] = p.astype(BF)
            oi += 1


def _proj(x2, g, mpack, mod_of_tile, w, widths, kinds, out_widths, tm, rope_args=None, scales=None):
    rows = x2.shape[0]
    n = w.shape[1]
    in_specs = [pl.BlockSpec((tm, D), lambda i: (i, 0)),
                pl.BlockSpec((1, D), lambda i: (0, 0)),
                pl.BlockSpec((1, MOD_ROWS, D), lambda i: (mod_of_tile(i), 0, 0)),
                pl.BlockSpec((D, n), lambda i: (0, 0))]
    args = [x2, g.reshape(1, D), mpack, w]
    if rope_args is not None:
        ct, st, gqk, bd = rope_args
        nt = ct.shape[0] // tm
        in_specs += [pl.BlockSpec((tm, LANES), lambda i: (i % nt, 0)),
                     pl.BlockSpec((tm, LANES), lambda i: (i % nt, 0)),
                     pl.BlockSpec(gqk.shape, lambda i: (0, 0)),
                     pl.BlockSpec(bd.shape, lambda i: (0, 0))]
        args += [ct, st, gqk, bd]
    return pl.pallas_call(
        functools.partial(_proj_body, widths=widths, kinds=kinds, scales=scales or (1.0,) * len(widths)),
        grid=(rows // tm,),
        in_specs=in_specs,
        out_specs=[pl.BlockSpec((tm, ow), lambda i: (i, 0)) for ow in out_widths],
        out_shape=[jax.ShapeDtypeStruct((rows, ow), BF) for ow in out_widths],
        compiler_params=_params(("arbitrary",)),
        name="norm_mod_proj",
    )(*args)


GQA_UNIT = 2

def _gqa_body(q_ref, kc_ref, vc_ref, *rest, with_lat, tq):
    if with_lat:
        kl_ref, vl_ref, o_ref = rest
    else:
        (o_ref,) = rest
    units = [(j, t0) for j in range(A_KV) for t0 in range(0, A_GROUP, GQA_UNIT)]
    lane = lax.broadcasted_iota(jnp.int32, (tq, LANES), 1)
    own = [(lane // HD) == j for j in range(A_KV)]

    def scores(unit):
        j, t0 = unit
        qg = jnp.concatenate(
            [jnp.where(own[j], q_ref[:, LANES * t:LANES * (t + 1)], 0.0) for t in range(t0, t0 + GQA_UNIT)],
            axis=0)
        s_c = _dot_nt(qg, kc_ref[...])
        s_l = _dot_nt(qg, kl_ref[...]) if with_lat else None
        return s_c, s_l

    res = {}
    nxt = scores(units[0])
    for u, (j, t0) in enumerate(units):
        s_c, s_l = nxt
        if u + 1 < len(units):
            nxt = scores(units[u + 1])
        vcols = slice(LANES * j, LANES * (j + 1))
        m = jnp.max(s_c, axis=1, keepdims=True)
        if with_lat:
            m = jnp.maximum(m, jnp.max(s_l, axis=1, keepdims=True))
        acc = _dot(jnp.exp2(s_c - m).astype(BF), vc_ref[:, vcols])
        if with_lat:
            acc = acc + _dot(jnp.exp2(s_l - m).astype(BF), vl_ref[:, vcols])
        den_lane = HD * (1 - j)
        o = acc / acc[:, den_lane:den_lane + 1]
        for g in range(GQA_UNIT):
            res[(j, t0 + g)] = o[g * tq:(g + 1) * tq]
    for t in range(A_GROUP):
        o_ref[:, LANES * t:LANES * (t + 1)] = jnp.where(own[0], res[(0, t)], res[(1, t)]).astype(BF)


def _gqa(q, k_ctx, v_ctx, k_lat, v_lat, bsz, tq):
    sq = q.shape[0] // bsz
    nc = k_ctx.shape[0] // bsz
    nq = sq // tq
    with_lat = k_lat is not None
    in_specs = [pl.BlockSpec((tq, A_HEADS * HD), lambda b, i: (b * nq + i, 0)),
                pl.BlockSpec((nc, A_KV * HD), lambda b, i: (b, 0)),
                pl.BlockSpec((nc, 2 * A_KV * HD), lambda b, i: (b, 0))]
    args = [q, k_ctx, v_ctx]
    if with_lat:
        sl = k_lat.shape[0] // bsz
        in_specs += [pl.BlockSpec((sl, A_KV * HD), lambda b, i: (b, 0)),
                     pl.BlockSpec((sl, 2 * A_KV * HD), lambda b, i: (b, 0))]
        args += [k_lat, v_lat]
    return pl.pallas_call(
        functools.partial(_gqa_body, with_lat=with_lat, tq=tq),
        grid=(bsz, nq),
        in_specs=in_specs,
        out_specs=pl.BlockSpec((tq, A_HEADS * HD), lambda b, i: (b * nq + i, 0)),
        out_shape=jax.ShapeDtypeStruct(q.shape, BF),
        compiler_params=_params(("arbitrary", "arbitrary")),
        name="gqa_attention",
    )(*args)


GLA_Q = 0
GLA_K = B_HEADS * B_DK
GLA_V = 2 * B_HEADS * B_DK
GLA_R = GLA_V + B_HEADS * B_DV
GLA_W = GLA_R + B_HEADS * B_DV
GLA_GROUP = 8


def _gla_body(gc_ref, gl_ref, zc_ref, zl_ref, w2f_ref, w2b_ref, bf_ref, bb_ref, gg_ref,
              oc_ref, ol_ref, ofc_ref, ofl_ref, st_ref):
    nk = B_HEADS * B_DK
    ch = B_CHUNK
    row = lax.broadcasted_iota(jnp.int32, (ch, ch), 0)
    col = lax.broadcasted_iota(jnp.int32, (ch, ch), 1)
    lane = lax.broadcasted_iota(jnp.int32, (ch, nk), 1)
    head_masks = [(lane // B_DK) == h for h in range(B_HEADS)]

    def group_out(g_ref, z_ref, r0s, fwd, s_t):
        w2 = (w2f_ref if fwd else w2b_ref)[...]
        bias = (bf_ref if fwd else bb_ref)[...]
        keep = (row >= col) if fwd else (row <= col)
        tri = jnp.where(keep, 1.0, 0.0).astype(BF)
        blks = [g_ref[pl.ds(r0, ch), :] for r0 in r0s]
        pres = [_dot(z_ref[pl.ds(r0, ch), :], w2) + bias for r0 in r0s]
        las = [-(jnp.maximum(-p, 0.0) + jnp.log(1.0 + jnp.exp(-jnp.abs(p)))) * (1.0 / B_TAU) for p in pres]
        bcums = [_split_dot(tri, la) for la in las]
        qms, kis, kss, decs = [], [], [], []
        for blk, bcum in zip(blks, bcums):
            q = blk[:, GLA_Q:GLA_Q + nk].astype(F32)
            k = blk[:, GLA_K:GLA_K + nk].astype(F32)
            b_last = bcum[ch - 1:ch, :] if fwd else bcum[0:1, :]
            q_dec = q * jnp.exp(bcum)
            k_state = k * jnp.exp(b_last - bcum)
            qms.append([jnp.where(m, q_dec, 0.0).astype(BF) for m in head_masks])
            kis.append((k * jnp.exp(-bcum)).astype(BF))
            kss.append([jnp.where(m, k_state, 0.0).astype(BF) for m in head_masks])
            decs.append(jnp.exp(b_last))
        vs = [[blk[:, GLA_V + B_DV * h:GLA_V + B_DV * (h + 1)] for h in range(B_HEADS)] for blk in blks]
        atts = [[jnp.where(keep, _dot_nt(qm, ki), 0.0).astype(BF) for qm in qmu] for qmu, ki in zip(qms, kis)]
        o_intra = [[_dot(a, v) for a, v in zip(au, vu)] for au, vu in zip(atts, vs)]
        d_ss = []
        for vu, ksu in zip(vs, kss):
            d_s = _dot_tn(vu[0], ksu[0])
            for h in range(1, B_HEADS):
                d_s = d_s + _dot_tn(vu[h], ksu[h])
            d_ss.append(d_s)
        outs = []
        for u in range(len(r0s)):
            s_b = s_t.astype(BF)
            outs.append(jnp.concatenate(
                [o_intra[u][h] + _dot_nt(qms[u][h], s_b) for h in range(B_HEADS)], axis=1))
            s_t = decs[u] * s_t + d_ss[u]
        return outs, blks, s_t

    def fwd_pass(g_ref, z_ref, of_ref, n):
        grp = min(GLA_GROUP, n)
        assert n % grp == 0

        def body(i, carry):
            r0s = [pl.multiple_of((i * grp + u) * ch, ch) for u in range(grp)]
            outs, _, s_t = group_out(g_ref, z_ref, r0s, True, st_ref[...])
            for r0, o in zip(r0s, outs):
                of_ref[pl.ds(r0, ch), :] = o
            st_ref[...] = s_t
            return carry
        lax.fori_loop(0, n // grp, body, 0)

    def bwd_pass(g_ref, z_ref, of_ref, o_ref, n):
        grp = min(GLA_GROUP, n)
        assert n % grp == 0

        def body(i, carry):
            r0s = [pl.multiple_of((n - 1 - (i * grp + u)) * ch, ch) for u in range(grp)]
            outs, blks, s_t = group_out(g_ref, z_ref, r0s, False, st_ref[...])
            for r0, o, blk in zip(r0s, outs, blks):
                o = o + of_ref[pl.ds(r0, ch), :]
                parts = []
                for h in range(B_HEADS):
                    oh = o[:, B_DV * h:B_DV * (h + 1)]
                    ms = jnp.mean(oh * oh, axis=-1, keepdims=True)
                    parts.append(oh * lax.rsqrt(ms + EPS))
                y = jnp.concatenate(parts, axis=1) * gg_ref[...]
                r = blk[:, GLA_R:GLA_R + B_HEADS * B_DV].astype(F32)
                o_ref[pl.ds(r0, ch), :] = (y * _silu(r)).astype(BF)
            st_ref[...] = s_t
            return carry
        lax.fori_loop(0, n // grp, body, 0)

    nc = gc_ref.shape[0] // ch
    nl = gl_ref.shape[0] // ch
    st_ref[...] = jnp.zeros_like(st_ref)
    fwd_pass(gc_ref, zc_ref, ofc_ref, nc)
    fwd_pass(gl_ref, zl_ref, ofl_ref, nl)
    st_ref[...] = jnp.zeros_like(st_ref)
    bwd_pass(gc_ref, zc_ref, ofc_ref, oc_ref, nc)
    bwd_pass(gl_ref, zl_ref, ofl_ref, ol_ref, nl)


def _gla(g_ctx, g_lat, z_ctx, z_lat, w2f, w2b, b_f, b_b, gg, bsz):
    nc = g_ctx.shape[0] // bsz
    sl = g_lat.shape[0] // bsz
    nk = B_HEADS * B_DK
    nv = B_HEADS * B_DV
    full = lambda shape: pl.BlockSpec(shape, lambda b: (0, 0))
    return pl.pallas_call(
        _gla_body,
        grid=(bsz,),
        in_specs=[pl.BlockSpec((nc, GLA_W), lambda b: (b, 0)),
                  pl.BlockSpec((sl, GLA_W), lambda b: (b, 0)),
                  pl.BlockSpec((nc, LANES), lambda b: (b, 0)),
                  pl.BlockSpec((sl, LANES), lambda b: (b, 0)),
                  full((LANES, nk)), full((LANES, nk)), full((1, nk)), full((1, nk)), full((1, nv))],
        out_specs=[pl.BlockSpec((nc, nv), lambda b: (b, 0)),
                   pl.BlockSpec((sl, nv), lambda b: (b, 0))],
        out_shape=[jax.ShapeDtypeStruct((g_ctx.shape[0], nv), BF),
                   jax.ShapeDtypeStruct((g_lat.shape[0], nv), BF)],
        scratch_shapes=[pltpu.VMEM((nc, nv), F32), pltpu.VMEM((sl, nv), F32),
                        pltpu.VMEM((B_DV, nk), F32)],
        compiler_params=_params(("arbitrary",)),
        name="gla_bidir",
    )(g_ctx, g_lat, z_ctx, z_lat, w2f, w2b, b_f, b_b, gg)


def _mixer_residual(x_ref, m_ref, pair_refs):
    acc = _dot(pair_refs[0][...], pair_refs[1][...])
    for t in range(1, len(pair_refs) // 2):
        acc = acc + _dot(pair_refs[2 * t][...], pair_refs[2 * t + 1][...])
    return x_ref[...] + m_ref[0, 2:3, :] * acc


def _pair_specs(pairs, tm, index):
    in_specs, args = [], []
    for a, w in pairs:
        in_specs += [pl.BlockSpec((tm, a.shape[1]), index), pl.BlockSpec(w.shape, lambda *_: (0, 0))]
        args += [a, w]
    return in_specs, args


def _ffn_body(x_ref, g_ref, m_ref, *rest, n_pairs):
    pair_refs = rest[:2 * n_pairs]
    wg_ref, wu_ref, wd_ref, o_ref, xn_ref, h_ref, acc_ref = rest[2 * n_pairs:]
    j = pl.program_id(1)

    @pl.when(j == 0)
    def _():
        xn = _mixer_residual(x_ref, m_ref, pair_refs)
        xn_ref[...] = xn
        h_ref[...] = _norm_mod(xn, g_ref[...], m_ref[0, 3:4, :], m_ref[0, 4:5, :]).astype(BF)
        acc_ref[...] = jnp.zeros_like(acc_ref)

    h = h_ref[...]
    a = (_silu(_dot(h, wg_ref[...])) * _dot(h, wu_ref[...])).astype(BF)
    acc_ref[...] += _dot(a, wd_ref[...])

    @pl.when(j == pl.num_programs(1) - 1)
    def _():
        o_ref[...] = xn_ref[...] + m_ref[0, 5:6, :] * acc_ref[...]


def _ffn(x2, g, mpack, mod_of_tile, pairs, wg, wu, wd, tm, tf):
    rows = x2.shape[0]
    ff = wg.shape[1]
    pair_specs, pair_args = _pair_specs(pairs, tm, lambda i, j: (i, 0))
    return pl.pallas_call(
        functools.partial(_ffn_body, n_pairs=len(pairs)),
        grid=(rows // tm, ff // tf),
        in_specs=[pl.BlockSpec((tm, D), lambda i, j: (i, 0)),
                  pl.BlockSpec((1, D), lambda i, j: (0, 0)),
                  pl.BlockSpec((1, MOD_ROWS, D), lambda i, j: (mod_of_tile(i), 0, 0)),
                  *pair_specs,
                  pl.BlockSpec((D, tf), lambda i, j: (0, j)),
                  pl.BlockSpec((D, tf), lambda i, j: (0, j)),
                  pl.BlockSpec((tf, D), lambda i, j: (j, 0))],
        out_specs=pl.BlockSpec((tm, D), lambda i, j: (i, 0)),
        out_shape=jax.ShapeDtypeStruct((rows, D), F32),
        scratch_shapes=[pltpu.VMEM((tm, D), F32), pltpu.VMEM((tm, D), BF), pltpu.VMEM((tm, D), F32)],
        compiler_params=_params(("arbitrary", "arbitrary")),
        name="ffn_swiglu",
    )(x2, g.reshape(1, D), mpack, *pair_args, wg, wu, wd)


NA_QROWS = 4
NA_KROWS = 12
NA_TQ = NA_QROWS * GRID_W
NA_HG = 8
NA_W = NA_HG * HD


def _na_body(q_ref, k0_ref, k1_ref, k2_ref, v0_ref, v1_ref, v2_ref, kc_ref, vc_ref, b_ref, o_ref):
    pair = LANES // HD
    lane = lax.broadcasted_iota(jnp.int32, (NA_TQ, LANES), 1)
    own = [(lane // HD) == half for half in range(pair)]

    def scores(hh):
        tile = slice(LANES * (hh // pair), LANES * (hh // pair + 1))
        q = jnp.where(own[hh % pair], q_ref[:, tile], 0.0)
        s_loc = jnp.concatenate([_dot_nt(q, kr[:, tile]) for kr in (k0_ref, k1_ref, k2_ref)], axis=1)
        bias = b_ref[0, :, hh].reshape(NA_TQ, NA_KROWS * GRID_W)
        return s_loc + bias, _dot_nt(q, kc_ref[:, tile])

    nxt = scores(0)
    outs = []
    for hh in range(NA_HG):
        s_loc, s_ctx = nxt
        if hh + 1 < NA_HG:
            nxt = scores(hh + 1)
        vcols = slice(LANES * hh, LANES * (hh + 1))
        m = jnp.maximum(jnp.max(s_loc, axis=1, keepdims=True), jnp.max(s_ctx, axis=1, keepdims=True))
        p_loc = jnp.exp2(s_loc - m).astype(BF)
        acc = _dot(jnp.exp2(s_ctx - m).astype(BF), vc_ref[:, vcols])
        for t, vr in enumerate((v0_ref, v1_ref, v2_ref)):
            acc = acc + _dot(p_loc[:, t * NA_TQ:(t + 1) * NA_TQ], vr[:, vcols])
        den_lane = HD * (1 - hh % pair)
        outs.append(acc / acc[:, den_lane:den_lane + 1])
        if len(outs) == pair:
            t0 = hh // pair
            o_ref[:, LANES * t0:LANES * (t0 + 1)] = jnp.where(own[0], outs[0], outs[1]).astype(BF)
            outs = []


def _na_bias(rpb, rows):
    heads, n_dr, n_dc = rpb.shape
    nblk = rows // NA_QROWS
    pad_l = GRID_W - NA_COLS
    rext = jnp.pad(rpb.astype(F32) * LOG2E, ((0, 0), (0, 0), (pad_l, 2 * GRID_W - n_dc - pad_l)))
    flat = jnp.tile(rext, (1, 1, GRID_W))[:, :, :GRID_W * (2 * GRID_W - 1)]
    toe = flat.reshape(heads, n_dr, GRID_W, 2 * GRID_W - 1)[:, :, :, GRID_W - 1:]
    cols = np.arange(GRID_W)
    cs = np.clip(cols - NA_COLS // 2, 0, GRID_W - NA_COLS)
    col_ok = (cols[None, :] >= cs[:, None]) & (cols[None, :] < cs[:, None] + NA_COLS)
    toe = jnp.where(col_ok, toe, NEG_INF)
    toe_t = jnp.pad(toe.transpose(0, 2, 1, 3), ((0, 0), (0, 0), (NA_KROWS, NA_KROWS), (0, 0)),
                    constant_values=NEG_INF)
    n_off = toe_t.shape[2]
    cuts = []
    for kind, j in enumerate((0, min(1, nblk - 1), nblk - 1)):
        ws = NA_QROWS * int(np.clip(j - 1, 0, nblk - 3))
        for qr in range(NA_QROWS):
            r = NA_QROWS * j + qr
            rs = int(np.clip(r - NA_ROWS // 2, 0, rows - NA_ROWS))
            cuts.append((kind, qr, NA_KROWS + ws - r + NA_ROWS - 1, max(rs - ws, 0), min(rs + NA_ROWS - ws, NA_KROWS)))
    width = NA_KROWS * GRID_W

    def body(t_ref, o_ref):
        t = t_ref[0]
        key_row = lax.broadcasted_iota(jnp.int32, (GRID_W, width), 1) // GRID_W
        for kind, qr, d0, lo, hi in cuts:
            blk = t[:, d0 * GRID_W:d0 * GRID_W + width]
            o_ref[kind, qr, 0] = jnp.where((key_row >= lo) & (key_row < hi), blk, NEG_INF)

    return pl.pallas_call(
        body,
        grid=(heads,),
        in_specs=[pl.BlockSpec((1, GRID_W, n_off * GRID_W), lambda h: (h, 0, 0))],
        out_specs=pl.BlockSpec((3, NA_QROWS, 1, GRID_W, width), lambda h: (0, 0, h, 0, 0)),
        out_shape=jax.ShapeDtypeStruct((3, NA_QROWS, heads, GRID_W, width), F32),
        compiler_params=_params(("arbitrary",)),
        name="natten_bias",
    )(toe_t.reshape(heads, GRID_W, n_off * GRID_W))


def _natten(q, k, v, k_ctx, v_ctx, bias, bsz):
    s = q.shape[0] // bsz
    nc = k_ctx.shape[0] // bsz
    nblk = s // NA_TQ
    ngrp = C_HEADS // NA_HG
    v_w = NA_HG * LANES

    def kv_spec(t, width):
        return pl.BlockSpec((NA_TQ, width),
                            lambda j, g, b: (b * nblk + jnp.clip(j - 1, 0, nblk - 3) + t, g))

    def kind(j):
        return jnp.where(j == 0, 0, jnp.where(j == nblk - 1, 2, 1))

    return pl.pallas_call(
        _na_body,
        grid=(nblk, ngrp, bsz),
        in_specs=[pl.BlockSpec((NA_TQ, NA_W), lambda j, g, b: (b * nblk + j, g)),
                  kv_spec(0, NA_W), kv_spec(1, NA_W), kv_spec(2, NA_W),
                  kv_spec(0, v_w), kv_spec(1, v_w), kv_spec(2, v_w),
                  pl.BlockSpec((nc, NA_W), lambda j, g, b: (b, g)),
                  pl.BlockSpec((nc, v_w), lambda j, g, b: (b, g)),
                  pl.BlockSpec((1, NA_QROWS, NA_HG, GRID_W, NA_KROWS * GRID_W),
                               lambda j, g, b: (kind(j), 0, g, 0, 0))],
        out_specs=pl.BlockSpec((NA_TQ, NA_W), lambda j, g, b: (b * nblk + j, g)),
        out_shape=jax.ShapeDtypeStruct(q.shape, BF),
        compiler_params=_params(("arbitrary", "arbitrary", "arbitrary")),
        name="natten",
    )(q, k, k, k, v, v, v, k_ctx, v_ctx, bias)


MOE_TM = 512
MOE_TF = 1792
META_E1, META_E2, META_R1, META_R2, META_G1, META_G2 = range(6)
ROW_TILE = D // LANES


def _to_token_tiles(dst_ref, value):
    n = value.shape[0]
    for s in range(ROW_TILE):
        dst_ref[pl.ds(s, n, stride=ROW_TILE), :] = value[:, LANES * s:LANES * (s + 1)]


def _from_token_tiles(src_ref, n):
    return jnp.concatenate([src_ref[pl.ds(s, n, stride=ROW_TILE), :] for s in range(ROW_TILE)], axis=1)


def _route_body(x_ref, g_ref, m_ref, *rest, n_pairs):
    pair_refs = rest[:2 * n_pairs]
    wr_ref, xn_ref, h_ref, meta_ref, rec_ref, cnt_ref, tri_ref, carry_ref = rest[2 * n_pairs:]
    tm = x_ref.shape[0]

    @pl.when(pl.program_id(0) == 0)
    def _():
        row = lax.broadcasted_iota(jnp.int32, (tm, tm), 0)
        col = lax.broadcasted_iota(jnp.int32, (tm, tm), 1)
        tri_ref[...] = jnp.where(row >= col, 1.0, 0.0).astype(BF)
        carry_ref[...] = jnp.zeros_like(carry_ref)

    xn = _mixer_residual(x_ref, m_ref, pair_refs)
    xn_ref[...] = xn
    h = _norm_mod(xn, g_ref[...], m_ref[0, 3:4, :], m_ref[0, 4:5, :])
    _to_token_tiles(h_ref, h)
    h_hi = h.astype(BF)
    h_lo = (h - h_hi.astype(F32)).astype(BF)
    w = wr_ref[...]
    w_hi = w.astype(BF)
    w_lo = (w - w_hi.astype(F32)).astype(BF)
    logits = _dot(h_hi, w_hi) + (_dot(h_lo, w_hi) + _dot(h_hi, w_lo))
    lane = lax.broadcasted_iota(jnp.int32, logits.shape, 1)
    lg = jnp.where(lane < N_EXPERTS, logits, NEG_INF)
    m1 = jnp.max(lg, axis=1, keepdims=True)
    i1 = jnp.min(jnp.where(lg == m1, lane, LANES), axis=1, keepdims=True)
    lg2 = jnp.where(lane == i1, NEG_INF, lg)
    m2 = jnp.max(lg2, axis=1, keepdims=True)
    i2 = jnp.min(jnp.where(lg2 == m2, lane, LANES), axis=1, keepdims=True)
    e = jnp.exp(m2 - m1)
    g1 = 1.0 / (1.0 + e)
    g2 = e / (1.0 + e)
    oh = jnp.where(lane == i1, 1.0, 0.0) + jnp.where(lane == i2, 1.0, 0.0)
    cum = _dot(tri_ref[...], oh.astype(BF))
    base = carry_ref[0:1, :] + cum - oh
    r1 = jnp.sum(jnp.where(lane == i1, base, 0.0), axis=1, keepdims=True)
    r2 = jnp.sum(jnp.where(lane == i2, base, 0.0), axis=1, keepdims=True)
    carry_ref[...] = carry_ref[...] + cum[tm - 1:tm, :]
    cnt_ref[...] = carry_ref[...]
    meta = jnp.zeros(logits.shape, F32)
    for ln, val in ((META_E1, i1.astype(F32)), (META_E2, i2.astype(F32)), (META_R1, r1), (META_R2, r2),
                    (META_G1, g1), (META_G2, g2)):
        meta = jnp.where(lane == ln, val, meta)
    meta_ref[...] = meta
    rec_ref[...] = meta.T[:rec_ref.shape[0], :]


def _route(x2, g, mpack, mod_of_tile, pairs, wr, tm):
    rows = x2.shape[0]
    pair_specs, pair_args = _pair_specs(pairs, tm, lambda i: (i, 0))
    return pl.pallas_call(
        functools.partial(_route_body, n_pairs=len(pairs)),
        grid=(rows // tm,),
        in_specs=[pl.BlockSpec((tm, D), lambda i: (i, 0)),
                  pl.BlockSpec((1, D), lambda i: (0, 0)),
                  pl.BlockSpec((1, MOD_ROWS, D), lambda i: (mod_of_tile(i), 0, 0)),
                  *pair_specs,
                  pl.BlockSpec((D, LANES), lambda i: (0, 0))],
        out_specs=[pl.BlockSpec((tm, D), lambda i: (i, 0)),
                   pl.BlockSpec((tm * ROW_TILE, LANES), lambda i: (i, 0)),
                   pl.BlockSpec((tm, LANES), lambda i: (i, 0)),
                   pl.BlockSpec((8, tm), lambda i: (0, i)),
                   pl.BlockSpec((8, LANES), lambda i: (0, 0))],
        out_shape=[jax.ShapeDtypeStruct((rows, D), F32),
                   jax.ShapeDtypeStruct((rows * ROW_TILE, LANES), F32),
                   jax.ShapeDtypeStruct((rows, LANES), F32),
                   jax.ShapeDtypeStruct((8, rows), F32),
                   jax.ShapeDtypeStruct((8, LANES), F32)],
        scratch_shapes=[pltpu.VMEM((tm, tm), BF), pltpu.VMEM((8, LANES), F32)],
        compiler_params=_params(("arbitrary",)),
        name="moe_route",
    )(x2, g.reshape(1, D), mpack, *pair_args, wr)


def _token_copy(src_ref, src_tok, dst_ref, dst_tok, sem):
    src = src_ref.at[pl.ds(pl.multiple_of(src_tok * ROW_TILE, ROW_TILE), ROW_TILE), :]
    dst = dst_ref.at[pl.ds(pl.multiple_of(dst_tok * ROW_TILE, ROW_TILE), ROW_TILE), :]
    return pltpu.make_async_copy(src, dst, sem)


def _experts_body(te_ref, nu_ref, cur_ref, nxt_ref, dprev_ref, h_hbm, wg_ref, wu_ref, wd_ref, y_hbm,
                  xbuf, h_ref, acc_ref, ybuf, gsem, ssem, *, n_steps):
    assert n_steps >= 2
    i = pl.program_id(0)
    j = pl.program_id(1)
    used = i < nu_ref[0]
    slot = i % 2
    other = 1 - slot

    def gather_row(idx_ref, r, sl):
        _token_copy(h_hbm, idx_ref[0, 0, r], xbuf.at[sl], r, gsem.at[sl]).start()

    def issue_gathers():
        for r in range(MOE_TM):
            gather_row(nxt_ref, r, other)

    def issue_scatters():
        for r in range(MOE_TM):
            _token_copy(ybuf.at[other], r, y_hbm, dprev_ref[0, 0, r], ssem.at[other]).start()

    def wait_rows(buf, sem):
        pltpu.make_async_copy(buf, buf, sem).wait()

    def swiglu_step():
        h = h_ref[...]
        a = (_silu(_dot(h, wg_ref[0])) * _dot(h, wu_ref[0])).astype(BF)
        return _dot(a, wd_ref[0])

    @pl.when((j == 0) & (i == 0))
    def _():
        ybuf[...] = jnp.zeros_like(ybuf)
        n_spare = MOE_TM * ROW_TILE
        for sl in range(2):
            spare = pltpu.make_async_copy(
                ybuf.at[sl], y_hbm.at[pl.ds(y_hbm.shape[0] - (2 - sl) * n_spare, n_spare), :], ssem.at[sl])
            spare.start()
            spare.wait()

        def body(r, c):
            gather_row(cur_ref, r, 0)
            return c
        lax.fori_loop(0, MOE_TM, body, 0, unroll=8)

    @pl.when(j == 0)
    def _():
        wait_rows(xbuf.at[slot], gsem.at[slot])

    @pl.when(used & (j == 0))
    def _():
        h_ref[...] = _from_token_tiles(xbuf.at[slot], MOE_TM).astype(BF)
        acc_ref[...] = swiglu_step()
        issue_gathers()

    if n_steps > 2:
        @pl.when(used & (j > 0) & (j < n_steps - 1))
        def _():
            acc_ref[...] += swiglu_step()

    @pl.when(used & (j == n_steps - 1))
    def _():
        acc_ref[...] += swiglu_step()
        issue_scatters()

    @pl.when(jnp.logical_not(used) & (j == 0))
    def _():
        issue_gathers()

    @pl.when(jnp.logical_not(used) & (j == n_steps - 1))
    def _():
        issue_scatters()

    @pl.when(j == n_steps - 1)
    def _():
        @pl.when(i > 0)
        def _():
            wait_rows(ybuf.at[slot], ssem.at[slot])

        @pl.when(used)
        def _():
            _to_token_tiles(ybuf.at[slot], acc_ref[...])

        @pl.when(i == pl.num_programs(0) - 1)
        def _():
            wait_rows(ybuf.at[other], ssem.at[other])
            wait_rows(xbuf.at[other], gsem.at[other])


def _experts(h_tiles, tok_of_slot, dest_of_slot, tile_expert, n_used, wg, wu, wd, out_tokens):
    n_tiles = tile_expert.shape[0]
    ff = wg.shape[2]
    nf = ff // MOE_TF
    spare_rows = out_tokens - MOE_TM + jnp.arange(MOE_TM, dtype=jnp.int32)
    dest_prev = jnp.concatenate([spare_rows, dest_of_slot[:-MOE_TM]])

    def jj(i, j, nu):
        return jnp.where(i < nu[0], j, nf - 1)

    def smem_rows(index):
        return pl.BlockSpec((1, 1, MOE_TM), lambda i, j, te, nu: (index(i), 0, 0), memory_space=pltpu.SMEM)

    grid_spec = pltpu.PrefetchScalarGridSpec(
        num_scalar_prefetch=2,
        grid=(n_tiles, nf),
        in_specs=[smem_rows(lambda i: i),
                  smem_rows(lambda i: jnp.minimum(i + 1, n_tiles - 1)),
                  smem_rows(lambda i: i),
                  pl.BlockSpec(memory_space=pl.ANY),
                  pl.BlockSpec((1, D, MOE_TF), lambda i, j, te, nu: (te[i], 0, jj(i, j, nu))),
                  pl.BlockSpec((1, D, MOE_TF), lambda i, j, te, nu: (te[i], 0, jj(i, j, nu))),
                  pl.BlockSpec((1, MOE_TF, D), lambda i, j, te, nu: (te[i], jj(i, j, nu), 0))],
        out_specs=pl.BlockSpec(memory_space=pl.ANY),
        scratch_shapes=[pltpu.VMEM((2, MOE_TM * ROW_TILE, LANES), F32),
                        pltpu.VMEM((MOE_TM, D), BF),
                        pltpu.VMEM((MOE_TM, D), F32),
                        pltpu.VMEM((2, MOE_TM * ROW_TILE, LANES), F32),
                        pltpu.SemaphoreType.DMA((2,)),
                        pltpu.SemaphoreType.DMA((2,))])
    tok3 = tok_of_slot.reshape(n_tiles, 1, MOE_TM)
    return pl.pallas_call(
        functools.partial(_experts_body, n_steps=nf),
        grid_spec=grid_spec,
        out_shape=jax.ShapeDtypeStruct((out_tokens * ROW_TILE, LANES), F32),
        compiler_params=_params(("arbitrary", "arbitrary")),
        name="moe_experts",
    )(tile_expert, n_used, tok3, tok3, dest_prev.reshape(n_tiles, 1, MOE_TM), h_tiles, wg, wu, wd)


def _combine_body(y1_ref, y2_ref, x_ref, m_ref, meta_ref, gf_ref, o_ref):
    n = x_ref.shape[0]
    meta = meta_ref[...]
    lane = lax.broadcasted_iota(jnp.int32, meta.shape, 1)
    g1 = jnp.sum(jnp.where(lane == META_G1, meta, 0.0), axis=1, keepdims=True)
    g2 = jnp.sum(jnp.where(lane == META_G2, meta, 0.0), axis=1, keepdims=True)
    moe = g1 * _from_token_tiles(y1_ref, n) + g2 * _from_token_tiles(y2_ref, n)
    y = x_ref[...] + m_ref[0, 5:6, :] * moe
    ms = jnp.mean(y * y, axis=-1, keepdims=True)
    o_ref[...] = y * lax.rsqrt(ms + EPS) * gf_ref[...]


def _combine(x2, mpack, mod_of_row, meta, y_tiles, g_final, tm):
    rows = x2.shape[0]
    nt = rows // tm
    return pl.pallas_call(
        _combine_body,
        grid=(nt,),
        in_specs=[pl.BlockSpec((tm * ROW_TILE, LANES), lambda i: (i, 0)),
                  pl.BlockSpec((tm * ROW_TILE, LANES), lambda i: (i + nt, 0)),
                  pl.BlockSpec((tm, D), lambda i: (i, 0)),
                  pl.BlockSpec((1, MOD_ROWS, D), lambda i: (mod_of_row(i * tm), 0, 0)),
                  pl.BlockSpec((tm, LANES), lambda i: (i, 0)),
                  pl.BlockSpec((1, D), lambda i: (0, 0))],
        out_specs=pl.BlockSpec((tm, D), lambda i: (i, 0)),
        out_shape=jax.ShapeDtypeStruct((rows, D), F32),
        compiler_params=_params(("arbitrary",)),
        name="moe_combine",
    )(y_tiles, y_tiles, x2, mpack, meta, g_final.reshape(1, D))


def _moe(x2, g, mpack, mod_of_tile, mod_of_row, pairs, wr, wg, wu, wd, g_final, tm):
    rows = x2.shape[0]
    ne = wg.shape[0]
    x2, h_tiles, meta, rec, cnt = _route(x2, g, mpack, mod_of_tile, pairs, wr, tm)
    counts = cnt[0, :ne].astype(jnp.int32)
    tiles_e = (counts + MOE_TM - 1) // MOE_TM
    tile_end = jnp.cumsum(tiles_e)
    group_off = (tile_end - tiles_e) * MOE_TM
    n_tiles = 2 * rows // MOE_TM + ne
    n_slots = n_tiles * MOE_TM
    tile_expert = jnp.minimum(
        jnp.sum(jnp.arange(n_tiles)[:, None] >= tile_end[None, :], axis=1), ne - 1).astype(jnp.int32)
    n_used = tile_end[-1:].astype(jnp.int32)
    eids = jnp.arange(ne)

    def slot_of(lane_e, lane_r):
        e = rec[lane_e].astype(jnp.int32)
        off = jnp.sum(jnp.where(e[None, :] == eids[:, None], group_off[:, None], 0), axis=0)
        return off + rec[lane_r].astype(jnp.int32)

    tok = jnp.arange(rows, dtype=jnp.int32)
    dest = jnp.full((n_slots,), -1, jnp.int32).at[
        jnp.concatenate([slot_of(META_E1, META_R1), slot_of(META_E2, META_R2)])].set(
        jnp.concatenate([tok, rows + tok]), unique_indices=True, mode="promise_in_bounds")
    pad = dest < 0
    dest_of_slot = jnp.where(pad, 2 * rows + jnp.arange(n_slots, dtype=jnp.int32) % (2 * MOE_TM), dest)
    tok_of_slot = jnp.where(pad, 0, jnp.where(dest >= rows, dest - rows, dest))
    y_tiles = _experts(h_tiles, tok_of_slot, dest_of_slot, tile_expert, n_used, wg, wu, wd, 2 * rows + 2 * MOE_TM)
    return _combine(x2, mpack, mod_of_row, meta, y_tiles, g_final, MOE_TM)


def _rope_tables(seq_len):
    t = np.arange(seq_len)
    row = (t // GRID_W).astype(np.float32)
    col = (t % GRID_W).astype(np.float32)
    n_axis = HD // 4
    inv = jnp.power(ROPE_THETA, -jnp.arange(n_axis, dtype=F32) / n_axis)
    ang = jnp.concatenate([jnp.asarray(row)[:, None] * inv, jnp.asarray(col)[:, None] * inv], axis=-1)
    cos, sin = jnp.cos(ang), jnp.sin(ang)
    ct = jnp.tile(jnp.concatenate([cos, cos], axis=-1), (1, LANES // HD))
    st = jnp.tile(jnp.concatenate([-sin, sin], axis=-1), (1, LANES // HD))
    return ct, st


def kernel(x, c, ctx, c_ctx, w_mod, b_mod, g_norm1, g_norm2, w_in_ab, g_q, g_k, w_a2_f, b_a_f, w_a2_b,
           b_a_b, g_gla, w_out_ab, w_ff_gate, w_ff_up, w_ff_down, w_in_c, rpb_c, w_out_c, w_router,
           w_moe_gate, w_moe_up, w_moe_down, g_final):
    bsz, seq, _ = x.shape
    nc = ctx.shape[1]
    depth = w_mod.shape[0]
    assert depth == 2 and seq % GRID_W == 0

    cond_rows = -(-(bsz + 1) // 8) * 8
    cond = jnp.zeros((cond_rows, D), F32).at[:bsz].set(c).at[bsz].set(c_ctx)
    mods = _adaln(cond, w_mod, b_mod).reshape(depth, cond_rows, N_MOD, D)
    mods = jnp.pad(mods, ((0, 0), (0, 0), (0, MOD_ROWS - N_MOD), (0, 0)))

    xl = x.reshape(bsz * seq, D)
    xc = ctx.reshape(bsz * nc, D)
    tm_l = min(1024, seq)
    tm_c = min(1024, bsz * nc)
    tpb = seq // tm_l
    lat_mod = lambda i: i // tpb
    ctx_mod = lambda i: bsz

    mp = mods[0]
    w_in = w_in_ab[0]
    perm = np.concatenate([np.arange(0, HD, 2), np.arange(1, HD, 2)])
    nq, nkv = A_HEADS * HD, A_KV * HD
    o_q, o_k, o_v = 0, nq, nq + nkv
    o_bq = o_v + nkv
    o_bk = o_bq + B_HEADS * B_DK
    o_bv = o_bk + B_HEADS * B_DK
    o_br = o_bv + B_HEADS * B_DV
    o_zf = o_br + B_HEADS * B_DV
    o_end = o_zf + 2 * B_RANK
    assert A_KV * HD == LANES
    q_order = np.arange(A_HEADS).reshape(A_KV, A_GROUP).T.reshape(-1)
    wq = w_in[:, o_q:o_k].reshape(D, A_HEADS, HD)[:, q_order][:, :, perm].reshape(D, nq)
    wk = w_in[:, o_k:o_v].reshape(D, A_KV, HD)[:, :, perm].reshape(D, nkv)
    wz = jnp.pad(w_in[:, o_zf:o_end], ((0, 0), (0, LANES - 2 * B_RANK)))
    w0 = jnp.concatenate([wq, wk, w_in[:, o_v:o_bq], w_in[:, o_bq:o_bk] * (B_DK ** -0.5),
                          w_in[:, o_bk:o_zf], wz], axis=1).astype(BF)
    widths0 = (nq + nkv, nkv, GLA_W, LANES)
    outw0 = (nq, nkv, 2 * nkv, GLA_W, LANES)
    gqk = jnp.concatenate([jnp.tile(g_q[0][perm] * (HD ** -0.5 * LOG2E), A_HEADS),
                           jnp.tile(g_k[0][perm], A_KV)]).reshape(1, nq + nkv)
    bd = jnp.asarray(np.tile(np.kron(np.eye(LANES // HD), np.ones((HD, HD))), (2, 1)), BF)
    ct, st = _rope_tables(seq)
    ct_c = jnp.ones((tm_c, LANES), F32)
    st_c = jnp.zeros((tm_c, LANES), F32)

    kinds0 = ("qk", "values", "plain", "plain")
    ql, kl, vl, gl, zl = _proj(xl, g_norm1[0], mp, lat_mod, w0, widths0, kinds0, outw0, tm_l, (ct, st, gqk, bd))
    qc, kc, vc, gc, zc = _proj(xc, g_norm1[0], mp, ctx_mod, w0, widths0, kinds0, outw0, tm_c,
                               (ct_c, st_c, gqk, bd))

    oa_l = _gqa(ql, kc, vc, kl, vl, bsz, 256)
    oa_c = _gqa(qc, kc, vc, None, None, bsz, min(128, nc))

    w2f = jnp.zeros((LANES, B_HEADS * B_DK), F32).at[:B_RANK].set(w_a2_f[0]).astype(BF)
    w2b = jnp.zeros((LANES, B_HEADS * B_DK), F32).at[B_RANK:2 * B_RANK].set(w_a2_b[0]).astype(BF)
    ob_c, ob_l = _gla(gc, gl, zc, zl, w2f, w2b, b_a_f[0].reshape(1, -1), b_a_b[0].reshape(1, -1),
                      jnp.tile(g_gla[0], B_HEADS).reshape(1, -1), bsz)

    wo = w_out_ab[0].astype(BF)
    wo_a = wo[:nq].reshape(A_HEADS, HD, D)[q_order].reshape(nq, D)
    wo_b = wo[nq:]
    wg, wu, wd = w_ff_gate[0].astype(BF), w_ff_up[0].astype(BF), w_ff_down[0].astype(BF)
    xl = _ffn(xl, g_norm2[0], mp, lat_mod, [(oa_l, wo_a), (ob_l, wo_b)], wg, wu, wd, tm_l, 512)
    xc = _ffn(xc, g_norm2[0], mp, ctx_mod, [(oa_c, wo_a), (ob_c, wo_b)], wg, wu, wd, tm_c, 512)

    mp = mods[1]
    cw = C_HEADS * HD
    w1 = w_in_c[0].astype(BF)
    q1, k1, v1 = _proj(xl, g_norm1[1], mp, lat_mod, w1, (cw, cw, cw), ("plain", "plain", "values"),
                       (cw, cw, 2 * cw), tm_l, scales=(HD ** -0.5 * LOG2E, 1.0, 1.0))
    k1c, v1c = _proj(xc, g_norm1[1], mp, ctx_mod, w1[:, cw:], (cw, cw), ("plain", "values"), (cw, 2 * cw), tm_c)

    bias = _na_bias(rpb_c[0], seq // GRID_W)
    o1 = _natten(q1, k1, v1, k1c, v1c, bias, bsz)

    wr = jnp.pad(w_router[0], ((0, 0), (0, LANES - N_EXPERTS)))
    out = _moe(xl, g_norm2[1], mp, lat_mod, lambda r: r // seq, [(o1, w_out_c[0].astype(BF))], wr,
               w_moe_gate[0].astype(BF), w_moe_up[0].astype(BF), w_moe_down[0].astype(BF), g_final, tm_l)
    return out.reshape(bsz, seq, D)
```

```python
import functools

import numpy as np
import jax
import jax.numpy as jnp
from jax import lax
from jax.experimental import pallas as pl
from jax.experimental.pallas import tpu as pltpu

F32 = jnp.float32
BF = jnp.bfloat16

D = 1024
EPS = 1e-6
N_MOD = 6
MOD_ROWS = 8
GRID_W = 64
HD = 64
A_HEADS = 8
A_KV = 2
A_GROUP = A_HEADS // A_KV
ROPE_THETA = 10000.0
LOG2E = 1.4426950408889634
B_HEADS = 4
B_DK = 64
B_DV = 128
B_RANK = 16
B_TAU = 16.0
B_CHUNK = 64
C_HEADS = 16
NA_ROWS = 8
NA_COLS = 16
N_EXPERTS = 8
LANES = 128
VMEM_LIMIT = 56 * 2 ** 20

NEG_INF = float("-inf")


def _params(sem):
    return pltpu.CompilerParams(dimension_semantics=sem, vmem_limit_bytes=VMEM_LIMIT)


def _dot(a, b):
    return jnp.dot(a, b, preferred_element_type=F32)


def _dot_nt(a, b):
    return lax.dot_general(a, b, (((1,), (1,)), ((), ())), preferred_element_type=F32)


def _dot_tn(a, b):
    return lax.dot_general(a, b, (((0,), (0,)), ((), ())), preferred_element_type=F32)


def _silu(x):
    return x / (1.0 + jnp.exp(-x))


def _split_dot(lhs_bf_exact, x):
    hi = x.astype(BF)
    lo = (x - hi.astype(F32)).astype(BF)
    return _dot(lhs_bf_exact, hi) + _dot(lhs_bf_exact, lo)


def _norm_mod(x, g, shift, scale):
    ms = jnp.mean(x * x, axis=-1, keepdims=True)
    return (x * lax.rsqrt(ms + EPS) * g) * (1.0 + scale) + shift


def _adaln_body(c_ref, w_ref, b_ref, o_ref):
    s = _silu(c_ref[...])
    o_ref[0] = _dot(s.astype(BF), w_ref[0].astype(BF)) + b_ref[0]


def _adaln(cond, w_mod, b_mod):
    depth, _, n = w_mod.shape
    rows = cond.shape[0]
    tn = 1536
    return pl.pallas_call(
        _adaln_body,
        grid=(depth, n // tn),
        in_specs=[pl.BlockSpec((rows, D), lambda l, j: (0, 0)),
                  pl.BlockSpec((1, D, tn), lambda l, j: (l, 0, j)),
                  pl.BlockSpec((1, 1, tn), lambda l, j: (l, 0, j))],
        out_specs=pl.BlockSpec((1, rows, tn), lambda l, j: (l, 0, j)),
        out_shape=jax.ShapeDtypeStruct((depth, rows, n), F32),
        compiler_params=_params(("arbitrary", "arbitrary")),
        name="adaln",
    )(cond, w_mod, b_mod.reshape(depth, 1, n))


def _qk_norm_rope(p, gqk, bd, ct, st):
    pp = p * p
    sums = []
    for c in range(p.shape[1] // LANES):
        x = pp[:, c * LANES:(c + 1) * LANES]
        hi = x.astype(BF)
        lo = (x - hi.astype(F32)).astype(BF)
        sums.append(_dot(jnp.concatenate([hi, lo], axis=1), bd))
    ss = jnp.concatenate(sums, axis=1)
    y = p * lax.rsqrt(ss * (1.0 / HD) + EPS) * gqk
    lane = lax.broadcasted_iota(jnp.int32, ct.shape, 1)
    first = (lane % HD) < (HD // 2)
    outs = []
    for c in range(p.shape[1] // LANES):
        yc = y[:, c * LANES:(c + 1) * LANES]
        partner = jnp.where(first, pltpu.roll(yc, LANES - HD // 2, 1), pltpu.roll(yc, HD // 2, 1))
        outs.append(yc * ct + partner * st)
    return jnp.concatenate(outs, axis=1)


def _values_with_ones(p):
    lane = lax.broadcasted_iota(jnp.int32, (p.shape[0], LANES), 1)
    low = lane < HD
    tiles = []
    for c in range(p.shape[1] // LANES):
        x = p[:, c * LANES:(c + 1) * LANES]
        tiles += [jnp.where(low, x, 1.0), jnp.where(low, 1.0, x)]
    return jnp.concatenate(tiles, axis=1)


def _proj_body(x_ref, g_ref, m_ref, w_ref, *rest, widths, kinds, scales, col0):
    if "qk" in kinds:
        ct_ref, st_ref, gqk_ref, bd_ref = rest[:4]
        o_refs = rest[4:]
    else:
        o_refs = rest
    h = _norm_mod(x_ref[...], g_ref[...], m_ref[0, 0:1, :], m_ref[0, 1:2, :]).astype(BF)
    off = col0
    oi = 0
    for wd, kind, scale in zip(widths, kinds, scales):
        p = _dot(h, w_ref[:, off:off + wd])
        off += wd
        if scale != 1.0:
            p = p * scale
        if kind == "qk":
            p = _qk_norm_rope(p, gqk_ref[...], bd_ref[...], ct_ref[...], st_ref[...])
            nq = A_HEADS * HD
            o_refs[oi][...] = p[:, :nq].astype(BF)
            o_refs[oi + 1][...] = p[:, nq:].astype(BF)
            oi += 2
        else:
            if kind == "values":
                p = _values_with_ones(p)
            o_refs[oi][...] = p.astype(BF)
            oi += 1


def _proj(x2, g, mpack, mod_of_tile, w, widths, kinds, out_widths, tm, rope_args=None, scales=None, col0=0):
    rows = x2.shape[0]
    n = w.shape[1]
    in_specs = [pl.BlockSpec((tm, D), lambda i: (i, 0)),
                pl.BlockSpec((1, D), lambda i: (0, 0)),
                pl.BlockSpec((1, MOD_ROWS, D), lambda i: (mod_of_tile(i), 0, 0)),
                pl.BlockSpec((D, n), lambda i: (0, 0))]
    args = [x2, g.reshape(1, D), mpack, w]
    if rope_args is not None:
        ct, st, gqk, bd = rope_args
        nt = ct.shape[0] // tm
        in_specs += [pl.BlockSpec((tm, LANES), lambda i: (i % nt, 0)),
                     pl.BlockSpec((tm, LANES), lambda i: (i % nt, 0)),
                     pl.BlockSpec(gqk.shape, lambda i: (0, 0)),
                     pl.BlockSpec(bd.shape, lambda i: (0, 0))]
        args += [ct, st, gqk, bd]
    return pl.pallas_call(
        functools.partial(_proj_body, widths=widths, kinds=kinds, scales=scales or (1.0,) * len(widths),
                          col0=col0),
        grid=(rows // tm,),
        in_specs=in_specs,
        out_specs=[pl.BlockSpec((tm, ow), lambda i: (i, 0)) for ow in out_widths],
        out_shape=[jax.ShapeDtypeStruct((rows, ow), BF) for ow in out_widths],
        compiler_params=_params(("arbitrary",)),
        name="norm_mod_proj",
    )(*args)


GQA_UNIT = 2

def _gqa_body(q_ref, kc_ref, vc_ref, *rest, with_lat, tq):
    if with_lat:
        kl_ref, vl_ref, o_ref = rest
    else:
        (o_ref,) = rest
    units = [(j, t0) for j in range(A_KV) for t0 in range(0, A_GROUP, GQA_UNIT)]
    lane = lax.broadcasted_iota(jnp.int32, (tq, LANES), 1)
    own = [(lane // HD) == j for j in range(A_KV)]

    def scores(unit):
        j, t0 = unit
        qg = jnp.concatenate(
            [jnp.where(own[j], q_ref[:, LANES * t:LANES * (t + 1)], 0.0) for t in range(t0, t0 + GQA_UNIT)],
            axis=0)
        s_c = _dot_nt(qg, kc_ref[...])
        s_l = _dot_nt(qg, kl_ref[...]) if with_lat else None
        return s_c, s_l

    res = {}
    nxt = scores(units[0])
    for u, (j, t0) in enumerate(units):
        s_c, s_l = nxt
        if u + 1 < len(units):
            nxt = scores(units[u + 1])
        vcols = slice(LANES * j, LANES * (j + 1))
        m = jnp.max(s_c, axis=1, keepdims=True)
        if with_lat:
            m = jnp.maximum(m, jnp.max(s_l, axis=1, keepdims=True))
        acc = _dot(jnp.exp2(s_c - m).astype(BF), vc_ref[:, vcols])
        if with_lat:
            acc = acc + _dot(jnp.exp2(s_l - m).astype(BF), vl_ref[:, vcols])
        den_lane = HD * (1 - j)
        o = acc / acc[:, den_lane:den_lane + 1]
        for g in range(GQA_UNIT):
            res[(j, t0 + g)] = o[g * tq:(g + 1) * tq]
    for t in range(A_GROUP):
        o_ref[:, LANES * t:LANES * (t + 1)] = jnp.where(own[0], res[(0, t)], res[(1, t)]).astype(BF)


def _gqa(q, k_ctx, v_ctx, k_lat, v_lat, bsz, tq):
    sq = q.shape[0] // bsz
    nc = k_ctx.shape[0] // bsz
    nq = sq // tq
    with_lat = k_lat is not None
    in_specs = [pl.BlockSpec((tq, A_HEADS * HD), lambda b, i: (b * nq + i, 0)),
                pl.BlockSpec((nc, A_KV * HD), lambda b, i: (b, 0)),
                pl.BlockSpec((nc, 2 * A_KV * HD), lambda b, i: (b, 0))]
    args = [q, k_ctx, v_ctx]
    if with_lat:
        sl = k_lat.shape[0] // bsz
        in_specs += [pl.BlockSpec((sl, A_KV * HD), lambda b, i: (b, 0)),
                     pl.BlockSpec((sl, 2 * A_KV * HD), lambda b, i: (b, 0))]
        args += [k_lat, v_lat]
    return pl.pallas_call(
        functools.partial(_gqa_body, with_lat=with_lat, tq=tq),
        grid=(bsz, nq),
        in_specs=in_specs,
        out_specs=pl.BlockSpec((tq, A_HEADS * HD), lambda b, i: (b * nq + i, 0)),
        out_shape=jax.ShapeDtypeStruct(q.shape, BF),
        compiler_params=_params(("arbitrary", "arbitrary")),
        name="gqa_attention",
    )(*args)


GLA_Q = 0
GLA_K = B_HEADS * B_DK
GLA_V = 2 * B_HEADS * B_DK
GLA_R = GLA_V + B_HEADS * B_DV
GLA_W = GLA_R + B_HEADS * B_DV
GLA_GROUP = 8


def _gla_body(gc_ref, gl_ref, zc_ref, zl_ref, w2f_ref, w2b_ref, bf_ref, bb_ref, gg_ref,
              oc_ref, ol_ref, ofc_ref, ofl_ref, st_ref):
    nk = B_HEADS * B_DK
    ch = B_CHUNK
    row = lax.broadcasted_iota(jnp.int32, (ch, ch), 0)
    col = lax.broadcasted_iota(jnp.int32, (ch, ch), 1)
    lane = lax.broadcasted_iota(jnp.int32, (ch, nk), 1)
    head_masks = [(lane // B_DK) == h for h in range(B_HEADS)]

    def group_out(g_ref, z_ref, r0s, fwd, s_t):
        w2 = (w2f_ref if fwd else w2b_ref)[...]
        bias = (bf_ref if fwd else bb_ref)[...]
        keep = (row >= col) if fwd else (row <= col)
        tri = jnp.where(keep, 1.0, 0.0).astype(BF)
        blks = [g_ref[pl.ds(r0, ch), :] for r0 in r0s]
        pres = [_dot(z_ref[pl.ds(r0, ch), :], w2) + bias for r0 in r0s]
        las = [-(jnp.maximum(-p, 0.0) + jnp.log(1.0 + jnp.exp(-jnp.abs(p)))) * (1.0 / B_TAU) for p in pres]
        bcums = [_split_dot(tri, la) for la in las]
        qms, kis, kss, decs = [], [], [], []
        for blk, bcum in zip(blks, bcums):
            q = blk[:, GLA_Q:GLA_Q + nk].astype(F32)
            k = blk[:, GLA_K:GLA_K + nk].astype(F32)
            b_last = bcum[ch - 1:ch, :] if fwd else bcum[0:1, :]
            q_dec = q * jnp.exp(bcum)
            k_state = k * jnp.exp(b_last - bcum)
            qms.append([jnp.where(m, q_dec, 0.0).astype(BF) for m in head_masks])
            kis.append((k * jnp.exp(-bcum)).astype(BF))
            kss.append([jnp.where(m, k_state, 0.0).astype(BF) for m in head_masks])
            decs.append(jnp.exp(b_last))
        vs = [[blk[:, GLA_V + B_DV * h:GLA_V + B_DV * (h + 1)] for h in range(B_HEADS)] for blk in blks]
        atts = [[jnp.where(keep, _dot_nt(qm, ki), 0.0).astype(BF) for qm in qmu] for qmu, ki in zip(qms, kis)]
        o_intra = [[_dot(a, v) for a, v in zip(au, vu)] for au, vu in zip(atts, vs)]
        d_ss = []
        for vu, ksu in zip(vs, kss):
            d_s = _dot_tn(vu[0], ksu[0])
            for h in range(1, B_HEADS):
                d_s = d_s + _dot_tn(vu[h], ksu[h])
            d_ss.append(d_s)
        outs = []
        for u in range(len(r0s)):
            s_b = s_t.astype(BF)
            outs.append(jnp.concatenate(
                [o_intra[u][h] + _dot_nt(qms[u][h], s_b) for h in range(B_HEADS)], axis=1))
            s_t = decs[u] * s_t + d_ss[u]
        return outs, blks, s_t

    def fwd_pass(g_ref, z_ref, of_ref, n):
        grp = min(GLA_GROUP, n)
        assert n % grp == 0

        def body(i, carry):
            r0s = [pl.multiple_of((i * grp + u) * ch, ch) for u in range(grp)]
            outs, _, s_t = group_out(g_ref, z_ref, r0s, True, st_ref[...])
            for r0, o in zip(r0s, outs):
                of_ref[pl.ds(r0, ch), :] = o
            st_ref[...] = s_t
            return carry
        lax.fori_loop(0, n // grp, body, 0)

    def bwd_pass(g_ref, z_ref, of_ref, o_ref, n):
        grp = min(GLA_GROUP, n)
        assert n % grp == 0

        def body(i, carry):
            r0s = [pl.multiple_of((n - 1 - (i * grp + u)) * ch, ch) for u in range(grp)]
            outs, blks, s_t = group_out(g_ref, z_ref, r0s, False, st_ref[...])
            for r0, o, blk in zip(r0s, outs, blks):
                o = o + of_ref[pl.ds(r0, ch), :]
                parts = []
                for h in range(B_HEADS):
                    oh = o[:, B_DV * h:B_DV * (h + 1)]
                    ms = jnp.mean(oh * oh, axis=-1, keepdims=True)
                    parts.append(oh * lax.rsqrt(ms + EPS))
                y = jnp.concatenate(parts, axis=1) * gg_ref[...]
                r = blk[:, GLA_R:GLA_R + B_HEADS * B_DV].astype(F32)
                o_ref[pl.ds(r0, ch), :] = (y * _silu(r)).astype(BF)
            st_ref[...] = s_t
            return carry
        lax.fori_loop(0, n // grp, body, 0)

    nc = gc_ref.shape[0] // ch
    nl = gl_ref.shape[0] // ch
    st_ref[...] = jnp.zeros_like(st_ref)
    fwd_pass(gc_ref, zc_ref, ofc_ref, nc)
    fwd_pass(gl_ref, zl_ref, ofl_ref, nl)
    st_ref[...] = jnp.zeros_like(st_ref)
    bwd_pass(gc_ref, zc_ref, ofc_ref, oc_ref, nc)
    bwd_pass(gl_ref, zl_ref, ofl_ref, ol_ref, nl)


def _gla(g_ctx, g_lat, z_ctx, z_lat, w2f, w2b, b_f, b_b, gg, bsz):
    nc = g_ctx.shape[0] // bsz
    sl = g_lat.shape[0] // bsz
    nk = B_HEADS * B_DK
    nv = B_HEADS * B_DV
    full = lambda shape: pl.BlockSpec(shape, lambda b: (0, 0))
    return pl.pallas_call(
        _gla_body,
        grid=(bsz,),
        in_specs=[pl.BlockSpec((nc, GLA_W), lambda b: (b, 0)),
                  pl.BlockSpec((sl, GLA_W), lambda b: (b, 0)),
                  pl.BlockSpec((nc, LANES), lambda b: (b, 0)),
                  pl.BlockSpec((sl, LANES), lambda b: (b, 0)),
                  full((LANES, nk)), full((LANES, nk)), full((1, nk)), full((1, nk)), full((1, nv))],
        out_specs=[pl.BlockSpec((nc, nv), lambda b: (b, 0)),
                   pl.BlockSpec((sl, nv), lambda b: (b, 0))],
        out_shape=[jax.ShapeDtypeStruct((g_ctx.shape[0], nv), BF),
                   jax.ShapeDtypeStruct((g_lat.shape[0], nv), BF)],
        scratch_shapes=[pltpu.VMEM((nc, nv), F32), pltpu.VMEM((sl, nv), F32),
                        pltpu.VMEM((B_DV, nk), F32)],
        compiler_params=_params(("arbitrary",)),
        name="gla_bidir",
    )(g_ctx, g_lat, z_ctx, z_lat, w2f, w2b, b_f, b_b, gg)


def _mixer_residual(x_ref, m_ref, pair_refs):
    acc = _dot(pair_refs[0][...], pair_refs[1][...])
    for t in range(1, len(pair_refs) // 2):
        acc = acc + _dot(pair_refs[2 * t][...], pair_refs[2 * t + 1][...])
    return x_ref[...] + m_ref[0, 2:3, :] * acc


def _pair_specs(pairs, tm, index):
    in_specs, args = [], []
    for a, w in pairs:
        in_specs += [pl.BlockSpec((tm, a.shape[1]), index), pl.BlockSpec(w.shape, lambda *_: (0, 0))]
        args += [a, w]
    return in_specs, args


def _ffn_body(x_ref, g_ref, m_ref, *rest, n_pairs):
    pair_refs = rest[:2 * n_pairs]
    wg_ref, wu_ref, wd_ref, o_ref, xn_ref, h_ref, acc_ref = rest[2 * n_pairs:]
    j = pl.program_id(1)

    @pl.when(j == 0)
    def _():
        xn = _mixer_residual(x_ref, m_ref, pair_refs)
        xn_ref[...] = xn
        h_ref[...] = _norm_mod(xn, g_ref[...], m_ref[0, 3:4, :], m_ref[0, 4:5, :]).astype(BF)
        acc_ref[...] = jnp.zeros_like(acc_ref)

    h = h_ref[...]
    a = (_silu(_dot(h, wg_ref[...])) * _dot(h, wu_ref[...])).astype(BF)
    acc_ref[...] += _dot(a, wd_ref[...])

    @pl.when(j == pl.num_programs(1) - 1)
    def _():
        o_ref[...] = xn_ref[...] + m_ref[0, 5:6, :] * acc_ref[...]


def _ffn(x2, g, mpack, mod_of_tile, pairs, wg, wu, wd, tm, tf):
    rows = x2.shape[0]
    ff = wg.shape[1]
    pair_specs, pair_args = _pair_specs(pairs, tm, lambda i, j: (i, 0))
    return pl.pallas_call(
        functools.partial(_ffn_body, n_pairs=len(pairs)),
        grid=(rows // tm, ff // tf),
        in_specs=[pl.BlockSpec((tm, D), lambda i, j: (i, 0)),
                  pl.BlockSpec((1, D), lambda i, j: (0, 0)),
                  pl.BlockSpec((1, MOD_ROWS, D), lambda i, j: (mod_of_tile(i), 0, 0)),
                  *pair_specs,
                  pl.BlockSpec((D, tf), lambda i, j: (0, j)),
                  pl.BlockSpec((D, tf), lambda i, j: (0, j)),
                  pl.BlockSpec((tf, D), lambda i, j: (j, 0))],
        out_specs=pl.BlockSpec((tm, D), lambda i, j: (i, 0)),
        out_shape=jax.ShapeDtypeStruct((rows, D), F32),
        scratch_shapes=[pltpu.VMEM((tm, D), F32), pltpu.VMEM((tm, D), BF), pltpu.VMEM((tm, D), F32)],
        compiler_params=_params(("arbitrary", "arbitrary")),
        name="ffn_swiglu",
    )(x2, g.reshape(1, D), mpack, *pair_args, wg, wu, wd)


NA_QROWS = 4
NA_KROWS = 12
NA_TQ = NA_QROWS * GRID_W
NA_HG = 8
NA_W = NA_HG * HD


def _na_body(q_ref, k0_ref, k1_ref, k2_ref, v0_ref, v1_ref, v2_ref, kc_ref, vc_ref, b_ref, o_ref):
    pair = LANES // HD
    lane = lax.broadcasted_iota(jnp.int32, (NA_TQ, LANES), 1)
    own = [(lane // HD) == half for half in range(pair)]

    def scores(hh):
        tile = slice(LANES * (hh // pair), LANES * (hh // pair + 1))
        q = jnp.where(own[hh % pair], q_ref[:, tile], 0.0)
        s_loc = jnp.concatenate([_dot_nt(q, kr[:, tile]) for kr in (k0_ref, k1_ref, k2_ref)], axis=1)
        bias = b_ref[0, :, hh].reshape(NA_TQ, NA_KROWS * GRID_W)
        return s_loc + bias, _dot_nt(q, kc_ref[:, tile])

    nxt = scores(0)
    outs = []
    for hh in range(NA_HG):
        s_loc, s_ctx = nxt
        if hh + 1 < NA_HG:
            nxt = scores(hh + 1)
        vcols = slice(LANES * hh, LANES * (hh + 1))
        m = jnp.maximum(jnp.max(s_loc, axis=1, keepdims=True), jnp.max(s_ctx, axis=1, keepdims=True))
        p_loc = jnp.exp2(s_loc - m).astype(BF)
        acc = _dot(jnp.exp2(s_ctx - m).astype(BF), vc_ref[:, vcols])
        for t, vr in enumerate((v0_ref, v1_ref, v2_ref)):
            acc = acc + _dot(p_loc[:, t * NA_TQ:(t + 1) * NA_TQ], vr[:, vcols])
        den_lane = HD * (1 - hh % pair)
        outs.append(acc / acc[:, den_lane:den_lane + 1])
        if len(outs) == pair:
            t0 = hh // pair
            o_ref[:, LANES * t0:LANES * (t0 + 1)] = jnp.where(own[0], outs[0], outs[1]).astype(BF)
            outs = []


def _na_bias(rpb, rows):
    heads, n_dr, n_dc = rpb.shape
    nblk = rows // NA_QROWS
    n_off = n_dr + 2 * NA_KROWS
    rpad = jnp.pad(rpb.astype(F32) * LOG2E, ((0, 0), (NA_KROWS, NA_KROWS), (0, 0)), constant_values=NEG_INF)
    rvals = jnp.repeat(rpad.transpose(0, 2, 1), GRID_W, axis=2)
    cuts = []
    for kind, j in enumerate((0, min(1, nblk - 1), nblk - 1)):
        ws = NA_QROWS * int(np.clip(j - 1, 0, nblk - 3))
        for qr in range(NA_QROWS):
            r = NA_QROWS * j + qr
            rs = int(np.clip(r - NA_ROWS // 2, 0, rows - NA_ROWS))
            cuts.append((kind, qr, NA_KROWS + ws - r + NA_ROWS - 1, max(rs - ws, 0), min(rs + NA_ROWS - ws, NA_KROWS)))
    width = NA_KROWS * GRID_W

    def body(r_ref, o_ref):
        shape = (GRID_W, n_off * GRID_W)
        qc = lax.broadcasted_iota(jnp.int32, shape, 0)
        kc = lax.broadcasted_iota(jnp.int32, shape, 1) % GRID_W
        delta = kc - qc + NA_COLS - 1
        cs = jnp.clip(qc - NA_COLS // 2, 0, GRID_W - NA_COLS)
        toe = jnp.full(shape, NEG_INF, F32)
        for d in range(n_dc):
            toe = jnp.where(delta == d, r_ref[0, d:d + 1, :], toe)
        toe = jnp.where((kc >= cs) & (kc < cs + NA_COLS), toe, NEG_INF)
        key_row = lax.broadcasted_iota(jnp.int32, (GRID_W, width), 1) // GRID_W
        for kind, qr, d0, lo, hi in cuts:
            blk = toe[:, d0 * GRID_W:d0 * GRID_W + width]
            o_ref[kind, qr, 0] = jnp.where((key_row >= lo) & (key_row < hi), blk, NEG_INF)

    return pl.pallas_call(
        body,
        grid=(heads,),
        in_specs=[pl.BlockSpec((1, n_dc, n_off * GRID_W), lambda h: (h, 0, 0))],
        out_specs=pl.BlockSpec((3, NA_QROWS, 1, GRID_W, width), lambda h: (0, 0, h, 0, 0)),
        out_shape=jax.ShapeDtypeStruct((3, NA_QROWS, heads, GRID_W, width), F32),
        compiler_params=_params(("arbitrary",)),
        name="natten_bias",
    )(rvals)


def _natten(q, k, v, k_ctx, v_ctx, bias, bsz):
    s = q.shape[0] // bsz
    nc = k_ctx.shape[0] // bsz
    nblk = s // NA_TQ
    ngrp = C_HEADS // NA_HG
    v_w = NA_HG * LANES

    def kv_spec(t, width):
        return pl.BlockSpec((NA_TQ, width),
                            lambda j, g, b: (b * nblk + jnp.clip(j - 1, 0, nblk - 3) + t, g))

    def kind(j):
        return jnp.where(j == 0, 0, jnp.where(j == nblk - 1, 2, 1))

    return pl.pallas_call(
        _na_body,
        grid=(nblk, ngrp, bsz),
        in_specs=[pl.BlockSpec((NA_TQ, NA_W), lambda j, g, b: (b * nblk + j, g)),
                  kv_spec(0, NA_W), kv_spec(1, NA_W), kv_spec(2, NA_W),
                  kv_spec(0, v_w), kv_spec(1, v_w), kv_spec(2, v_w),
                  pl.BlockSpec((nc, NA_W), lambda j, g, b: (b, g)),
                  pl.BlockSpec((nc, v_w), lambda j, g, b: (b, g)),
                  pl.BlockSpec((1, NA_QROWS, NA_HG, GRID_W, NA_KROWS * GRID_W),
                               lambda j, g, b: (kind(j), 0, g, 0, 0))],
        out_specs=pl.BlockSpec((NA_TQ, NA_W), lambda j, g, b: (b * nblk + j, g)),
        out_shape=jax.ShapeDtypeStruct(q.shape, BF),
        compiler_params=_params(("arbitrary", "arbitrary", "arbitrary")),
        name="natten",
    )(q, k, k, k, v, v, v, k_ctx, v_ctx, bias)


MOE_TM = 512
MOE_TF = 1792
META_E1, META_E2, META_R1, META_R2, META_G1, META_G2 = range(6)
ROW_TILE = D // LANES


def _to_token_tiles(dst_ref, value):
    n = value.shape[0]
    for s in range(ROW_TILE):
        dst_ref[pl.ds(s, n, stride=ROW_TILE), :] = value[:, LANES * s:LANES * (s + 1)]


def _from_token_tiles(src_ref, n):
    return jnp.concatenate([src_ref[pl.ds(s, n, stride=ROW_TILE), :] for s in range(ROW_TILE)], axis=1)


def _route_body(x_ref, g_ref, m_ref, *rest, n_pairs):
    pair_refs = rest[:2 * n_pairs]
    wr_ref, xn_ref, h_ref, meta_ref, rec_ref, cnt_ref, tri_ref, carry_ref = rest[2 * n_pairs:]
    tm = x_ref.shape[0]

    @pl.when(pl.program_id(0) == 0)
    def _():
        row = lax.broadcasted_iota(jnp.int32, (tm, tm), 0)
        col = lax.broadcasted_iota(jnp.int32, (tm, tm), 1)
        tri_ref[...] = jnp.where(row >= col, 1.0, 0.0).astype(BF)
        carry_ref[...] = jnp.zeros_like(carry_ref)

    xn = _mixer_residual(x_ref, m_ref, pair_refs)
    xn_ref[...] = xn
    h = _norm_mod(xn, g_ref[...], m_ref[0, 3:4, :], m_ref[0, 4:5, :])
    _to_token_tiles(h_ref, h)
    h_hi = h.astype(BF)
    h_lo = (h - h_hi.astype(F32)).astype(BF)
    w = wr_ref[...]
    w_hi = w.astype(BF).astype(F32)
    w_cat = (w_hi + pltpu.roll(w - w_hi, N_EXPERTS, 1)).astype(BF)
    terms = _dot(h_hi, w_cat) + _dot(h_lo, w_cat)
    logits = terms + pltpu.roll(terms, LANES - N_EXPERTS, 1)
    lane = lax.broadcasted_iota(jnp.int32, logits.shape, 1)
    lg = jnp.where(lane < N_EXPERTS, logits, NEG_INF)
    m1 = jnp.max(lg, axis=1, keepdims=True)
    i1 = jnp.min(jnp.where(lg == m1, lane, LANES), axis=1, keepdims=True)
    lg2 = jnp.where(lane == i1, NEG_INF, lg)
    m2 = jnp.max(lg2, axis=1, keepdims=True)
    i2 = jnp.min(jnp.where(lg2 == m2, lane, LANES), axis=1, keepdims=True)
    e = jnp.exp(m2 - m1)
    g1 = 1.0 / (1.0 + e)
    g2 = e / (1.0 + e)
    oh = jnp.where(lane == i1, 1.0, 0.0) + jnp.where(lane == i2, 1.0, 0.0)
    cum = _dot(tri_ref[...], oh.astype(BF))
    base = carry_ref[0:1, :] + cum - oh
    r1 = jnp.sum(jnp.where(lane == i1, base, 0.0), axis=1, keepdims=True)
    r2 = jnp.sum(jnp.where(lane == i2, base, 0.0), axis=1, keepdims=True)
    carry_ref[...] = carry_ref[...] + cum[tm - 1:tm, :]
    cnt_ref[...] = carry_ref[...]
    meta = jnp.zeros(logits.shape, F32)
    for ln, val in ((META_E1, i1.astype(F32)), (META_E2, i2.astype(F32)), (META_R1, r1), (META_R2, r2),
                    (META_G1, g1), (META_G2, g2)):
        meta = jnp.where(lane == ln, val, meta)
    meta_ref[...] = meta
    rec_ref[...] = meta.T[:rec_ref.shape[0], :]


def _route(x2, g, mpack, mod_of_tile, pairs, wr, tm):
    rows = x2.shape[0]
    pair_specs, pair_args = _pair_specs(pairs, tm, lambda i: (i, 0))
    return pl.pallas_call(
        functools.partial(_route_body, n_pairs=len(pairs)),
        grid=(rows // tm,),
        in_specs=[pl.BlockSpec((tm, D), lambda i: (i, 0)),
                  pl.BlockSpec((1, D), lambda i: (0, 0)),
                  pl.BlockSpec((1, MOD_ROWS, D), lambda i: (mod_of_tile(i), 0, 0)),
                  *pair_specs,
                  pl.BlockSpec((D, LANES), lambda i: (0, 0))],
        out_specs=[pl.BlockSpec((tm, D), lambda i: (i, 0)),
                   pl.BlockSpec((tm * ROW_TILE, LANES), lambda i: (i, 0)),
                   pl.BlockSpec((tm, LANES), lambda i: (i, 0)),
                   pl.BlockSpec((8, tm), lambda i: (0, i)),
                   pl.BlockSpec((8, LANES), lambda i: (0, 0))],
        out_shape=[jax.ShapeDtypeStruct((rows, D), F32),
                   jax.ShapeDtypeStruct((rows * ROW_TILE, LANES), F32),
                   jax.ShapeDtypeStruct((rows, LANES), F32),
                   jax.ShapeDtypeStruct((8, rows), F32),
                   jax.ShapeDtypeStruct((8, LANES), F32)],
        scratch_shapes=[pltpu.VMEM((tm, tm), BF), pltpu.VMEM((8, LANES), F32)],
        compiler_params=_params(("arbitrary",)),
        name="moe_route",
    )(x2, g.reshape(1, D), mpack, *pair_args, wr)


def _token_copy(src_ref, src_tok, dst_ref, dst_tok, sem):
    src = src_ref.at[pl.ds(pl.multiple_of(src_tok * ROW_TILE, ROW_TILE), ROW_TILE), :]
    dst = dst_ref.at[pl.ds(pl.multiple_of(dst_tok * ROW_TILE, ROW_TILE), ROW_TILE), :]
    return pltpu.make_async_copy(src, dst, sem)


def _experts_body(te_ref, nu_ref, cur_ref, nxt_ref, dprev_ref, h_hbm, wg_ref, wu_ref, wd_ref, y_hbm,
                  xbuf, h_ref, acc_ref, ybuf, gsem, ssem, *, n_steps):
    assert n_steps >= 2
    i = pl.program_id(0)
    j = pl.program_id(1)
    used = i < nu_ref[0]
    slot = i % 2
    other = 1 - slot

    def gather_row(idx_ref, r, sl):
        _token_copy(h_hbm, idx_ref[0, 0, r], xbuf.at[sl], r, gsem.at[sl]).start()

    def issue_gathers():
        for r in range(MOE_TM):
            gather_row(nxt_ref, r, other)

    def issue_scatters():
        for r in range(MOE_TM):
            _token_copy(ybuf.at[other], r, y_hbm, dprev_ref[0, 0, r], ssem.at[other]).start()

    def wait_rows(buf, sem):
        pltpu.make_async_copy(buf, buf, sem).wait()

    def swiglu_step():
        h = h_ref[...]
        a = (_silu(_dot(h, wg_ref[0])) * _dot(h, wu_ref[0])).astype(BF)
        return _dot(a, wd_ref[0])

    @pl.when((j == 0) & (i == 0))
    def _():
        ybuf[...] = jnp.zeros_like(ybuf)
        n_spare = MOE_TM * ROW_TILE
        for sl in range(2):
            spare = pltpu.make_async_copy(
                ybuf.at[sl], y_hbm.at[pl.ds(y_hbm.shape[0] - (2 - sl) * n_spare, n_spare), :], ssem.at[sl])
            spare.start()
            spare.wait()

        def body(r, c):
            gather_row(cur_ref, r, 0)
            return c
        lax.fori_loop(0, MOE_TM, body, 0, unroll=8)

    @pl.when(j == 0)
    def _():
        wait_rows(xbuf.at[slot], gsem.at[slot])

    @pl.when(used & (j == 0))
    def _():
        h_ref[...] = _from_token_tiles(xbuf.at[slot], MOE_TM).astype(BF)
        acc_ref[...] = swiglu_step()
        issue_gathers()

    if n_steps > 2:
        @pl.when(used & (j > 0) & (j < n_steps - 1))
        def _():
            acc_ref[...] += swiglu_step()

    @pl.when(used & (j == n_steps - 1))
    def _():
        acc_ref[...] += swiglu_step()
        issue_scatters()

    @pl.when(jnp.logical_not(used) & (j == 0))
    def _():
        issue_gathers()

    @pl.when(jnp.logical_not(used) & (j == n_steps - 1))
    def _():
        issue_scatters()

    @pl.when(j == n_steps - 1)
    def _():
        @pl.when(i > 0)
        def _():
            wait_rows(ybuf.at[slot], ssem.at[slot])

        @pl.when(used)
        def _():
            _to_token_tiles(ybuf.at[slot], acc_ref[...])

        @pl.when(i == pl.num_programs(0) - 1)
        def _():
            wait_rows(ybuf.at[other], ssem.at[other])
            wait_rows(xbuf.at[other], gsem.at[other])


def _experts(h_tiles, tok_of_slot, dest_of_slot, tile_expert, n_used, wg, wu, wd, out_tokens):
    n_tiles = tile_expert.shape[0]
    ff = wg.shape[2]
    nf = ff // MOE_TF
    spare_rows = out_tokens - MOE_TM + jnp.arange(MOE_TM, dtype=jnp.int32)
    dest_prev = jnp.concatenate([spare_rows, dest_of_slot[:-MOE_TM]])

    def jj(i, j, nu):
        return jnp.where(i < nu[0], j, nf - 1)

    def smem_rows(index):
        return pl.BlockSpec((1, 1, MOE_TM), lambda i, j, te, nu: (index(i), 0, 0), memory_space=pltpu.SMEM)

    grid_spec = pltpu.PrefetchScalarGridSpec(
        num_scalar_prefetch=2,
        grid=(n_tiles, nf),
        in_specs=[smem_rows(lambda i: i),
                  smem_rows(lambda i: jnp.minimum(i + 1, n_tiles - 1)),
                  smem_rows(lambda i: i),
                  pl.BlockSpec(memory_space=pl.ANY),
                  pl.BlockSpec((1, D, MOE_TF), lambda i, j, te, nu: (te[i], 0, jj(i, j, nu))),
                  pl.BlockSpec((1, D, MOE_TF), lambda i, j, te, nu: (te[i], 0, jj(i, j, nu))),
                  pl.BlockSpec((1, MOE_TF, D), lambda i, j, te, nu: (te[i], jj(i, j, nu), 0))],
        out_specs=pl.BlockSpec(memory_space=pl.ANY),
        scratch_shapes=[pltpu.VMEM((2, MOE_TM * ROW_TILE, LANES), F32),
                        pltpu.VMEM((MOE_TM, D), BF),
                        pltpu.VMEM((MOE_TM, D), F32),
                        pltpu.VMEM((2, MOE_TM * ROW_TILE, LANES), F32),
                        pltpu.SemaphoreType.DMA((2,)),
                        pltpu.SemaphoreType.DMA((2,))])
    tok3 = tok_of_slot.reshape(n_tiles, 1, MOE_TM)
    return pl.pallas_call(
        functools.partial(_experts_body, n_steps=nf),
        grid_spec=grid_spec,
        out_shape=jax.ShapeDtypeStruct((out_tokens * ROW_TILE, LANES), F32),
        compiler_params=_params(("arbitrary", "arbitrary")),
        name="moe_experts",
    )(tile_expert, n_used, tok3, tok3, dest_prev.reshape(n_tiles, 1, MOE_TM), h_tiles, wg, wu, wd)


def _combine_body(y1_ref, y2_ref, x_ref, m_ref, meta_ref, gf_ref, o_ref):
    n = x_ref.shape[0]
    meta = meta_ref[...]
    lane = lax.broadcasted_iota(jnp.int32, meta.shape, 1)
    g1 = jnp.sum(jnp.where(lane == META_G1, meta, 0.0), axis=1, keepdims=True)
    g2 = jnp.sum(jnp.where(lane == META_G2, meta, 0.0), axis=1, keepdims=True)
    moe = g1 * _from_token_tiles(y1_ref, n) + g2 * _from_token_tiles(y2_ref, n)
    y = x_ref[...] + m_ref[0, 5:6, :] * moe
    ms = jnp.mean(y * y, axis=-1, keepdims=True)
    o_ref[...] = y * lax.rsqrt(ms + EPS) * gf_ref[...]


def _combine(x2, mpack, mod_of_row, meta, y_tiles, g_final, tm):
    rows = x2.shape[0]
    nt = rows // tm
    return pl.pallas_call(
        _combine_body,
        grid=(nt,),
        in_specs=[pl.BlockSpec((tm * ROW_TILE, LANES), lambda i: (i, 0)),
                  pl.BlockSpec((tm * ROW_TILE, LANES), lambda i: (i + nt, 0)),
                  pl.BlockSpec((tm, D), lambda i: (i, 0)),
                  pl.BlockSpec((1, MOD_ROWS, D), lambda i: (mod_of_row(i * tm), 0, 0)),
                  pl.BlockSpec((tm, LANES), lambda i: (i, 0)),
                  pl.BlockSpec((1, D), lambda i: (0, 0))],
        out_specs=pl.BlockSpec((tm, D), lambda i: (i, 0)),
        out_shape=jax.ShapeDtypeStruct((rows, D), F32),
        compiler_params=_params(("arbitrary",)),
        name="moe_combine",
    )(y_tiles, y_tiles, x2, mpack, meta, g_final.reshape(1, D))


def _moe(x2, g, mpack, mod_of_tile, mod_of_row, pairs, wr, wg, wu, wd, g_final, tm):
    rows = x2.shape[0]
    ne = wg.shape[0]
    x2, h_tiles, meta, rec, cnt = _route(x2, g, mpack, mod_of_tile, pairs, wr, tm)
    counts = cnt[0, :ne].astype(jnp.int32)
    tiles_e = (counts + MOE_TM - 1) // MOE_TM
    tile_end = jnp.cumsum(tiles_e)
    group_off = (tile_end - tiles_e) * MOE_TM
    n_tiles = 2 * rows // MOE_TM + ne
    n_slots = n_tiles * MOE_TM
    tile_expert = jnp.minimum(
        jnp.sum(jnp.arange(n_tiles)[:, None] >= tile_end[None, :], axis=1), ne - 1).astype(jnp.int32)
    n_used = tile_end[-1:].astype(jnp.int32)
    eids = jnp.arange(ne)

    def slot_of(lane_e, lane_r):
        e = rec[lane_e].astype(jnp.int32)
        off = jnp.sum(jnp.where(e[None, :] == eids[:, None], group_off[:, None], 0), axis=0)
        return off + rec[lane_r].astype(jnp.int32)

    tok = jnp.arange(rows, dtype=jnp.int32)
    dest = jnp.full((n_slots,), -1, jnp.int32).at[
        jnp.concatenate([slot_of(META_E1, META_R1), slot_of(META_E2, META_R2)])].set(
        jnp.concatenate([tok, rows + tok]), unique_indices=True, mode="promise_in_bounds")
    pad = dest < 0
    dest_of_slot = jnp.where(pad, 2 * rows + jnp.arange(n_slots, dtype=jnp.int32) % (2 * MOE_TM), dest)
    tok_of_slot = jnp.where(pad, 0, jnp.where(dest >= rows, dest - rows, dest))
    y_tiles = _experts(h_tiles, tok_of_slot, dest_of_slot, tile_expert, n_used, wg, wu, wd, 2 * rows + 2 * MOE_TM)
    return _combine(x2, mpack, mod_of_row, meta, y_tiles, g_final, MOE_TM)


def _rope_tables(seq_len):
    t = np.arange(seq_len)
    row = (t // GRID_W).astype(np.float32)
    col = (t % GRID_W).astype(np.float32)
    n_axis = HD // 4
    inv = jnp.power(ROPE_THETA, -jnp.arange(n_axis, dtype=F32) / n_axis)
    ang = jnp.concatenate([jnp.asarray(row)[:, None] * inv, jnp.asarray(col)[:, None] * inv], axis=-1)
    cos, sin = jnp.cos(ang), jnp.sin(ang)
    ct = jnp.tile(jnp.concatenate([cos, cos], axis=-1), (1, LANES // HD))
    st = jnp.tile(jnp.concatenate([-sin, sin], axis=-1), (1, LANES // HD))
    return ct, st


def kernel(x, c, ctx, c_ctx, w_mod, b_mod, g_norm1, g_norm2, w_in_ab, g_q, g_k, w_a2_f, b_a_f, w_a2_b,
           b_a_b, g_gla, w_out_ab, w_ff_gate, w_ff_up, w_ff_down, w_in_c, rpb_c, w_out_c, w_router,
           w_moe_gate, w_moe_up, w_moe_down, g_final):
    bsz, seq, _ = x.shape
    nc = ctx.shape[1]
    depth = w_mod.shape[0]
    assert depth == 2 and seq % GRID_W == 0

    cond_rows = -(-(bsz + 1) // 8) * 8
    cond = jnp.zeros((cond_rows, D), F32).at[:bsz].set(c).at[bsz].set(c_ctx)
    mods = _adaln(cond, w_mod, b_mod).reshape(depth, cond_rows, N_MOD, D)
    mods = jnp.pad(mods, ((0, 0), (0, 0), (0, MOD_ROWS - N_MOD), (0, 0)))

    xl = x.reshape(bsz * seq, D)
    xc = ctx.reshape(bsz * nc, D)
    tm_l = min(1024, seq)
    tm_c = min(1024, bsz * nc)
    tpb = seq // tm_l
    lat_mod = lambda i: i // tpb
    ctx_mod = lambda i: bsz

    mp = mods[0]
    w_in = w_in_ab[0]
    perm = np.concatenate([np.arange(0, HD, 2), np.arange(1, HD, 2)])
    nq, nkv = A_HEADS * HD, A_KV * HD
    o_q, o_k, o_v = 0, nq, nq + nkv
    o_bq = o_v + nkv
    o_bk = o_bq + B_HEADS * B_DK
    o_bv = o_bk + B_HEADS * B_DK
    o_br = o_bv + B_HEADS * B_DV
    o_zf = o_br + B_HEADS * B_DV
    o_end = o_zf + 2 * B_RANK
    assert A_KV * HD == LANES
    q_order = np.arange(A_HEADS).reshape(A_KV, A_GROUP).T.reshape(-1)
    wq = w_in[:, o_q:o_k].reshape(D, A_HEADS, HD)[:, q_order][:, :, perm].reshape(D, nq)
    wk = w_in[:, o_k:o_v].reshape(D, A_KV, HD)[:, :, perm].reshape(D, nkv)
    wz = jnp.pad(w_in[:, o_zf:o_end], ((0, 0), (0, LANES - 2 * B_RANK)))
    w0 = jnp.concatenate([wq, wk, w_in[:, o_v:o_bq], w_in[:, o_bq:o_bk] * (B_DK ** -0.5),
                          w_in[:, o_bk:o_zf], wz], axis=1).astype(BF)
    widths0 = (nq + nkv, nkv, GLA_W, LANES)
    outw0 = (nq, nkv, 2 * nkv, GLA_W, LANES)
    gqk = jnp.concatenate([jnp.tile(g_q[0][perm] * (HD ** -0.5 * LOG2E), A_HEADS),
                           jnp.tile(g_k[0][perm], A_KV)]).reshape(1, nq + nkv)
    bd = jnp.asarray(np.tile(np.kron(np.eye(LANES // HD), np.ones((HD, HD))), (2, 1)), BF)
    ct, st = _rope_tables(seq)
    ct_c = jnp.ones((tm_c, LANES), F32)
    st_c = jnp.zeros((tm_c, LANES), F32)

    kinds0 = ("qk", "values", "plain", "plain")
    ql, kl, vl, gl, zl = _proj(xl, g_norm1[0], mp, lat_mod, w0, widths0, kinds0, outw0, tm_l, (ct, st, gqk, bd))
    qc, kc, vc, gc, zc = _proj(xc, g_norm1[0], mp, ctx_mod, w0, widths0, kinds0, outw0, tm_c,
                               (ct_c, st_c, gqk, bd))

    oa_l = _gqa(ql, kc, vc, kl, vl, bsz, 256)
    oa_c = _gqa(qc, kc, vc, None, None, bsz, min(128, nc))

    w2f = jnp.zeros((LANES, B_HEADS * B_DK), F32).at[:B_RANK].set(w_a2_f[0]).astype(BF)
    w2b = jnp.zeros((LANES, B_HEADS * B_DK), F32).at[B_RANK:2 * B_RANK].set(w_a2_b[0]).astype(BF)
    ob_c, ob_l = _gla(gc, gl, zc, zl, w2f, w2b, b_a_f[0].reshape(1, -1), b_a_b[0].reshape(1, -1),
                      jnp.tile(g_gla[0], B_HEADS).reshape(1, -1), bsz)

    wo = w_out_ab[0].astype(BF)
    wo_a = wo[:nq].reshape(A_HEADS, HD, D)[q_order].reshape(nq, D)
    wo_b = wo[nq:]
    wg, wu, wd = w_ff_gate[0].astype(BF), w_ff_up[0].astype(BF), w_ff_down[0].astype(BF)
    xl = _ffn(xl, g_norm2[0], mp, lat_mod, [(oa_l, wo_a), (ob_l, wo_b)], wg, wu, wd, tm_l, 512)
    xc = _ffn(xc, g_norm2[0], mp, ctx_mod, [(oa_c, wo_a), (ob_c, wo_b)], wg, wu, wd, tm_c, 512)

    mp = mods[1]
    cw = C_HEADS * HD
    w1 = w_in_c[0].astype(BF)
    q1, k1, v1 = _proj(xl, g_norm1[1], mp, lat_mod, w1, (cw, cw, cw), ("plain", "plain", "values"),
                       (cw, cw, 2 * cw), tm_l, scales=(HD ** -0.5 * LOG2E, 1.0, 1.0))
    k1c, v1c = _proj(xc, g_norm1[1], mp, ctx_mod, w1, (cw, cw), ("plain", "values"), (cw, 2 * cw), tm_c,
                     col0=cw)

    bias = _na_bias(rpb_c[0], seq // GRID_W)
    o1 = _natten(q1, k1, v1, k1c, v1c, bias, bsz)

    wr = jnp.pad(w_router[0], ((0, 0), (0, LANES - N_EXPERTS)))
    out = _moe(xl, g_norm2[1], mp, lat_mod, lambda r: r // seq, [(o1, w_out_c[0].astype(BF))], wr,
               w_moe_gate[0].astype(BF), w_moe_up[0].astype(BF), w_moe_down[0].astype(BF), g_final, tm_l)
    return out.reshape(bsz, seq, D)
```

```python
import functools

import numpy as np
import jax
import jax.numpy as jnp
from jax import lax
from jax.experimental import pallas as pl
from jax.experimental.pallas import tpu as pltpu

F32 = jnp.float32
BF = jnp.bfloat16

D = 1024
EPS = 1e-6
N_MOD = 6
MOD_ROWS = 8
GRID_W = 64
HD = 64
A_HEADS = 8
A_KV = 2
A_GROUP = A_HEADS // A_KV
ROPE_THETA = 10000.0
LOG2E = 1.4426950408889634
B_HEADS = 4
B_DK = 64
B_DV = 128
B_RANK = 16
B_TAU = 16.0
B_CHUNK = 64
C_HEADS = 16
NA_ROWS = 8
NA_COLS = 16
N_EXPERTS = 8
LANES = 128
VMEM_LIMIT = 56 * 2 ** 20
ROW_TM = 1024
FFN_TF = 512

NEG_INF = float("-inf")


def _params(sem):
    return pltpu.CompilerParams(dimension_semantics=sem, vmem_limit_bytes=VMEM_LIMIT)


def _dot(a, b):
    return jnp.dot(a, b, preferred_element_type=F32)


def _dot_nt(a, b):
    return lax.dot_general(a, b, (((1,), (1,)), ((), ())), preferred_element_type=F32)


def _dot_tn(a, b):
    return lax.dot_general(a, b, (((0,), (0,)), ((), ())), preferred_element_type=F32)


def _silu(x):
    return x / (1.0 + jnp.exp(-x))


def _split_dot(lhs_bf_exact, x):
    hi = x.astype(BF)
    lo = (x - hi.astype(F32)).astype(BF)
    return _dot(lhs_bf_exact, hi) + _dot(lhs_bf_exact, lo)


def _norm_mod(x, g, shift, scale):
    ms = jnp.mean(x * x, axis=-1, keepdims=True)
    return (x * lax.rsqrt(ms + EPS) * g) * (1.0 + scale) + shift


def _adaln_body(c_ref, w_ref, b_ref, o_ref):
    s = _silu(c_ref[...])
    o_ref[0] = _dot(s.astype(BF), w_ref[0].astype(BF)) + b_ref[0]


def _adaln(cond, w_mod, b_mod):
    depth, _, n = w_mod.shape
    rows = cond.shape[0]
    tn = n // 4
    return pl.pallas_call(
        _adaln_body,
        grid=(depth, n // tn),
        in_specs=[pl.BlockSpec((rows, D), lambda l, j: (0, 0)),
                  pl.BlockSpec((1, D, tn), lambda l, j: (l, 0, j)),
                  pl.BlockSpec((1, 1, tn), lambda l, j: (l, 0, j))],
        out_specs=pl.BlockSpec((1, rows, tn), lambda l, j: (l, 0, j)),
        out_shape=jax.ShapeDtypeStruct((depth, rows, n), F32),
        compiler_params=_params(("arbitrary", "arbitrary")),
        name="adaln",
    )(cond, w_mod, b_mod.reshape(depth, 1, n))


def _qk_norm_rope(p, gqk, bd, ct, st):
    pp = p * p
    sums = []
    for c in range(p.shape[1] // LANES):
        x = pp[:, c * LANES:(c + 1) * LANES]
        hi = x.astype(BF)
        lo = (x - hi.astype(F32)).astype(BF)
        sums.append(_dot(jnp.concatenate([hi, lo], axis=1), bd))
    ss = jnp.concatenate(sums, axis=1)
    y = p * lax.rsqrt(ss * (1.0 / HD) + EPS) * gqk
    lane = lax.broadcasted_iota(jnp.int32, ct.shape, 1)
    first = (lane % HD) < (HD // 2)
    outs = []
    for c in range(p.shape[1] // LANES):
        yc = y[:, c * LANES:(c + 1) * LANES]
        partner = jnp.where(first, pltpu.roll(yc, LANES - HD // 2, 1), pltpu.roll(yc, HD // 2, 1))
        outs.append(yc * ct + partner * st)
    return jnp.concatenate(outs, axis=1)


def _values_with_ones(p):
    lane = lax.broadcasted_iota(jnp.int32, (p.shape[0], LANES), 1)
    low = lane < HD
    tiles = []
    for c in range(p.shape[1] // LANES):
        x = p[:, c * LANES:(c + 1) * LANES]
        tiles += [jnp.where(low, x, 1.0), jnp.where(low, 1.0, x)]
    return jnp.concatenate(tiles, axis=1)


def _proj_body(x_ref, g_ref, m_ref, w_ref, *rest, widths, kinds, scales, col0):
    if "qk" in kinds:
        ct_ref, st_ref, gqk_ref, bd_ref = rest[:4]
        o_refs = rest[4:]
    else:
        o_refs = rest
    h = _norm_mod(x_ref[...], g_ref[...], m_ref[0, 0:1, :], m_ref[0, 1:2, :]).astype(BF)
    off = col0
    oi = 0
    for wd, kind, scale in zip(widths, kinds, scales):
        p = _dot(h, w_ref[:, off:off + wd])
        off += wd
        if scale != 1.0:
            p = p * scale
        if kind == "qk":
            p = _qk_norm_rope(p, gqk_ref[...], bd_ref[...], ct_ref[...], st_ref[...])
            nq = A_HEADS * HD
            o_refs[oi][...] = p[:, :nq].astype(BF)
            o_refs[oi + 1][...] = p[:, nq:].astype(BF)
            oi += 2
        else:
            if kind == "values":
                p = _values_with_ones(p)
            o_refs[oi][...] = p.astype(BF)
            oi += 1


def _proj(x2, g, mpack, mod_of_tile, w, widths, kinds, out_widths, tm, rope_args=None, scales=None, col0=0):
    rows = x2.shape[0]
    n = w.shape[1]
    in_specs = [pl.BlockSpec((tm, D), lambda i: (i, 0)),
                pl.BlockSpec((1, D), lambda i: (0, 0)),
                pl.BlockSpec((1, MOD_ROWS, D), lambda i: (mod_of_tile(i), 0, 0)),
                pl.BlockSpec((D, n), lambda i: (0, 0))]
    args = [x2, g.reshape(1, D), mpack, w]
    if rope_args is not None:
        ct, st, gqk, bd = rope_args
        nt = ct.shape[0] // tm
        in_specs += [pl.BlockSpec((tm, LANES), lambda i: (i % nt, 0)),
                     pl.BlockSpec((tm, LANES), lambda i: (i % nt, 0)),
                     pl.BlockSpec(gqk.shape, lambda i: (0, 0)),
                     pl.BlockSpec(bd.shape, lambda i: (0, 0))]
        args += [ct, st, gqk, bd]
    return pl.pallas_call(
        functools.partial(_proj_body, widths=widths, kinds=kinds, scales=scales or (1.0,) * len(widths),
                          col0=col0),
        grid=(rows // tm,),
        in_specs=in_specs,
        out_specs=[pl.BlockSpec((tm, ow), lambda i: (i, 0)) for ow in out_widths],
        out_shape=[jax.ShapeDtypeStruct((rows, ow), BF) for ow in out_widths],
        compiler_params=_params(("arbitrary",)),
        name="norm_mod_proj",
    )(*args)


GQA_UNIT = 2
GQA_TQ = 256

def _gqa_body(q_ref, kc_ref, vc_ref, *rest, with_lat, tq):
    if with_lat:
        kl_ref, vl_ref, o_ref = rest
    else:
        (o_ref,) = rest
    units = [(j, t0) for j in range(A_KV) for t0 in range(0, A_GROUP, GQA_UNIT)]
    lane = lax.broadcasted_iota(jnp.int32, (tq, LANES), 1)
    own = [(lane // HD) == j for j in range(A_KV)]

    def scores(unit):
        j, t0 = unit
        qg = jnp.concatenate(
            [jnp.where(own[j], q_ref[:, LANES * t:LANES * (t + 1)], 0.0) for t in range(t0, t0 + GQA_UNIT)],
            axis=0)
        s_c = _dot_nt(qg, kc_ref[...])
        s_l = _dot_nt(qg, kl_ref[...]) if with_lat else None
        return s_c, s_l

    res = {}
    nxt = scores(units[0])
    for u, (j, t0) in enumerate(units):
        s_c, s_l = nxt
        if u + 1 < len(units):
            nxt = scores(units[u + 1])
        vcols = slice(LANES * j, LANES * (j + 1))
        m = jnp.max(s_c, axis=1, keepdims=True)
        if with_lat:
            m = jnp.maximum(m, jnp.max(s_l, axis=1, keepdims=True))
        acc = _dot(jnp.exp2(s_c - m).astype(BF), vc_ref[:, vcols])
        if with_lat:
            acc = acc + _dot(jnp.exp2(s_l - m).astype(BF), vl_ref[:, vcols])
        den_lane = HD * (1 - j)
        o = acc / acc[:, den_lane:den_lane + 1]
        for g in range(GQA_UNIT):
            res[(j, t0 + g)] = o[g * tq:(g + 1) * tq]
    for t in range(A_GROUP):
        o_ref[:, LANES * t:LANES * (t + 1)] = jnp.where(own[0], res[(0, t)], res[(1, t)]).astype(BF)


def _gqa(q, k_ctx, v_ctx, k_lat, v_lat, bsz, tq):
    sq = q.shape[0] // bsz
    nc = k_ctx.shape[0] // bsz
    nq = sq // tq
    with_lat = k_lat is not None
    in_specs = [pl.BlockSpec((tq, A_HEADS * HD), lambda b, i: (b * nq + i, 0)),
                pl.BlockSpec((nc, A_KV * HD), lambda b, i: (b, 0)),
                pl.BlockSpec((nc, 2 * A_KV * HD), lambda b, i: (b, 0))]
    args = [q, k_ctx, v_ctx]
    if with_lat:
        sl = k_lat.shape[0] // bsz
        in_specs += [pl.BlockSpec((sl, A_KV * HD), lambda b, i: (b, 0)),
                     pl.BlockSpec((sl, 2 * A_KV * HD), lambda b, i: (b, 0))]
        args += [k_lat, v_lat]
    return pl.pallas_call(
        functools.partial(_gqa_body, with_lat=with_lat, tq=tq),
        grid=(bsz, nq),
        in_specs=in_specs,
        out_specs=pl.BlockSpec((tq, A_HEADS * HD), lambda b, i: (b * nq + i, 0)),
        out_shape=jax.ShapeDtypeStruct(q.shape, BF),
        compiler_params=_params(("arbitrary", "arbitrary")),
        name="gqa_attention",
    )(*args)


GLA_Q = 0
GLA_K = B_HEADS * B_DK
GLA_V = 2 * B_HEADS * B_DK
GLA_R = GLA_V + B_HEADS * B_DV
GLA_W = GLA_R + B_HEADS * B_DV
GLA_GROUP = 8


def _gla_body(gc_ref, gl_ref, zc_ref, zl_ref, w2f_ref, w2b_ref, bf_ref, bb_ref, gg_ref,
              oc_ref, ol_ref, ofc_ref, ofl_ref, st_ref):
    nk = B_HEADS * B_DK
    ch = B_CHUNK
    row = lax.broadcasted_iota(jnp.int32, (ch, ch), 0)
    col = lax.broadcasted_iota(jnp.int32, (ch, ch), 1)
    lane = lax.broadcasted_iota(jnp.int32, (ch, nk), 1)
    head_masks = [(lane // B_DK) == h for h in range(B_HEADS)]

    def group_out(g_ref, z_ref, r0s, fwd, s_t):
        w2 = (w2f_ref if fwd else w2b_ref)[...]
        bias = (bf_ref if fwd else bb_ref)[...]
        keep = (row >= col) if fwd else (row <= col)
        tri = jnp.where(keep, 1.0, 0.0).astype(BF)
        blks = [g_ref[pl.ds(r0, ch), :] for r0 in r0s]
        pres = [_dot(z_ref[pl.ds(r0, ch), :], w2) + bias for r0 in r0s]
        las = [-(jnp.maximum(-p, 0.0) + jnp.log(1.0 + jnp.exp(-jnp.abs(p)))) * (1.0 / B_TAU) for p in pres]
        bcums = [_split_dot(tri, la) for la in las]
        qms, kis, kss, decs = [], [], [], []
        for blk, bcum in zip(blks, bcums):
            q = blk[:, GLA_Q:GLA_Q + nk].astype(F32)
            k = blk[:, GLA_K:GLA_K + nk].astype(F32)
            b_last = bcum[ch - 1:ch, :] if fwd else bcum[0:1, :]
            q_dec = q * jnp.exp(bcum)
            k_state = k * jnp.exp(b_last - bcum)
            qms.append([jnp.where(m, q_dec, 0.0).astype(BF) for m in head_masks])
            kis.append((k * jnp.exp(-bcum)).astype(BF))
            kss.append([jnp.where(m, k_state, 0.0).astype(BF) for m in head_masks])
            decs.append(jnp.exp(b_last))
        vs = [[blk[:, GLA_V + B_DV * h:GLA_V + B_DV * (h + 1)] for h in range(B_HEADS)] for blk in blks]
        atts = [[jnp.where(keep, _dot_nt(qm, ki), 0.0).astype(BF) for qm in qmu] for qmu, ki in zip(qms, kis)]
        o_intra = [[_dot(a, v) for a, v in zip(au, vu)] for au, vu in zip(atts, vs)]
        d_ss = []
        for vu, ksu in zip(vs, kss):
            d_s = _dot_tn(vu[0], ksu[0])
            for h in range(1, B_HEADS):
                d_s = d_s + _dot_tn(vu[h], ksu[h])
            d_ss.append(d_s)
        outs = []
        for u in range(len(r0s)):
            s_b = s_t.astype(BF)
            outs.append(jnp.concatenate(
                [o_intra[u][h] + _dot_nt(qms[u][h], s_b) for h in range(B_HEADS)], axis=1))
            s_t = decs[u] * s_t + d_ss[u]
        return outs, blks, s_t

    def fwd_pass(g_ref, z_ref, of_ref, n):
        grp = min(GLA_GROUP, n)
        assert n % grp == 0

        def body(i, carry):
            r0s = [pl.multiple_of((i * grp + u) * ch, ch) for u in range(grp)]
            outs, _, s_t = group_out(g_ref, z_ref, r0s, True, st_ref[...])
            for r0, o in zip(r0s, outs):
                of_ref[pl.ds(r0, ch), :] = o
            st_ref[...] = s_t
            return carry
        lax.fori_loop(0, n // grp, body, 0)

    def bwd_pass(g_ref, z_ref, of_ref, o_ref, n):
        grp = min(GLA_GROUP, n)
        assert n % grp == 0

        def body(i, carry):
            r0s = [pl.multiple_of((n - 1 - (i * grp + u)) * ch, ch) for u in range(grp)]
            outs, blks, s_t = group_out(g_ref, z_ref, r0s, False, st_ref[...])
            for r0, o, blk in zip(r0s, outs, blks):
                o = o + of_ref[pl.ds(r0, ch), :]
                parts = []
                for h in range(B_HEADS):
                    oh = o[:, B_DV * h:B_DV * (h + 1)]
                    ms = jnp.mean(oh * oh, axis=-1, keepdims=True)
                    parts.append(oh * lax.rsqrt(ms + EPS))
                y = jnp.concatenate(parts, axis=1) * gg_ref[...]
                r = blk[:, GLA_R:GLA_R + B_HEADS * B_DV].astype(F32)
                o_ref[pl.ds(r0, ch), :] = (y * _silu(r)).astype(BF)
            st_ref[...] = s_t
            return carry
        lax.fori_loop(0, n // grp, body, 0)

    nc = gc_ref.shape[0] // ch
    nl = gl_ref.shape[0] // ch
    st_ref[...] = jnp.zeros_like(st_ref)
    fwd_pass(gc_ref, zc_ref, ofc_ref, nc)
    fwd_pass(gl_ref, zl_ref, ofl_ref, nl)
    st_ref[...] = jnp.zeros_like(st_ref)
    bwd_pass(gc_ref, zc_ref, ofc_ref, oc_ref, nc)
    bwd_pass(gl_ref, zl_ref, ofl_ref, ol_ref, nl)


def _gla(g_ctx, g_lat, z_ctx, z_lat, w2f, w2b, b_f, b_b, gg, bsz):
    nc = g_ctx.shape[0] // bsz
    sl = g_lat.shape[0] // bsz
    nk = B_HEADS * B_DK
    nv = B_HEADS * B_DV
    full = lambda shape: pl.BlockSpec(shape, lambda b: (0, 0))
    return pl.pallas_call(
        _gla_body,
        grid=(bsz,),
        in_specs=[pl.BlockSpec((nc, GLA_W), lambda b: (b, 0)),
                  pl.BlockSpec((sl, GLA_W), lambda b: (b, 0)),
                  pl.BlockSpec((nc, LANES), lambda b: (b, 0)),
                  pl.BlockSpec((sl, LANES), lambda b: (b, 0)),
                  full((LANES, nk)), full((LANES, nk)), full((1, nk)), full((1, nk)), full((1, nv))],
        out_specs=[pl.BlockSpec((nc, nv), lambda b: (b, 0)),
                   pl.BlockSpec((sl, nv), lambda b: (b, 0))],
        out_shape=[jax.ShapeDtypeStruct((g_ctx.shape[0], nv), BF),
                   jax.ShapeDtypeStruct((g_lat.shape[0], nv), BF)],
        scratch_shapes=[pltpu.VMEM((nc, nv), F32), pltpu.VMEM((sl, nv), F32),
                        pltpu.VMEM((B_DV, nk), F32)],
        compiler_params=_params(("arbitrary",)),
        name="gla_bidir",
    )(g_ctx, g_lat, z_ctx, z_lat, w2f, w2b, b_f, b_b, gg)


def _mixer_residual(x_ref, m_ref, pair_refs):
    acc = _dot(pair_refs[0][...], pair_refs[1][...])
    for t in range(1, len(pair_refs) // 2):
        acc = acc + _dot(pair_refs[2 * t][...], pair_refs[2 * t + 1][...])
    return x_ref[...] + m_ref[0, 2:3, :] * acc


def _pair_specs(pairs, tm, index):
    in_specs, args = [], []
    for a, w in pairs:
        in_specs += [pl.BlockSpec((tm, a.shape[1]), index), pl.BlockSpec(w.shape, lambda *_: (0, 0))]
        args += [a, w]
    return in_specs, args


def _ffn_body(x_ref, g_ref, m_ref, *rest, n_pairs):
    pair_refs = rest[:2 * n_pairs]
    wg_ref, wu_ref, wd_ref, o_ref, xn_ref, h_ref, acc_ref = rest[2 * n_pairs:]
    j = pl.program_id(1)

    def swiglu_step():
        h = h_ref[...]
        a = (_silu(_dot(h, wg_ref[...])) * _dot(h, wu_ref[...])).astype(BF)
        return _dot(a, wd_ref[...])

    @pl.when(j == 0)
    def _():
        xn = _mixer_residual(x_ref, m_ref, pair_refs)
        xn_ref[...] = xn
        h_ref[...] = _norm_mod(xn, g_ref[...], m_ref[0, 3:4, :], m_ref[0, 4:5, :]).astype(BF)
        acc_ref[...] = swiglu_step()

    @pl.when(j > 0)
    def _():
        acc_ref[...] += swiglu_step()

    @pl.when(j == pl.num_programs(1) - 1)
    def _():
        o_ref[...] = xn_ref[...] + m_ref[0, 5:6, :] * acc_ref[...]


def _ffn(x2, g, mpack, mod_of_tile, pairs, wg, wu, wd, tm, tf):
    rows = x2.shape[0]
    ff = wg.shape[1]
    pair_specs, pair_args = _pair_specs(pairs, tm, lambda i, j: (i, 0))
    return pl.pallas_call(
        functools.partial(_ffn_body, n_pairs=len(pairs)),
        grid=(rows // tm, ff // tf),
        in_specs=[pl.BlockSpec((tm, D), lambda i, j: (i, 0)),
                  pl.BlockSpec((1, D), lambda i, j: (0, 0)),
                  pl.BlockSpec((1, MOD_ROWS, D), lambda i, j: (mod_of_tile(i), 0, 0)),
                  *pair_specs,
                  pl.BlockSpec((D, tf), lambda i, j: (0, j)),
                  pl.BlockSpec((D, tf), lambda i, j: (0, j)),
                  pl.BlockSpec((tf, D), lambda i, j: (j, 0))],
        out_specs=pl.BlockSpec((tm, D), lambda i, j: (i, 0)),
        out_shape=jax.ShapeDtypeStruct((rows, D), F32),
        scratch_shapes=[pltpu.VMEM((tm, D), F32), pltpu.VMEM((tm, D), BF), pltpu.VMEM((tm, D), F32)],
        compiler_params=_params(("arbitrary", "arbitrary")),
        name="ffn_swiglu",
    )(x2, g.reshape(1, D), mpack, *pair_args, wg, wu, wd)


NA_QROWS = 4
NA_KROWS = 12
NA_TQ = NA_QROWS * GRID_W
NA_HG = 8
NA_W = NA_HG * HD


def _na_body(q_ref, k0_ref, k1_ref, k2_ref, v0_ref, v1_ref, v2_ref, kc_ref, vc_ref, b_ref, o_ref):
    pair = LANES // HD
    lane = lax.broadcasted_iota(jnp.int32, (NA_TQ, LANES), 1)
    own = [(lane // HD) == half for half in range(pair)]

    def scores(hh):
        tile = slice(LANES * (hh // pair), LANES * (hh // pair + 1))
        q = jnp.where(own[hh % pair], q_ref[:, tile], 0.0)
        s_loc = jnp.concatenate([_dot_nt(q, kr[:, tile]) for kr in (k0_ref, k1_ref, k2_ref)], axis=1)
        bias = b_ref[0, :, hh].reshape(NA_TQ, NA_KROWS * GRID_W)
        return s_loc + bias, _dot_nt(q, kc_ref[:, tile])

    nxt = scores(0)
    outs = []
    for hh in range(NA_HG):
        s_loc, s_ctx = nxt
        if hh + 1 < NA_HG:
            nxt = scores(hh + 1)
        vcols = slice(LANES * hh, LANES * (hh + 1))
        m = jnp.maximum(jnp.max(s_loc, axis=1, keepdims=True), jnp.max(s_ctx, axis=1, keepdims=True))
        p_loc = jnp.exp2(s_loc - m).astype(BF)
        acc = _dot(jnp.exp2(s_ctx - m).astype(BF), vc_ref[:, vcols])
        for t, vr in enumerate((v0_ref, v1_ref, v2_ref)):
            acc = acc + _dot(p_loc[:, t * NA_TQ:(t + 1) * NA_TQ], vr[:, vcols])
        den_lane = HD * (1 - hh % pair)
        outs.append(acc / acc[:, den_lane:den_lane + 1])
        if len(outs) == pair:
            t0 = hh // pair
            o_ref[:, LANES * t0:LANES * (t0 + 1)] = jnp.where(own[0], outs[0], outs[1]).astype(BF)
            outs = []


def _na_bias(rpb, rows):
    heads, n_dr, n_dc = rpb.shape
    nblk = rows // NA_QROWS
    n_off = n_dr + 2 * NA_KROWS
    rpad = jnp.pad(rpb.astype(F32) * LOG2E, ((0, 0), (NA_KROWS, NA_KROWS), (0, 0)), constant_values=NEG_INF)
    rvals = jnp.repeat(rpad.transpose(0, 2, 1), GRID_W, axis=2)
    cuts = []
    for kind, j in enumerate((0, min(1, nblk - 1), nblk - 1)):
        ws = NA_QROWS * int(np.clip(j - 1, 0, nblk - 3))
        for qr in range(NA_QROWS):
            r = NA_QROWS * j + qr
            rs = int(np.clip(r - NA_ROWS // 2, 0, rows - NA_ROWS))
            cuts.append((kind, qr, NA_KROWS + ws - r + NA_ROWS - 1, max(rs - ws, 0), min(rs + NA_ROWS - ws, NA_KROWS)))
    width = NA_KROWS * GRID_W

    def body(r_ref, o_ref):
        shape = (GRID_W, n_off * GRID_W)
        qc = lax.broadcasted_iota(jnp.int32, shape, 0)
        kc = lax.broadcasted_iota(jnp.int32, shape, 1) % GRID_W
        delta = kc - qc + NA_COLS - 1
        cs = jnp.clip(qc - NA_COLS // 2, 0, GRID_W - NA_COLS)
        toe = jnp.full(shape, NEG_INF, F32)
        for d in range(n_dc):
            toe = jnp.where(delta == d, r_ref[0, d:d + 1, :], toe)
        toe = jnp.where((kc >= cs) & (kc < cs + NA_COLS), toe, NEG_INF)
        key_row = lax.broadcasted_iota(jnp.int32, (GRID_W, width), 1) // GRID_W
        for kind, qr, d0, lo, hi in cuts:
            blk = toe[:, d0 * GRID_W:d0 * GRID_W + width]
            o_ref[kind, qr, 0] = jnp.where((key_row >= lo) & (key_row < hi), blk, NEG_INF)

    return pl.pallas_call(
        body,
        grid=(heads,),
        in_specs=[pl.BlockSpec((1, n_dc, n_off * GRID_W), lambda h: (h, 0, 0))],
        out_specs=pl.BlockSpec((3, NA_QROWS, 1, GRID_W, width), lambda h: (0, 0, h, 0, 0)),
        out_shape=jax.ShapeDtypeStruct((3, NA_QROWS, heads, GRID_W, width), F32),
        compiler_params=_params(("arbitrary",)),
        name="natten_bias",
    )(rvals)


def _natten(q, k, v, k_ctx, v_ctx, bias, bsz):
    s = q.shape[0] // bsz
    nc = k_ctx.shape[0] // bsz
    nblk = s // NA_TQ
    ngrp = C_HEADS // NA_HG
    v_w = NA_HG * LANES

    def kv_spec(t, width):
        return pl.BlockSpec((NA_TQ, width),
                            lambda j, g, b: (b * nblk + jnp.clip(j - 1, 0, nblk - 3) + t, g))

    def kind(j):
        return jnp.where(j == 0, 0, jnp.where(j == nblk - 1, 2, 1))

    return pl.pallas_call(
        _na_body,
        grid=(nblk, ngrp, bsz),
        in_specs=[pl.BlockSpec((NA_TQ, NA_W), lambda j, g, b: (b * nblk + j, g)),
                  kv_spec(0, NA_W), kv_spec(1, NA_W), kv_spec(2, NA_W),
                  kv_spec(0, v_w), kv_spec(1, v_w), kv_spec(2, v_w),
                  pl.BlockSpec((nc, NA_W), lambda j, g, b: (b, g)),
                  pl.BlockSpec((nc, v_w), lambda j, g, b: (b, g)),
                  pl.BlockSpec((1, NA_QROWS, NA_HG, GRID_W, NA_KROWS * GRID_W),
                               lambda j, g, b: (kind(j), 0, g, 0, 0))],
        out_specs=pl.BlockSpec((NA_TQ, NA_W), lambda j, g, b: (b * nblk + j, g)),
        out_shape=jax.ShapeDtypeStruct(q.shape, BF),
        compiler_params=_params(("arbitrary", "arbitrary", "arbitrary")),
        name="natten",
    )(q, k, k, k, v, v, v, k_ctx, v_ctx, bias)


MOE_TM = 512
MOE_TF = 1792
META_E1, META_E2, META_R1, META_R2, META_G1, META_G2 = range(6)
ROW_TILE = D // LANES


def _to_token_tiles(dst_ref, value):
    n = value.shape[0]
    for s in range(ROW_TILE):
        dst_ref[pl.ds(s, n, stride=ROW_TILE), :] = value[:, LANES * s:LANES * (s + 1)]


def _from_token_tiles(src_ref, n):
    return jnp.concatenate([src_ref[pl.ds(s, n, stride=ROW_TILE), :] for s in range(ROW_TILE)], axis=1)


def _route_body(x_ref, g_ref, m_ref, *rest, n_pairs):
    pair_refs = rest[:2 * n_pairs]
    wr_ref, xn_ref, h_ref, meta_ref, rec_ref, cnt_ref, tri_ref, carry_ref = rest[2 * n_pairs:]
    tm = x_ref.shape[0]

    @pl.when(pl.program_id(0) == 0)
    def _():
        row = lax.broadcasted_iota(jnp.int32, (tm, tm), 0)
        col = lax.broadcasted_iota(jnp.int32, (tm, tm), 1)
        tri_ref[...] = jnp.where(row >= col, 1.0, 0.0).astype(BF)
        carry_ref[...] = jnp.zeros_like(carry_ref)

    xn = _mixer_residual(x_ref, m_ref, pair_refs)
    xn_ref[...] = xn
    h = _norm_mod(xn, g_ref[...], m_ref[0, 3:4, :], m_ref[0, 4:5, :])
    _to_token_tiles(h_ref, h)
    h_hi = h.astype(BF)
    h_lo = (h - h_hi.astype(F32)).astype(BF)
    w = wr_ref[...]
    w_hi = w.astype(BF).astype(F32)
    w_cat = (w_hi + pltpu.roll(w - w_hi, N_EXPERTS, 1)).astype(BF)
    terms = _dot(h_hi, w_cat) + _dot(h_lo, w_cat)
    logits = terms + pltpu.roll(terms, LANES - N_EXPERTS, 1)
    lane = lax.broadcasted_iota(jnp.int32, logits.shape, 1)
    lg = jnp.where(lane < N_EXPERTS, logits, NEG_INF)
    m1 = jnp.max(lg, axis=1, keepdims=True)
    i1 = jnp.min(jnp.where(lg == m1, lane, LANES), axis=1, keepdims=True)
    lg2 = jnp.where(lane == i1, NEG_INF, lg)
    m2 = jnp.max(lg2, axis=1, keepdims=True)
    i2 = jnp.min(jnp.where(lg2 == m2, lane, LANES), axis=1, keepdims=True)
    e = jnp.exp(m2 - m1)
    g1 = 1.0 / (1.0 + e)
    g2 = e / (1.0 + e)
    oh = jnp.where(lane == i1, 1.0, 0.0) + jnp.where(lane == i2, 1.0, 0.0)
    cum = _dot(tri_ref[...], oh.astype(BF))
    base = carry_ref[0:1, :] + cum - oh
    r1 = jnp.sum(jnp.where(lane == i1, base, 0.0), axis=1, keepdims=True)
    r2 = jnp.sum(jnp.where(lane == i2, base, 0.0), axis=1, keepdims=True)
    carry_ref[...] = carry_ref[...] + cum[tm - 1:tm, :]
    cnt_ref[...] = carry_ref[...]
    meta = jnp.zeros(logits.shape, F32)
    for ln, val in ((META_E1, i1.astype(F32)), (META_E2, i2.astype(F32)), (META_R1, r1), (META_R2, r2),
                    (META_G1, g1), (META_G2, g2)):
        meta = jnp.where(lane == ln, val, meta)
    meta_ref[...] = meta
    rec_ref[...] = meta.T[:rec_ref.shape[0], :]


def _route(x2, g, mpack, mod_of_tile, pairs, wr, tm):
    rows = x2.shape[0]
    pair_specs, pair_args = _pair_specs(pairs, tm, lambda i: (i, 0))
    return pl.pallas_call(
        functools.partial(_route_body, n_pairs=len(pairs)),
        grid=(rows // tm,),
        in_specs=[pl.BlockSpec((tm, D), lambda i: (i, 0)),
                  pl.BlockSpec((1, D), lambda i: (0, 0)),
                  pl.BlockSpec((1, MOD_ROWS, D), lambda i: (mod_of_tile(i), 0, 0)),
                  *pair_specs,
                  pl.BlockSpec((D, LANES), lambda i: (0, 0))],
        out_specs=[pl.BlockSpec((tm, D), lambda i: (i, 0)),
                   pl.BlockSpec((tm * ROW_TILE, LANES), lambda i: (i, 0)),
                   pl.BlockSpec((tm, LANES), lambda i: (i, 0)),
                   pl.BlockSpec((8, tm), lambda i: (0, i)),
                   pl.BlockSpec((8, LANES), lambda i: (0, 0))],
        out_shape=[jax.ShapeDtypeStruct((rows, D), F32),
                   jax.ShapeDtypeStruct((rows * ROW_TILE, LANES), F32),
                   jax.ShapeDtypeStruct((rows, LANES), F32),
                   jax.ShapeDtypeStruct((8, rows), F32),
                   jax.ShapeDtypeStruct((8, LANES), F32)],
        scratch_shapes=[pltpu.VMEM((tm, tm), BF), pltpu.VMEM((8, LANES), F32)],
        compiler_params=_params(("arbitrary",)),
        name="moe_route",
    )(x2, g.reshape(1, D), mpack, *pair_args, wr)


def _token_copy(src_ref, src_tok, dst_ref, dst_tok, sem):
    src = src_ref.at[pl.ds(pl.multiple_of(src_tok * ROW_TILE, ROW_TILE), ROW_TILE), :]
    dst = dst_ref.at[pl.ds(pl.multiple_of(dst_tok * ROW_TILE, ROW_TILE), ROW_TILE), :]
    return pltpu.make_async_copy(src, dst, sem)


def _experts_body(te_ref, nu_ref, cur_ref, nxt_ref, dprev_ref, h_hbm, wg_ref, wu_ref, wd_ref, y_hbm,
                  xbuf, h_ref, acc_ref, ybuf, gsem, ssem, *, n_steps):
    assert n_steps >= 2
    i = pl.program_id(0)
    j = pl.program_id(1)
    used = i < nu_ref[0]
    slot = i % 2
    other = 1 - slot

    def gather_row(idx_ref, r, sl):
        _token_copy(h_hbm, idx_ref[0, 0, r], xbuf.at[sl], r, gsem.at[sl]).start()

    def issue_gathers():
        for r in range(MOE_TM):
            gather_row(nxt_ref, r, other)

    def issue_scatters():
        for r in range(MOE_TM):
            _token_copy(ybuf.at[other], r, y_hbm, dprev_ref[0, 0, r], ssem.at[other]).start()

    def wait_rows(buf, sem):
        pltpu.make_async_copy(buf, buf, sem).wait()

    def swiglu_step():
        h = h_ref[...]
        a = (_silu(_dot(h, wg_ref[0])) * _dot(h, wu_ref[0])).astype(BF)
        return _dot(a, wd_ref[0])

    @pl.when((j == 0) & (i == 0))
    def _():
        ybuf[...] = jnp.zeros_like(ybuf)
        n_spare = MOE_TM * ROW_TILE
        for sl in range(2):
            spare = pltpu.make_async_copy(
                ybuf.at[sl], y_hbm.at[pl.ds(y_hbm.shape[0] - (2 - sl) * n_spare, n_spare), :], ssem.at[sl])
            spare.start()
            spare.wait()

        def body(r, c):
            gather_row(cur_ref, r, 0)
            return c
        lax.fori_loop(0, MOE_TM, body, 0, unroll=8)

    @pl.when(j == 0)
    def _():
        wait_rows(xbuf.at[slot], gsem.at[slot])

    @pl.when(used & (j == 0))
    def _():
        h_ref[...] = _from_token_tiles(xbuf.at[slot], MOE_TM).astype(BF)
        acc_ref[...] = swiglu_step()
        issue_gathers()

    if n_steps > 2:
        @pl.when(used & (j > 0) & (j < n_steps - 1))
        def _():
            acc_ref[...] += swiglu_step()

    @pl.when(used & (j == n_steps - 1))
    def _():
        acc_ref[...] += swiglu_step()
        issue_scatters()

    @pl.when(jnp.logical_not(used) & (j == 0))
    def _():
        issue_gathers()

    @pl.when(jnp.logical_not(used) & (j == n_steps - 1))
    def _():
        issue_scatters()

    @pl.when(j == n_steps - 1)
    def _():
        @pl.when(i > 0)
        def _():
            wait_rows(ybuf.at[slot], ssem.at[slot])

        @pl.when(used)
        def _():
            _to_token_tiles(ybuf.at[slot], acc_ref[...])

        @pl.when(i == pl.num_programs(0) - 1)
        def _():
            wait_rows(ybuf.at[other], ssem.at[other])
            wait_rows(xbuf.at[other], gsem.at[other])


def _experts(h_tiles, tok_of_slot, dest_of_slot, tile_expert, n_used, wg, wu, wd, out_tokens):
    n_tiles = tile_expert.shape[0]
    ff = wg.shape[2]
    nf = ff // MOE_TF
    spare_rows = out_tokens - MOE_TM + jnp.arange(MOE_TM, dtype=jnp.int32)
    dest_prev = jnp.concatenate([spare_rows, dest_of_slot[:-MOE_TM]])

    def jj(i, j, nu):
        return jnp.where(i < nu[0], j, nf - 1)

    def smem_rows(index):
        return pl.BlockSpec((1, 1, MOE_TM), lambda i, j, te, nu: (index(i), 0, 0), memory_space=pltpu.SMEM)

    grid_spec = pltpu.PrefetchScalarGridSpec(
        num_scalar_prefetch=2,
        grid=(n_tiles, nf),
        in_specs=[smem_rows(lambda i: i),
                  smem_rows(lambda i: jnp.minimum(i + 1, n_tiles - 1)),
                  smem_rows(lambda i: i),
                  pl.BlockSpec(memory_space=pl.ANY),
                  pl.BlockSpec((1, D, MOE_TF), lambda i, j, te, nu: (te[i], 0, jj(i, j, nu))),
                  pl.BlockSpec((1, D, MOE_TF), lambda i, j, te, nu: (te[i], 0, jj(i, j, nu))),
                  pl.BlockSpec((1, MOE_TF, D), lambda i, j, te, nu: (te[i], jj(i, j, nu), 0))],
        out_specs=pl.BlockSpec(memory_space=pl.ANY),
        scratch_shapes=[pltpu.VMEM((2, MOE_TM * ROW_TILE, LANES), F32),
                        pltpu.VMEM((MOE_TM, D), BF),
                        pltpu.VMEM((MOE_TM, D), F32),
                        pltpu.VMEM((2, MOE_TM * ROW_TILE, LANES), F32),
                        pltpu.SemaphoreType.DMA((2,)),
                        pltpu.SemaphoreType.DMA((2,))])
    tok3 = tok_of_slot.reshape(n_tiles, 1, MOE_TM)
    return pl.pallas_call(
        functools.partial(_experts_body, n_steps=nf),
        grid_spec=grid_spec,
        out_shape=jax.ShapeDtypeStruct((out_tokens * ROW_TILE, LANES), F32),
        compiler_params=_params(("arbitrary", "arbitrary")),
        name="moe_experts",
    )(tile_expert, n_used, tok3, tok3, dest_prev.reshape(n_tiles, 1, MOE_TM), h_tiles, wg, wu, wd)


def _combine_body(y1_ref, y2_ref, x_ref, m_ref, meta_ref, gf_ref, o_ref):
    n = x_ref.shape[0]
    meta = meta_ref[...]
    lane = lax.broadcasted_iota(jnp.int32, meta.shape, 1)
    g1 = jnp.sum(jnp.where(lane == META_G1, meta, 0.0), axis=1, keepdims=True)
    g2 = jnp.sum(jnp.where(lane == META_G2, meta, 0.0), axis=1, keepdims=True)
    moe = g1 * _from_token_tiles(y1_ref, n) + g2 * _from_token_tiles(y2_ref, n)
    y = x_ref[...] + m_ref[0, 5:6, :] * moe
    ms = jnp.mean(y * y, axis=-1, keepdims=True)
    o_ref[...] = y * lax.rsqrt(ms + EPS) * gf_ref[...]


def _combine(x2, mpack, mod_of_row, meta, y_tiles, g_final, tm):
    rows = x2.shape[0]
    nt = rows // tm
    return pl.pallas_call(
        _combine_body,
        grid=(nt,),
        in_specs=[pl.BlockSpec((tm * ROW_TILE, LANES), lambda i: (i, 0)),
                  pl.BlockSpec((tm * ROW_TILE, LANES), lambda i: (i + nt, 0)),
                  pl.BlockSpec((tm, D), lambda i: (i, 0)),
                  pl.BlockSpec((1, MOD_ROWS, D), lambda i: (mod_of_row(i * tm), 0, 0)),
                  pl.BlockSpec((tm, LANES), lambda i: (i, 0)),
                  pl.BlockSpec((1, D), lambda i: (0, 0))],
        out_specs=pl.BlockSpec((tm, D), lambda i: (i, 0)),
        out_shape=jax.ShapeDtypeStruct((rows, D), F32),
        compiler_params=_params(("arbitrary",)),
        name="moe_combine",
    )(y_tiles, y_tiles, x2, mpack, meta, g_final.reshape(1, D))


def _moe(x2, g, mpack, mod_of_tile, mod_of_row, pairs, wr, wg, wu, wd, g_final, tm):
    rows = x2.shape[0]
    ne = wg.shape[0]
    x2, h_tiles, meta, rec, cnt = _route(x2, g, mpack, mod_of_tile, pairs, wr, tm)
    counts = cnt[0, :ne].astype(jnp.int32)
    tiles_e = (counts + MOE_TM - 1) // MOE_TM
    tile_end = jnp.cumsum(tiles_e)
    group_off = (tile_end - tiles_e) * MOE_TM
    n_tiles = 2 * rows // MOE_TM + ne
    n_slots = n_tiles * MOE_TM
    tile_expert = jnp.minimum(
        jnp.sum(jnp.arange(n_tiles)[:, None] >= tile_end[None, :], axis=1), ne - 1).astype(jnp.int32)
    n_used = tile_end[-1:].astype(jnp.int32)
    eids = jnp.arange(ne)

    def slot_of(lane_e, lane_r):
        e = rec[lane_e].astype(jnp.int32)
        off = jnp.sum(jnp.where(e[None, :] == eids[:, None], group_off[:, None], 0), axis=0)
        return off + rec[lane_r].astype(jnp.int32)

    tok = jnp.arange(rows, dtype=jnp.int32)
    dest = jnp.full((n_slots,), -1, jnp.int32).at[
        jnp.concatenate([slot_of(META_E1, META_R1), slot_of(META_E2, META_R2)])].set(
        jnp.concatenate([tok, rows + tok]), unique_indices=True, mode="promise_in_bounds")
    pad = dest < 0
    dest_of_slot = jnp.where(pad, 2 * rows + jnp.arange(n_slots, dtype=jnp.int32) % (2 * MOE_TM), dest)
    tok_of_slot = jnp.where(pad, 0, jnp.where(dest >= rows, dest - rows, dest))
    y_tiles = _experts(h_tiles, tok_of_slot, dest_of_slot, tile_expert, n_used, wg, wu, wd, 2 * rows + 2 * MOE_TM)
    return _combine(x2, mpack, mod_of_row, meta, y_tiles, g_final, tm)


def _rope_tables(seq_len):
    t = np.arange(seq_len)
    row = (t // GRID_W).astype(np.float32)
    col = (t % GRID_W).astype(np.float32)
    n_axis = HD // 4
    inv = jnp.power(ROPE_THETA, -jnp.arange(n_axis, dtype=F32) / n_axis)
    ang = jnp.concatenate([jnp.asarray(row)[:, None] * inv, jnp.asarray(col)[:, None] * inv], axis=-1)
    cos, sin = jnp.cos(ang), jnp.sin(ang)
    ct = jnp.tile(jnp.concatenate([cos, cos], axis=-1), (1, LANES // HD))
    st = jnp.tile(jnp.concatenate([-sin, sin], axis=-1), (1, LANES // HD))
    return ct, st


def kernel(x, c, ctx, c_ctx, w_mod, b_mod, g_norm1, g_norm2, w_in_ab, g_q, g_k, w_a2_f, b_a_f, w_a2_b,
           b_a_b, g_gla, w_out_ab, w_ff_gate, w_ff_up, w_ff_down, w_in_c, rpb_c, w_out_c, w_router,
           w_moe_gate, w_moe_up, w_moe_down, g_final):
    bsz, seq, _ = x.shape
    nc = ctx.shape[1]
    depth = w_mod.shape[0]
    assert depth == 2 and seq % GRID_W == 0

    cond_rows = -(-(bsz + 1) // 8) * 8
    cond = jnp.zeros((cond_rows, D), F32).at[:bsz].set(c).at[bsz].set(c_ctx)
    mods = _adaln(cond, w_mod, b_mod).reshape(depth, cond_rows, N_MOD, D)
    mods = jnp.pad(mods, ((0, 0), (0, 0), (0, MOD_ROWS - N_MOD), (0, 0)))

    xl = x.reshape(bsz * seq, D)
    xc = ctx.reshape(bsz * nc, D)
    tm_l = min(ROW_TM, seq)
    tm_c = min(ROW_TM, bsz * nc)
    tpb = seq // tm_l
    lat_mod = lambda i: i // tpb
    ctx_mod = lambda i: bsz

    mp = mods[0]
    w_in = w_in_ab[0]
    perm = np.concatenate([np.arange(0, HD, 2), np.arange(1, HD, 2)])
    nq, nkv = A_HEADS * HD, A_KV * HD
    o_q, o_k, o_v = 0, nq, nq + nkv
    o_bq = o_v + nkv
    o_bk = o_bq + B_HEADS * B_DK
    o_bv = o_bk + B_HEADS * B_DK
    o_br = o_bv + B_HEADS * B_DV
    o_zf = o_br + B_HEADS * B_DV
    o_end = o_zf + 2 * B_RANK
    assert A_KV * HD == LANES
    q_order = np.arange(A_HEADS).reshape(A_KV, A_GROUP).T.reshape(-1)
    wq = w_in[:, o_q:o_k].reshape(D, A_HEADS, HD)[:, q_order][:, :, perm].reshape(D, nq)
    wk = w_in[:, o_k:o_v].reshape(D, A_KV, HD)[:, :, perm].reshape(D, nkv)
    wz = jnp.pad(w_in[:, o_zf:o_end], ((0, 0), (0, LANES - 2 * B_RANK)))
    w0 = jnp.concatenate([wq, wk, w_in[:, o_v:o_bq], w_in[:, o_bq:o_bk] * (B_DK ** -0.5),
                          w_in[:, o_bk:o_zf], wz], axis=1).astype(BF)
    widths0 = (nq + nkv, nkv, GLA_W, LANES)
    outw0 = (nq, nkv, 2 * nkv, GLA_W, LANES)
    gqk = jnp.concatenate([jnp.tile(g_q[0][perm] * (HD ** -0.5 * LOG2E), A_HEADS),
                           jnp.tile(g_k[0][perm], A_KV)]).reshape(1, nq + nkv)
    bd = jnp.asarray(np.tile(np.kron(np.eye(LANES // HD), np.ones((HD, HD))), (2, 1)), BF)
    ct, st = _rope_tables(seq)
    ct_c = jnp.ones((tm_c, LANES), F32)
    st_c = jnp.zeros((tm_c, LANES), F32)

    kinds0 = ("qk", "values", "plain", "plain")
    ql, kl, vl, gl, zl = _proj(xl, g_norm1[0], mp, lat_mod, w0, widths0, kinds0, outw0, tm_l, (ct, st, gqk, bd))
    qc, kc, vc, gc, zc = _proj(xc, g_norm1[0], mp, ctx_mod, w0, widths0, kinds0, outw0, tm_c,
                               (ct_c, st_c, gqk, bd))

    oa_l = _gqa(ql, kc, vc, kl, vl, bsz, min(GQA_TQ, seq))
    oa_c = _gqa(qc, kc, vc, None, None, bsz, min(GQA_TQ, nc))

    w2f = jnp.zeros((LANES, B_HEADS * B_DK), F32).at[:B_RANK].set(w_a2_f[0]).astype(BF)
    w2b = jnp.zeros((LANES, B_HEADS * B_DK), F32).at[B_RANK:2 * B_RANK].set(w_a2_b[0]).astype(BF)
    ob_c, ob_l = _gla(gc, gl, zc, zl, w2f, w2b, b_a_f[0].reshape(1, -1), b_a_b[0].reshape(1, -1),
                      jnp.tile(g_gla[0], B_HEADS).reshape(1, -1), bsz)

    wo = w_out_ab[0].astype(BF)
    wo_a = wo[:nq].reshape(A_HEADS, HD, D)[q_order].reshape(nq, D)
    wo_b = wo[nq:]
    wg, wu, wd = w_ff_gate[0].astype(BF), w_ff_up[0].astype(BF), w_ff_down[0].astype(BF)
    xl = _ffn(xl, g_norm2[0], mp, lat_mod, [(oa_l, wo_a), (ob_l, wo_b)], wg, wu, wd, tm_l, FFN_TF)
    xc = _ffn(xc, g_norm2[0], mp, ctx_mod, [(oa_c, wo_a), (ob_c, wo_b)], wg, wu, wd, tm_c, FFN_TF)

    mp = mods[1]
    cw = C_HEADS * HD
    w1 = w_in_c[0].astype(BF)
    q1, k1, v1 = _proj(xl, g_norm1[1], mp, lat_mod, w1, (cw, cw, cw), ("plain", "plain", "values"),
                       (cw, cw, 2 * cw), tm_l, scales=(HD ** -0.5 * LOG2E, 1.0, 1.0))
    k1c, v1c = _proj(xc, g_norm1[1], mp, ctx_mod, w1, (cw, cw), ("plain", "values"), (cw, 2 * cw), tm_c,
                     col0=cw)

    bias = _na_bias(rpb_c[0], seq // GRID_W)
    o1 = _natten(q1, k1, v1, k1c, v1c, bias, bsz)

    wr = jnp.pad(w_router[0], ((0, 0), (0, LANES - N_EXPERTS)))
    out = _moe(xl, g_norm2[1], mp, lat_mod, lambda r: r // seq, [(o1, w_out_c[0].astype(BF))], wr,
               w_moe_gate[0].astype(BF), w_moe_up[0].astype(BF), w_moe_down[0].astype(BF), g_final, tm_l)
    return out.reshape(bsz, seq, D)
```

```python
import functools

import numpy as np
import jax
import jax.numpy as jnp
from jax import lax
from jax.experimental import pallas as pl
from jax.experimental.pallas import tpu as pltpu

F32 = jnp.float32
BF = jnp.bfloat16

D = 1024
EPS = 1e-6
N_MOD = 6
MOD_ROWS = 8
GRID_W = 64
HD = 64
A_HEADS = 8
A_KV = 2
A_GROUP = A_HEADS // A_KV
ROPE_THETA = 10000.0
LOG2E = 1.4426950408889634
B_HEADS = 4
B_DK = 64
B_DV = 128
B_RANK = 16
B_TAU = 16.0
B_CHUNK = 64
C_HEADS = 16
NA_ROWS = 8
NA_COLS = 16
N_EXPERTS = 8
LANES = 128
VMEM_LIMIT = 56 * 2 ** 20
ROW_TM = 1024
FFN_TF = 512

NEG_INF = float("-inf")


def _params(sem):
    return pltpu.CompilerParams(dimension_semantics=sem, vmem_limit_bytes=VMEM_LIMIT)


def _dot(a, b):
    return jnp.dot(a, b, preferred_element_type=F32)


def _dot_nt(a, b):
    return lax.dot_general(a, b, (((1,), (1,)), ((), ())), preferred_element_type=F32)


def _dot_tn(a, b):
    return lax.dot_general(a, b, (((0,), (0,)), ((), ())), preferred_element_type=F32)


def _silu(x):
    return x / (1.0 + jnp.exp(-x))


def _split_dot(lhs_bf_exact, x):
    hi = x.astype(BF)
    lo = (x - hi.astype(F32)).astype(BF)
    return _dot(lhs_bf_exact, hi) + _dot(lhs_bf_exact, lo)


def _norm_mod(x, g, shift, scale):
    ms = jnp.mean(x * x, axis=-1, keepdims=True)
    return (x * lax.rsqrt(ms + EPS) * g) * (1.0 + scale) + shift


def _adaln_body(c_ref, w_ref, b_ref, o_ref):
    s = _silu(c_ref[...])
    o_ref[0] = _dot(s.astype(BF), w_ref[0].astype(BF)) + b_ref[0]


def _adaln(cond, w_mod, b_mod):
    depth, _, n = w_mod.shape
    rows = cond.shape[0]
    tn = n // 4
    return pl.pallas_call(
        _adaln_body,
        grid=(depth, n // tn),
        in_specs=[pl.BlockSpec((rows, D), lambda l, j: (0, 0)),
                  pl.BlockSpec((1, D, tn), lambda l, j: (l, 0, j)),
                  pl.BlockSpec((1, 1, tn), lambda l, j: (l, 0, j))],
        out_specs=pl.BlockSpec((1, rows, tn), lambda l, j: (l, 0, j)),
        out_shape=jax.ShapeDtypeStruct((depth, rows, n), F32),
        compiler_params=_params(("arbitrary", "arbitrary")),
        name="adaln",
    )(cond, w_mod, b_mod.reshape(depth, 1, n))


def _qk_norm_rope(p, gqk, bd, ct, st):
    pp = p * p
    sums = []
    for c in range(p.shape[1] // LANES):
        x = pp[:, c * LANES:(c + 1) * LANES]
        hi = x.astype(BF)
        lo = (x - hi.astype(F32)).astype(BF)
        sums.append(_dot(jnp.concatenate([hi, lo], axis=1), bd))
    ss = jnp.concatenate(sums, axis=1)
    y = p * lax.rsqrt(ss * (1.0 / HD) + EPS) * gqk
    lane = lax.broadcasted_iota(jnp.int32, ct.shape, 1)
    first = (lane % HD) < (HD // 2)
    outs = []
    for c in range(p.shape[1] // LANES):
        yc = y[:, c * LANES:(c + 1) * LANES]
        partner = jnp.where(first, pltpu.roll(yc, LANES - HD // 2, 1), pltpu.roll(yc, HD // 2, 1))
        outs.append(yc * ct + partner * st)
    return jnp.concatenate(outs, axis=1)


def _values_with_ones(p):
    lane = lax.broadcasted_iota(jnp.int32, (p.shape[0], LANES), 1)
    low = lane < HD
    tiles = []
    for c in range(p.shape[1] // LANES):
        x = p[:, c * LANES:(c + 1) * LANES]
        tiles += [jnp.where(low, x, 1.0), jnp.where(low, 1.0, x)]
    return jnp.concatenate(tiles, axis=1)


def _proj_body(x_ref, g_ref, m_ref, w_ref, *rest, widths, kinds, scales, col0):
    if "qk" in kinds:
        ct_ref, st_ref, gqk_ref, bd_ref = rest[:4]
        o_refs = rest[4:]
    else:
        o_refs = rest
    h = _norm_mod(x_ref[...], g_ref[...], m_ref[0, 0:1, :], m_ref[0, 1:2, :]).astype(BF)
    off = col0
    oi = 0
    for wd, kind, scale in zip(widths, kinds, scales):
        p = _dot(h, w_ref[:, off:off + wd])
        off += wd
        if scale != 1.0:
            p = p * scale
        if kind == "qk":
            p = _qk_norm_rope(p, gqk_ref[...], bd_ref[...], ct_ref[...], st_ref[...])
            nq = A_HEADS * HD
            o_refs[oi][...] = p[:, :nq].astype(BF)
            o_refs[oi + 1][...] = p[:, nq:].astype(BF)
            oi += 2
        else:
            if kind == "values":
                p = _values_with_ones(p)
            o_refs[oi][...] = p.astype(BF)
            oi += 1


def _proj(x2, g, mpack, mod_of_tile, w, widths, kinds, out_widths, tm, rope_args=None, scales=None, col0=0):
    rows = x2.shape[0]
    n = w.shape[1]
    in_specs = [pl.BlockSpec((tm, D), lambda i: (i, 0)),
                pl.BlockSpec((1, D), lambda i: (0, 0)),
                pl.BlockSpec((1, MOD_ROWS, D), lambda i: (mod_of_tile(i), 0, 0)),
                pl.BlockSpec((D, n), lambda i: (0, 0))]
    args = [x2, g.reshape(1, D), mpack, w]
    if rope_args is not None:
        ct, st, gqk, bd = rope_args
        nt = ct.shape[0] // tm
        in_specs += [pl.BlockSpec((tm, LANES), lambda i: (i % nt, 0)),
                     pl.BlockSpec((tm, LANES), lambda i: (i % nt, 0)),
                     pl.BlockSpec(gqk.shape, lambda i: (0, 0)),
                     pl.BlockSpec(bd.shape, lambda i: (0, 0))]
        args += [ct, st, gqk, bd]
    return pl.pallas_call(
        functools.partial(_proj_body, widths=widths, kinds=kinds, scales=scales or (1.0,) * len(widths),
                          col0=col0),
        grid=(rows // tm,),
        in_specs=in_specs,
        out_specs=[pl.BlockSpec((tm, ow), lambda i: (i, 0)) for ow in out_widths],
        out_shape=[jax.ShapeDtypeStruct((rows, ow), BF) for ow in out_widths],
        compiler_params=_params(("arbitrary",)),
        name="norm_mod_proj",
    )(*args)


GQA_UNIT = 2
GQA_TQ = 256

def _gqa_body(q_ref, kc_ref, vc_ref, *rest, with_lat, tq):
    if with_lat:
        kl_ref, vl_ref, o_ref = rest
    else:
        (o_ref,) = rest
    units = [(j, t0) for j in range(A_KV) for t0 in range(0, A_GROUP, GQA_UNIT)]
    lane = lax.broadcasted_iota(jnp.int32, (tq, LANES), 1)
    own = [(lane // HD) == j for j in range(A_KV)]

    def scores(unit):
        j, t0 = unit
        qg = jnp.concatenate(
            [jnp.where(own[j], q_ref[:, LANES * t:LANES * (t + 1)], 0.0) for t in range(t0, t0 + GQA_UNIT)],
            axis=0)
        s_c = _dot_nt(qg, kc_ref[...])
        s_l = _dot_nt(qg, kl_ref[...]) if with_lat else None
        return s_c, s_l

    res = {}
    nxt = scores(units[0])
    for u, (j, t0) in enumerate(units):
        s_c, s_l = nxt
        if u + 1 < len(units):
            nxt = scores(units[u + 1])
        vcols = slice(LANES * j, LANES * (j + 1))
        m = jnp.max(s_c, axis=1, keepdims=True)
        if with_lat:
            m = jnp.maximum(m, jnp.max(s_l, axis=1, keepdims=True))
        acc = _dot(jnp.exp2(s_c - m).astype(BF), vc_ref[:, vcols])
        if with_lat:
            acc = acc + _dot(jnp.exp2(s_l - m).astype(BF), vl_ref[:, vcols])
        den_lane = HD * (1 - j)
        o = acc / acc[:, den_lane:den_lane + 1]
        for g in range(GQA_UNIT):
            res[(j, t0 + g)] = o[g * tq:(g + 1) * tq]
    for t in range(A_GROUP):
        o_ref[:, LANES * t:LANES * (t + 1)] = jnp.where(own[0], res[(0, t)], res[(1, t)]).astype(BF)


def _gqa(q, k_ctx, v_ctx, k_lat, v_lat, bsz, tq):
    sq = q.shape[0] // bsz
    nc = k_ctx.shape[0] // bsz
    nq = sq // tq
    with_lat = k_lat is not None
    in_specs = [pl.BlockSpec((tq, A_HEADS * HD), lambda b, i: (b * nq + i, 0)),
                pl.BlockSpec((nc, A_KV * HD), lambda b, i: (b, 0)),
                pl.BlockSpec((nc, 2 * A_KV * HD), lambda b, i: (b, 0))]
    args = [q, k_ctx, v_ctx]
    if with_lat:
        sl = k_lat.shape[0] // bsz
        in_specs += [pl.BlockSpec((sl, A_KV * HD), lambda b, i: (b, 0)),
                     pl.BlockSpec((sl, 2 * A_KV * HD), lambda b, i: (b, 0))]
        args += [k_lat, v_lat]
    return pl.pallas_call(
        functools.partial(_gqa_body, with_lat=with_lat, tq=tq),
        grid=(bsz, nq),
        in_specs=in_specs,
        out_specs=pl.BlockSpec((tq, A_HEADS * HD), lambda b, i: (b * nq + i, 0)),
        out_shape=jax.ShapeDtypeStruct(q.shape, BF),
        compiler_params=_params(("arbitrary", "arbitrary")),
        name="gqa_attention",
    )(*args)


GLA_Q = 0
GLA_K = B_HEADS * B_DK
GLA_V = 2 * B_HEADS * B_DK
GLA_R = GLA_V + B_HEADS * B_DV
GLA_W = GLA_R + B_HEADS * B_DV
GLA_GROUP = 8


def _gla_body(gc_ref, gl_ref, zc_ref, zl_ref, w2f_ref, w2b_ref, bf_ref, bb_ref, gg_ref,
              oc_ref, ol_ref, ofc_ref, ofl_ref, st_ref):
    nk = B_HEADS * B_DK
    ch = B_CHUNK
    row = lax.broadcasted_iota(jnp.int32, (ch, ch), 0)
    col = lax.broadcasted_iota(jnp.int32, (ch, ch), 1)
    lane = lax.broadcasted_iota(jnp.int32, (ch, nk), 1)
    head_masks = [(lane // B_DK) == h for h in range(B_HEADS)]

    def group_out(g_ref, z_ref, r0s, fwd, s_t):
        w2 = (w2f_ref if fwd else w2b_ref)[...]
        bias = (bf_ref if fwd else bb_ref)[...]
        keep = (row >= col) if fwd else (row <= col)
        tri = jnp.where(keep, 1.0, 0.0).astype(BF)
        blks = [g_ref[pl.ds(r0, ch), :] for r0 in r0s]
        pres = [_dot(z_ref[pl.ds(r0, ch), :], w2) + bias for r0 in r0s]
        las = [-(jnp.maximum(-p, 0.0) + jnp.log(1.0 + jnp.exp(-jnp.abs(p)))) * (1.0 / B_TAU) for p in pres]
        bcums = [_split_dot(tri, la) for la in las]
        qms, kis, kss, decs = [], [], [], []
        for blk, bcum in zip(blks, bcums):
            q = blk[:, GLA_Q:GLA_Q + nk].astype(F32)
            k = blk[:, GLA_K:GLA_K + nk].astype(F32)
            b_last = bcum[ch - 1:ch, :] if fwd else bcum[0:1, :]
            q_dec = q * jnp.exp(bcum)
            k_state = k * jnp.exp(b_last - bcum)
            qms.append([jnp.where(m, q_dec, 0.0).astype(BF) for m in head_masks])
            kis.append((k * jnp.exp(-bcum)).astype(BF))
            kss.append([jnp.where(m, k_state, 0.0).astype(BF) for m in head_masks])
            decs.append(jnp.exp(b_last))
        vs = [[blk[:, GLA_V + B_DV * h:GLA_V + B_DV * (h + 1)] for h in range(B_HEADS)] for blk in blks]
        atts = [[jnp.where(keep, _dot_nt(qm, ki), 0.0).astype(BF) for qm in qmu] for qmu, ki in zip(qms, kis)]
        o_intra = [[_dot(a, v) for a, v in zip(au, vu)] for au, vu in zip(atts, vs)]
        d_ss = []
        for vu, ksu in zip(vs, kss):
            d_s = _dot_tn(vu[0], ksu[0])
            for h in range(1, B_HEADS):
                d_s = d_s + _dot_tn(vu[h], ksu[h])
            d_ss.append(d_s)
        outs = []
        for u in range(len(r0s)):
            s_b = s_t.astype(BF)
            outs.append(jnp.concatenate(
                [o_intra[u][h] + _dot_nt(qms[u][h], s_b) for h in range(B_HEADS)], axis=1))
            s_t = decs[u] * s_t + d_ss[u]
        return outs, blks, s_t

    def fwd_pass(g_ref, z_ref, of_ref, n):
        grp = min(GLA_GROUP, n)
        assert n % grp == 0

        def body(i, carry):
            r0s = [pl.multiple_of((i * grp + u) * ch, ch) for u in range(grp)]
            outs, _, s_t = group_out(g_ref, z_ref, r0s, True, st_ref[...])
            for r0, o in zip(r0s, outs):
                of_ref[pl.ds(r0, ch), :] = o
            st_ref[...] = s_t
            return carry
        lax.fori_loop(0, n // grp, body, 0)

    def bwd_pass(g_ref, z_ref, of_ref, o_ref, n):
        grp = min(GLA_GROUP, n)
        assert n % grp == 0

        def body(i, carry):
            r0s = [pl.multiple_of((n - 1 - (i * grp + u)) * ch, ch) for u in range(grp)]
            outs, blks, s_t = group_out(g_ref, z_ref, r0s, False, st_ref[...])
            for r0, o, blk in zip(r0s, outs, blks):
                o = o + of_ref[pl.ds(r0, ch), :]
                parts = []
                for h in range(B_HEADS):
                    oh = o[:, B_DV * h:B_DV * (h + 1)]
                    ms = jnp.mean(oh * oh, axis=-1, keepdims=True)
                    parts.append(oh * lax.rsqrt(ms + EPS))
                y = jnp.concatenate(parts, axis=1) * gg_ref[...]
                r = blk[:, GLA_R:GLA_R + B_HEADS * B_DV].astype(F32)
                o_ref[pl.ds(r0, ch), :] = (y * _silu(r)).astype(BF)
            st_ref[...] = s_t
            return carry
        lax.fori_loop(0, n // grp, body, 0)

    nc = gc_ref.shape[0] // ch
    nl = gl_ref.shape[0] // ch
    st_ref[...] = jnp.zeros_like(st_ref)
    fwd_pass(gc_ref, zc_ref, ofc_ref, nc)
    fwd_pass(gl_ref, zl_ref, ofl_ref, nl)
    st_ref[...] = jnp.zeros_like(st_ref)
    bwd_pass(gc_ref, zc_ref, ofc_ref, oc_ref, nc)
    bwd_pass(gl_ref, zl_ref, ofl_ref, ol_ref, nl)


def _gla(g_ctx, g_lat, z_ctx, z_lat, w2f, w2b, b_f, b_b, gg, bsz):
    nc = g_ctx.shape[0] // bsz
    sl = g_lat.shape[0] // bsz
    nk = B_HEADS * B_DK
    nv = B_HEADS * B_DV
    full = lambda shape: pl.BlockSpec(shape, lambda b: (0, 0))
    return pl.pallas_call(
        _gla_body,
        grid=(bsz,),
        in_specs=[pl.BlockSpec((nc, GLA_W), lambda b: (b, 0)),
                  pl.BlockSpec((sl, GLA_W), lambda b: (b, 0)),
                  pl.BlockSpec((nc, LANES), lambda b: (b, 0)),
                  pl.BlockSpec((sl, LANES), lambda b: (b, 0)),
                  full((LANES, nk)), full((LANES, nk)), full((1, nk)), full((1, nk)), full((1, nv))],
        out_specs=[pl.BlockSpec((nc, nv), lambda b: (b, 0)),
                   pl.BlockSpec((sl, nv), lambda b: (b, 0))],
        out_shape=[jax.ShapeDtypeStruct((g_ctx.shape[0], nv), BF),
                   jax.ShapeDtypeStruct((g_lat.shape[0], nv), BF)],
        scratch_shapes=[pltpu.VMEM((nc, nv), F32), pltpu.VMEM((sl, nv), F32),
                        pltpu.VMEM((B_DV, nk), F32)],
        compiler_params=_params(("arbitrary",)),
        name="gla_bidir",
    )(g_ctx, g_lat, z_ctx, z_lat, w2f, w2b, b_f, b_b, gg)


def _mixer_residual(x_ref, m_ref, pair_refs):
    acc = _dot(pair_refs[0][...], pair_refs[1][...])
    for t in range(1, len(pair_refs) // 2):
        acc = acc + _dot(pair_refs[2 * t][...], pair_refs[2 * t + 1][...])
    return x_ref[...] + m_ref[0, 2:3, :] * acc


def _pair_specs(pairs, tm, index):
    in_specs, args = [], []
    for a, w in pairs:
        in_specs += [pl.BlockSpec((tm, a.shape[1]), index), pl.BlockSpec(w.shape, lambda *_: (0, 0))]
        args += [a, w]
    return in_specs, args


def _ffn_body(x_ref, g_ref, m_ref, *rest, n_pairs):
    pair_refs = rest[:2 * n_pairs]
    wg_ref, wu_ref, wd_ref, o_ref, xn_ref, h_ref, acc_ref = rest[2 * n_pairs:]
    j = pl.program_id(1)

    def swiglu_step():
        h = h_ref[...]
        a = (_silu(_dot(h, wg_ref[...])) * _dot(h, wu_ref[...])).astype(BF)
        return _dot(a, wd_ref[...])

    @pl.when(j == 0)
    def _():
        xn = _mixer_residual(x_ref, m_ref, pair_refs)
        xn_ref[...] = xn
        h_ref[...] = _norm_mod(xn, g_ref[...], m_ref[0, 3:4, :], m_ref[0, 4:5, :]).astype(BF)
        acc_ref[...] = swiglu_step()

    last = pl.num_programs(1) - 1

    @pl.when((j > 0) & (j < last))
    def _():
        acc_ref[...] += swiglu_step()

    @pl.when(j == last)
    def _():
        o_ref[...] = xn_ref[...] + m_ref[0, 5:6, :] * (acc_ref[...] + swiglu_step())


def _ffn(x2, g, mpack, mod_of_tile, pairs, wg, wu, wd, tm, tf):
    rows = x2.shape[0]
    ff = wg.shape[1]
    assert ff // tf >= 2
    pair_specs, pair_args = _pair_specs(pairs, tm, lambda i, j: (i, 0))
    return pl.pallas_call(
        functools.partial(_ffn_body, n_pairs=len(pairs)),
        grid=(rows // tm, ff // tf),
        in_specs=[pl.BlockSpec((tm, D), lambda i, j: (i, 0)),
                  pl.BlockSpec((1, D), lambda i, j: (0, 0)),
                  pl.BlockSpec((1, MOD_ROWS, D), lambda i, j: (mod_of_tile(i), 0, 0)),
                  *pair_specs,
                  pl.BlockSpec((D, tf), lambda i, j: (0, j)),
                  pl.BlockSpec((D, tf), lambda i, j: (0, j)),
                  pl.BlockSpec((tf, D), lambda i, j: (j, 0))],
        out_specs=pl.BlockSpec((tm, D), lambda i, j: (i, 0)),
        out_shape=jax.ShapeDtypeStruct((rows, D), F32),
        scratch_shapes=[pltpu.VMEM((tm, D), F32), pltpu.VMEM((tm, D), BF), pltpu.VMEM((tm, D), F32)],
        compiler_params=_params(("arbitrary", "arbitrary")),
        name="ffn_swiglu",
    )(x2, g.reshape(1, D), mpack, *pair_args, wg, wu, wd)


NA_QROWS = 4
NA_KROWS = 12
NA_TQ = NA_QROWS * GRID_W
NA_HG = 16
NA_W = NA_HG * HD


def _na_body(q_ref, k0_ref, k1_ref, k2_ref, v0_ref, v1_ref, v2_ref, kc_ref, vc_ref, b_ref, o_ref):
    pair = LANES // HD
    lane = lax.broadcasted_iota(jnp.int32, (NA_TQ, LANES), 1)
    own = [(lane // HD) == half for half in range(pair)]

    def scores(hh):
        tile = slice(LANES * (hh // pair), LANES * (hh // pair + 1))
        q = jnp.where(own[hh % pair], q_ref[:, tile], 0.0)
        s_loc = jnp.concatenate([_dot_nt(q, kr[:, tile]) for kr in (k0_ref, k1_ref, k2_ref)], axis=1)
        bias = b_ref[0, :, hh].reshape(NA_TQ, NA_KROWS * GRID_W)
        return s_loc + bias, _dot_nt(q, kc_ref[:, tile])

    nxt = scores(0)
    outs = []
    for hh in range(NA_HG):
        s_loc, s_ctx = nxt
        if hh + 1 < NA_HG:
            nxt = scores(hh + 1)
        vcols = slice(LANES * hh, LANES * (hh + 1))
        m = jnp.maximum(jnp.max(s_loc, axis=1, keepdims=True), jnp.max(s_ctx, axis=1, keepdims=True))
        p_loc = jnp.exp2(s_loc - m).astype(BF)
        acc = _dot(jnp.exp2(s_ctx - m).astype(BF), vc_ref[:, vcols])
        for t, vr in enumerate((v0_ref, v1_ref, v2_ref)):
            acc = acc + _dot(p_loc[:, t * NA_TQ:(t + 1) * NA_TQ], vr[:, vcols])
        den_lane = HD * (1 - hh % pair)
        outs.append(acc / acc[:, den_lane:den_lane + 1])
        if len(outs) == pair:
            t0 = hh // pair
            o_ref[:, LANES * t0:LANES * (t0 + 1)] = jnp.where(own[0], outs[0], outs[1]).astype(BF)
            outs = []


def _na_bias(rpb, rows):
    heads, n_dr, n_dc = rpb.shape
    nblk = rows // NA_QROWS
    n_off = n_dr + 2 * NA_KROWS
    rpad = jnp.pad(rpb.astype(F32) * LOG2E, ((0, 0), (NA_KROWS, NA_KROWS), (0, 0)), constant_values=NEG_INF)
    rvals = jnp.repeat(rpad.transpose(0, 2, 1), GRID_W, axis=2)
    cuts = []
    for kind, j in enumerate((0, min(1, nblk - 1), nblk - 1)):
        ws = NA_QROWS * int(np.clip(j - 1, 0, nblk - 3))
        for qr in range(NA_QROWS):
            r = NA_QROWS * j + qr
            rs = int(np.clip(r - NA_ROWS // 2, 0, rows - NA_ROWS))
            cuts.append((kind, qr, NA_KROWS + ws - r + NA_ROWS - 1, max(rs - ws, 0), min(rs + NA_ROWS - ws, NA_KROWS)))
    width = NA_KROWS * GRID_W

    def body(r_ref, o_ref):
        shape = (GRID_W, n_off * GRID_W)
        qc = lax.broadcasted_iota(jnp.int32, shape, 0)
        kc = lax.broadcasted_iota(jnp.int32, shape, 1) % GRID_W
        delta = kc - qc + NA_COLS - 1
        cs = jnp.clip(qc - NA_COLS // 2, 0, GRID_W - NA_COLS)
        toe = jnp.full(shape, NEG_INF, F32)
        for d in range(n_dc):
            toe = jnp.where(delta == d, r_ref[0, d:d + 1, :], toe)
        toe = jnp.where((kc >= cs) & (kc < cs + NA_COLS), toe, NEG_INF)
        key_row = lax.broadcasted_iota(jnp.int32, (GRID_W, width), 1) // GRID_W
        for kind, qr, d0, lo, hi in cuts:
            blk = toe[:, d0 * GRID_W:d0 * GRID_W + width]
            o_ref[kind, qr, 0] = jnp.where((key_row >= lo) & (key_row < hi), blk, NEG_INF)

    return pl.pallas_call(
        body,
        grid=(heads,),
        in_specs=[pl.BlockSpec((1, n_dc, n_off * GRID_W), lambda h: (h, 0, 0))],
        out_specs=pl.BlockSpec((3, NA_QROWS, 1, GRID_W, width), lambda h: (0, 0, h, 0, 0)),
        out_shape=jax.ShapeDtypeStruct((3, NA_QROWS, heads, GRID_W, width), F32),
        compiler_params=_params(("arbitrary",)),
        name="natten_bias",
    )(rvals)


def _natten(q, k, v, k_ctx, v_ctx, bias, bsz):
    s = q.shape[0] // bsz
    nc = k_ctx.shape[0] // bsz
    nblk = s // NA_TQ
    ngrp = C_HEADS // NA_HG
    v_w = NA_HG * LANES

    def kv_spec(t, width):
        return pl.BlockSpec((NA_TQ, width),
                            lambda j, g, b: (b * nblk + jnp.clip(j - 1, 0, nblk - 3) + t, g))

    def kind(j):
        return jnp.where(j == 0, 0, jnp.where(j == nblk - 1, 2, 1))

    return pl.pallas_call(
        _na_body,
        grid=(nblk, ngrp, bsz),
        in_specs=[pl.BlockSpec((NA_TQ, NA_W), lambda j, g, b: (b * nblk + j, g)),
                  kv_spec(0, NA_W), kv_spec(1, NA_W), kv_spec(2, NA_W),
                  kv_spec(0, v_w), kv_spec(1, v_w), kv_spec(2, v_w),
                  pl.BlockSpec((nc, NA_W), lambda j, g, b: (b, g)),
                  pl.BlockSpec((nc, v_w), lambda j, g, b: (b, g)),
                  pl.BlockSpec((1, NA_QROWS, NA_HG, GRID_W, NA_KROWS * GRID_W),
                               lambda j, g, b: (kind(j), 0, g, 0, 0))],
        out_specs=pl.BlockSpec((NA_TQ, NA_W), lambda j, g, b: (b * nblk + j, g)),
        out_shape=jax.ShapeDtypeStruct(q.shape, BF),
        compiler_params=_params(("arbitrary", "arbitrary", "arbitrary")),
        name="natten",
    )(q, k, k, k, v, v, v, k_ctx, v_ctx, bias)


MOE_TM = 512
MOE_TF = 1792
META_E1, META_E2, META_R1, META_R2, META_G1, META_G2 = range(6)
ROW_TILE = D // LANES


def _to_token_tiles(dst_ref, value):
    n = value.shape[0]
    for s in range(ROW_TILE):
        dst_ref[pl.ds(s, n, stride=ROW_TILE), :] = value[:, LANES * s:LANES * (s + 1)]


def _from_token_tiles(src_ref, n):
    return jnp.concatenate([src_ref[pl.ds(s, n, stride=ROW_TILE), :] for s in range(ROW_TILE)], axis=1)


def _route_body(x_ref, g_ref, m_ref, *rest, n_pairs):
    pair_refs = rest[:2 * n_pairs]
    wr_ref, xn_ref, h_ref, meta_ref, rec_ref, cnt_ref, tri_ref, carry_ref = rest[2 * n_pairs:]
    tm = x_ref.shape[0]

    @pl.when(pl.program_id(0) == 0)
    def _():
        row = lax.broadcasted_iota(jnp.int32, (tm, tm), 0)
        col = lax.broadcasted_iota(jnp.int32, (tm, tm), 1)
        tri_ref[...] = jnp.where(row >= col, 1.0, 0.0).astype(BF)
        carry_ref[...] = jnp.zeros_like(carry_ref)

    xn = _mixer_residual(x_ref, m_ref, pair_refs)
    xn_ref[...] = xn
    h = _norm_mod(xn, g_ref[...], m_ref[0, 3:4, :], m_ref[0, 4:5, :])
    _to_token_tiles(h_ref, h)
    h_hi = h.astype(BF)
    h_lo = (h - h_hi.astype(F32)).astype(BF)
    w = wr_ref[...]
    w_hi = w.astype(BF).astype(F32)
    w_cat = (w_hi + pltpu.roll(w - w_hi, N_EXPERTS, 1)).astype(BF)
    terms = _dot(h_hi, w_cat) + _dot(h_lo, w_cat)
    logits = terms + pltpu.roll(terms, LANES - N_EXPERTS, 1)
    lane = lax.broadcasted_iota(jnp.int32, logits.shape, 1)
    lg = jnp.where(lane < N_EXPERTS, logits, NEG_INF)
    m1 = jnp.max(lg, axis=1, keepdims=True)
    i1 = jnp.min(jnp.where(lg == m1, lane, LANES), axis=1, keepdims=True)
    lg2 = jnp.where(lane == i1, NEG_INF, lg)
    m2 = jnp.max(lg2, axis=1, keepdims=True)
    i2 = jnp.min(jnp.where(lg2 == m2, lane, LANES), axis=1, keepdims=True)
    e = jnp.exp(m2 - m1)
    g1 = 1.0 / (1.0 + e)
    g2 = e / (1.0 + e)
    oh = jnp.where(lane == i1, 1.0, 0.0) + jnp.where(lane == i2, 1.0, 0.0)
    cum = _dot(tri_ref[...], oh.astype(BF))
    base = carry_ref[0:1, :] + cum - oh
    r1 = jnp.sum(jnp.where(lane == i1, base, 0.0), axis=1, keepdims=True)
    r2 = jnp.sum(jnp.where(lane == i2, base, 0.0), axis=1, keepdims=True)
    carry_ref[...] = carry_ref[...] + cum[tm - 1:tm, :]
    cnt_ref[...] = carry_ref[...]
    meta = jnp.zeros(logits.shape, F32)
    for ln, val in ((META_E1, i1.astype(F32)), (META_E2, i2.astype(F32)), (META_R1, r1), (META_R2, r2),
                    (META_G1, g1), (META_G2, g2)):
        meta = jnp.where(lane == ln, val, meta)
    meta_ref[...] = meta
    rec_ref[...] = meta.T[:rec_ref.shape[0], :]


def _route(x2, g, mpack, mod_of_tile, pairs, wr, tm):
    rows = x2.shape[0]
    pair_specs, pair_args = _pair_specs(pairs, tm, lambda i: (i, 0))
    return pl.pallas_call(
        functools.partial(_route_body, n_pairs=len(pairs)),
        grid=(rows // tm,),
        in_specs=[pl.BlockSpec((tm, D), lambda i: (i, 0)),
                  pl.BlockSpec((1, D), lambda i: (0, 0)),
                  pl.BlockSpec((1, MOD_ROWS, D), lambda i: (mod_of_tile(i), 0, 0)),
                  *pair_specs,
                  pl.BlockSpec((D, LANES), lambda i: (0, 0))],
        out_specs=[pl.BlockSpec((tm, D), lambda i: (i, 0)),
                   pl.BlockSpec((tm * ROW_TILE, LANES), lambda i: (i, 0)),
                   pl.BlockSpec((tm, LANES), lambda i: (i, 0)),
                   pl.BlockSpec((8, tm), lambda i: (0, i)),
                   pl.BlockSpec((8, LANES), lambda i: (0, 0))],
        out_shape=[jax.ShapeDtypeStruct((rows, D), F32),
                   jax.ShapeDtypeStruct((rows * ROW_TILE, LANES), F32),
                   jax.ShapeDtypeStruct((rows, LANES), F32),
                   jax.ShapeDtypeStruct((8, rows), F32),
                   jax.ShapeDtypeStruct((8, LANES), F32)],
        scratch_shapes=[pltpu.VMEM((tm, tm), BF), pltpu.VMEM((8, LANES), F32)],
        compiler_params=_params(("arbitrary",)),
        name="moe_route",
    )(x2, g.reshape(1, D), mpack, *pair_args, wr)


def _token_copy(src_ref, src_tok, dst_ref, dst_tok, sem):
    src = src_ref.at[pl.ds(pl.multiple_of(src_tok * ROW_TILE, ROW_TILE), ROW_TILE), :]
    dst = dst_ref.at[pl.ds(pl.multiple_of(dst_tok * ROW_TILE, ROW_TILE), ROW_TILE), :]
    return pltpu.make_async_copy(src, dst, sem)


def _experts_body(te_ref, nu_ref, cur_ref, nxt_ref, dprev_ref, h_hbm, wg_ref, wu_ref, wd_ref, y_hbm,
                  xbuf, h_ref, acc_ref, ybuf, gsem, ssem, *, n_steps):
    assert n_steps >= 2
    i = pl.program_id(0)
    j = pl.program_id(1)
    used = i < nu_ref[0]
    slot = i % 2
    other = 1 - slot

    def gather_row(idx_ref, r, sl):
        _token_copy(h_hbm, idx_ref[0, 0, r], xbuf.at[sl], r, gsem.at[sl]).start()

    def issue_gathers():
        for r in range(MOE_TM):
            gather_row(nxt_ref, r, other)

    def issue_scatters():
        for r in range(MOE_TM):
            _token_copy(ybuf.at[other], r, y_hbm, dprev_ref[0, 0, r], ssem.at[other]).start()

    def wait_rows(buf, sem):
        pltpu.make_async_copy(buf, buf, sem).wait()

    def swiglu_step():
        h = h_ref[...]
        a = (_silu(_dot(h, wg_ref[0])) * _dot(h, wu_ref[0])).astype(BF)
        return _dot(a, wd_ref[0])

    @pl.when((j == 0) & (i == 0))
    def _():
        ybuf[...] = jnp.zeros_like(ybuf)
        n_spare = MOE_TM * ROW_TILE
        for sl in range(2):
            spare = pltpu.make_async_copy(
                ybuf.at[sl], y_hbm.at[pl.ds(y_hbm.shape[0] - (2 - sl) * n_spare, n_spare), :], ssem.at[sl])
            spare.start()
            spare.wait()

        def body(r, c):
            gather_row(cur_ref, r, 0)
            return c
        lax.fori_loop(0, MOE_TM, body, 0, unroll=8)

    @pl.when(j == 0)
    def _():
        wait_rows(xbuf.at[slot], gsem.at[slot])

    @pl.when(used & (j == 0))
    def _():
        h_ref[...] = _from_token_tiles(xbuf.at[slot], MOE_TM).astype(BF)
        acc_ref[...] = swiglu_step()
        issue_gathers()

    if n_steps > 2:
        @pl.when(used & (j > 0) & (j < n_steps - 1))
        def _():
            acc_ref[...] += swiglu_step()

    @pl.when(used & (j == n_steps - 1))
    def _():
        acc_ref[...] += swiglu_step()
        issue_scatters()

    @pl.when(jnp.logical_not(used) & (j == 0))
    def _():
        issue_gathers()

    @pl.when(jnp.logical_not(used) & (j == n_steps - 1))
    def _():
        issue_scatters()

    @pl.when(j == n_steps - 1)
    def _():
        @pl.when(i > 0)
        def _():
            wait_rows(ybuf.at[slot], ssem.at[slot])

        @pl.when(used)
        def _():
            _to_token_tiles(ybuf.at[slot], acc_ref[...])

        @pl.when(i == pl.num_programs(0) - 1)
        def _():
            wait_rows(ybuf.at[other], ssem.at[other])
            wait_rows(xbuf.at[other], gsem.at[other])


def _experts(h_tiles, tok_of_slot, dest_of_slot, tile_expert, n_used, wg, wu, wd, out_tokens):
    n_tiles = tile_expert.shape[0]
    ff = wg.shape[2]
    nf = ff // MOE_TF
    spare_rows = out_tokens - MOE_TM + jnp.arange(MOE_TM, dtype=jnp.int32)
    dest_prev = jnp.concatenate([spare_rows, dest_of_slot[:-MOE_TM]])

    def jj(i, j, nu):
        return jnp.where(i < nu[0], j, nf - 1)

    def smem_rows(index):
        return pl.BlockSpec((1, 1, MOE_TM), lambda i, j, te, nu: (index(i), 0, 0), memory_space=pltpu.SMEM)

    grid_spec = pltpu.PrefetchScalarGridSpec(
        num_scalar_prefetch=2,
        grid=(n_tiles, nf),
        in_specs=[smem_rows(lambda i: i),
                  smem_rows(lambda i: jnp.minimum(i + 1, n_tiles - 1)),
                  smem_rows(lambda i: i),
                  pl.BlockSpec(memory_space=pl.ANY),
                  pl.BlockSpec((1, D, MOE_TF), lambda i, j, te, nu: (te[i], 0, jj(i, j, nu))),
                  pl.BlockSpec((1, D, MOE_TF), lambda i, j, te, nu: (te[i], 0, jj(i, j, nu))),
                  pl.BlockSpec((1, MOE_TF, D), lambda i, j, te, nu: (te[i], jj(i, j, nu), 0))],
        out_specs=pl.BlockSpec(memory_space=pl.ANY),
        scratch_shapes=[pltpu.VMEM((2, MOE_TM * ROW_TILE, LANES), F32),
                        pltpu.VMEM((MOE_TM, D), BF),
                        pltpu.VMEM((MOE_TM, D), F32),
                        pltpu.VMEM((2, MOE_TM * ROW_TILE, LANES), F32),
                        pltpu.SemaphoreType.DMA((2,)),
                        pltpu.SemaphoreType.DMA((2,))])
    tok3 = tok_of_slot.reshape(n_tiles, 1, MOE_TM)
    return pl.pallas_call(
        functools.partial(_experts_body, n_steps=nf),
        grid_spec=grid_spec,
        out_shape=jax.ShapeDtypeStruct((out_tokens * ROW_TILE, LANES), F32),
        compiler_params=_params(("arbitrary", "arbitrary")),
        name="moe_experts",
    )(tile_expert, n_used, tok3, tok3, dest_prev.reshape(n_tiles, 1, MOE_TM), h_tiles, wg, wu, wd)


def _combine_body(y1_ref, y2_ref, x_ref, m_ref, meta_ref, gf_ref, o_ref):
    n = x_ref.shape[0]
    meta = meta_ref[...]
    lane = lax.broadcasted_iota(jnp.int32, meta.shape, 1)
    g1 = jnp.sum(jnp.where(lane == META_G1, meta, 0.0), axis=1, keepdims=True)
    g2 = jnp.sum(jnp.where(lane == META_G2, meta, 0.0), axis=1, keepdims=True)
    moe = g1 * _from_token_tiles(y1_ref, n) + g2 * _from_token_tiles(y2_ref, n)
    y = x_ref[...] + m_ref[0, 5:6, :] * moe
    ms = jnp.mean(y * y, axis=-1, keepdims=True)
    o_ref[...] = y * lax.rsqrt(ms + EPS) * gf_ref[...]


def _combine(x2, mpack, mod_of_row, meta, y_tiles, g_final, tm):
    rows = x2.shape[0]
    nt = rows // tm
    return pl.pallas_call(
        _combine_body,
        grid=(nt,),
        in_specs=[pl.BlockSpec((tm * ROW_TILE, LANES), lambda i: (i, 0)),
                  pl.BlockSpec((tm * ROW_TILE, LANES), lambda i: (i + nt, 0)),
                  pl.BlockSpec((tm, D), lambda i: (i, 0)),
                  pl.BlockSpec((1, MOD_ROWS, D), lambda i: (mod_of_row(i * tm), 0, 0)),
                  pl.BlockSpec((tm, LANES), lambda i: (i, 0)),
                  pl.BlockSpec((1, D), lambda i: (0, 0))],
        out_specs=pl.BlockSpec((tm, D), lambda i: (i, 0)),
        out_shape=jax.ShapeDtypeStruct((rows, D), F32),
        compiler_params=_params(("arbitrary",)),
        name="moe_combine",
    )(y_tiles, y_tiles, x2, mpack, meta, g_final.reshape(1, D))


def _moe(x2, g, mpack, mod_of_tile, mod_of_row, pairs, wr, wg, wu, wd, g_final, tm):
    rows = x2.shape[0]
    ne = wg.shape[0]
    x2, h_tiles, meta, rec, cnt = _route(x2, g, mpack, mod_of_tile, pairs, wr, tm)
    counts = cnt[0, :ne].astype(jnp.int32)
    tiles_e = (counts + MOE_TM - 1) // MOE_TM
    tile_end = jnp.cumsum(tiles_e)
    group_off = (tile_end - tiles_e) * MOE_TM
    n_tiles = 2 * rows // MOE_TM + ne
    n_slots = n_tiles * MOE_TM
    tile_expert = jnp.minimum(
        jnp.sum(jnp.arange(n_tiles)[:, None] >= tile_end[None, :], axis=1), ne - 1).astype(jnp.int32)
    n_used = tile_end[-1:].astype(jnp.int32)
    eids = jnp.arange(ne)

    def slot_of(lane_e, lane_r):
        e = rec[lane_e].astype(jnp.int32)
        off = jnp.sum(jnp.where(e[None, :] == eids[:, None], group_off[:, None], 0), axis=0)
        return off + rec[lane_r].astype(jnp.int32)

    tok = jnp.arange(rows, dtype=jnp.int32)
    dest = jnp.full((n_slots,), -1, jnp.int32).at[
        jnp.concatenate([slot_of(META_E1, META_R1), slot_of(META_E2, META_R2)])].set(
        jnp.concatenate([tok, rows + tok]), unique_indices=True, mode="promise_in_bounds")
    pad = dest < 0
    dest_of_slot = jnp.where(pad, 2 * rows + jnp.arange(n_slots, dtype=jnp.int32) % (2 * MOE_TM), dest)
    tok_of_slot = jnp.where(pad, 0, jnp.where(dest >= rows, dest - rows, dest))
    y_tiles = _experts(h_tiles, tok_of_slot, dest_of_slot, tile_expert, n_used, wg, wu, wd, 2 * rows + 2 * MOE_TM)
    return _combine(x2, mpack, mod_of_row, meta, y_tiles, g_final, tm)


def _rope_tables(seq_len):
    t = np.arange(seq_len)
    row = (t // GRID_W).astype(np.float32)
    col = (t % GRID_W).astype(np.float32)
    n_axis = HD // 4
    inv = jnp.power(ROPE_THETA, -jnp.arange(n_axis, dtype=F32) / n_axis)
    ang = jnp.concatenate([jnp.asarray(row)[:, None] * inv, jnp.asarray(col)[:, None] * inv], axis=-1)
    cos, sin = jnp.cos(ang), jnp.sin(ang)
    ct = jnp.tile(jnp.concatenate([cos, cos], axis=-1), (1, LANES // HD))
    st = jnp.tile(jnp.concatenate([-sin, sin], axis=-1), (1, LANES // HD))
    return ct, st


def kernel(x, c, ctx, c_ctx, w_mod, b_mod, g_norm1, g_norm2, w_in_ab, g_q, g_k, w_a2_f, b_a_f, w_a2_b,
           b_a_b, g_gla, w_out_ab, w_ff_gate, w_ff_up, w_ff_down, w_in_c, rpb_c, w_out_c, w_router,
           w_moe_gate, w_moe_up, w_moe_down, g_final):
    bsz, seq, _ = x.shape
    nc = ctx.shape[1]
    depth = w_mod.shape[0]
    assert depth == 2 and seq % GRID_W == 0

    cond_rows = -(-(bsz + 1) // 8) * 8
    cond = jnp.zeros((cond_rows, D), F32).at[:bsz].set(c).at[bsz].set(c_ctx)
    mods = _adaln(cond, w_mod, b_mod).reshape(depth, cond_rows, N_MOD, D)
    mods = jnp.pad(mods, ((0, 0), (0, 0), (0, MOD_ROWS - N_MOD), (0, 0)))

    xl = x.reshape(bsz * seq, D)
    xc = ctx.reshape(bsz * nc, D)
    tm_l = min(ROW_TM, seq)
    tm_c = min(ROW_TM, bsz * nc)
    tpb = seq // tm_l
    lat_mod = lambda i: i // tpb
    ctx_mod = lambda i: bsz

    mp = mods[0]
    w_in = w_in_ab[0]
    perm = np.concatenate([np.arange(0, HD, 2), np.arange(1, HD, 2)])
    nq, nkv = A_HEADS * HD, A_KV * HD
    o_q, o_k, o_v = 0, nq, nq + nkv
    o_bq = o_v + nkv
    o_bk = o_bq + B_HEADS * B_DK
    o_bv = o_bk + B_HEADS * B_DK
    o_br = o_bv + B_HEADS * B_DV
    o_zf = o_br + B_HEADS * B_DV
    o_end = o_zf + 2 * B_RANK
    assert A_KV * HD == LANES
    q_order = np.arange(A_HEADS).reshape(A_KV, A_GROUP).T.reshape(-1)
    wq = w_in[:, o_q:o_k].reshape(D, A_HEADS, HD)[:, q_order][:, :, perm].reshape(D, nq)
    wk = w_in[:, o_k:o_v].reshape(D, A_KV, HD)[:, :, perm].reshape(D, nkv)
    wz = jnp.pad(w_in[:, o_zf:o_end], ((0, 0), (0, LANES - 2 * B_RANK)))
    w0 = jnp.concatenate([wq, wk, w_in[:, o_v:o_bq], w_in[:, o_bq:o_bk] * (B_DK ** -0.5),
                          w_in[:, o_bk:o_zf], wz], axis=1).astype(BF)
    widths0 = (nq + nkv, nkv, GLA_W, LANES)
    outw0 = (nq, nkv, 2 * nkv, GLA_W, LANES)
    gqk = jnp.concatenate([jnp.tile(g_q[0][perm] * (HD ** -0.5 * LOG2E), A_HEADS),
                           jnp.tile(g_k[0][perm], A_KV)]).reshape(1, nq + nkv)
    bd = jnp.asarray(np.tile(np.kron(np.eye(LANES // HD), np.ones((HD, HD))), (2, 1)), BF)
    ct, st = _rope_tables(seq)
    ct_c = jnp.ones((tm_c, LANES), F32)
    st_c = jnp.zeros((tm_c, LANES), F32)

    kinds0 = ("qk", "values", "plain", "plain")
    ql, kl, vl, gl, zl = _proj(xl, g_norm1[0], mp, lat_mod, w0, widths0, kinds0, outw0, tm_l, (ct, st, gqk, bd))
    qc, kc, vc, gc, zc = _proj(xc, g_norm1[0], mp, ctx_mod, w0, widths0, kinds0, outw0, tm_c,
                               (ct_c, st_c, gqk, bd))

    oa_l = _gqa(ql, kc, vc, kl, vl, bsz, min(GQA_TQ, seq))
    oa_c = _gqa(qc, kc, vc, None, None, bsz, min(GQA_TQ, nc))

    w2f = jnp.zeros((LANES, B_HEADS * B_DK), F32).at[:B_RANK].set(w_a2_f[0]).astype(BF)
    w2b = jnp.zeros((LANES, B_HEADS * B_DK), F32).at[B_RANK:2 * B_RANK].set(w_a2_b[0]).astype(BF)
    ob_c, ob_l = _gla(gc, gl, zc, zl, w2f, w2b, b_a_f[0].reshape(1, -1), b_a_b[0].reshape(1, -1),
                      jnp.tile(g_gla[0], B_HEADS).reshape(1, -1), bsz)

    wo = w_out_ab[0].astype(BF)
    wo_a = wo[:nq].reshape(A_HEADS, HD, D)[q_order].reshape(nq, D)
    wo_b = wo[nq:]
    wg, wu, wd = w_ff_gate[0].astype(BF), w_ff_up[0].astype(BF), w_ff_down[0].astype(BF)
    xl = _ffn(xl, g_norm2[0], mp, lat_mod, [(oa_l, wo_a), (ob_l, wo_b)], wg, wu, wd, tm_l, FFN_TF)
    xc = _ffn(xc, g_norm2[0], mp, ctx_mod, [(oa_c, wo_a), (ob_c, wo_b)], wg, wu, wd, tm_c, FFN_TF)

    mp = mods[1]
    cw = C_HEADS * HD
    w1 = w_in_c[0].astype(BF)
    q1, k1, v1 = _proj(xl, g_norm1[1], mp, lat_mod, w1, (cw, cw, cw), ("plain", "plain", "values"),
                       (cw, cw, 2 * cw), tm_l, scales=(HD ** -0.5 * LOG2E, 1.0, 1.0))
    k1c, v1c = _proj(xc, g_norm1[1], mp, ctx_mod, w1, (cw, cw), ("plain", "values"), (cw, 2 * cw), tm_c,
                     col0=cw)

    bias = _na_bias(rpb_c[0], seq // GRID_W)
    o1 = _natten(q1, k1, v1, k1c, v1c, bias, bsz)

    wr = jnp.pad(w_router[0], ((0, 0), (0, LANES - N_EXPERTS)))
    out = _moe(xl, g_norm2[1], mp, lat_mod, lambda r: r // seq, [(o1, w_out_c[0].astype(BF))], wr,
               w_moe_gate[0].astype(BF), w_moe_up[0].astype(BF), w_moe_down[0].astype(BF), g_final, tm_l)
    return out.reshape(bsz, seq, D)
```

```python
import functools

import numpy as np
import jax
import jax.numpy as jnp
from jax import lax
from jax.experimental import pallas as pl
from jax.experimental.pallas import tpu as pltpu

F32 = jnp.float32
BF = jnp.bfloat16

D = 1024
EPS = 1e-6
N_MOD = 6
MOD_ROWS = 8
GRID_W = 64
HD = 64
A_HEADS = 8
A_KV = 2
A_GROUP = A_HEADS // A_KV
ROPE_THETA = 10000.0
LOG2E = 1.4426950408889634
B_HEADS = 4
B_DK = 64
B_DV = 128
B_RANK = 16
B_TAU = 16.0
B_CHUNK = 64
C_HEADS = 16
NA_ROWS = 8
NA_COLS = 16
N_EXPERTS = 8
LANES = 128
VMEM_LIMIT = 56 * 2 ** 20
ROW_TM = 1024
FFN_TF = 512

NEG_INF = float("-inf")


def _params(sem):
    return pltpu.CompilerParams(dimension_semantics=sem, vmem_limit_bytes=VMEM_LIMIT)


def _dot(a, b):
    return jnp.dot(a, b, preferred_element_type=F32)


def _dot_nt(a, b):
    return lax.dot_general(a, b, (((1,), (1,)), ((), ())), preferred_element_type=F32)


def _dot_tn(a, b):
    return lax.dot_general(a, b, (((0,), (0,)), ((), ())), preferred_element_type=F32)


def _silu(x):
    return x / (1.0 + jnp.exp(-x))


def _split_dot(lhs_bf_exact, x):
    hi = x.astype(BF)
    lo = (x - hi.astype(F32)).astype(BF)
    return _dot(lhs_bf_exact, hi) + _dot(lhs_bf_exact, lo)


def _norm_mod(x, g, shift, scale):
    ms = jnp.mean(x * x, axis=-1, keepdims=True)
    return (x * lax.rsqrt(ms + EPS) * g) * (1.0 + scale) + shift


def _adaln_body(c_ref, w_ref, b_ref, o_ref):
    s = _silu(c_ref[...])
    o_ref[0] = _dot(s.astype(BF), w_ref[0].astype(BF)) + b_ref[0]


def _adaln(cond, w_mod, b_mod):
    depth, _, n = w_mod.shape
    rows = cond.shape[0]
    tn = n // 4
    return pl.pallas_call(
        _adaln_body,
        grid=(depth, n // tn),
        in_specs=[pl.BlockSpec((rows, D), lambda l, j: (0, 0)),
                  pl.BlockSpec((1, D, tn), lambda l, j: (l, 0, j)),
                  pl.BlockSpec((1, 1, tn), lambda l, j: (l, 0, j))],
        out_specs=pl.BlockSpec((1, rows, tn), lambda l, j: (l, 0, j)),
        out_shape=jax.ShapeDtypeStruct((depth, rows, n), F32),
        compiler_params=_params(("arbitrary", "arbitrary")),
        name="adaln",
    )(cond, w_mod, b_mod.reshape(depth, 1, n))


def _qk_norm_rope(p, gqk, bd, ct, st):
    pp = p * p
    sums = []
    for c in range(p.shape[1] // LANES):
        x = pp[:, c * LANES:(c + 1) * LANES]
        hi = x.astype(BF)
        lo = (x - hi.astype(F32)).astype(BF)
        sums.append(_dot(jnp.concatenate([hi, lo], axis=1), bd))
    ss = jnp.concatenate(sums, axis=1)
    y = p * lax.rsqrt(ss * (1.0 / HD) + EPS) * gqk
    lane = lax.broadcasted_iota(jnp.int32, ct.shape, 1)
    first = (lane % HD) < (HD // 2)
    outs = []
    for c in range(p.shape[1] // LANES):
        yc = y[:, c * LANES:(c + 1) * LANES]
        partner = jnp.where(first, pltpu.roll(yc, LANES - HD // 2, 1), pltpu.roll(yc, HD // 2, 1))
        outs.append(yc * ct + partner * st)
    return jnp.concatenate(outs, axis=1)


def _values_with_ones(p):
    lane = lax.broadcasted_iota(jnp.int32, (p.shape[0], LANES), 1)
    low = lane < HD
    tiles = []
    for c in range(p.shape[1] // LANES):
        x = p[:, c * LANES:(c + 1) * LANES]
        tiles += [jnp.where(low, x, 1.0), jnp.where(low, 1.0, x)]
    return jnp.concatenate(tiles, axis=1)


def _proj_body(x_ref, g_ref, m_ref, w_ref, *rest, widths, kinds, scales, col0):
    if "qk" in kinds:
        ct_ref, st_ref, gqk_ref, bd_ref = rest[:4]
        o_refs = rest[4:]
    else:
        o_refs = rest
    h = _norm_mod(x_ref[...], g_ref[...], m_ref[0, 0:1, :], m_ref[0, 1:2, :]).astype(BF)
    off = col0
    oi = 0
    for wd, kind, scale in zip(widths, kinds, scales):
        p = _dot(h, w_ref[:, off:off + wd])
        off += wd
        if scale != 1.0:
            p = p * scale
        if kind == "qk":
            p = _qk_norm_rope(p, gqk_ref[...], bd_ref[...], ct_ref[...], st_ref[...])
            nq = A_HEADS * HD
            o_refs[oi][...] = p[:, :nq].astype(BF)
            o_refs[oi + 1][...] = p[:, nq:].astype(BF)
            oi += 2
        else:
            if kind == "values":
                p = _values_with_ones(p)
            o_refs[oi][...] = p.astype(BF)
            oi += 1


def _proj(x2, g, mpack, mod_of_tile, w, widths, kinds, out_widths, tm, rope_args=None, scales=None, col0=0):
    rows = x2.shape[0]
    n = w.shape[1]
    in_specs = [pl.BlockSpec((tm, D), lambda i: (i, 0)),
                pl.BlockSpec((1, D), lambda i: (0, 0)),
                pl.BlockSpec((1, MOD_ROWS, D), lambda i: (mod_of_tile(i), 0, 0)),
                pl.BlockSpec((D, n), lambda i: (0, 0))]
    args = [x2, g.reshape(1, D), mpack, w]
    if rope_args is not None:
        ct, st, gqk, bd = rope_args
        nt = ct.shape[0] // tm
        in_specs += [pl.BlockSpec((tm, LANES), lambda i: (i % nt, 0)),
                     pl.BlockSpec((tm, LANES), lambda i: (i % nt, 0)),
                     pl.BlockSpec(gqk.shape, lambda i: (0, 0)),
                     pl.BlockSpec(bd.shape, lambda i: (0, 0))]
        args += [ct, st, gqk, bd]
    return pl.pallas_call(
        functools.partial(_proj_body, widths=widths, kinds=kinds, scales=scales or (1.0,) * len(widths),
                          col0=col0),
        grid=(rows // tm,),
        in_specs=in_specs,
        out_specs=[pl.BlockSpec((tm, ow), lambda i: (i, 0)) for ow in out_widths],
        out_shape=[jax.ShapeDtypeStruct((rows, ow), BF) for ow in out_widths],
        compiler_params=_params(("arbitrary",)),
        name="norm_mod_proj",
    )(*args)


GQA_UNIT = 2
GQA_TQ = 512

def _gqa_body(q_ref, kc_ref, vc_ref, *rest, with_lat, tq):
    if with_lat:
        kl_ref, vl_ref, o_ref = rest
    else:
        (o_ref,) = rest
    units = [(j, t0) for j in range(A_KV) for t0 in range(0, A_GROUP, GQA_UNIT)]
    lane = lax.broadcasted_iota(jnp.int32, (tq, LANES), 1)
    own = [(lane // HD) == j for j in range(A_KV)]

    def scores(unit):
        j, t0 = unit
        qg = jnp.concatenate(
            [jnp.where(own[j], q_ref[:, LANES * t:LANES * (t + 1)], 0.0) for t in range(t0, t0 + GQA_UNIT)],
            axis=0)
        s_c = _dot_nt(qg, kc_ref[...])
        s_l = _dot_nt(qg, kl_ref[...]) if with_lat else None
        return s_c, s_l

    res = {}
    nxt = scores(units[0])
    for u, (j, t0) in enumerate(units):
        s_c, s_l = nxt
        if u + 1 < len(units):
            nxt = scores(units[u + 1])
        vcols = slice(LANES * j, LANES * (j + 1))
        m = jnp.max(s_c, axis=1, keepdims=True)
        if with_lat:
            m = jnp.maximum(m, jnp.max(s_l, axis=1, keepdims=True))
        acc = _dot(jnp.exp2(s_c - m).astype(BF), vc_ref[:, vcols])
        if with_lat:
            acc = acc + _dot(jnp.exp2(s_l - m).astype(BF), vl_ref[:, vcols])
        den_lane = HD * (1 - j)
        o = acc / acc[:, den_lane:den_lane + 1]
        for g in range(GQA_UNIT):
            res[(j, t0 + g)] = o[g * tq:(g + 1) * tq]
    for t in range(A_GROUP):
        o_ref[:, LANES * t:LANES * (t + 1)] = jnp.where(own[0], res[(0, t)], res[(1, t)]).astype(BF)


def _gqa(q, k_ctx, v_ctx, k_lat, v_lat, bsz, tq):
    sq = q.shape[0] // bsz
    nc = k_ctx.shape[0] // bsz
    nq = sq // tq
    with_lat = k_lat is not None
    in_specs = [pl.BlockSpec((tq, A_HEADS * HD), lambda b, i: (b * nq + i, 0)),
                pl.BlockSpec((nc, A_KV * HD), lambda b, i: (b, 0)),
                pl.BlockSpec((nc, 2 * A_KV * HD), lambda b, i: (b, 0))]
    args = [q, k_ctx, v_ctx]
    if with_lat:
        sl = k_lat.shape[0] // bsz
        in_specs += [pl.BlockSpec((sl, A_KV * HD), lambda b, i: (b, 0)),
                     pl.BlockSpec((sl, 2 * A_KV * HD), lambda b, i: (b, 0))]
        args += [k_lat, v_lat]
    return pl.pallas_call(
        functools.partial(_gqa_body, with_lat=with_lat, tq=tq),
        grid=(bsz, nq),
        in_specs=in_specs,
        out_specs=pl.BlockSpec((tq, A_HEADS * HD), lambda b, i: (b * nq + i, 0)),
        out_shape=jax.ShapeDtypeStruct(q.shape, BF),
        compiler_params=_params(("arbitrary", "arbitrary")),
        name="gqa_attention",
    )(*args)


GLA_Q = 0
GLA_K = B_HEADS * B_DK
GLA_V = 2 * B_HEADS * B_DK
GLA_R = GLA_V + B_HEADS * B_DV
GLA_W = GLA_R + B_HEADS * B_DV
GLA_GROUP = 8


def _gla_body(gc_ref, gl_ref, zc_ref, zl_ref, w2f_ref, w2b_ref, bf_ref, bb_ref, gg_ref,
              oc_ref, ol_ref, ofc_ref, ofl_ref, st_ref):
    nk = B_HEADS * B_DK
    ch = B_CHUNK
    row = lax.broadcasted_iota(jnp.int32, (ch, ch), 0)
    col = lax.broadcasted_iota(jnp.int32, (ch, ch), 1)
    lane = lax.broadcasted_iota(jnp.int32, (ch, nk), 1)
    head_masks = [(lane // B_DK) == h for h in range(B_HEADS)]

    def group_out(g_ref, z_ref, r0s, fwd, s_t):
        w2 = (w2f_ref if fwd else w2b_ref)[...]
        bias = (bf_ref if fwd else bb_ref)[...]
        keep = (row >= col) if fwd else (row <= col)
        tri = jnp.where(keep, 1.0, 0.0).astype(BF)
        blks = [g_ref[pl.ds(r0, ch), :] for r0 in r0s]
        pres = [_dot(z_ref[pl.ds(r0, ch), :], w2) + bias for r0 in r0s]
        las = [-(jnp.maximum(-p, 0.0) + jnp.log(1.0 + jnp.exp(-jnp.abs(p)))) * (1.0 / B_TAU) for p in pres]
        bcums = [_split_dot(tri, la) for la in las]
        qms, kis, kss, decs = [], [], [], []
        for blk, bcum in zip(blks, bcums):
            q = blk[:, GLA_Q:GLA_Q + nk].astype(F32)
            k = blk[:, GLA_K:GLA_K + nk].astype(F32)
            b_last = bcum[ch - 1:ch, :] if fwd else bcum[0:1, :]
            q_dec = q * jnp.exp(bcum)
            k_state = k * jnp.exp(b_last - bcum)
            qms.append([jnp.where(m, q_dec, 0.0).astype(BF) for m in head_masks])
            kis.append((k * jnp.exp(-bcum)).astype(BF))
            kss.append([jnp.where(m, k_state, 0.0).astype(BF) for m in head_masks])
            decs.append(jnp.exp(b_last))
        vs = [[blk[:, GLA_V + B_DV * h:GLA_V + B_DV * (h + 1)] for h in range(B_HEADS)] for blk in blks]
        atts = [[jnp.where(keep, _dot_nt(qm, ki), 0.0).astype(BF) for qm in qmu] for qmu, ki in zip(qms, kis)]
        o_intra = [[_dot(a, v) for a, v in zip(au, vu)] for au, vu in zip(atts, vs)]
        d_ss = []
        for vu, ksu in zip(vs, kss):
            d_s = _dot_tn(vu[0], ksu[0])
            for h in range(1, B_HEADS):
                d_s = d_s + _dot_tn(vu[h], ksu[h])
            d_ss.append(d_s)
        outs = []
        for u in range(len(r0s)):
            s_b = s_t.astype(BF)
            outs.append(jnp.concatenate(
                [o_intra[u][h] + _dot_nt(qms[u][h], s_b) for h in range(B_HEADS)], axis=1))
            s_t = decs[u] * s_t + d_ss[u]
        return outs, blks, s_t

    def fwd_pass(g_ref, z_ref, of_ref, n):
        grp = min(GLA_GROUP, n)
        assert n % grp == 0

        def body(i, carry):
            r0s = [pl.multiple_of((i * grp + u) * ch, ch) for u in range(grp)]
            outs, _, s_t = group_out(g_ref, z_ref, r0s, True, st_ref[...])
            for r0, o in zip(r0s, outs):
                of_ref[pl.ds(r0, ch), :] = o
            st_ref[...] = s_t
            return carry
        lax.fori_loop(0, n // grp, body, 0)

    def bwd_pass(g_ref, z_ref, of_ref, o_ref, n):
        grp = min(GLA_GROUP, n)
        assert n % grp == 0

        def body(i, carry):
            r0s = [pl.multiple_of((n - 1 - (i * grp + u)) * ch, ch) for u in range(grp)]
            outs, blks, s_t = group_out(g_ref, z_ref, r0s, False, st_ref[...])
            for r0, o, blk in zip(r0s, outs, blks):
                o = o + of_ref[pl.ds(r0, ch), :]
                parts = []
                for h in range(B_HEADS):
                    oh = o[:, B_DV * h:B_DV * (h + 1)]
                    ms = jnp.mean(oh * oh, axis=-1, keepdims=True)
                    parts.append(oh * lax.rsqrt(ms + EPS))
                y = jnp.concatenate(parts, axis=1) * gg_ref[...]
                r = blk[:, GLA_R:GLA_R + B_HEADS * B_DV].astype(F32)
                o_ref[pl.ds(r0, ch), :] = (y * _silu(r)).astype(BF)
            st_ref[...] = s_t
            return carry
        lax.fori_loop(0, n // grp, body, 0)

    nc = gc_ref.shape[0] // ch
    nl = gl_ref.shape[0] // ch
    st_ref[...] = jnp.zeros_like(st_ref)
    fwd_pass(gc_ref, zc_ref, ofc_ref, nc)
    fwd_pass(gl_ref, zl_ref, ofl_ref, nl)
    st_ref[...] = jnp.zeros_like(st_ref)
    bwd_pass(gc_ref, zc_ref, ofc_ref, oc_ref, nc)
    bwd_pass(gl_ref, zl_ref, ofl_ref, ol_ref, nl)


def _gla(g_ctx, g_lat, z_ctx, z_lat, w2f, w2b, b_f, b_b, gg, bsz):
    nc = g_ctx.shape[0] // bsz
    sl = g_lat.shape[0] // bsz
    nk = B_HEADS * B_DK
    nv = B_HEADS * B_DV
    full = lambda shape: pl.BlockSpec(shape, lambda b: (0, 0))
    return pl.pallas_call(
        _gla_body,
        grid=(bsz,),
        in_specs=[pl.BlockSpec((nc, GLA_W), lambda b: (b, 0)),
                  pl.BlockSpec((sl, GLA_W), lambda b: (b, 0)),
                  pl.BlockSpec((nc, LANES), lambda b: (b, 0)),
                  pl.BlockSpec((sl, LANES), lambda b: (b, 0)),
                  full((LANES, nk)), full((LANES, nk)), full((1, nk)), full((1, nk)), full((1, nv))],
        out_specs=[pl.BlockSpec((nc, nv), lambda b: (b, 0)),
                   pl.BlockSpec((sl, nv), lambda b: (b, 0))],
        out_shape=[jax.ShapeDtypeStruct((g_ctx.shape[0], nv), BF),
                   jax.ShapeDtypeStruct((g_lat.shape[0], nv), BF)],
        scratch_shapes=[pltpu.VMEM((nc, nv), F32), pltpu.VMEM((sl, nv), F32),
                        pltpu.VMEM((B_DV, nk), F32)],
        compiler_params=_params(("arbitrary",)),
        name="gla_bidir",
    )(g_ctx, g_lat, z_ctx, z_lat, w2f, w2b, b_f, b_b, gg)


def _mixer_residual(x_ref, m_ref, pair_refs):
    acc = _dot(pair_refs[0][...], pair_refs[1][...])
    for t in range(1, len(pair_refs) // 2):
        acc = acc + _dot(pair_refs[2 * t][...], pair_refs[2 * t + 1][...])
    return x_ref[...] + m_ref[0, 2:3, :] * acc


def _pair_specs(pairs, tm, index):
    in_specs, args = [], []
    for a, w in pairs:
        in_specs += [pl.BlockSpec((tm, a.shape[1]), index), pl.BlockSpec(w.shape, lambda *_: (0, 0))]
        args += [a, w]
    return in_specs, args


def _ffn_body(x_ref, g_ref, m_ref, *rest, n_pairs):
    pair_refs = rest[:2 * n_pairs]
    wg_ref, wu_ref, wd_ref, o_ref, xn_ref, h_ref, acc_ref = rest[2 * n_pairs:]
    j = pl.program_id(1)

    def swiglu_step():
        h = h_ref[...]
        a = (_silu(_dot(h, wg_ref[...])) * _dot(h, wu_ref[...])).astype(BF)
        return _dot(a, wd_ref[...])

    @pl.when(j == 0)
    def _():
        xn = _mixer_residual(x_ref, m_ref, pair_refs)
        xn_ref[...] = xn
        h_ref[...] = _norm_mod(xn, g_ref[...], m_ref[0, 3:4, :], m_ref[0, 4:5, :]).astype(BF)
        acc_ref[...] = swiglu_step()

    last = pl.num_programs(1) - 1

    @pl.when((j > 0) & (j < last))
    def _():
        acc_ref[...] += swiglu_step()

    @pl.when(j == last)
    def _():
        o_ref[...] = xn_ref[...] + m_ref[0, 5:6, :] * (acc_ref[...] + swiglu_step())


def _ffn(x2, g, mpack, mod_of_tile, pairs, wg, wu, wd, tm, tf):
    rows = x2.shape[0]
    ff = wg.shape[1]
    assert ff // tf >= 2
    pair_specs, pair_args = _pair_specs(pairs, tm, lambda i, j: (i, 0))
    return pl.pallas_call(
        functools.partial(_ffn_body, n_pairs=len(pairs)),
        grid=(rows // tm, ff // tf),
        in_specs=[pl.BlockSpec((tm, D), lambda i, j: (i, 0)),
                  pl.BlockSpec((1, D), lambda i, j: (0, 0)),
                  pl.BlockSpec((1, MOD_ROWS, D), lambda i, j: (mod_of_tile(i), 0, 0)),
                  *pair_specs,
                  pl.BlockSpec((D, tf), lambda i, j: (0, j)),
                  pl.BlockSpec((D, tf), lambda i, j: (0, j)),
                  pl.BlockSpec((tf, D), lambda i, j: (j, 0))],
        out_specs=pl.BlockSpec((tm, D), lambda i, j: (i, 0)),
        out_shape=jax.ShapeDtypeStruct((rows, D), F32),
        scratch_shapes=[pltpu.VMEM((tm, D), F32), pltpu.VMEM((tm, D), BF), pltpu.VMEM((tm, D), F32)],
        compiler_params=_params(("arbitrary", "arbitrary")),
        name="ffn_swiglu",
    )(x2, g.reshape(1, D), mpack, *pair_args, wg, wu, wd)


NA_QROWS = 4
NA_KROWS = 12
NA_TQ = NA_QROWS * GRID_W
NA_HG = 16
NA_W = NA_HG * HD


def _na_body(q_ref, k0_ref, k1_ref, k2_ref, v0_ref, v1_ref, v2_ref, kc_ref, vc_ref, b_ref, o_ref):
    pair = LANES // HD
    lane = lax.broadcasted_iota(jnp.int32, (NA_TQ, LANES), 1)
    own = [(lane // HD) == half for half in range(pair)]

    def scores(hh):
        tile = slice(LANES * (hh // pair), LANES * (hh // pair + 1))
        q = jnp.where(own[hh % pair], q_ref[:, tile], 0.0)
        s_loc = jnp.concatenate([_dot_nt(q, kr[:, tile]) for kr in (k0_ref, k1_ref, k2_ref)], axis=1)
        bias = b_ref[0, :, hh].reshape(NA_TQ, NA_KROWS * GRID_W)
        return s_loc + bias, _dot_nt(q, kc_ref[:, tile])

    nxt = scores(0)
    outs = []
    for hh in range(NA_HG):
        s_loc, s_ctx = nxt
        if hh + 1 < NA_HG:
            nxt = scores(hh + 1)
        vcols = slice(LANES * hh, LANES * (hh + 1))
        m = jnp.maximum(jnp.max(s_loc, axis=1, keepdims=True), jnp.max(s_ctx, axis=1, keepdims=True))
        p_loc = jnp.exp2(s_loc - m).astype(BF)
        acc = _dot(jnp.exp2(s_ctx - m).astype(BF), vc_ref[:, vcols])
        for t, vr in enumerate((v0_ref, v1_ref, v2_ref)):
            acc = acc + _dot(p_loc[:, t * NA_TQ:(t + 1) * NA_TQ], vr[:, vcols])
        den_lane = HD * (1 - hh % pair)
        outs.append(acc / acc[:, den_lane:den_lane + 1])
        if len(outs) == pair:
            t0 = hh // pair
            o_ref[:, LANES * t0:LANES * (t0 + 1)] = jnp.where(own[0], outs[0], outs[1]).astype(BF)
            outs = []


def _na_bias(rpb, rows):
    heads, n_dr, n_dc = rpb.shape
    nblk = rows // NA_QROWS
    n_off = n_dr + 2 * NA_KROWS
    rpad = jnp.pad(rpb.astype(F32) * LOG2E, ((0, 0), (NA_KROWS, NA_KROWS), (0, 0)), constant_values=NEG_INF)
    rvals = jnp.repeat(rpad.transpose(0, 2, 1), GRID_W, axis=2)
    cuts = []
    for kind, j in enumerate((0, min(1, nblk - 1), nblk - 1)):
        ws = NA_QROWS * int(np.clip(j - 1, 0, nblk - 3))
        for qr in range(NA_QROWS):
            r = NA_QROWS * j + qr
            rs = int(np.clip(r - NA_ROWS // 2, 0, rows - NA_ROWS))
            cuts.append((kind, qr, NA_KROWS + ws - r + NA_ROWS - 1, max(rs - ws, 0), min(rs + NA_ROWS - ws, NA_KROWS)))
    width = NA_KROWS * GRID_W

    def body(r_ref, o_ref):
        shape = (GRID_W, n_off * GRID_W)
        qc = lax.broadcasted_iota(jnp.int32, shape, 0)
        kc = lax.broadcasted_iota(jnp.int32, shape, 1) % GRID_W
        delta = kc - qc + NA_COLS - 1
        cs = jnp.clip(qc - NA_COLS // 2, 0, GRID_W - NA_COLS)
        toe = jnp.full(shape, NEG_INF, F32)
        for d in range(n_dc):
            toe = jnp.where(delta == d, r_ref[0, d:d + 1, :], toe)
        toe = jnp.where((kc >= cs) & (kc < cs + NA_COLS), toe, NEG_INF)
        key_row = lax.broadcasted_iota(jnp.int32, (GRID_W, width), 1) // GRID_W
        for kind, qr, d0, lo, hi in cuts:
            blk = toe[:, d0 * GRID_W:d0 * GRID_W + width]
            o_ref[kind, qr, 0] = jnp.where((key_row >= lo) & (key_row < hi), blk, NEG_INF)

    return pl.pallas_call(
        body,
        grid=(heads,),
        in_specs=[pl.BlockSpec((1, n_dc, n_off * GRID_W), lambda h: (h, 0, 0))],
        out_specs=pl.BlockSpec((3, NA_QROWS, 1, GRID_W, width), lambda h: (0, 0, h, 0, 0)),
        out_shape=jax.ShapeDtypeStruct((3, NA_QROWS, heads, GRID_W, width), F32),
        compiler_params=_params(("arbitrary",)),
        name="natten_bias",
    )(rvals)


def _natten(q, k, v, k_ctx, v_ctx, bias, bsz):
    s = q.shape[0] // bsz
    nc = k_ctx.shape[0] // bsz
    nblk = s // NA_TQ
    ngrp = C_HEADS // NA_HG
    v_w = NA_HG * LANES

    def kv_spec(t, width):
        return pl.BlockSpec((NA_TQ, width),
                            lambda j, g, b: (b * nblk + jnp.clip(j - 1, 0, nblk - 3) + t, g))

    def kind(j):
        return jnp.where(j == 0, 0, jnp.where(j == nblk - 1, 2, 1))

    return pl.pallas_call(
        _na_body,
        grid=(nblk, ngrp, bsz),
        in_specs=[pl.BlockSpec((NA_TQ, NA_W), lambda j, g, b: (b * nblk + j, g)),
                  kv_spec(0, NA_W), kv_spec(1, NA_W), kv_spec(2, NA_W),
                  kv_spec(0, v_w), kv_spec(1, v_w), kv_spec(2, v_w),
                  pl.BlockSpec((nc, NA_W), lambda j, g, b: (b, g)),
                  pl.BlockSpec((nc, v_w), lambda j, g, b: (b, g)),
                  pl.BlockSpec((1, NA_QROWS, NA_HG, GRID_W, NA_KROWS * GRID_W),
                               lambda j, g, b: (kind(j), 0, g, 0, 0))],
        out_specs=pl.BlockSpec((NA_TQ, NA_W), lambda j, g, b: (b * nblk + j, g)),
        out_shape=jax.ShapeDtypeStruct(q.shape, BF),
        compiler_params=_params(("arbitrary", "arbitrary", "arbitrary")),
        name="natten",
    )(q, k, k, k, v, v, v, k_ctx, v_ctx, bias)


MOE_TM = 512
MOE_TF = 1792
META_E1, META_E2, META_R1, META_R2, META_G1, META_G2 = range(6)
ROW_TILE = D // LANES


def _to_token_tiles(dst_ref, value):
    n = value.shape[0]
    for s in range(ROW_TILE):
        dst_ref[pl.ds(s, n, stride=ROW_TILE), :] = value[:, LANES * s:LANES * (s + 1)]


def _from_token_tiles(src_ref, n):
    return jnp.concatenate([src_ref[pl.ds(s, n, stride=ROW_TILE), :] for s in range(ROW_TILE)], axis=1)


def _route_body(x_ref, g_ref, m_ref, *rest, n_pairs):
    pair_refs = rest[:2 * n_pairs]
    wr_ref, xn_ref, h_ref, meta_ref, rec_ref, cnt_ref, tri_ref, carry_ref = rest[2 * n_pairs:]
    tm = x_ref.shape[0]

    @pl.when(pl.program_id(0) == 0)
    def _():
        row = lax.broadcasted_iota(jnp.int32, (tm, tm), 0)
        col = lax.broadcasted_iota(jnp.int32, (tm, tm), 1)
        tri_ref[...] = jnp.where(row >= col, 1.0, 0.0).astype(BF)
        carry_ref[...] = jnp.zeros_like(carry_ref)

    xn = _mixer_residual(x_ref, m_ref, pair_refs)
    xn_ref[...] = xn
    h = _norm_mod(xn, g_ref[...], m_ref[0, 3:4, :], m_ref[0, 4:5, :])
    _to_token_tiles(h_ref, h)
    h_hi = h.astype(BF)
    h_lo = (h - h_hi.astype(F32)).astype(BF)
    w = wr_ref[...]
    w_hi = w.astype(BF).astype(F32)
    w_cat = (w_hi + pltpu.roll(w - w_hi, N_EXPERTS, 1)).astype(BF)
    terms = _dot(h_hi, w_cat) + _dot(h_lo, w_cat)
    logits = terms + pltpu.roll(terms, LANES - N_EXPERTS, 1)
    lane = lax.broadcasted_iota(jnp.int32, logits.shape, 1)
    lg = jnp.where(lane < N_EXPERTS, logits, NEG_INF)
    m1 = jnp.max(lg, axis=1, keepdims=True)
    i1 = jnp.min(jnp.where(lg == m1, lane, LANES), axis=1, keepdims=True)
    lg2 = jnp.where(lane == i1, NEG_INF, lg)
    m2 = jnp.max(lg2, axis=1, keepdims=True)
    i2 = jnp.min(jnp.where(lg2 == m2, lane, LANES), axis=1, keepdims=True)
    e = jnp.exp(m2 - m1)
    g1 = 1.0 / (1.0 + e)
    g2 = e / (1.0 + e)
    oh = jnp.where(lane == i1, 1.0, 0.0) + jnp.where(lane == i2, 1.0, 0.0)
    cum = _dot(tri_ref[...], oh.astype(BF))
    base = carry_ref[0:1, :] + cum - oh
    r1 = jnp.sum(jnp.where(lane == i1, base, 0.0), axis=1, keepdims=True)
    r2 = jnp.sum(jnp.where(lane == i2, base, 0.0), axis=1, keepdims=True)
    carry_ref[...] = carry_ref[...] + cum[tm - 1:tm, :]
    cnt_ref[...] = carry_ref[...]
    meta = jnp.zeros(logits.shape, F32)
    for ln, val in ((META_E1, i1.astype(F32)), (META_E2, i2.astype(F32)), (META_R1, r1), (META_R2, r2),
                    (META_G1, g1), (META_G2, g2)):
        meta = jnp.where(lane == ln, val, meta)
    meta_ref[...] = meta
    rec_ref[...] = meta.T[:rec_ref.shape[0], :]


def _route(x2, g, mpack, mod_of_tile, pairs, wr, tm):
    rows = x2.shape[0]
    pair_specs, pair_args = _pair_specs(pairs, tm, lambda i: (i, 0))
    return pl.pallas_call(
        functools.partial(_route_body, n_pairs=len(pairs)),
        grid=(rows // tm,),
        in_specs=[pl.BlockSpec((tm, D), lambda i: (i, 0)),
                  pl.BlockSpec((1, D), lambda i: (0, 0)),
                  pl.BlockSpec((1, MOD_ROWS, D), lambda i: (mod_of_tile(i), 0, 0)),
                  *pair_specs,
                  pl.BlockSpec((D, LANES), lambda i: (0, 0))],
        out_specs=[pl.BlockSpec((tm, D), lambda i: (i, 0)),
                   pl.BlockSpec((tm * ROW_TILE, LANES), lambda i: (i, 0)),
                   pl.BlockSpec((tm, LANES), lambda i: (i, 0)),
                   pl.BlockSpec((8, tm), lambda i: (0, i)),
                   pl.BlockSpec((8, LANES), lambda i: (0, 0))],
        out_shape=[jax.ShapeDtypeStruct((rows, D), F32),
                   jax.ShapeDtypeStruct((rows * ROW_TILE, LANES), F32),
                   jax.ShapeDtypeStruct((rows, LANES), F32),
                   jax.ShapeDtypeStruct((8, rows), F32),
                   jax.ShapeDtypeStruct((8, LANES), F32)],
        scratch_shapes=[pltpu.VMEM((tm, tm), BF), pltpu.VMEM((8, LANES), F32)],
        compiler_params=_params(("arbitrary",)),
        name="moe_route",
    )(x2, g.reshape(1, D), mpack, *pair_args, wr)


def _token_copy(src_ref, src_tok, dst_ref, dst_tok, sem):
    src = src_ref.at[pl.ds(pl.multiple_of(src_tok * ROW_TILE, ROW_TILE), ROW_TILE), :]
    dst = dst_ref.at[pl.ds(pl.multiple_of(dst_tok * ROW_TILE, ROW_TILE), ROW_TILE), :]
    return pltpu.make_async_copy(src, dst, sem)


def _experts_body(te_ref, nu_ref, cur_ref, nxt_ref, dprev_ref, h_hbm, wg_ref, wu_ref, wd_ref, y_hbm,
                  xbuf, h_ref, acc_ref, ybuf, gsem, ssem, *, n_steps):
    assert n_steps >= 2
    i = pl.program_id(0)
    j = pl.program_id(1)
    used = i < nu_ref[0]
    slot = i % 2
    other = 1 - slot

    def gather_row(idx_ref, r, sl):
        _token_copy(h_hbm, idx_ref[0, 0, r], xbuf.at[sl], r, gsem.at[sl]).start()

    def issue_gathers():
        for r in range(MOE_TM):
            gather_row(nxt_ref, r, other)

    def issue_scatters():
        for r in range(MOE_TM):
            _token_copy(ybuf.at[other], r, y_hbm, dprev_ref[0, 0, r], ssem.at[other]).start()

    def wait_rows(buf, sem):
        pltpu.make_async_copy(buf, buf, sem).wait()

    def swiglu_step():
        h = h_ref[...]
        a = (_silu(_dot(h, wg_ref[0])) * _dot(h, wu_ref[0])).astype(BF)
        return _dot(a, wd_ref[0])

    @pl.when((j == 0) & (i == 0))
    def _():
        ybuf[...] = jnp.zeros_like(ybuf)
        n_spare = MOE_TM * ROW_TILE
        for sl in range(2):
            spare = pltpu.make_async_copy(
                ybuf.at[sl], y_hbm.at[pl.ds(y_hbm.shape[0] - (2 - sl) * n_spare, n_spare), :], ssem.at[sl])
            spare.start()
            spare.wait()

        def body(r, c):
            gather_row(cur_ref, r, 0)
            return c
        lax.fori_loop(0, MOE_TM, body, 0, unroll=8)

    @pl.when(j == 0)
    def _():
        wait_rows(xbuf.at[slot], gsem.at[slot])

    @pl.when(used & (j == 0))
    def _():
        h_ref[...] = _from_token_tiles(xbuf.at[slot], MOE_TM).astype(BF)
        acc_ref[...] = swiglu_step()
        issue_gathers()

    if n_steps > 2:
        @pl.when(used & (j > 0) & (j < n_steps - 1))
        def _():
            acc_ref[...] += swiglu_step()

    def wait_own_ybuf():
        @pl.when(i > 0)
        def _():
            wait_rows(ybuf.at[slot], ssem.at[slot])

    @pl.when(used & (j == n_steps - 1))
    def _():
        wait_own_ybuf()
        _to_token_tiles(ybuf.at[slot], acc_ref[...] + swiglu_step())
        issue_scatters()

    @pl.when(jnp.logical_not(used) & (j == 0))
    def _():
        issue_gathers()

    @pl.when(jnp.logical_not(used) & (j == n_steps - 1))
    def _():
        wait_own_ybuf()
        issue_scatters()

    @pl.when((j == n_steps - 1) & (i == pl.num_programs(0) - 1))
    def _():
        wait_rows(ybuf.at[other], ssem.at[other])
        wait_rows(xbuf.at[other], gsem.at[other])


def _experts(h_tiles, tok_of_slot, dest_of_slot, tile_expert, n_used, wg, wu, wd, out_tokens):
    n_tiles = tile_expert.shape[0]
    ff = wg.shape[2]
    nf = ff // MOE_TF
    spare_rows = out_tokens - MOE_TM + jnp.arange(MOE_TM, dtype=jnp.int32)
    dest_prev = jnp.concatenate([spare_rows, dest_of_slot[:-MOE_TM]])

    def jj(i, j, nu):
        return jnp.where(i < nu[0], j, nf - 1)

    def smem_rows(index):
        return pl.BlockSpec((1, 1, MOE_TM), lambda i, j, te, nu: (index(i), 0, 0), memory_space=pltpu.SMEM)

    grid_spec = pltpu.PrefetchScalarGridSpec(
        num_scalar_prefetch=2,
        grid=(n_tiles, nf),
        in_specs=[smem_rows(lambda i: i),
                  smem_rows(lambda i: jnp.minimum(i + 1, n_tiles - 1)),
                  smem_rows(lambda i: i),
                  pl.BlockSpec(memory_space=pl.ANY),
                  pl.BlockSpec((1, D, MOE_TF), lambda i, j, te, nu: (te[i], 0, jj(i, j, nu))),
                  pl.BlockSpec((1, D, MOE_TF), lambda i, j, te, nu: (te[i], 0, jj(i, j, nu))),
                  pl.BlockSpec((1, MOE_TF, D), lambda i, j, te, nu: (te[i], jj(i, j, nu), 0))],
        out_specs=pl.BlockSpec(memory_space=pl.ANY),
        scratch_shapes=[pltpu.VMEM((2, MOE_TM * ROW_TILE, LANES), F32),
                        pltpu.VMEM((MOE_TM, D), BF),
                        pltpu.VMEM((MOE_TM, D), F32),
                        pltpu.VMEM((2, MOE_TM * ROW_TILE, LANES), F32),
                        pltpu.SemaphoreType.DMA((2,)),
                        pltpu.SemaphoreType.DMA((2,))])
    tok3 = tok_of_slot.reshape(n_tiles, 1, MOE_TM)
    return pl.pallas_call(
        functools.partial(_experts_body, n_steps=nf),
        grid_spec=grid_spec,
        out_shape=jax.ShapeDtypeStruct((out_tokens * ROW_TILE, LANES), F32),
        compiler_params=_params(("arbitrary", "arbitrary")),
        name="moe_experts",
    )(tile_expert, n_used, tok3, tok3, dest_prev.reshape(n_tiles, 1, MOE_TM), h_tiles, wg, wu, wd)


def _combine_body(y1_ref, y2_ref, x_ref, m_ref, meta_ref, gf_ref, o_ref):
    n = x_ref.shape[0]
    meta = meta_ref[...]
    lane = lax.broadcasted_iota(jnp.int32, meta.shape, 1)
    g1 = jnp.sum(jnp.where(lane == META_G1, meta, 0.0), axis=1, keepdims=True)
    g2 = jnp.sum(jnp.where(lane == META_G2, meta, 0.0), axis=1, keepdims=True)
    moe = g1 * _from_token_tiles(y1_ref, n) + g2 * _from_token_tiles(y2_ref, n)
    y = x_ref[...] + m_ref[0, 5:6, :] * moe
    ms = jnp.mean(y * y, axis=-1, keepdims=True)
    o_ref[...] = y * lax.rsqrt(ms + EPS) * gf_ref[...]


def _combine(x2, mpack, mod_of_row, meta, y_tiles, g_final, tm):
    rows = x2.shape[0]
    nt = rows // tm
    return pl.pallas_call(
        _combine_body,
        grid=(nt,),
        in_specs=[pl.BlockSpec((tm * ROW_TILE, LANES), lambda i: (i, 0)),
                  pl.BlockSpec((tm * ROW_TILE, LANES), lambda i: (i + nt, 0)),
                  pl.BlockSpec((tm, D), lambda i: (i, 0)),
                  pl.BlockSpec((1, MOD_ROWS, D), lambda i: (mod_of_row(i * tm), 0, 0)),
                  pl.BlockSpec((tm, LANES), lambda i: (i, 0)),
                  pl.BlockSpec((1, D), lambda i: (0, 0))],
        out_specs=pl.BlockSpec((tm, D), lambda i: (i, 0)),
        out_shape=jax.ShapeDtypeStruct((rows, D), F32),
        compiler_params=_params(("arbitrary",)),
        name="moe_combine",
    )(y_tiles, y_tiles, x2, mpack, meta, g_final.reshape(1, D))


def _moe(x2, g, mpack, mod_of_tile, mod_of_row, pairs, wr, wg, wu, wd, g_final, tm):
    rows = x2.shape[0]
    ne = wg.shape[0]
    x2, h_tiles, meta, rec, cnt = _route(x2, g, mpack, mod_of_tile, pairs, wr, tm)
    counts = cnt[0, :ne].astype(jnp.int32)
    tiles_e = (counts + MOE_TM - 1) // MOE_TM
    tile_end = jnp.cumsum(tiles_e)
    group_off = (tile_end - tiles_e) * MOE_TM
    n_tiles = 2 * rows // MOE_TM + ne
    n_slots = n_tiles * MOE_TM
    tile_expert = jnp.minimum(
        jnp.sum(jnp.arange(n_tiles)[:, None] >= tile_end[None, :], axis=1), ne - 1).astype(jnp.int32)
    n_used = tile_end[-1:].astype(jnp.int32)
    eids = jnp.arange(ne)

    def slot_of(lane_e, lane_r):
        e = rec[lane_e].astype(jnp.int32)
        off = jnp.sum(jnp.where(e[None, :] == eids[:, None], group_off[:, None], 0), axis=0)
        return off + rec[lane_r].astype(jnp.int32)

    tok = jnp.arange(rows, dtype=jnp.int32)
    dest = jnp.full((n_slots,), -1, jnp.int32).at[
        jnp.concatenate([slot_of(META_E1, META_R1), slot_of(META_E2, META_R2)])].set(
        jnp.concatenate([tok, rows + tok]), unique_indices=True, mode="promise_in_bounds")
    pad = dest < 0
    dest_of_slot = jnp.where(pad, 2 * rows + jnp.arange(n_slots, dtype=jnp.int32) % (2 * MOE_TM), dest)
    tok_of_slot = jnp.where(pad, 0, jnp.where(dest >= rows, dest - rows, dest))
    y_tiles = _experts(h_tiles, tok_of_slot, dest_of_slot, tile_expert, n_used, wg, wu, wd, 2 * rows + 2 * MOE_TM)
    return _combine(x2, mpack, mod_of_row, meta, y_tiles, g_final, tm)


def _rope_tables(seq_len):
    t = np.arange(seq_len)
    row = (t // GRID_W).astype(np.float32)
    col = (t % GRID_W).astype(np.float32)
    n_axis = HD // 4
    inv = jnp.power(ROPE_THETA, -jnp.arange(n_axis, dtype=F32) / n_axis)
    ang = jnp.concatenate([jnp.asarray(row)[:, None] * inv, jnp.asarray(col)[:, None] * inv], axis=-1)
    cos, sin = jnp.cos(ang), jnp.sin(ang)
    ct = jnp.tile(jnp.concatenate([cos, cos], axis=-1), (1, LANES // HD))
    st = jnp.tile(jnp.concatenate([-sin, sin], axis=-1), (1, LANES // HD))
    return ct, st


def kernel(x, c, ctx, c_ctx, w_mod, b_mod, g_norm1, g_norm2, w_in_ab, g_q, g_k, w_a2_f, b_a_f, w_a2_b,
           b_a_b, g_gla, w_out_ab, w_ff_gate, w_ff_up, w_ff_down, w_in_c, rpb_c, w_out_c, w_router,
           w_moe_gate, w_moe_up, w_moe_down, g_final):
    bsz, seq, _ = x.shape
    nc = ctx.shape[1]
    depth = w_mod.shape[0]
    assert depth == 2 and seq % GRID_W == 0

    cond_rows = -(-(bsz + 1) // 8) * 8
    cond = jnp.zeros((cond_rows, D), F32).at[:bsz].set(c).at[bsz].set(c_ctx)
    mods = _adaln(cond, w_mod, b_mod).reshape(depth, cond_rows, N_MOD, D)
    mods = jnp.pad(mods, ((0, 0), (0, 0), (0, MOD_ROWS - N_MOD), (0, 0)))

    xl = x.reshape(bsz * seq, D)
    xc = ctx.reshape(bsz * nc, D)
    tm_l = min(ROW_TM, seq)
    tm_c = min(ROW_TM, bsz * nc)
    tpb = seq // tm_l
    lat_mod = lambda i: i // tpb
    ctx_mod = lambda i: bsz

    mp = mods[0]
    w_in = w_in_ab[0]
    perm = np.concatenate([np.arange(0, HD, 2), np.arange(1, HD, 2)])
    nq, nkv = A_HEADS * HD, A_KV * HD
    o_q, o_k, o_v = 0, nq, nq + nkv
    o_bq = o_v + nkv
    o_bk = o_bq + B_HEADS * B_DK
    o_bv = o_bk + B_HEADS * B_DK
    o_br = o_bv + B_HEADS * B_DV
    o_zf = o_br + B_HEADS * B_DV
    o_end = o_zf + 2 * B_RANK
    assert A_KV * HD == LANES
    q_order = np.arange(A_HEADS).reshape(A_KV, A_GROUP).T.reshape(-1)
    wq = w_in[:, o_q:o_k].reshape(D, A_HEADS, HD)[:, q_order][:, :, perm].reshape(D, nq)
    wk = w_in[:, o_k:o_v].reshape(D, A_KV, HD)[:, :, perm].reshape(D, nkv)
    wz = jnp.pad(w_in[:, o_zf:o_end], ((0, 0), (0, LANES - 2 * B_RANK)))
    w0 = jnp.concatenate([wq, wk, w_in[:, o_v:o_bq], w_in[:, o_bq:o_bk] * (B_DK ** -0.5),
                          w_in[:, o_bk:o_zf], wz], axis=1).astype(BF)
    widths0 = (nq + nkv, nkv, GLA_W, LANES)
    outw0 = (nq, nkv, 2 * nkv, GLA_W, LANES)
    gqk = jnp.concatenate([jnp.tile(g_q[0][perm] * (HD ** -0.5 * LOG2E), A_HEADS),
                           jnp.tile(g_k[0][perm], A_KV)]).reshape(1, nq + nkv)
    bd = jnp.asarray(np.tile(np.kron(np.eye(LANES // HD), np.ones((HD, HD))), (2, 1)), BF)
    ct, st = _rope_tables(seq)
    ct_c = jnp.ones((tm_c, LANES), F32)
    st_c = jnp.zeros((tm_c, LANES), F32)

    kinds0 = ("qk", "values", "plain", "plain")
    ql, kl, vl, gl, zl = _proj(xl, g_norm1[0], mp, lat_mod, w0, widths0, kinds0, outw0, tm_l, (ct, st, gqk, bd))
    qc, kc, vc, gc, zc = _proj(xc, g_norm1[0], mp, ctx_mod, w0, widths0, kinds0, outw0, tm_c,
                               (ct_c, st_c, gqk, bd))

    oa_l = _gqa(ql, kc, vc, kl, vl, bsz, min(GQA_TQ, seq))
    oa_c = _gqa(qc, kc, vc, None, None, bsz, min(GQA_TQ, nc))

    w2f = jnp.zeros((LANES, B_HEADS * B_DK), F32).at[:B_RANK].set(w_a2_f[0]).astype(BF)
    w2b = jnp.zeros((LANES, B_HEADS * B_DK), F32).at[B_RANK:2 * B_RANK].set(w_a2_b[0]).astype(BF)
    ob_c, ob_l = _gla(gc, gl, zc, zl, w2f, w2b, b_a_f[0].reshape(1, -1), b_a_b[0].reshape(1, -1),
                      jnp.tile(g_gla[0], B_HEADS).reshape(1, -1), bsz)

    wo = w_out_ab[0].astype(BF)
    wo_a = wo[:nq].reshape(A_HEADS, HD, D)[q_order].reshape(nq, D)
    wo_b = wo[nq:]
    wg, wu, wd = w_ff_gate[0].astype(BF), w_ff_up[0].astype(BF), w_ff_down[0].astype(BF)
    xl = _ffn(xl, g_norm2[0], mp, lat_mod, [(oa_l, wo_a), (ob_l, wo_b)], wg, wu, wd, tm_l, FFN_TF)
    xc = _ffn(xc, g_norm2[0], mp, ctx_mod, [(oa_c, wo_a), (ob_c, wo_b)], wg, wu, wd, tm_c, FFN_TF)

    mp = mods[1]
    cw = C_HEADS * HD
    w1 = w_in_c[0].astype(BF)
    q1, k1, v1 = _proj(xl, g_norm1[1], mp, lat_mod, w1, (cw, cw, cw), ("plain", "plain", "values"),
                       (cw, cw, 2 * cw), tm_l, scales=(HD ** -0.5 * LOG2E, 1.0, 1.0))
    k1c, v1c = _proj(xc, g_norm1[1], mp, ctx_mod, w1, (cw, cw), ("plain", "values"), (cw, 2 * cw), tm_c,
                     col0=cw)

    bias = _na_bias(rpb_c[0], seq // GRID_W)
    o1 = _natten(q1, k1, v1, k1c, v1c, bias, bsz)

    wr = jnp.pad(w_router[0], ((0, 0), (0, LANES - N_EXPERTS)))
    out = _moe(xl, g_norm2[1], mp, lat_mod, lambda r: r // seq, [(o1, w_out_c[0].astype(BF))], wr,
               w_moe_gate[0].astype(BF), w_moe_up[0].astype(BF), w_moe_down[0].astype(BF), g_final, tm_l)
    return out.reshape(bsz, seq, D)
```

```python
import functools

import numpy as np
import jax
import jax.numpy as jnp
from jax import lax
from jax.experimental import pallas as pl
from jax.experimental.pallas import tpu as pltpu

F32 = jnp.float32
BF = jnp.bfloat16

D = 1024
EPS = 1e-6
N_MOD = 6
MOD_ROWS = 8
GRID_W = 64
HD = 64
A_HEADS = 8
A_KV = 2
A_GROUP = A_HEADS // A_KV
ROPE_THETA = 10000.0
LOG2E = 1.4426950408889634
B_HEADS = 4
B_DK = 64
B_DV = 128
B_RANK = 16
B_TAU = 16.0
B_CHUNK = 64
C_HEADS = 16
NA_ROWS = 8
NA_COLS = 16
N_EXPERTS = 8
LANES = 128
VMEM_LIMIT = 56 * 2 ** 20
ROW_TM = 1024
FFN_TF = 512

NEG_INF = float("-inf")


def _params(sem):
    return pltpu.CompilerParams(dimension_semantics=sem, vmem_limit_bytes=VMEM_LIMIT)


def _dot(a, b):
    return jnp.dot(a, b, preferred_element_type=F32)


def _dot_nt(a, b):
    return lax.dot_general(a, b, (((1,), (1,)), ((), ())), preferred_element_type=F32)


def _dot_tn(a, b):
    return lax.dot_general(a, b, (((0,), (0,)), ((), ())), preferred_element_type=F32)


def _silu(x):
    return x / (1.0 + jnp.exp(-x))


def _split_dot(lhs_bf_exact, x):
    hi = x.astype(BF)
    lo = (x - hi.astype(F32)).astype(BF)
    return _dot(lhs_bf_exact, hi) + _dot(lhs_bf_exact, lo)


def _norm_mod(x, g, shift, scale):
    ms = jnp.mean(x * x, axis=-1, keepdims=True)
    return (x * lax.rsqrt(ms + EPS) * g) * (1.0 + scale) + shift


def _adaln_body(c_ref, w_ref, b_ref, o_ref):
    s = _silu(c_ref[...])
    o_ref[0] = _dot(s.astype(BF), w_ref[0].astype(BF)) + b_ref[0]


def _adaln(cond, w_mod, b_mod):
    depth, _, n = w_mod.shape
    rows = cond.shape[0]
    tn = n // 4
    return pl.pallas_call(
        _adaln_body,
        grid=(depth, n // tn),
        in_specs=[pl.BlockSpec((rows, D), lambda l, j: (0, 0)),
                  pl.BlockSpec((1, D, tn), lambda l, j: (l, 0, j)),
                  pl.BlockSpec((1, 1, tn), lambda l, j: (l, 0, j))],
        out_specs=pl.BlockSpec((1, rows, tn), lambda l, j: (l, 0, j)),
        out_shape=jax.ShapeDtypeStruct((depth, rows, n), F32),
        compiler_params=_params(("arbitrary", "arbitrary")),
        name="adaln",
    )(cond, w_mod, b_mod.reshape(depth, 1, n))


def _qk_norm_rope(p, gqk, bd, ct, st):
    pp = p * p
    sums = []
    for c in range(p.shape[1] // LANES):
        x = pp[:, c * LANES:(c + 1) * LANES]
        hi = x.astype(BF)
        lo = (x - hi.astype(F32)).astype(BF)
        sums.append(_dot(jnp.concatenate([hi, lo], axis=1), bd))
    ss = jnp.concatenate(sums, axis=1)
    y = p * lax.rsqrt(ss * (1.0 / HD) + EPS) * gqk
    lane = lax.broadcasted_iota(jnp.int32, ct.shape, 1)
    first = (lane % HD) < (HD // 2)
    outs = []
    for c in range(p.shape[1] // LANES):
        yc = y[:, c * LANES:(c + 1) * LANES]
        partner = jnp.where(first, pltpu.roll(yc, LANES - HD // 2, 1), pltpu.roll(yc, HD // 2, 1))
        outs.append(yc * ct + partner * st)
    return jnp.concatenate(outs, axis=1)


def _values_with_ones(p):
    lane = lax.broadcasted_iota(jnp.int32, (p.shape[0], LANES), 1)
    low = lane < HD
    tiles = []
    for c in range(p.shape[1] // LANES):
        x = p[:, c * LANES:(c + 1) * LANES]
        tiles += [jnp.where(low, x, 1.0), jnp.where(low, 1.0, x)]
    return jnp.concatenate(tiles, axis=1)


def _proj_body(x_ref, g_ref, m_ref, w_ref, *rest, widths, kinds, scales, col0):
    if "qk" in kinds:
        ct_ref, st_ref, gqk_ref, bd_ref = rest[:4]
        o_refs = rest[4:]
    else:
        o_refs = rest
    h = _norm_mod(x_ref[...], g_ref[...], m_ref[0, 0:1, :], m_ref[0, 1:2, :]).astype(BF)
    off = col0
    oi = 0
    for wd, kind, scale in zip(widths, kinds, scales):
        p = _dot(h, w_ref[:, off:off + wd])
        off += wd
        if scale != 1.0:
            p = p * scale
        if kind == "qk":
            p = _qk_norm_rope(p, gqk_ref[...], bd_ref[...], ct_ref[...], st_ref[...])
            nq = A_HEADS * HD
            o_refs[oi][...] = p[:, :nq].astype(BF)
            o_refs[oi + 1][...] = p[:, nq:].astype(BF)
            oi += 2
        else:
            if kind == "values":
                p = _values_with_ones(p)
            o_refs[oi][...] = p.astype(BF)
            oi += 1


def _proj(x2, g, mpack, mod_of_tile, w, widths, kinds, out_widths, tm, rope_args=None, scales=None, col0=0):
    rows = x2.shape[0]
    n = w.shape[1]
    in_specs = [pl.BlockSpec((tm, D), lambda i: (i, 0)),
                pl.BlockSpec((1, D), lambda i: (0, 0)),
                pl.BlockSpec((1, MOD_ROWS, D), lambda i: (mod_of_tile(i), 0, 0)),
                pl.BlockSpec((D, n), lambda i: (0, 0))]
    args = [x2, g.reshape(1, D), mpack, w]
    if rope_args is not None:
        ct, st, gqk, bd = rope_args
        nt = ct.shape[0] // tm
        in_specs += [pl.BlockSpec((tm, LANES), lambda i: (i % nt, 0)),
                     pl.BlockSpec((tm, LANES), lambda i: (i % nt, 0)),
                     pl.BlockSpec(gqk.shape, lambda i: (0, 0)),
                     pl.BlockSpec(bd.shape, lambda i: (0, 0))]
        args += [ct, st, gqk, bd]
    return pl.pallas_call(
        functools.partial(_proj_body, widths=widths, kinds=kinds, scales=scales or (1.0,) * len(widths),
                          col0=col0),
        grid=(rows // tm,),
        in_specs=in_specs,
        out_specs=[pl.BlockSpec((tm, ow), lambda i: (i, 0)) for ow in out_widths],
        out_shape=[jax.ShapeDtypeStruct((rows, ow), BF) for ow in out_widths],
        compiler_params=_params(("arbitrary",)),
        name="norm_mod_proj",
    )(*args)


GQA_UNIT = 2
GQA_TQ = 512

def _gqa_body(q_ref, kc_ref, vc_ref, *rest, with_lat, tq):
    if with_lat:
        kl_ref, vl_ref, o_ref = rest
    else:
        (o_ref,) = rest
    units = [(j, t0) for j in range(A_KV) for t0 in range(0, A_GROUP, GQA_UNIT)]
    lane = lax.broadcasted_iota(jnp.int32, (tq, LANES), 1)
    own = [(lane // HD) == j for j in range(A_KV)]

    def scores(unit):
        j, t0 = unit
        qg = jnp.concatenate(
            [jnp.where(own[j], q_ref[:, LANES * t:LANES * (t + 1)], 0.0) for t in range(t0, t0 + GQA_UNIT)],
            axis=0)
        s_c = _dot_nt(qg, kc_ref[...])
        s_l = _dot_nt(qg, kl_ref[...]) if with_lat else None
        return s_c, s_l

    res = {}
    nxt = scores(units[0])
    for u, (j, t0) in enumerate(units):
        s_c, s_l = nxt
        if u + 1 < len(units):
            nxt = scores(units[u + 1])
        vcols = slice(LANES * j, LANES * (j + 1))
        m = jnp.max(s_c, axis=1, keepdims=True)
        if with_lat:
            m = jnp.maximum(m, jnp.max(s_l, axis=1, keepdims=True))
        acc = _dot(jnp.exp2(s_c - m).astype(BF), vc_ref[:, vcols])
        if with_lat:
            acc = acc + _dot(jnp.exp2(s_l - m).astype(BF), vl_ref[:, vcols])
        den_lane = HD * (1 - j)
        o = acc / acc[:, den_lane:den_lane + 1]
        for g in range(GQA_UNIT):
            res[(j, t0 + g)] = o[g * tq:(g + 1) * tq]
    for t in range(A_GROUP):
        o_ref[:, LANES * t:LANES * (t + 1)] = jnp.where(own[0], res[(0, t)], res[(1, t)]).astype(BF)


def _gqa(q, k_ctx, v_ctx, k_lat, v_lat, bsz, tq):
    sq = q.shape[0] // bsz
    nc = k_ctx.shape[0] // bsz
    nq = sq // tq
    with_lat = k_lat is not None
    in_specs = [pl.BlockSpec((tq, A_HEADS * HD), lambda b, i: (b * nq + i, 0)),
                pl.BlockSpec((nc, A_KV * HD), lambda b, i: (b, 0)),
                pl.BlockSpec((nc, 2 * A_KV * HD), lambda b, i: (b, 0))]
    args = [q, k_ctx, v_ctx]
    if with_lat:
        sl = k_lat.shape[0] // bsz
        in_specs += [pl.BlockSpec((sl, A_KV * HD), lambda b, i: (b, 0)),
                     pl.BlockSpec((sl, 2 * A_KV * HD), lambda b, i: (b, 0))]
        args += [k_lat, v_lat]
    return pl.pallas_call(
        functools.partial(_gqa_body, with_lat=with_lat, tq=tq),
        grid=(bsz, nq),
        in_specs=in_specs,
        out_specs=pl.BlockSpec((tq, A_HEADS * HD), lambda b, i: (b * nq + i, 0)),
        out_shape=jax.ShapeDtypeStruct(q.shape, BF),
        compiler_params=_params(("arbitrary", "arbitrary")),
        name="gqa_attention",
    )(*args)


GLA_Q = 0
GLA_K = B_HEADS * B_DK
GLA_V = 2 * B_HEADS * B_DK
GLA_R = GLA_V + B_HEADS * B_DV
GLA_W = GLA_R + B_HEADS * B_DV
GLA_GROUP = 8


def _gla_body(gc_ref, gl_ref, zc_ref, zl_ref, w2f_ref, w2b_ref, bf_ref, bb_ref, gg_ref,
              oc_ref, ol_ref, ofc_ref, ofl_ref, st_ref):
    nk = B_HEADS * B_DK
    ch = B_CHUNK
    row = lax.broadcasted_iota(jnp.int32, (ch, ch), 0)
    col = lax.broadcasted_iota(jnp.int32, (ch, ch), 1)
    lane = lax.broadcasted_iota(jnp.int32, (ch, nk), 1)
    head_masks = [(lane // B_DK) == h for h in range(B_HEADS)]

    def group_out(g_ref, z_ref, r0s, fwd, s_t):
        w2 = (w2f_ref if fwd else w2b_ref)[...]
        bias = (bf_ref if fwd else bb_ref)[...]
        keep = (row >= col) if fwd else (row <= col)
        tri = jnp.where(keep, 1.0, 0.0).astype(BF)
        blks = [g_ref[pl.ds(r0, ch), :] for r0 in r0s]
        pres = [_dot(z_ref[pl.ds(r0, ch), :], w2) + bias for r0 in r0s]
        las = [-(jnp.maximum(-p, 0.0) + jnp.log(1.0 + jnp.exp(-jnp.abs(p)))) * (1.0 / B_TAU) for p in pres]
        bcums = [_split_dot(tri, la) for la in las]
        qms, kis, kss, decs = [], [], [], []
        for blk, bcum in zip(blks, bcums):
            q = blk[:, GLA_Q:GLA_Q + nk].astype(F32)
            k = blk[:, GLA_K:GLA_K + nk].astype(F32)
            b_last = bcum[ch - 1:ch, :] if fwd else bcum[0:1, :]
            q_dec = q * jnp.exp(bcum)
            k_state = k * jnp.exp(b_last - bcum)
            qms.append([jnp.where(m, q_dec, 0.0).astype(BF) for m in head_masks])
            kis.append((k * jnp.exp(-bcum)).astype(BF))
            kss.append([jnp.where(m, k_state, 0.0).astype(BF) for m in head_masks])
            decs.append(jnp.exp(b_last))
        vs = [[blk[:, GLA_V + B_DV * h:GLA_V + B_DV * (h + 1)] for h in range(B_HEADS)] for blk in blks]
        atts = [[jnp.where(keep, _dot_nt(qm, ki), 0.0).astype(BF) for qm in qmu] for qmu, ki in zip(qms, kis)]
        o_intra = [[_dot(a, v) for a, v in zip(au, vu)] for au, vu in zip(atts, vs)]
        d_ss = []
        for vu, ksu in zip(vs, kss):
            d_s = _dot_tn(vu[0], ksu[0])
            for h in range(1, B_HEADS):
                d_s = d_s + _dot_tn(vu[h], ksu[h])
            d_ss.append(d_s)
        outs = []
        for u in range(len(r0s)):
            s_b = s_t.astype(BF)
            outs.append(jnp.concatenate(
                [o_intra[u][h] + _dot_nt(qms[u][h], s_b) for h in range(B_HEADS)], axis=1))
            s_t = decs[u] * s_t + d_ss[u]
        return outs, blks, s_t

    def fwd_pass(g_ref, z_ref, of_ref, n):
        grp = min(GLA_GROUP, n)
        assert n % grp == 0

        def body(i, carry):
            r0s = [pl.multiple_of((i * grp + u) * ch, ch) for u in range(grp)]
            outs, _, s_t = group_out(g_ref, z_ref, r0s, True, st_ref[...])
            for r0, o in zip(r0s, outs):
                of_ref[pl.ds(r0, ch), :] = o
            st_ref[...] = s_t
            return carry
        lax.fori_loop(0, n // grp, body, 0)

    def bwd_pass(g_ref, z_ref, of_ref, o_ref, n):
        grp = min(GLA_GROUP, n)
        assert n % grp == 0

        def body(i, carry):
            r0s = [pl.multiple_of((n - 1 - (i * grp + u)) * ch, ch) for u in range(grp)]
            outs, blks, s_t = group_out(g_ref, z_ref, r0s, False, st_ref[...])
            for r0, o, blk in zip(r0s, outs, blks):
                o = o + of_ref[pl.ds(r0, ch), :]
                parts = []
                for h in range(B_HEADS):
                    oh = o[:, B_DV * h:B_DV * (h + 1)]
                    ms = jnp.mean(oh * oh, axis=-1, keepdims=True)
                    parts.append(oh * lax.rsqrt(ms + EPS))
                y = jnp.concatenate(parts, axis=1) * gg_ref[...]
                r = blk[:, GLA_R:GLA_R + B_HEADS * B_DV].astype(F32)
                o_ref[pl.ds(r0, ch), :] = (y * _silu(r)).astype(BF)
            st_ref[...] = s_t
            return carry
        lax.fori_loop(0, n // grp, body, 0)

    nc = gc_ref.shape[0] // ch
    nl = gl_ref.shape[0] // ch
    st_ref[...] = jnp.zeros_like(st_ref)
    fwd_pass(gc_ref, zc_ref, ofc_ref, nc)
    fwd_pass(gl_ref, zl_ref, ofl_ref, nl)
    st_ref[...] = jnp.zeros_like(st_ref)
    bwd_pass(gc_ref, zc_ref, ofc_ref, oc_ref, nc)
    bwd_pass(gl_ref, zl_ref, ofl_ref, ol_ref, nl)


def _gla(g_ctx, g_lat, z_ctx, z_lat, w2f, w2b, b_f, b_b, gg, bsz):
    nc = g_ctx.shape[0] // bsz
    sl = g_lat.shape[0] // bsz
    nk = B_HEADS * B_DK
    nv = B_HEADS * B_DV
    full = lambda shape: pl.BlockSpec(shape, lambda b: (0, 0))
    return pl.pallas_call(
        _gla_body,
        grid=(bsz,),
        in_specs=[pl.BlockSpec((nc, GLA_W), lambda b: (b, 0)),
                  pl.BlockSpec((sl, GLA_W), lambda b: (b, 0)),
                  pl.BlockSpec((nc, LANES), lambda b: (b, 0)),
                  pl.BlockSpec((sl, LANES), lambda b: (b, 0)),
                  full((LANES, nk)), full((LANES, nk)), full((1, nk)), full((1, nk)), full((1, nv))],
        out_specs=[pl.BlockSpec((nc, nv), lambda b: (b, 0)),
                   pl.BlockSpec((sl, nv), lambda b: (b, 0))],
        out_shape=[jax.ShapeDtypeStruct((g_ctx.shape[0], nv), BF),
                   jax.ShapeDtypeStruct((g_lat.shape[0], nv), BF)],
        scratch_shapes=[pltpu.VMEM((nc, nv), F32), pltpu.VMEM((sl, nv), F32),
                        pltpu.VMEM((B_DV, nk), F32)],
        compiler_params=_params(("arbitrary",)),
        name="gla_bidir",
    )(g_ctx, g_lat, z_ctx, z_lat, w2f, w2b, b_f, b_b, gg)


def _mixer_residual(x_ref, m_ref, pair_refs):
    acc = _dot(pair_refs[0][...], pair_refs[1][...])
    for t in range(1, len(pair_refs) // 2):
        acc = acc + _dot(pair_refs[2 * t][...], pair_refs[2 * t + 1][...])
    return x_ref[...] + m_ref[0, 2:3, :] * acc


def _pair_specs(pairs, tm, index):
    in_specs, args = [], []
    for a, w in pairs:
        in_specs += [pl.BlockSpec((tm, a.shape[1]), index), pl.BlockSpec(w.shape, lambda *_: (0, 0))]
        args += [a, w]
    return in_specs, args


def _ffn_body(x_ref, g_ref, m_ref, *rest, n_pairs):
    pair_refs = rest[:2 * n_pairs]
    wg_ref, wu_ref, wd_ref, o_ref, xn_ref, h_ref, acc_ref = rest[2 * n_pairs:]
    j = pl.program_id(1)

    def swiglu_step():
        h = h_ref[...]
        a = (_silu(_dot(h, wg_ref[...])) * _dot(h, wu_ref[...])).astype(BF)
        return _dot(a, wd_ref[...])

    @pl.when(j == 0)
    def _():
        xn = _mixer_residual(x_ref, m_ref, pair_refs)
        xn_ref[...] = xn
        h_ref[...] = _norm_mod(xn, g_ref[...], m_ref[0, 3:4, :], m_ref[0, 4:5, :]).astype(BF)
        acc_ref[...] = swiglu_step()

    last = pl.num_programs(1) - 1

    @pl.when((j > 0) & (j < last))
    def _():
        acc_ref[...] += swiglu_step()

    @pl.when(j == last)
    def _():
        o_ref[...] = xn_ref[...] + m_ref[0, 5:6, :] * (acc_ref[...] + swiglu_step())


def _ffn(x2, g, mpack, mod_of_tile, pairs, wg, wu, wd, tm, tf):
    rows = x2.shape[0]
    ff = wg.shape[1]
    assert ff // tf >= 2
    pair_specs, pair_args = _pair_specs(pairs, tm, lambda i, j: (i, 0))
    return pl.pallas_call(
        functools.partial(_ffn_body, n_pairs=len(pairs)),
        grid=(rows // tm, ff // tf),
        in_specs=[pl.BlockSpec((tm, D), lambda i, j: (i, 0)),
                  pl.BlockSpec((1, D), lambda i, j: (0, 0)),
                  pl.BlockSpec((1, MOD_ROWS, D), lambda i, j: (mod_of_tile(i), 0, 0)),
                  *pair_specs,
                  pl.BlockSpec((D, tf), lambda i, j: (0, j)),
                  pl.BlockSpec((D, tf), lambda i, j: (0, j)),
                  pl.BlockSpec((tf, D), lambda i, j: (j, 0))],
        out_specs=pl.BlockSpec((tm, D), lambda i, j: (i, 0)),
        out_shape=jax.ShapeDtypeStruct((rows, D), F32),
        scratch_shapes=[pltpu.VMEM((tm, D), F32), pltpu.VMEM((tm, D), BF), pltpu.VMEM((tm, D), F32)],
        compiler_params=_params(("arbitrary", "arbitrary")),
        name="ffn_swiglu",
    )(x2, g.reshape(1, D), mpack, *pair_args, wg, wu, wd)


NA_QROWS = 4
NA_KROWS = 12
NA_TQ = NA_QROWS * GRID_W
NA_HG = 16
NA_W = NA_HG * HD


def _na_body(q_ref, k0_ref, k1_ref, k2_ref, v0_ref, v1_ref, v2_ref, kc_ref, vc_ref, b_ref, o_ref):
    pair = LANES // HD
    lane = lax.broadcasted_iota(jnp.int32, (NA_TQ, LANES), 1)
    own = [(lane // HD) == half for half in range(pair)]

    def scores(hh):
        tile = slice(LANES * (hh // pair), LANES * (hh // pair + 1))
        q = jnp.where(own[hh % pair], q_ref[:, tile], 0.0)
        s_loc = jnp.concatenate([_dot_nt(q, kr[:, tile]) for kr in (k0_ref, k1_ref, k2_ref)], axis=1)
        bias = b_ref[0, :, hh].reshape(NA_TQ, NA_KROWS * GRID_W)
        return s_loc + bias, _dot_nt(q, kc_ref[:, tile])

    nxt = scores(0)
    outs = []
    for hh in range(NA_HG):
        s_loc, s_ctx = nxt
        if hh + 1 < NA_HG:
            nxt = scores(hh + 1)
        vcols = slice(LANES * hh, LANES * (hh + 1))
        m = jnp.maximum(jnp.max(s_loc, axis=1, keepdims=True), jnp.max(s_ctx, axis=1, keepdims=True))
        p_loc = jnp.exp2(s_loc - m).astype(BF)
        acc = _dot(jnp.exp2(s_ctx - m).astype(BF), vc_ref[:, vcols])
        for t, vr in enumerate((v0_ref, v1_ref, v2_ref)):
            acc = acc + _dot(p_loc[:, t * NA_TQ:(t + 1) * NA_TQ], vr[:, vcols])
        den_lane = HD * (1 - hh % pair)
        outs.append(acc / acc[:, den_lane:den_lane + 1])
        if len(outs) == pair:
            t0 = hh // pair
            o_ref[:, LANES * t0:LANES * (t0 + 1)] = jnp.where(own[0], outs[0], outs[1]).astype(BF)
            outs = []


def _na_bias(rpb, rows):
    heads, n_dr, n_dc = rpb.shape
    nblk = rows // NA_QROWS
    n_off = n_dr + 2 * NA_KROWS
    rpad = jnp.pad(rpb.astype(F32) * LOG2E, ((0, 0), (NA_KROWS, NA_KROWS), (0, 0)), constant_values=NEG_INF)
    rvals = jnp.repeat(rpad.transpose(0, 2, 1), GRID_W, axis=2)
    cuts = []
    for kind, j in enumerate((0, min(1, nblk - 1), nblk - 1)):
        ws = NA_QROWS * int(np.clip(j - 1, 0, nblk - 3))
        for qr in range(NA_QROWS):
            r = NA_QROWS * j + qr
            rs = int(np.clip(r - NA_ROWS // 2, 0, rows - NA_ROWS))
            cuts.append((kind, qr, NA_KROWS + ws - r + NA_ROWS - 1, max(rs - ws, 0), min(rs + NA_ROWS - ws, NA_KROWS)))
    width = NA_KROWS * GRID_W

    def body(r_ref, o_ref):
        shape = (GRID_W, n_off * GRID_W)
        qc = lax.broadcasted_iota(jnp.int32, shape, 0)
        kc = lax.broadcasted_iota(jnp.int32, shape, 1) % GRID_W
        delta = kc - qc + NA_COLS - 1
        cs = jnp.clip(qc - NA_COLS // 2, 0, GRID_W - NA_COLS)
        toe = jnp.full(shape, NEG_INF, F32)
        for d in range(n_dc):
            toe = jnp.where(delta == d, r_ref[0, d:d + 1, :], toe)
        toe = jnp.where((kc >= cs) & (kc < cs + NA_COLS), toe, NEG_INF)
        key_row = lax.broadcasted_iota(jnp.int32, (GRID_W, width), 1) // GRID_W
        for kind, qr, d0, lo, hi in cuts:
            blk = toe[:, d0 * GRID_W:d0 * GRID_W + width]
            o_ref[kind, qr, 0] = jnp.where((key_row >= lo) & (key_row < hi), blk, NEG_INF)

    return pl.pallas_call(
        body,
        grid=(heads,),
        in_specs=[pl.BlockSpec((1, n_dc, n_off * GRID_W), lambda h: (h, 0, 0))],
        out_specs=pl.BlockSpec((3, NA_QROWS, 1, GRID_W, width), lambda h: (0, 0, h, 0, 0)),
        out_shape=jax.ShapeDtypeStruct((3, NA_QROWS, heads, GRID_W, width), F32),
        compiler_params=_params(("arbitrary",)),
        name="natten_bias",
    )(rvals)


def _natten(q, k, v, k_ctx, v_ctx, bias, bsz):
    s = q.shape[0] // bsz
    nc = k_ctx.shape[0] // bsz
    nblk = s // NA_TQ
    ngrp = C_HEADS // NA_HG
    v_w = NA_HG * LANES

    def kv_spec(t, width):
        return pl.BlockSpec((NA_TQ, width),
                            lambda j, g, b: (b * nblk + jnp.clip(j - 1, 0, nblk - 3) + t, g))

    def kind(j):
        return jnp.where(j == 0, 0, jnp.where(j == nblk - 1, 2, 1))

    return pl.pallas_call(
        _na_body,
        grid=(nblk, ngrp, bsz),
        in_specs=[pl.BlockSpec((NA_TQ, NA_W), lambda j, g, b: (b * nblk + j, g)),
                  kv_spec(0, NA_W), kv_spec(1, NA_W), kv_spec(2, NA_W),
                  kv_spec(0, v_w), kv_spec(1, v_w), kv_spec(2, v_w),
                  pl.BlockSpec((nc, NA_W), lambda j, g, b: (b, g)),
                  pl.BlockSpec((nc, v_w), lambda j, g, b: (b, g)),
                  pl.BlockSpec((1, NA_QROWS, NA_HG, GRID_W, NA_KROWS * GRID_W),
                               lambda j, g, b: (kind(j), 0, g, 0, 0))],
        out_specs=pl.BlockSpec((NA_TQ, NA_W), lambda j, g, b: (b * nblk + j, g)),
        out_shape=jax.ShapeDtypeStruct(q.shape, BF),
        compiler_params=_params(("arbitrary", "arbitrary", "arbitrary")),
        name="natten",
    )(q, k, k, k, v, v, v, k_ctx, v_ctx, bias)


MOE_TM = 512
MOE_TF = 1792
META_E1, META_E2, META_R1, META_R2, META_G1, META_G2 = range(6)
ROW_TILE = D // LANES


def _to_token_tiles(dst_ref, value):
    n = value.shape[0]
    for s in range(ROW_TILE):
        dst_ref[pl.ds(s, n, stride=ROW_TILE), :] = value[:, LANES * s:LANES * (s + 1)]


def _from_token_tiles(src_ref, n):
    return jnp.concatenate([src_ref[pl.ds(s, n, stride=ROW_TILE), :] for s in range(ROW_TILE)], axis=1)


def _route_body(x_ref, g_ref, m_ref, *rest, n_pairs):
    pair_refs = rest[:2 * n_pairs]
    wr_ref, xn_ref, h_ref, meta_ref, rec_ref, cnt_ref, tri_ref, carry_ref = rest[2 * n_pairs:]
    tm = x_ref.shape[0]

    @pl.when(pl.program_id(0) == 0)
    def _():
        row = lax.broadcasted_iota(jnp.int32, (tm, tm), 0)
        col = lax.broadcasted_iota(jnp.int32, (tm, tm), 1)
        tri_ref[...] = jnp.where(row >= col, 1.0, 0.0).astype(BF)
        carry_ref[...] = jnp.zeros_like(carry_ref)

    xn = _mixer_residual(x_ref, m_ref, pair_refs)
    xn_ref[...] = xn
    h = _norm_mod(xn, g_ref[...], m_ref[0, 3:4, :], m_ref[0, 4:5, :])
    _to_token_tiles(h_ref, h)
    h_hi = h.astype(BF)
    h_lo = (h - h_hi.astype(F32)).astype(BF)
    w = wr_ref[...]
    w_hi = w.astype(BF).astype(F32)
    w_cat = (w_hi + pltpu.roll(w - w_hi, N_EXPERTS, 1)).astype(BF)
    terms = _dot(h_hi, w_cat) + _dot(h_lo, w_cat)
    logits = terms + pltpu.roll(terms, LANES - N_EXPERTS, 1)
    lane = lax.broadcasted_iota(jnp.int32, logits.shape, 1)
    lg = jnp.where(lane < N_EXPERTS, logits, NEG_INF)
    m1 = jnp.max(lg, axis=1, keepdims=True)
    i1 = jnp.min(jnp.where(lg == m1, lane, LANES), axis=1, keepdims=True)
    lg2 = jnp.where(lane == i1, NEG_INF, lg)
    m2 = jnp.max(lg2, axis=1, keepdims=True)
    i2 = jnp.min(jnp.where(lg2 == m2, lane, LANES), axis=1, keepdims=True)
    e = jnp.exp(m2 - m1)
    g1 = 1.0 / (1.0 + e)
    g2 = e / (1.0 + e)
    oh = jnp.where(lane == i1, 1.0, 0.0) + jnp.where(lane == i2, 1.0, 0.0)
    cum = _dot(tri_ref[...], oh.astype(BF))
    base = carry_ref[0:1, :] + cum - oh
    r1 = jnp.sum(jnp.where(lane == i1, base, 0.0), axis=1, keepdims=True)
    r2 = jnp.sum(jnp.where(lane == i2, base, 0.0), axis=1, keepdims=True)
    carry_ref[...] = carry_ref[...] + cum[tm - 1:tm, :]
    cnt_ref[...] = carry_ref[...]
    meta = jnp.zeros(logits.shape, F32)
    for ln, val in ((META_E1, i1.astype(F32)), (META_E2, i2.astype(F32)), (META_R1, r1), (META_R2, r2),
                    (META_G1, g1), (META_G2, g2)):
        meta = jnp.where(lane == ln, val, meta)
    meta_ref[...] = meta
    rec_ref[...] = meta.T[:rec_ref.shape[0], :]


def _route(x2, g, mpack, mod_of_tile, pairs, wr, tm):
    rows = x2.shape[0]
    pair_specs, pair_args = _pair_specs(pairs, tm, lambda i: (i, 0))
    return pl.pallas_call(
        functools.partial(_route_body, n_pairs=len(pairs)),
        grid=(rows // tm,),
        in_specs=[pl.BlockSpec((tm, D), lambda i: (i, 0)),
                  pl.BlockSpec((1, D), lambda i: (0, 0)),
                  pl.BlockSpec((1, MOD_ROWS, D), lambda i: (mod_of_tile(i), 0, 0)),
                  *pair_specs,
                  pl.BlockSpec((D, LANES), lambda i: (0, 0))],
        out_specs=[pl.BlockSpec((tm, D), lambda i: (i, 0)),
                   pl.BlockSpec((tm * ROW_TILE, LANES), lambda i: (i, 0)),
                   pl.BlockSpec((tm, LANES), lambda i: (i, 0)),
                   pl.BlockSpec((8, tm), lambda i: (0, i)),
                   pl.BlockSpec((8, LANES), lambda i: (0, 0))],
        out_shape=[jax.ShapeDtypeStruct((rows, D), F32),
                   jax.ShapeDtypeStruct((rows * ROW_TILE, LANES), F32),
                   jax.ShapeDtypeStruct((rows, LANES), F32),
                   jax.ShapeDtypeStruct((8, rows), F32),
                   jax.ShapeDtypeStruct((8, LANES), F32)],
        scratch_shapes=[pltpu.VMEM((tm, tm), BF), pltpu.VMEM((8, LANES), F32)],
        compiler_params=_params(("arbitrary",)),
        name="moe_route",
    )(x2, g.reshape(1, D), mpack, *pair_args, wr)


def _token_copy(src_ref, src_tok, dst_ref, dst_tok, sem):
    src = src_ref.at[pl.ds(pl.multiple_of(src_tok * ROW_TILE, ROW_TILE), ROW_TILE), :]
    dst = dst_ref.at[pl.ds(pl.multiple_of(dst_tok * ROW_TILE, ROW_TILE), ROW_TILE), :]
    return pltpu.make_async_copy(src, dst, sem)


def _experts_body(te_ref, nu_ref, cur_ref, nxt_ref, dprev_ref, h_hbm, wg_ref, wu_ref, wd_ref, y_hbm,
                  xbuf, h_ref, acc_ref, ybuf, gsem, ssem, *, n_steps):
    assert n_steps >= 2
    i = pl.program_id(0)
    j = pl.program_id(1)
    used = i < nu_ref[0]
    slot = i % 2
    other = 1 - slot

    def gather_row(idx_ref, r, sl, priority=0):
        _token_copy(h_hbm, idx_ref[0, 0, r], xbuf.at[sl], r, gsem.at[sl]).start(priority=priority)

    def issue_gathers():
        for r in range(MOE_TM):
            gather_row(nxt_ref, r, other, priority=r % 2)

    def issue_scatters():
        for r in range(MOE_TM):
            _token_copy(ybuf.at[other], r, y_hbm, dprev_ref[0, 0, r], ssem.at[other]).start(priority=r % 2)

    def wait_rows(buf, sem):
        pltpu.make_async_copy(buf, buf, sem).wait()

    def swiglu_step():
        h = h_ref[...]
        a = (_silu(_dot(h, wg_ref[0])) * _dot(h, wu_ref[0])).astype(BF)
        return _dot(a, wd_ref[0])

    @pl.when((j == 0) & (i == 0))
    def _():
        ybuf[...] = jnp.zeros_like(ybuf)
        n_spare = MOE_TM * ROW_TILE
        for sl in range(2):
            spare = pltpu.make_async_copy(
                ybuf.at[sl], y_hbm.at[pl.ds(y_hbm.shape[0] - (2 - sl) * n_spare, n_spare), :], ssem.at[sl])
            spare.start()
            spare.wait()

        def body(r, c):
            gather_row(cur_ref, r, 0)
            return c
        lax.fori_loop(0, MOE_TM, body, 0, unroll=8)

    @pl.when(j == 0)
    def _():
        wait_rows(xbuf.at[slot], gsem.at[slot])

    @pl.when(used & (j == 0))
    def _():
        h_ref[...] = _from_token_tiles(xbuf.at[slot], MOE_TM).astype(BF)
        acc_ref[...] = swiglu_step()
        issue_gathers()

    if n_steps > 2:
        @pl.when(used & (j > 0) & (j < n_steps - 1))
        def _():
            acc_ref[...] += swiglu_step()

    def wait_own_ybuf():
        @pl.when(i > 0)
        def _():
            wait_rows(ybuf.at[slot], ssem.at[slot])

    @pl.when(used & (j == n_steps - 1))
    def _():
        wait_own_ybuf()
        _to_token_tiles(ybuf.at[slot], acc_ref[...] + swiglu_step())
        issue_scatters()

    @pl.when(jnp.logical_not(used) & (j == 0))
    def _():
        issue_gathers()

    @pl.when(jnp.logical_not(used) & (j == n_steps - 1))
    def _():
        wait_own_ybuf()
        issue_scatters()

    @pl.when((j == n_steps - 1) & (i == pl.num_programs(0) - 1))
    def _():
        wait_rows(ybuf.at[other], ssem.at[other])
        wait_rows(xbuf.at[other], gsem.at[other])


def _experts(h_tiles, tok_of_slot, dest_of_slot, tile_expert, n_used, wg, wu, wd, out_tokens):
    n_tiles = tile_expert.shape[0]
    ff = wg.shape[2]
    nf = ff // MOE_TF
    spare_rows = out_tokens - MOE_TM + jnp.arange(MOE_TM, dtype=jnp.int32)
    dest_prev = jnp.concatenate([spare_rows, dest_of_slot[:-MOE_TM]])

    def jj(i, j, nu):
        return jnp.where(i < nu[0], j, nf - 1)

    def smem_rows(index):
        return pl.BlockSpec((1, 1, MOE_TM), lambda i, j, te, nu: (index(i), 0, 0), memory_space=pltpu.SMEM)

    grid_spec = pltpu.PrefetchScalarGridSpec(
        num_scalar_prefetch=2,
        grid=(n_tiles, nf),
        in_specs=[smem_rows(lambda i: i),
                  smem_rows(lambda i: jnp.minimum(i + 1, n_tiles - 1)),
                  smem_rows(lambda i: i),
                  pl.BlockSpec(memory_space=pl.ANY),
                  pl.BlockSpec((1, D, MOE_TF), lambda i, j, te, nu: (te[i], 0, jj(i, j, nu))),
                  pl.BlockSpec((1, D, MOE_TF), lambda i, j, te, nu: (te[i], 0, jj(i, j, nu))),
                  pl.BlockSpec((1, MOE_TF, D), lambda i, j, te, nu: (te[i], jj(i, j, nu), 0))],
        out_specs=pl.BlockSpec(memory_space=pl.ANY),
        scratch_shapes=[pltpu.VMEM((2, MOE_TM * ROW_TILE, LANES), F32),
                        pltpu.VMEM((MOE_TM, D), BF),
                        pltpu.VMEM((MOE_TM, D), F32),
                        pltpu.VMEM((2, MOE_TM * ROW_TILE, LANES), F32),
                        pltpu.SemaphoreType.DMA((2,)),
                        pltpu.SemaphoreType.DMA((2,))])
    tok3 = tok_of_slot.reshape(n_tiles, 1, MOE_TM)
    return pl.pallas_call(
        functools.partial(_experts_body, n_steps=nf),
        grid_spec=grid_spec,
        out_shape=jax.ShapeDtypeStruct((out_tokens * ROW_TILE, LANES), F32),
        compiler_params=_params(("arbitrary", "arbitrary")),
        name="moe_experts",
    )(tile_expert, n_used, tok3, tok3, dest_prev.reshape(n_tiles, 1, MOE_TM), h_tiles, wg, wu, wd)


def _combine_body(y1_ref, y2_ref, x_ref, m_ref, meta_ref, gf_ref, o_ref):
    n = x_ref.shape[0]
    meta = meta_ref[...]
    lane = lax.broadcasted_iota(jnp.int32, meta.shape, 1)
    g1 = jnp.sum(jnp.where(lane == META_G1, meta, 0.0), axis=1, keepdims=True)
    g2 = jnp.sum(jnp.where(lane == META_G2, meta, 0.0), axis=1, keepdims=True)
    moe = g1 * _from_token_tiles(y1_ref, n) + g2 * _from_token_tiles(y2_ref, n)
    y = x_ref[...] + m_ref[0, 5:6, :] * moe
    ms = jnp.mean(y * y, axis=-1, keepdims=True)
    o_ref[...] = y * lax.rsqrt(ms + EPS) * gf_ref[...]


def _combine(x2, mpack, mod_of_row, meta, y_tiles, g_final, tm):
    rows = x2.shape[0]
    nt = rows // tm
    return pl.pallas_call(
        _combine_body,
        grid=(nt,),
        in_specs=[pl.BlockSpec((tm * ROW_TILE, LANES), lambda i: (i, 0)),
                  pl.BlockSpec((tm * ROW_TILE, LANES), lambda i: (i + nt, 0)),
                  pl.BlockSpec((tm, D), lambda i: (i, 0)),
                  pl.BlockSpec((1, MOD_ROWS, D), lambda i: (mod_of_row(i * tm), 0, 0)),
                  pl.BlockSpec((tm, LANES), lambda i: (i, 0)),
                  pl.BlockSpec((1, D), lambda i: (0, 0))],
        out_specs=pl.BlockSpec((tm, D), lambda i: (i, 0)),
        out_shape=jax.ShapeDtypeStruct((rows, D), F32),
        compiler_params=_params(("arbitrary",)),
        name="moe_combine",
    )(y_tiles, y_tiles, x2, mpack, meta, g_final.reshape(1, D))


def _moe(x2, g, mpack, mod_of_tile, mod_of_row, pairs, wr, wg, wu, wd, g_final, tm):
    rows = x2.shape[0]
    ne = wg.shape[0]
    x2, h_tiles, meta, rec, cnt = _route(x2, g, mpack, mod_of_tile, pairs, wr, tm)
    counts = cnt[0, :ne].astype(jnp.int32)
    tiles_e = (counts + MOE_TM - 1) // MOE_TM
    tile_end = jnp.cumsum(tiles_e)
    group_off = (tile_end - tiles_e) * MOE_TM
    n_tiles = 2 * rows // MOE_TM + ne
    n_slots = n_tiles * MOE_TM
    tile_expert = jnp.minimum(
        jnp.sum(jnp.arange(n_tiles)[:, None] >= tile_end[None, :], axis=1), ne - 1).astype(jnp.int32)
    n_used = tile_end[-1:].astype(jnp.int32)
    eids = jnp.arange(ne)

    def slot_of(lane_e, lane_r):
        e = rec[lane_e].astype(jnp.int32)
        off = jnp.sum(jnp.where(e[None, :] == eids[:, None], group_off[:, None], 0), axis=0)
        return off + rec[lane_r].astype(jnp.int32)

    tok = jnp.arange(rows, dtype=jnp.int32)
    dest = jnp.full((n_slots,), -1, jnp.int32).at[
        jnp.concatenate([slot_of(META_E1, META_R1), slot_of(META_E2, META_R2)])].set(
        jnp.concatenate([tok, rows + tok]), unique_indices=True, mode="promise_in_bounds")
    pad = dest < 0
    dest_of_slot = jnp.where(pad, 2 * rows + jnp.arange(n_slots, dtype=jnp.int32) % (2 * MOE_TM), dest)
    tok_of_slot = jnp.where(pad, 0, jnp.where(dest >= rows, dest - rows, dest))
    y_tiles = _experts(h_tiles, tok_of_slot, dest_of_slot, tile_expert, n_used, wg, wu, wd, 2 * rows + 2 * MOE_TM)
    return _combine(x2, mpack, mod_of_row, meta, y_tiles, g_final, tm)


def _rope_tables(seq_len):
    t = np.arange(seq_len)
    row = (t // GRID_W).astype(np.float32)
    col = (t % GRID_W).astype(np.float32)
    n_axis = HD // 4
    inv = jnp.power(ROPE_THETA, -jnp.arange(n_axis, dtype=F32) / n_axis)
    ang = jnp.concatenate([jnp.asarray(row)[:, None] * inv, jnp.asarray(col)[:, None] * inv], axis=-1)
    cos, sin = jnp.cos(ang), jnp.sin(ang)
    ct = jnp.tile(jnp.concatenate([cos, cos], axis=-1), (1, LANES // HD))
    st = jnp.tile(jnp.concatenate([-sin, sin], axis=-1), (1, LANES // HD))
    return ct, st


def kernel(x, c, ctx, c_ctx, w_mod, b_mod, g_norm1, g_norm2, w_in_ab, g_q, g_k, w_a2_f, b_a_f, w_a2_b,
           b_a_b, g_gla, w_out_ab, w_ff_gate, w_ff_up, w_ff_down, w_in_c, rpb_c, w_out_c, w_router,
           w_moe_gate, w_moe_up, w_moe_down, g_final):
    bsz, seq, _ = x.shape
    nc = ctx.shape[1]
    depth = w_mod.shape[0]
    assert depth == 2 and seq % GRID_W == 0

    cond_rows = -(-(bsz + 1) // 8) * 8
    cond = jnp.zeros((cond_rows, D), F32).at[:bsz].set(c).at[bsz].set(c_ctx)
    mods = _adaln(cond, w_mod, b_mod).reshape(depth, cond_rows, N_MOD, D)
    mods = jnp.pad(mods, ((0, 0), (0, 0), (0, MOD_ROWS - N_MOD), (0, 0)))

    xl = x.reshape(bsz * seq, D)
    xc = ctx.reshape(bsz * nc, D)
    tm_l = min(ROW_TM, seq)
    tm_c = min(ROW_TM, bsz * nc)
    tpb = seq // tm_l
    lat_mod = lambda i: i // tpb
    ctx_mod = lambda i: bsz

    mp = mods[0]
    w_in = w_in_ab[0]
    perm = np.concatenate([np.arange(0, HD, 2), np.arange(1, HD, 2)])
    nq, nkv = A_HEADS * HD, A_KV * HD
    o_q, o_k, o_v = 0, nq, nq + nkv
    o_bq = o_v + nkv
    o_bk = o_bq + B_HEADS * B_DK
    o_bv = o_bk + B_HEADS * B_DK
    o_br = o_bv + B_HEADS * B_DV
    o_zf = o_br + B_HEADS * B_DV
    o_end = o_zf + 2 * B_RANK
    assert A_KV * HD == LANES
    q_order = np.arange(A_HEADS).reshape(A_KV, A_GROUP).T.reshape(-1)
    wq = w_in[:, o_q:o_k].reshape(D, A_HEADS, HD)[:, q_order][:, :, perm].reshape(D, nq)
    wk = w_in[:, o_k:o_v].reshape(D, A_KV, HD)[:, :, perm].reshape(D, nkv)
    wz = jnp.pad(w_in[:, o_zf:o_end], ((0, 0), (0, LANES - 2 * B_RANK)))
    w0 = jnp.concatenate([wq, wk, w_in[:, o_v:o_bq], w_in[:, o_bq:o_bk] * (B_DK ** -0.5),
                          w_in[:, o_bk:o_zf], wz], axis=1).astype(BF)
    widths0 = (nq + nkv, nkv, GLA_W, LANES)
    outw0 = (nq, nkv, 2 * nkv, GLA_W, LANES)
    gqk = jnp.concatenate([jnp.tile(g_q[0][perm] * (HD ** -0.5 * LOG2E), A_HEADS),
                           jnp.tile(g_k[0][perm], A_KV)]).reshape(1, nq + nkv)
    bd = jnp.asarray(np.tile(np.kron(np.eye(LANES // HD), np.ones((HD, HD))), (2, 1)), BF)
    ct, st = _rope_tables(seq)
    ct_c = jnp.ones((tm_c, LANES), F32)
    st_c = jnp.zeros((tm_c, LANES), F32)

    kinds0 = ("qk", "values", "plain", "plain")
    ql, kl, vl, gl, zl = _proj(xl, g_norm1[0], mp, lat_mod, w0, widths0, kinds0, outw0, tm_l, (ct, st, gqk, bd))
    qc, kc, vc, gc, zc = _proj(xc, g_norm1[0], mp, ctx_mod, w0, widths0, kinds0, outw0, tm_c,
                               (ct_c, st_c, gqk, bd))

    oa_l = _gqa(ql, kc, vc, kl, vl, bsz, min(GQA_TQ, seq))
    oa_c = _gqa(qc, kc, vc, None, None, bsz, min(GQA_TQ, nc))

    w2f = jnp.zeros((LANES, B_HEADS * B_DK), F32).at[:B_RANK].set(w_a2_f[0]).astype(BF)
    w2b = jnp.zeros((LANES, B_HEADS * B_DK), F32).at[B_RANK:2 * B_RANK].set(w_a2_b[0]).astype(BF)
    ob_c, ob_l = _gla(gc, gl, zc, zl, w2f, w2b, b_a_f[0].reshape(1, -1), b_a_b[0].reshape(1, -1),
                      jnp.tile(g_gla[0], B_HEADS).reshape(1, -1), bsz)

    wo = w_out_ab[0].astype(BF)
    wo_a = wo[:nq].reshape(A_HEADS, HD, D)[q_order].reshape(nq, D)
    wo_b = wo[nq:]
    wg, wu, wd = w_ff_gate[0].astype(BF), w_ff_up[0].astype(BF), w_ff_down[0].astype(BF)
    xl = _ffn(xl, g_norm2[0], mp, lat_mod, [(oa_l, wo_a), (ob_l, wo_b)], wg, wu, wd, tm_l, FFN_TF)
    xc = _ffn(xc, g_norm2[0], mp, ctx_mod, [(oa_c, wo_a), (ob_c, wo_b)], wg, wu, wd, tm_c, FFN_TF)

    mp = mods[1]
    cw = C_HEADS * HD
    w1 = w_in_c[0].astype(BF)
    q1, k1, v1 = _proj(xl, g_norm1[1], mp, lat_mod, w1, (cw, cw, cw), ("plain", "plain", "values"),
                       (cw, cw, 2 * cw), tm_l, scales=(HD ** -0.5 * LOG2E, 1.0, 1.0))
    k1c, v1c = _proj(xc, g_norm1[1], mp, ctx_mod, w1, (cw, cw), ("plain", "values"), (cw, 2 * cw), tm_c,
                     col0=cw)

    bias = _na_bias(rpb_c[0], seq // GRID_W)
    o1 = _natten(q1, k1, v1, k1c, v1c, bias, bsz)

    wr = jnp.pad(w_router[0], ((0, 0), (0, LANES - N_EXPERTS)))
    out = _moe(xl, g_norm2[1], mp, lat_mod, lambda r: r // seq, [(o1, w_out_c[0].astype(BF))], wr,
               w_moe_gate[0].astype(BF), w_moe_up[0].astype(BF), w_moe_down[0].astype(BF), g_final, tm_l)
    return out.reshape(bsz, seq, D)
```
